```python
import math
import jax, jax.numpy as jnp
from jax import lax
import numpy as np

D_MODEL = 4096
BATCH = 1
SEQ = 16384
DEPTH = 2
DEC_BATCH = 32
DEC_SEQ = 64
PAST_LEN = 2048

CHUNK = 64
N_META = 16
D_MIX = D_MODEL
EPS = 1e-6
SSD_HEAD_DIM = 64
SSD_D_INNER = D_MIX // 2
SSD_HEADS = SSD_D_INNER // SSD_HEAD_DIM
SSD_GROUPS = 8
SSD_STATE = 128
SSD_CONV = 4
SSD_CONV_DIM = SSD_D_INNER + 2 * SSD_GROUPS * SSD_STATE
SSD_CHUNK = CHUNK
SC_DIM = D_MIX // 4
SC_CONV = 3
MLA_HEADS = 8
MLA_NOPE = 128
MLA_ROPE = 64
MLA_V = 128
MLA_Q_LORA = 768
MLA_KV_LORA = 512
MLA_SCALE = (MLA_NOPE + MLA_ROPE) ** -0.5
ROPE_THETA = 10000.0
Q_BLOCK = 128
D_FF = 11008
FFN_CONV = 3
IN_SSD = SSD_D_INNER + SSD_CONV_DIM + SSD_HEADS
IN_SC = 3 * SC_DIM
IN_MLA = MLA_Q_LORA + MLA_KV_LORA + MLA_ROPE
IN_DIM = IN_SSD + IN_SC + IN_MLA

kernel_name = 'hymba_ssd_shortconv_mla_streaming_step'


def rmsnorm(x, g):
    xf = x.astype(jnp.float32)
    xf = xf * lax.rsqrt(jnp.mean(xf * xf, axis=-1, keepdims=True) + EPS)
    return (xf * g.astype(jnp.float32)).astype(x.dtype)


def causal_dwconv(x, prev, w):
    k = w.shape[0]
    l = x.shape[1]
    xp = jnp.concatenate([prev.astype(x.dtype), x], axis=1)
    y = w[0] * xp[:, 0:l]
    for i in range(1, k):
        y = y + w[i] * xp[:, i:i + l]
    return y, xp[:, l:]


def rope(x, pos):
    half = x.shape[-1] // 2
    inv = ROPE_THETA ** (-jnp.arange(half, dtype=jnp.float32) / half)
    ang = pos.astype(jnp.float32)[:, None] * inv[None, :]
    shape = (1, pos.shape[0]) + (1,) * (x.ndim - 3) + (half,)
    cos = jnp.cos(ang).reshape(shape)
    sin = jnp.sin(ang).reshape(shape)
    xf = x.astype(jnp.float32)
    x1, x2 = xf[..., :half], xf[..., half:]
    return jnp.concatenate([x1 * cos - x2 * sin, x1 * sin + x2 * cos], axis=-1).astype(x.dtype)


def segsum(a):
    t = a.shape[-1]
    cs = jnp.cumsum(a, axis=-1)
    diff = cs[..., :, None] - cs[..., None, :]
    mask = jnp.tril(jnp.ones((t, t), dtype=bool))
    return jnp.where(mask, diff, -jnp.inf)


def ssd_scan(xs, dt, a, bm, cm, h0):
    b, l, h, p = xs.shape
    g, n = bm.shape[-2], bm.shape[-1]
    r = h // g
    t = SSD_CHUNK
    pad = (-l) % t
    nc = (l + pad) // t

    def chunked(v):
        v = jnp.pad(v.astype(jnp.float32), [(0, 0), (0, pad)] + [(0, 0)] * (v.ndim - 2))
        return v.reshape((b, nc, t) + v.shape[2:])

    xc = chunked(xs).reshape(b, nc, t, g, r, p)
    dtc = chunked(dt).reshape(b, nc, t, g, r)
    bc = chunked(bm)
    cc = chunked(cm)
    xdt = xc * dtc[..., None]
    adt = jnp.moveaxis(dtc * a.astype(jnp.float32).reshape(g, r), 2, -1)
    a_cs = jnp.cumsum(adt, axis=-1)
    decay_in = jnp.exp(segsum(adt))
    cb = jnp.einsum('bctgn,bcsgn->bcgts', cc, bc)
    y_diag = jnp.einsum('bcgts,bcgrts,bcsgrp->bctgrp', cb, decay_in, xdt)
    decay_out = jnp.exp(a_cs[..., -1:] - a_cs)
    chunk_states = jnp.einsum('bcsgn,bcgrs,bcsgrp->bcgrpn', bc, decay_out, xdt)
    chunk_decay = jnp.exp(a_cs[..., -1])

    def step(hc, inp):
        st, dc = inp
        return hc * dc[..., None, None] + st, hc

    h_init = h0.astype(jnp.float32).reshape(b, g, r, p, n)
    h_fin, h_in = lax.scan(step, h_init, (jnp.moveaxis(chunk_states, 1, 0), jnp.moveaxis(chunk_decay, 1, 0)))
    h_in = jnp.moveaxis(h_in, 0, 1)
    y_off = jnp.einsum('bctgn,bcgrpn,bcgrt->bctgrp', cc, h_in, jnp.exp(a_cs))
    y = (y_diag + y_off).reshape(b, nc * t, h, p)[:, :l]
    return y.astype(xs.dtype), h_fin.reshape(b, h, p, n)


def ssd_mixer(u, conv_prev, h0, conv_w, conv_b, dt_bias, a_log, d_skip, norm_g):
    b, l, _ = u.shape
    z = u[..., :SSD_D_INNER]
    xbc = u[..., SSD_D_INNER:SSD_D_INNER + SSD_CONV_DIM]
    dt_raw = u[..., SSD_D_INNER + SSD_CONV_DIM:]
    xbc, conv_new = causal_dwconv(xbc, conv_prev, conv_w)
    xbc = jax.nn.silu(xbc + conv_b)
    gn = SSD_GROUPS * SSD_STATE
    xs = xbc[..., :SSD_D_INNER].reshape(b, l, SSD_HEADS, SSD_HEAD_DIM)
    bm = xbc[..., SSD_D_INNER:SSD_D_INNER + gn].reshape(b, l, SSD_GROUPS, SSD_STATE)
    cm = xbc[..., SSD_D_INNER + gn:].reshape(b, l, SSD_GROUPS, SSD_STATE)
    dt = jax.nn.softplus(dt_raw.astype(jnp.float32) + dt_bias.astype(jnp.float32))
    a = -jnp.exp(a_log.astype(jnp.float32))
    y, h_new = ssd_scan(xs, dt, a, bm, cm, h0)
    y = y + d_skip[:, None].astype(y.dtype) * xs
    y = y.reshape(b, l, SSD_D_INNER) * jax.nn.silu(z)
    gsz = SSD_D_INNER // SSD_GROUPS
    y = rmsnorm(y.reshape(b, l, SSD_GROUPS, gsz), norm_g.reshape(SSD_GROUPS, gsz)).reshape(b, l, SSD_D_INNER)
    return y, conv_new, h_new


def short_conv_mixer(u, prev, conv_w):
    b_gate = u[..., :SC_DIM]
    c_gate = u[..., SC_DIM:2 * SC_DIM]
    h = u[..., 2 * SC_DIM:]
    y, new = causal_dwconv(c_gate * h, prev, conv_w)
    return b_gate * y, new


def mla_project(u, pos, q_norm, w_uq, kv_norm):
    b, l, _ = u.shape
    c_q = rmsnorm(u[..., :MLA_Q_LORA], q_norm)
    c_kv = rmsnorm(u[..., MLA_Q_LORA:MLA_Q_LORA + MLA_KV_LORA], kv_norm)
    k_pe = rope(u[..., MLA_Q_LORA + MLA_KV_LORA:], pos)
    q = (c_q @ w_uq).reshape(b, l, MLA_HEADS, MLA_NOPE + MLA_ROPE)
    return q[..., :MLA_NOPE], rope(q[..., MLA_NOPE:], pos), c_kv, k_pe


def attend(q_nope, q_pe, k_nope, k_pe, v, mask):
    s = (jnp.einsum('bqhd,bkhd->bhqk', q_nope, k_nope).astype(jnp.float32)
         + jnp.einsum('bqhd,bkd->bhqk', q_pe, k_pe).astype(jnp.float32)) * MLA_SCALE
    if mask is not None:
        s = jnp.where(mask, s, -jnp.inf)
    p = jax.nn.softmax(s, axis=-1)
    return jnp.einsum('bhqk,bkhd->bqhd', p.astype(v.dtype), v)


def mla_prompt_attention(q_nope, q_pe, c_kv, k_pe, w_ukv):
    b, l = c_kv.shape[:2]
    n_real = l - N_META
    kv = (c_kv @ w_ukv).reshape(b, l, MLA_HEADS, MLA_NOPE + MLA_V)
    k_nope, v = kv[..., :MLA_NOPE], kv[..., MLA_NOPE:]
    k_chunk = jnp.concatenate([jnp.zeros((N_META,), jnp.int32),
                               1 + jnp.arange(n_real, dtype=jnp.int32) // CHUNK])
    meta_out = attend(q_nope[:, :N_META], q_pe[:, :N_META], k_nope[:, :N_META],
                      k_pe[:, :N_META], v[:, :N_META], None)

    def block(i):
        start = N_META + i * Q_BLOCK
        qn = lax.dynamic_slice_in_dim(q_nope, start, Q_BLOCK, axis=1)
        qp = lax.dynamic_slice_in_dim(q_pe, start, Q_BLOCK, axis=1)
        q_chunk = 1 + (i * Q_BLOCK + jnp.arange(Q_BLOCK, dtype=jnp.int32)) // CHUNK
        mask = k_chunk[None, :] <= q_chunk[:, None]
        return attend(qn, qp, k_nope, k_pe, v, mask)

    real_out = lax.map(block, jnp.arange(n_real // Q_BLOCK, dtype=jnp.int32))
    real_out = jnp.moveaxis(real_out, 0, 1).reshape(b, n_real, MLA_HEADS, MLA_V)
    return jnp.concatenate([meta_out, real_out], axis=1)


def mla_cached_attention(q_nope, q_pe, c_kv, k_pe, kv_cache, pe_cache, w_ukv):
    c_all = jnp.concatenate([kv_cache.astype(c_kv.dtype), c_kv], axis=1)
    pe_all = jnp.concatenate([pe_cache.astype(k_pe.dtype), k_pe], axis=1)
    w = w_ukv.reshape(MLA_KV_LORA, MLA_HEADS, MLA_NOPE + MLA_V)
    q_lat = jnp.einsum('bshd,chd->bshc', q_nope, w[..., :MLA_NOPE])
    s = (jnp.einsum('bshc,btc->bhst', q_lat, c_all).astype(jnp.float32)
         + jnp.einsum('bshd,btd->bhst', q_pe, pe_all).astype(jnp.float32)) * MLA_SCALE
    p = jax.nn.softmax(s, axis=-1)
    o_lat = jnp.einsum('bhst,btc->bshc', p.astype(c_all.dtype), c_all)
    return jnp.einsum('bshc,chd->bshd', o_lat, w[..., MLA_NOPE:])


def trunk_layer(x, pos, kv_cache, pe_cache, ssm0, ssd_conv0, sconv0, ffn_conv0,
                g_mix_pre, g_mix_post, g_ffn_pre, g_ffn_post, w_in, ssd_conv_w, ssd_conv_b,
                ssd_dt_bias, ssd_a_log, ssd_d, ssd_norm, sc_conv_w, mla_q_norm, mla_w_uq,
                mla_kv_norm, mla_w_ukv, w_out, ffn_w_gate, ffn_w_up, ffn_conv_w, ffn_w_down):
    b, l, _ = x.shape
    h = rmsnorm(x, g_mix_pre)
    u = h @ w_in
    u_ssd = u[..., :IN_SSD]
    u_sc = u[..., IN_SSD:IN_SSD + IN_SC]
    u_mla = u[..., IN_SSD + IN_SC:]
    y_ssd, ssd_conv_new, ssm_new = ssd_mixer(u_ssd, ssd_conv0, ssm0, ssd_conv_w, ssd_conv_b,
                                             ssd_dt_bias, ssd_a_log, ssd_d, ssd_norm)
    y_sc, sconv_new = short_conv_mixer(u_sc, sconv0, sc_conv_w)
    q_nope, q_pe, c_kv, k_pe = mla_project(u_mla, pos, mla_q_norm, mla_w_uq, mla_kv_norm)
    if kv_cache is None:
        y_mla = mla_prompt_attention(q_nope, q_pe, c_kv, k_pe, mla_w_ukv)
    else:
        y_mla = mla_cached_attention(q_nope, q_pe, c_kv, k_pe, kv_cache, pe_cache, mla_w_ukv)
    mix = jnp.concatenate([y_ssd, y_sc, y_mla.reshape(b, l, MLA_HEADS * MLA_V)], axis=-1) @ w_out
    x = x + rmsnorm(mix, g_mix_post)
    h = rmsnorm(x, g_ffn_pre)
    gate, ffn_conv_new = causal_dwconv(h @ ffn_w_gate, ffn_conv0, ffn_conv_w)
    f = (jax.nn.silu(gate) * (h @ ffn_w_up)) @ ffn_w_down
    x = x + rmsnorm(f, g_ffn_post)
    return x, c_kv, k_pe, ssm_new, ssd_conv_new, sconv_new, ffn_conv_new


def setup_inputs(seed: int = 0) -> dict:
    key = jax.random.key(seed)
    ks = iter(jax.random.split(key, 48))

    def nrm(shape, scale=1.0):
        return scale * jax.random.normal(next(ks), shape, jnp.float32)

    def gain(shape):
        return 1.0 + 0.05 * jax.random.normal(next(ks), shape, jnp.float32)

    x_prompt = nrm((BATCH, SEQ, D_MODEL))
    x_sample = nrm((DEC_BATCH, DEC_SEQ, D_MODEL))
    cache_kv_latent = nrm((DEPTH, DEC_BATCH, PAST_LEN, MLA_KV_LORA))
    cache_k_rope = nrm((DEPTH, DEC_BATCH, PAST_LEN, MLA_ROPE))
    state_ssm = nrm((DEPTH, DEC_BATCH, SSD_HEADS, SSD_HEAD_DIM, SSD_STATE), 0.1)
    state_ssd_conv = nrm((DEPTH, DEC_BATCH, SSD_CONV - 1, SSD_CONV_DIM))
    state_sconv = nrm((DEPTH, DEC_BATCH, SC_CONV - 1, SC_DIM))
    state_ffn_conv = nrm((DEPTH, DEC_BATCH, FFN_CONV - 1, D_FF))
    meta_tokens = nrm((N_META, D_MODEL))
    norm_mix_pre = gain((DEPTH, D_MODEL))
    norm_mix_post = gain((DEPTH, D_MODEL))
    norm_ffn_pre = gain((DEPTH, D_MODEL))
    norm_ffn_post = gain((DEPTH, D_MODEL))
    w_in = nrm((DEPTH, D_MODEL, IN_DIM), D_MODEL ** -0.5)
    ssd_conv_w = nrm((DEPTH, SSD_CONV, SSD_CONV_DIM), SSD_CONV ** -0.5)
    ssd_conv_b = nrm((DEPTH, SSD_CONV_DIM), 0.01)
    dt0 = jnp.exp(jax.random.uniform(next(ks), (DEPTH, SSD_HEADS), jnp.float32,
                                     minval=math.log(1e-3), maxval=math.log(1e-1)))
    ssd_dt_bias = dt0 + jnp.log(-jnp.expm1(-dt0))
    ssd_a_log = jnp.log(jax.random.uniform(next(ks), (DEPTH, SSD_HEADS), jnp.float32, minval=1.0, maxval=16.0))
    ssd_d = gain((DEPTH, SSD_HEADS))
    ssd_norm = gain((DEPTH, SSD_D_INNER))
    sc_conv_w = nrm((DEPTH, SC_CONV, SC_DIM), SC_CONV ** -0.5)
    mla_q_norm = gain((DEPTH, MLA_Q_LORA))
    mla_w_uq = nrm((DEPTH, MLA_Q_LORA, MLA_HEADS * (MLA_NOPE + MLA_ROPE)), MLA_Q_LORA ** -0.5)
    mla_kv_norm = gain((DEPTH, MLA_KV_LORA))
    mla_w_ukv = nrm((DEPTH, MLA_KV_LORA, MLA_HEADS * (MLA_NOPE + MLA_V)), MLA_KV_LORA ** -0.5)
    w_out = nrm((DEPTH, D_MIX, D_MODEL), D_MIX ** -0.5)
    ffn_w_gate = nrm((DEPTH, D_MODEL, D_FF), D_MODEL ** -0.5)
    ffn_w_up = nrm((DEPTH, D_MODEL, D_FF), D_MODEL ** -0.5)
    ffn_conv_w = nrm((DEPTH, FFN_CONV, D_FF), FFN_CONV ** -0.5)
    ffn_w_down = nrm((DEPTH, D_FF, D_MODEL), D_FF ** -0.5)
    return {'x_prompt': x_prompt, 'x_sample': x_sample,
            'cache_kv_latent': cache_kv_latent, 'cache_k_rope': cache_k_rope,
            'state_ssm': state_ssm, 'state_ssd_conv': state_ssd_conv,
            'state_sconv': state_sconv, 'state_ffn_conv': state_ffn_conv,
            'meta_tokens': meta_tokens, 'norm_mix_pre': norm_mix_pre, 'norm_mix_post': norm_mix_post,
            'norm_ffn_pre': norm_ffn_pre, 'norm_ffn_post': norm_ffn_post, 'w_in': w_in,
            'ssd_conv_w': ssd_conv_w, 'ssd_conv_b': ssd_conv_b, 'ssd_dt_bias': ssd_dt_bias,
            'ssd_a_log': ssd_a_log, 'ssd_d': ssd_d, 'ssd_norm': ssd_norm, 'sc_conv_w': sc_conv_w,
            'mla_q_norm': mla_q_norm, 'mla_w_uq': mla_w_uq, 'mla_kv_norm': mla_kv_norm,
            'mla_w_ukv': mla_w_ukv, 'w_out': w_out, 'ffn_w_gate': ffn_w_gate, 'ffn_w_up': ffn_w_up,
            'ffn_conv_w': ffn_conv_w, 'ffn_w_down': ffn_w_down}


def reference(x_prompt, x_sample, cache_kv_latent, cache_k_rope, state_ssm, state_ssd_conv,
              state_sconv, state_ffn_conv, meta_tokens, norm_mix_pre, norm_mix_post, norm_ffn_pre,
              norm_ffn_post, w_in, ssd_conv_w, ssd_conv_b, ssd_dt_bias, ssd_a_log, ssd_d, ssd_norm,
              sc_conv_w, mla_q_norm, mla_w_uq, mla_kv_norm, mla_w_ukv, w_out, ffn_w_gate, ffn_w_up,
              ffn_conv_w, ffn_w_down):
    bp = x_prompt.shape[0]
    ls = x_sample.shape[1]
    past = cache_kv_latent.shape[2]
    dt = x_prompt.dtype
    meta = jnp.broadcast_to(meta_tokens[None].astype(dt), (bp, N_META, D_MODEL))
    xp = jnp.concatenate([meta, x_prompt], axis=1)
    pos_p = jnp.arange(xp.shape[1], dtype=jnp.int32)
    pos_s = N_META + past + jnp.arange(ls, dtype=jnp.int32)
    xs = x_sample
    ssm0 = jnp.zeros((bp, SSD_HEADS, SSD_HEAD_DIM, SSD_STATE), jnp.float32)
    ssd_conv0 = jnp.zeros((bp, SSD_CONV - 1, SSD_CONV_DIM), dt)
    sconv0 = jnp.zeros((bp, SC_CONV - 1, SC_DIM), dt)
    ffn_conv0 = jnp.zeros((bp, FFN_CONV - 1, D_FF), dt)
    outs_p = []
    outs_s = []
    for i in range(DEPTH):
        lw = (norm_mix_pre[i], norm_mix_post[i], norm_ffn_pre[i], norm_ffn_post[i], w_in[i],
              ssd_conv_w[i], ssd_conv_b[i], ssd_dt_bias[i], ssd_a_log[i], ssd_d[i], ssd_norm[i],
              sc_conv_w[i], mla_q_norm[i], mla_w_uq[i], mla_kv_norm[i], mla_w_ukv[i], w_out[i],
              ffn_w_gate[i], ffn_w_up[i], ffn_conv_w[i], ffn_w_down[i])
        xp, *sp = trunk_layer(xp, pos_p, None, None, ssm0, ssd_conv0, sconv0, ffn_conv0, *lw)
        xs, *ss = trunk_layer(xs, pos_s, cache_kv_latent[i], cache_k_rope[i], state_ssm[i],
                              state_ssd_conv[i], state_sconv[i], state_ffn_conv[i], *lw)
        outs_p.append(sp)
        outs_s.append(ss)

    def stack(outs, j):
        return jnp.stack([o[j] for o in outs], axis=0)

    y_prompt = xp[:, N_META:]
    return (y_prompt, xs,
            stack(outs_p, 0), stack(outs_p, 1), stack(outs_p, 2), stack(outs_p, 3), stack(outs_p, 4), stack(outs_p, 5),
            stack(outs_s, 0), stack(outs_s, 1), stack(outs_s, 2), stack(outs_s, 3), stack(outs_s, 4), stack(outs_s, 5))
```

```python
import functools

import jax
import jax.numpy as jnp
import numpy as np
from jax import lax
from jax.experimental import pallas as pl
from jax.experimental.pallas import tpu as pltpu

F32 = jnp.float32
BF16 = jnp.bfloat16

D_MODEL = 4096
N_META = 16
CH = 64
NSKIP = CH - N_META
EPS = 1e-6
SSD_P = 64
SSD_DI = 2048
SSD_H = 32
SSD_G = 8
SSD_N = 128
SSD_K = 4
SSD_CD = 4096
SC_D = 1024
SC_K = 3
MLA_H = 8
NOPE = 128
ROPE = 64
MLA_V = 128
QL = 768
KVL = 512
HK = 256
SCALE = (NOPE + ROPE) ** -0.5
THETA = 10000.0
D_FF = 11008
DFP = 11264
FFN_K = 3
LANE = 128

C_Z, C_X, C_BC, C_SCB, C_SCC, C_SCH, C_MLA, NIN = 0, 2048, 4096, 6144, 7168, 8192, 9216, 10752
MLA_W = 1536

FLASH_TQ = 512
VMEM_LIMIT = 56 * 1024 * 1024
HI = lax.Precision.HIGHEST
NT = (((1,), (1,)), ((), ()))
TN = (((0,), (0,)), ((), ()))


def _cp(n):
    return pltpu.CompilerParams(dimension_semantics=("arbitrary",) * n, vmem_limit_bytes=VMEM_LIMIT)


def _pick(n, cands):
    for c in cands:
        if n % c == 0:
            return c
    raise ValueError(f"no tile for {n}")


MM_ROWS = (1088, 544, 512, 272, 256, 136, 128, 64)
EW_ROWS = (272, 256, 136, 128, 64)


def _rms(x, g):
    return x * lax.rsqrt(jnp.mean(x * x, axis=-1, keepdims=True) + EPS) * g


def _silu(x):
    return x * jax.nn.sigmoid(x)


def _norm_cast_kernel(x_ref, g_ref, h_ref):
    h_ref[...] = _rms(x_ref[...], g_ref[...]).astype(BF16)


def _norm_cast(x, g):
    r, d = x.shape
    bm = _pick(r, EW_ROWS)
    return pl.pallas_call(
        _norm_cast_kernel,
        grid=(r // bm,),
        in_specs=[pl.BlockSpec((bm, d), lambda i: (i, 0)), pl.BlockSpec((1, d), lambda i: (0, 0))],
        out_specs=pl.BlockSpec((bm, d), lambda i: (i, 0)),
        out_shape=jax.ShapeDtypeStruct((r, d), BF16),
        compiler_params=_cp(1), name="norm_cast",
    )(x, g.reshape(1, d))


def _resid_norm_kernel(x_ref, m_ref, gp_ref, gn_ref, x2_ref, h_ref):
    x2 = x_ref[...] + _rms(m_ref[...], gp_ref[...])
    x2_ref[...] = x2
    h_ref[...] = _rms(x2, gn_ref[...]).astype(BF16)


def _resid_kernel(x_ref, m_ref, gp_ref, x2_ref):
    x2_ref[...] = x_ref[...] + _rms(m_ref[...], gp_ref[...])


def _resid_norm(x, m, g_post, g_next):
    r, d = x.shape
    bm = _pick(r, EW_ROWS)
    row = pl.BlockSpec((bm, d), lambda i: (i, 0))
    vec = pl.BlockSpec((1, d), lambda i: (0, 0))
    if g_next is None:
        return pl.pallas_call(
            _resid_kernel, grid=(r // bm,), in_specs=[row, row, vec], out_specs=row,
            out_shape=jax.ShapeDtypeStruct((r, d), F32), compiler_params=_cp(1), name="resid",
        )(x, m, g_post.reshape(1, d)), None
    return pl.pallas_call(
        _resid_norm_kernel, grid=(r // bm,), in_specs=[row, row, vec, vec], out_specs=[row, row],
        out_shape=[jax.ShapeDtypeStruct((r, d), F32), jax.ShapeDtypeStruct((r, d), BF16)],
        compiler_params=_cp(1), name="resid_norm",
    )(x, m, g_post.reshape(1, d), g_next.reshape(1, d))


def _inproj_kernel(x_ref, w_ref, u_ref, dt_ref):
    acc = jnp.dot(x_ref[...], w_ref[...], preferred_element_type=F32)
    u_ref[...] = acc.astype(BF16)

    @pl.when(pl.program_id(1) == pl.num_programs(1) - 1)
    def _():
        dt_ref[...] = acc[:, acc.shape[1] - LANE:]


def _inproj(h, w):
    r, k = h.shape
    n = w.shape[1]
    bm = _pick(r, MM_ROWS)
    bn = 768
    return pl.pallas_call(
        _inproj_kernel,
        grid=(r // bm, n // bn),
        in_specs=[pl.BlockSpec((bm, k), lambda i, j: (i, 0)), pl.BlockSpec((k, bn), lambda i, j: (0, j))],
        out_specs=[pl.BlockSpec((bm, bn), lambda i, j: (i, j)), pl.BlockSpec((bm, LANE), lambda i, j: (i, 0))],
        out_shape=[jax.ShapeDtypeStruct((r, n), BF16), jax.ShapeDtypeStruct((r, LANE), F32)],
        compiler_params=_cp(2), name="inproj",
    )(h, w)


def _outproj_kernel(a_ref, b_ref, c_ref, wa_ref, wb_ref, wc_ref, o_ref):
    acc = jnp.dot(a_ref[...], wa_ref[...], preferred_element_type=F32)
    acc = acc + jnp.dot(b_ref[...], wb_ref[...], preferred_element_type=F32)
    acc = acc + jnp.dot(c_ref[...], wc_ref[...], preferred_element_type=F32)
    o_ref[...] = acc


def _outproj(y_ssd, y_sc, y_mla, w):
    r = y_ssd.shape[0]
    n = w.shape[1]
    bm = _pick(r, MM_ROWS)
    bn = 1024
    return pl.pallas_call(
        _outproj_kernel,
        grid=(r // bm, n // bn),
        in_specs=[pl.BlockSpec((bm, SSD_DI), lambda i, j: (i, 0)),
                  pl.BlockSpec((bm, SC_D), lambda i, j: (i, 0)),
                  pl.BlockSpec((bm, SC_D), lambda i, j: (i, 0)),
                  pl.BlockSpec((SSD_DI, bn), lambda i, j: (0, j)),
                  pl.BlockSpec((SC_D, bn), lambda i, j: (2, j)),
                  pl.BlockSpec((SC_D, bn), lambda i, j: (3, j))],
        out_specs=pl.BlockSpec((bm, bn), lambda i, j: (i, j)),
        out_shape=jax.ShapeDtypeStruct((r, n), F32),
        compiler_params=_cp(2), name="outproj",
    )(y_ssd, y_sc, y_mla, w, w, w)


def _gateup_kernel(x_ref, wg_ref, wu_ref, g_ref, u_ref):
    x = x_ref[...]
    g_ref[...] = jnp.dot(x, wg_ref[...], preferred_element_type=F32).astype(BF16)
    u_ref[...] = jnp.dot(x, wu_ref[...], preferred_element_type=F32).astype(BF16)


def _gateup(h, wg, wu):
    r, k = h.shape
    n = wg.shape[1]
    bm = _pick(r, MM_ROWS)
    bn = 512
    wspec = pl.BlockSpec((k, bn), lambda i, j: (0, j))
    ospec = pl.BlockSpec((bm, bn), lambda i, j: (i, j))
    return pl.pallas_call(
        _gateup_kernel,
        grid=(r // bm, n // bn),
        in_specs=[pl.BlockSpec((bm, k), lambda i, j: (i, 0)), wspec, wspec],
        out_specs=[ospec, ospec],
        out_shape=[jax.ShapeDtypeStruct((r, n), BF16)] * 2,
        compiler_params=_cp(2), name="gateup",
    )(h, wg, wu)


def _down_kernel(x_ref, w_ref, o_ref):
    p = jnp.dot(x_ref[...], w_ref[...], preferred_element_type=F32)

    @pl.when(pl.program_id(2) == 0)
    def _():
        o_ref[...] = p

    @pl.when(pl.program_id(2) > 0)
    def _():
        o_ref[...] += p


def _down(a, w):
    r, k = a.shape
    n = w.shape[1]
    bm = _pick(r, MM_ROWS)
    bn = 1024
    bk = k // 4
    return pl.pallas_call(
        _down_kernel,
        grid=(r // bm, n // bn, k // bk),
        in_specs=[pl.BlockSpec((bm, bk), lambda i, j, kk: (i, kk)), pl.BlockSpec((bk, bn), lambda i, j, kk: (kk, j))],
        out_specs=pl.BlockSpec((bm, bn), lambda i, j, kk: (i, j)),
        out_shape=jax.ShapeDtypeStruct((r, n), F32),
        compiler_params=_cp(3), name="down",
    )(a, w)


def _mixer_kernel(nskip, z_ref, x_ref, bc_ref, dt_ref, scb_ref, scc_ref, sch_ref,
                  cprev_ref, h0_ref, scprev_ref,
                  cw_ref, cb_ref, dtb_ref, alog_ref, dsk_ref, ng_ref, scw_ref,
                  y_ref, ysc_ref, cnew_ref, hnew_ref, scnew_ref,
                  buf, sbuf, hst, ybuf):
    t = CH
    c = pl.program_id(1)
    last = pl.num_programs(1) - 1

    @pl.when(c == 0)
    def _init():
        buf[0:8, :] = jnp.zeros((8, SSD_CD), F32)
        buf[8 - (SSD_K - 1):8, :] = cprev_ref[0]
        sbuf[0:8, :] = jnp.zeros((8, SC_D), F32)
        sbuf[8 - (SC_K - 1):8, :] = scprev_ref[0]
        hst[...] = h0_ref[0]

    rows = lax.broadcasted_iota(jnp.int32, (t, 1), 0) + c * t
    valid = rows >= nskip

    xin = jnp.concatenate([x_ref[...], bc_ref[...]], axis=1).astype(F32)
    xin = jnp.where(valid, xin, 0.0)
    buf[8:8 + t, :] = xin
    acc = cb_ref[...] + cw_ref[0:1, :] * buf[5:5 + t, :]
    for i in range(1, SSD_K):
        acc = acc + cw_ref[i:i + 1, :] * buf[5 + i:5 + i + t, :]
    xbc = _silu(acc)
    buf[0:8, :] = buf[t:t + 8, :]

    @pl.when(c == last)
    def _():
        cnew_ref[0] = buf[8 - (SSD_K - 1):8, :]

    lane = lax.broadcasted_iota(jnp.int32, (t, LANE), 1)
    v = dt_ref[...] + dtb_ref[...]
    dt = jnp.maximum(v, 0.0) + jnp.log1p(jnp.exp(-jnp.abs(v)))
    dt = jnp.where(valid & (lane < SSD_H), dt, 0.0)
    adt = dt * (-jnp.exp(alog_ref[...]))
    ri = lax.broadcasted_iota(jnp.int32, (t, t), 0)
    ci = lax.broadcasted_iota(jnp.int32, (t, t), 1)
    tri = ri >= ci
    acs = jnp.dot(tri.astype(F32), adt, precision=HI, preferred_element_type=F32)
    eye = (lax.broadcasted_iota(jnp.int32, (LANE, LANE), 0)
           == lax.broadcasted_iota(jnp.int32, (LANE, LANE), 1)).astype(F32)
    acs_t = lax.dot_general(eye, acs, NT, precision=HI, preferred_element_type=F32)

    for g in range(SSD_G):
        bg = xbc[:, SSD_DI + g * SSD_N:SSD_DI + (g + 1) * SSD_N].astype(BF16)
        cg = xbc[:, SSD_DI + SSD_G * SSD_N + g * SSD_N:SSD_DI + SSD_G * SSD_N + (g + 1) * SSD_N].astype(BF16)
        cb = lax.dot_general(cg, bg, NT, preferred_element_type=F32)
        for r in range(SSD_H // SSD_G):
            h = g * (SSD_H // SSD_G) + r
            acol = acs[:, h:h + 1]
            arow = acs_t[h:h + 1, :]
            alast = acs[t - 1:t, h:h + 1]
            decay = jnp.exp(jnp.where(tri, acol - arow, -jnp.inf))
            xh = xbc[:, h * SSD_P:(h + 1) * SSD_P]
            xdt = xh * dt[:, h:h + 1]
            ydiag = jnp.dot((cb * decay).astype(BF16), xdt.astype(BF16), preferred_element_type=F32)
            hh = hst[h]
            yoff = lax.dot_general(cg, hh.astype(BF16), NT, preferred_element_type=F32) * jnp.exp(acol)
            snew = lax.dot_general((xdt * jnp.exp(alast - acol)).astype(BF16), bg, TN,
                                   preferred_element_type=F32)
            hst[h] = jnp.exp(alast) * hh + snew
            ybuf[:, h * SSD_P:(h + 1) * SSD_P] = ydiag + yoff + dsk_ref[:, h * SSD_P:(h + 1) * SSD_P] * xh

    @pl.when(c == last)
    def _():
        hnew_ref[0] = hst[...]

    zf = z_ref[...].astype(F32)
    y = ybuf[...] * _silu(zf)
    gw = SSD_DI // SSD_G
    parts = []
    for g in range(SSD_G):
        seg = y[:, g * gw:(g + 1) * gw]
        parts.append(_rms(seg, ng_ref[:, g * gw:(g + 1) * gw]))
    y_ref[...] = jnp.concatenate(parts, axis=1).astype(BF16)

    p = scc_ref[...].astype(F32) * sch_ref[...].astype(F32)
    p = jnp.where(valid, p, 0.0)
    sbuf[8:8 + t, :] = p
    conv = scw_ref[0:1, :] * sbuf[6:6 + t, :]
    for i in range(1, SC_K):
        conv = conv + scw_ref[i:i + 1, :] * sbuf[6 + i:6 + i + t, :]
    ysc_ref[...] = (scb_ref[...].astype(F32) * conv).astype(BF16)
    sbuf[0:8, :] = sbuf[t:t + 8, :]

    @pl.when(c == last)
    def _():
        scnew_ref[0] = sbuf[8 - (SC_K - 1):8, :]


def _mixer_alias_kernel(nskip, *refs):
    n_in = 17
    _mixer_kernel(nskip, *refs[:n_in], *refs[n_in + 2:])


def _mixer(u, dtf, rb0, nb, nch, nskip, cprev, h0, scprev, lw, keep=None):
    r = u.shape[0]
    t = CH

    def rowmap(cb):
        return lambda b, c: (rb0 + b * nch + c, cb)

    def stmap(b, c):
        return (b, 0, 0)

    def full(shape):
        return pl.BlockSpec(shape, lambda b, c: (0,) * len(shape))

    in_specs = [
        pl.BlockSpec((t, SSD_DI), rowmap(C_Z // SSD_DI)),
        pl.BlockSpec((t, SSD_DI), rowmap(C_X // SSD_DI)),
        pl.BlockSpec((t, SSD_DI), rowmap(C_BC // SSD_DI)),
        pl.BlockSpec((t, LANE), rowmap(0)),
        pl.BlockSpec((t, SC_D), rowmap(C_SCB // SC_D)),
        pl.BlockSpec((t, SC_D), rowmap(C_SCC // SC_D)),
        pl.BlockSpec((t, SC_D), rowmap(C_SCH // SC_D)),
        pl.BlockSpec((1, SSD_K - 1, SSD_CD), stmap),
        pl.BlockSpec((1, SSD_H, SSD_P, SSD_N), lambda b, c: (b, 0, 0, 0)),
        pl.BlockSpec((1, SC_K - 1, SC_D), stmap),
        full((SSD_K, SSD_CD)), full((1, SSD_CD)), full((1, LANE)), full((1, LANE)),
        full((1, SSD_DI)), full((1, SSD_DI)), full((SC_K, SC_D)),
    ]
    out_specs = [
        pl.BlockSpec((t, SSD_DI), rowmap(0)),
        pl.BlockSpec((t, SC_D), rowmap(0)),
        pl.BlockSpec((1, SSD_K - 1, SSD_CD), stmap),
        pl.BlockSpec((1, SSD_H, SSD_P, SSD_N), lambda b, c: (b, 0, 0, 0)),
        pl.BlockSpec((1, SC_K - 1, SC_D), stmap),
    ]
    out_shape = [
        jax.ShapeDtypeStruct((r, SSD_DI), BF16),
        jax.ShapeDtypeStruct((r, SC_D), BF16),
        jax.ShapeDtypeStruct((nb, SSD_K - 1, SSD_CD), F32),
        jax.ShapeDtypeStruct((nb, SSD_H, SSD_P, SSD_N), F32),
        jax.ShapeDtypeStruct((nb, SC_K - 1, SC_D), F32),
    ]
    args = [u, u, u, dtf, u, u, u, cprev, h0, scprev,
            lw["cw"], lw["cb"], lw["dtb"], lw["alog"], lw["dsk"], lw["ng"], lw["scw"]]
    kern = functools.partial(_mixer_kernel, nskip)
    aliases = {}
    if keep is not None:
        aliases = {len(args): 0, len(args) + 1: 1}
        args += list(keep)
        in_specs += [pl.BlockSpec(memory_space=pl.ANY)] * 2
        kern = functools.partial(_mixer_alias_kernel, nskip)
    return pl.pallas_call(
        kern,
        grid=(nb, nch),
        in_specs=in_specs, out_specs=out_specs, out_shape=out_shape,
        scratch_shapes=[pltpu.VMEM((8 + t, SSD_CD), F32), pltpu.VMEM((8 + t, SC_D), F32),
                        pltpu.VMEM((SSD_H, SSD_P, SSD_N), F32), pltpu.VMEM((t, SSD_DI), F32)],
        input_output_aliases=aliases,
        compiler_params=_cp(2), name="mixer",
    )(*args)


def _mla_proj_kernel(blk_ref, tab_ref, qn_ref, kvn_ref, wa_ref, wb_ref, wk_ref, wv_ref,
                     q_ref, ckv_ref, kpe_ref, kcat_ref, v_ref):
    blk = blk_ref[...]
    cq = _rms(blk[:, :QL].astype(F32), qn_ref[...]).astype(BF16)
    ckv = _rms(blk[:, QL:QL + KVL].astype(F32), kvn_ref[...])
    ckv_ref[...] = ckv
    tab = tab_ref[...]
    tabr = pltpu.roll(tab, ROPE, 1)
    prod = blk[:, QL + KVL:QL + KVL + LANE].astype(F32) * tab
    ksum = prod + pltpu.roll(prod, ROPE, 1)
    lane = lax.broadcasted_iota(jnp.int32, ksum.shape, 1)
    kpe = jnp.where(lane < ROPE, ksum, 0.0)
    kpe_ref[...] = kpe
    qa = jnp.dot(cq, wa_ref[...], preferred_element_type=F32)
    qb = jnp.dot(cq, wb_ref[...], preferred_element_type=F32)
    ckvb = ckv.astype(BF16)
    kn = jnp.dot(ckvb, wk_ref[...], preferred_element_type=F32)
    v_ref[...] = jnp.dot(ckvb, wv_ref[...], preferred_element_type=F32).astype(BF16)
    kpeb = kpe.astype(BF16)
    for h in range(MLA_H):
        q_ref[:, h * HK:h * HK + NOPE] = qa[:, h * HK:h * HK + NOPE].astype(BF16)
        q_ref[:, h * HK + NOPE:(h + 1) * HK] = (
            qa[:, h * HK + NOPE:(h + 1) * HK] * tab + qb[:, h * LANE:(h + 1) * LANE] * tabr).astype(BF16)
        kcat_ref[:, h * HK:h * HK + NOPE] = kn[:, h * NOPE:(h + 1) * NOPE].astype(BF16)
        kcat_ref[:, h * HK + NOPE:(h + 1) * HK] = kpeb


def _mla_proj(u, tab, lw):
    r = u.shape[0]
    bm = _pick(r, (544, 512, 272, 256, 136, 128, 64))

    def full(shape):
        return pl.BlockSpec(shape, lambda i: (0,) * len(shape))

    def row(w):
        return pl.BlockSpec((bm, w), lambda i: (i, 0))

    return pl.pallas_call(
        _mla_proj_kernel,
        grid=(r // bm,),
        in_specs=[pl.BlockSpec((bm, MLA_W), lambda i: (i, C_MLA // MLA_W)), row(LANE),
                  full((1, QL)), full((1, KVL)), full((QL, MLA_H * HK)), full((QL, MLA_H * LANE)),
                  full((KVL, MLA_H * NOPE)), full((KVL, MLA_H * MLA_V))],
        out_specs=[row(MLA_H * HK), row(KVL), row(LANE), row(MLA_H * HK), row(MLA_H * MLA_V)],
        out_shape=[jax.ShapeDtypeStruct((r, MLA_H * HK), BF16), jax.ShapeDtypeStruct((r, KVL), F32),
                   jax.ShapeDtypeStruct((r, LANE), F32), jax.ShapeDtypeStruct((r, MLA_H * HK), BF16),
                   jax.ShapeDtypeStruct((r, MLA_H * MLA_V), BF16)],
        compiler_params=_cp(1), name="mla_proj",
    )(u, tab, lw["qn"], lw["kvn"], lw["wa"], lw["wb"], lw["wk"], lw["wv"])


def _attn0_kernel(q_ref, k_ref, v_ref, o_ref):
    col = lax.broadcasted_iota(jnp.int32, (CH, CH), 1)
    for h in range(MLA_H):
        s = lax.dot_general(q_ref[:, h * HK:(h + 1) * HK], k_ref[:, h * HK:(h + 1) * HK], NT,
                            preferred_element_type=F32) * SCALE
        s = jnp.where(col >= NSKIP, s, -jnp.inf)
        p = jnp.exp(s - jnp.max(s, axis=-1, keepdims=True))
        o = jnp.dot(p.astype(BF16), v_ref[:, h * MLA_V:(h + 1) * MLA_V], preferred_element_type=F32)
        o_ref[:, h * MLA_V:(h + 1) * MLA_V] = (o / jnp.sum(p, axis=-1, keepdims=True)).astype(BF16)


def _attn0(q, kcat, v):
    return pl.pallas_call(
        _attn0_kernel,
        grid=(1,),
        in_specs=[pl.BlockSpec((CH, MLA_H * HK), lambda i: (0, 0)),
                  pl.BlockSpec((CH, MLA_H * HK), lambda i: (0, 0)),
                  pl.BlockSpec((CH, MLA_H * MLA_V), lambda i: (0, 0))],
        out_specs=pl.BlockSpec((CH, MLA_H * MLA_V), lambda i: (0, 0)),
        out_shape=jax.ShapeDtypeStruct((CH, MLA_H * MLA_V), BF16),
        compiler_params=_cp(1), name="attn0",
    )(q, kcat, v)


def _flash_kernel(it_ref, jt_ref, q_ref, k_ref, v_ref, k0_ref, v0_ref, o_ref, m_s, l_s, acc_s):
    tq = q_ref.shape[0]
    tk = k_ref.shape[0]
    step = pl.program_id(0)
    i = it_ref[step]
    j = jt_ref[step]

    @pl.when(j == 0)
    def _init():
        col = lax.broadcasted_iota(jnp.int32, (tq, CH), 1)
        for h in range(MLA_H):
            s = lax.dot_general(q_ref[:, h * HK:(h + 1) * HK], k0_ref[:, h * HK:(h + 1) * HK], NT,
                                preferred_element_type=F32) * SCALE
            s = jnp.where(col >= NSKIP, s, -jnp.inf)
            m = jnp.max(s, axis=-1, keepdims=True)
            p = jnp.exp(s - m)
            m_s[h] = m
            l_s[h] = jnp.sum(p, axis=-1, keepdims=True)
            acc_s[:, h * MLA_V:(h + 1) * MLA_V] = jnp.dot(
                p.astype(BF16), v0_ref[:, h * MLA_V:(h + 1) * MLA_V], preferred_element_type=F32)

    def tile(masked):
        if masked:
            qc = lax.broadcasted_iota(jnp.int32, (tq, tk), 0) // CH
            kc = lax.broadcasted_iota(jnp.int32, (tq, tk), 1) // CH
            keep = kc <= qc
        for h in range(MLA_H):
            s = lax.dot_general(q_ref[:, h * HK:(h + 1) * HK], k_ref[:, h * HK:(h + 1) * HK], NT,
                                preferred_element_type=F32) * SCALE
            if masked:
                s = jnp.where(keep, s, -jnp.inf)
            m_prev = m_s[h]
            m_new = jnp.maximum(m_prev, jnp.max(s, axis=-1, keepdims=True))
            alpha = jnp.exp(m_prev - m_new)
            p = jnp.exp(s - m_new)
            l_s[h] = alpha * l_s[h] + jnp.sum(p, axis=-1, keepdims=True)
            acc_s[:, h * MLA_V:(h + 1) * MLA_V] = alpha * acc_s[:, h * MLA_V:(h + 1) * MLA_V] + jnp.dot(
                p.astype(BF16), v_ref[:, h * MLA_V:(h + 1) * MLA_V], preferred_element_type=F32)
            m_s[h] = m_new

    @pl.when(j < i)
    def _():
        tile(False)

    @pl.when(j == i)
    def _():
        tile(True)
        for h in range(MLA_H):
            o_ref[:, h * MLA_V:(h + 1) * MLA_V] = (
                acc_s[:, h * MLA_V:(h + 1) * MLA_V] / l_s[h]).astype(BF16)


def _flash(q, kcat, v, k0, v0):
    n = q.shape[0]
    tq = _pick(n, (FLASH_TQ, 256, 128, 64))
    nq = n // tq
    it = np.concatenate([np.full((i + 1,), i, np.int32) for i in range(nq)])
    jt = np.concatenate([np.arange(i + 1, dtype=np.int32) for i in range(nq)])
    grid_spec = pltpu.PrefetchScalarGridSpec(
        num_scalar_prefetch=2,
        grid=(int(it.shape[0]),),
        in_specs=[pl.BlockSpec((tq, MLA_H * HK), lambda s, it, jt: (it[s], 0)),
                  pl.BlockSpec((tq, MLA_H * HK), lambda s, it, jt: (jt[s], 0)),
                  pl.BlockSpec((tq, MLA_H * MLA_V), lambda s, it, jt: (jt[s], 0)),
                  pl.BlockSpec((CH, MLA_H * HK), lambda s, it, jt: (0, 0)),
                  pl.BlockSpec((CH, MLA_H * MLA_V), lambda s, it, jt: (0, 0))],
        out_specs=pl.BlockSpec((tq, MLA_H * MLA_V), lambda s, it, jt: (it[s], 0)),
        scratch_shapes=[pltpu.VMEM((MLA_H, tq, 1), F32), pltpu.VMEM((MLA_H, tq, 1), F32),
                        pltpu.VMEM((tq, MLA_H * MLA_V), F32)],
    )
    return pl.pallas_call(
        _flash_kernel,
        grid_spec=grid_spec,
        out_shape=jax.ShapeDtypeStruct((n, MLA_H * MLA_V), BF16),
        compiler_params=_cp(1), name="flash",
    )(jnp.asarray(it), jnp.asarray(jt), q, kcat, v, k0, v0)


def _cached_attn_kernel(q_ref, cnew_ref, pnew_ref, ckv_ref, cpe_ref, wkt_ref, wv_ref, o_ref, call, peall):
    past = ckv_ref.shape[2]
    call[0:past, :] = ckv_ref[0, 0].astype(BF16)
    call[past:past + CH, :] = cnew_ref[...].astype(BF16)
    peall[...] = jnp.zeros(peall.shape, BF16)
    peall[0:past, 0:ROPE] = cpe_ref[0, 0].astype(BF16)
    peall[past:past + CH, :] = pnew_ref[...].astype(BF16)
    qlat = []
    qpe = []
    for h in range(MLA_H):
        qn = q_ref[:, h * HK:h * HK + NOPE]
        qlat.append(jnp.dot(qn, wkt_ref[h], preferred_element_type=F32).astype(BF16))
        qpe.append(q_ref[:, h * HK + NOPE:(h + 1) * HK])
    qlat = jnp.concatenate(qlat, axis=0)
    qpe = jnp.concatenate(qpe, axis=0)
    s = (lax.dot_general(qlat, call[...], NT, preferred_element_type=F32)
         + lax.dot_general(qpe, peall[...], NT, preferred_element_type=F32)) * SCALE
    p = jnp.exp(s - jnp.max(s, axis=-1, keepdims=True))
    p = p / jnp.sum(p, axis=-1, keepdims=True)
    olat = jnp.dot(p.astype(BF16), call[...], preferred_element_type=F32).astype(BF16)
    for h in range(MLA_H):
        o_ref[:, h * MLA_V:(h + 1) * MLA_V] = jnp.dot(
            olat[h * CH:(h + 1) * CH, :], wv_ref[h], preferred_element_type=F32).astype(BF16)


def _cached_attn(q, ckv, kpe, cache_kv, cache_pe, layer, rb0, lw):
    nb, past = cache_kv.shape[1], cache_kv.shape[2]

    def rowmap(b):
        return (rb0 + b, 0)

    return pl.pallas_call(
        _cached_attn_kernel,
        grid=(nb,),
        in_specs=[pl.BlockSpec((CH, MLA_H * HK), rowmap),
                  pl.BlockSpec((CH, KVL), rowmap),
                  pl.BlockSpec((CH, LANE), rowmap),
                  pl.BlockSpec((1, 1, past, KVL), lambda b: (layer, b, 0, 0)),
                  pl.BlockSpec((1, 1, past, ROPE), lambda b: (layer, b, 0, 0)),
                  pl.BlockSpec((MLA_H, NOPE, KVL), lambda b: (0, 0, 0)),
                  pl.BlockSpec((MLA_H, KVL, MLA_V), lambda b: (0, 0, 0))],
        out_specs=pl.BlockSpec((CH, MLA_H * MLA_V), lambda b: (b, 0)),
        out_shape=jax.ShapeDtypeStruct((nb * CH, MLA_H * MLA_V), BF16),
        scratch_shapes=[pltpu.VMEM((past + CH, KVL), BF16), pltpu.VMEM((past + CH, LANE), BF16)],
        compiler_params=_cp(1), name="cached_attn",
    )(q, ckv, kpe, cache_kv, cache_pe, lw["wkt"], lw["wv3"])


def _act_kernel(nskip, g_ref, u_ref, prev_ref, w_ref, a_ref, new_ref, buf):
    t = CH
    c = pl.program_id(1)

    @pl.when(c == 0)
    def _():
        buf[0:8, :] = jnp.zeros((8, buf.shape[1]), F32)
        buf[8 - (FFN_K - 1):8, :] = prev_ref[0]

    rows = lax.broadcasted_iota(jnp.int32, (t, 1), 0) + c * t
    g = jnp.where(rows >= nskip, g_ref[...].astype(F32), 0.0)
    buf[8:8 + t, :] = g
    conv = w_ref[0:1, :] * buf[6:6 + t, :]
    for i in range(1, FFN_K):
        conv = conv + w_ref[i:i + 1, :] * buf[6 + i:6 + i + t, :]
    a_ref[...] = (_silu(conv) * u_ref[...].astype(F32)).astype(BF16)
    buf[0:8, :] = buf[t:t + 8, :]

    @pl.when(c == pl.num_programs(1) - 1)
    def _():
        new_ref[0] = buf[8 - (FFN_K - 1):8, :]


def _act(gate, up, rb0, nb, nch, nskip, prev, w, act_in=None):
    r, n = gate.shape
    t = CH

    def rowmap(b, c):
        return (rb0 + b * nch + c, 0)

    args = [gate, up, prev, w]
    in_specs = [pl.BlockSpec((t, n), rowmap), pl.BlockSpec((t, n), rowmap),
                pl.BlockSpec((1, FFN_K - 1, n), lambda b, c: (b, 0, 0)),
                pl.BlockSpec((FFN_K, n), lambda b, c: (0, 0))]
    kern = functools.partial(_act_kernel, nskip)
    aliases = {}
    if act_in is not None:
        args.append(act_in)
        in_specs.append(pl.BlockSpec(memory_space=pl.ANY))
        aliases = {4: 0}
        kern = functools.partial(_act_alias_kernel, nskip)
    return pl.pallas_call(
        kern,
        grid=(nb, nch),
        in_specs=in_specs,
        out_specs=[pl.BlockSpec((t, n), rowmap), pl.BlockSpec((1, FFN_K - 1, n), lambda b, c: (b, 0, 0))],
        out_shape=[jax.ShapeDtypeStruct((r, n), BF16), jax.ShapeDtypeStruct((nb, FFN_K - 1, n), F32)],
        scratch_shapes=[pltpu.VMEM((8 + t, n), F32)],
        input_output_aliases=aliases,
        compiler_params=_cp(2), name="act",
    )(*args)


def _act_alias_kernel(nskip, g_ref, u_ref, prev_ref, w_ref, keep_ref, a_ref, new_ref, buf):
    del keep_ref
    _act_kernel(nskip, g_ref, u_ref, prev_ref, w_ref, a_ref, new_ref, buf)


def _swap_half(w):
    half = w.shape[-1] // 2
    return jnp.concatenate([w[..., half:], w[..., :half]], axis=-1)


def _layer_weights(i, w_in, ssd_conv_w, ssd_conv_b, ssd_dt_bias, ssd_a_log, ssd_d, ssd_norm, sc_conv_w,
                   mla_q_norm, mla_w_uq, mla_kv_norm, mla_w_ukv, w_out, ffn_w_gate, ffn_w_up, ffn_conv_w,
                   ffn_w_down):
    w = w_in[i]
    o_dt = SSD_DI + SSD_CD
    o_sc = o_dt + SSD_H
    o_mla = o_sc + 3 * SC_D
    o_kr = o_mla + QL + KVL
    kr = w[:, o_kr:o_kr + ROPE]
    win = jnp.concatenate([w[:, :o_dt], w[:, o_sc:o_mla], w[:, o_mla:o_kr], kr, _swap_half(kr),
                           w[:, o_dt:o_sc], jnp.zeros((D_MODEL, LANE - SSD_H), F32)], axis=1).astype(BF16)
    uq = mla_w_uq[i].reshape(QL, MLA_H, NOPE + ROPE)
    pe = uq[..., NOPE:]
    zq = jnp.zeros((QL, MLA_H, HK - NOPE - ROPE), F32)
    wa = jnp.concatenate([uq[..., :NOPE], pe, zq], axis=-1).reshape(QL, MLA_H * HK).astype(BF16)
    wb = jnp.concatenate([_swap_half(pe), zq], axis=-1).reshape(QL, MLA_H * LANE).astype(BF16)
    ukv = mla_w_ukv[i].reshape(KVL, MLA_H, NOPE + MLA_V)
    padf = ((0, 0), (0, DFP - D_FF))
    return dict(
        win=win,
        cw=ssd_conv_w[i], cb=ssd_conv_b[i].reshape(1, SSD_CD),
        dtb=jnp.pad(ssd_dt_bias[i], (0, LANE - SSD_H)).reshape(1, LANE),
        alog=jnp.pad(ssd_a_log[i], (0, LANE - SSD_H)).reshape(1, LANE),
        dsk=jnp.repeat(ssd_d[i], SSD_P).reshape(1, SSD_DI),
        ng=ssd_norm[i].reshape(1, SSD_DI),
        scw=sc_conv_w[i],
        qn=mla_q_norm[i].reshape(1, QL), kvn=mla_kv_norm[i].reshape(1, KVL),
        wa=wa, wb=wb,
        wk=ukv[..., :NOPE].reshape(KVL, MLA_H * NOPE).astype(BF16),
        wv=ukv[..., NOPE:].reshape(KVL, MLA_H * MLA_V).astype(BF16),
        wkt=jnp.transpose(ukv[..., :NOPE], (1, 2, 0)).astype(BF16),
        wv3=jnp.transpose(ukv[..., NOPE:], (1, 0, 2)).astype(BF16),
        wout=w_out[i].astype(BF16),
        wg=jnp.pad(ffn_w_gate[i], padf).astype(BF16),
        wu=jnp.pad(ffn_w_up[i], padf).astype(BF16),
        fcw=jnp.pad(ffn_conv_w[i], padf),
        wd=jnp.pad(ffn_w_down[i], ((0, DFP - D_FF), (0, 0))).astype(BF16),
    )


def _rope_table(lp, ns, past):
    half = ROPE // 2
    pos = jnp.concatenate([jnp.maximum(jnp.arange(lp, dtype=jnp.int32) - NSKIP, 0),
                           N_META + past + jnp.arange(ns, dtype=jnp.int32) % CH])
    inv = THETA ** (-jnp.arange(half, dtype=F32) / half)
    ang = pos.astype(F32)[:, None] * inv[None, :]
    cos, sin = jnp.cos(ang), jnp.sin(ang)
    return jnp.concatenate([cos, cos, -sin, sin], axis=1)


def kernel(x_prompt, x_sample, cache_kv_latent, cache_k_rope, state_ssm, state_ssd_conv, state_sconv, state_ffn_conv, meta_tokens, norm_mix_pre, norm_mix_post, norm_ffn_pre, norm_ffn_post, w_in, ssd_conv_w, ssd_conv_b, ssd_dt_bias, ssd_a_log, ssd_d, ssd_norm, sc_conv_w, mla_q_norm, mla_w_uq, mla_kv_norm, mla_w_ukv, w_out, ffn_w_gate, ffn_w_up, ffn_conv_w, ffn_w_down):
    bp, seq, d = x_prompt.shape
    nb, ls, _ = x_sample.shape
    depth, _, past, _ = cache_kv_latent.shape
    assert bp == 1 and ls == CH and seq % CH == 0 and d == D_MODEL
    lp = CH + seq
    ns = nb * ls
    npc = lp // CH

    x = jnp.concatenate([jnp.zeros((NSKIP, d), F32), meta_tokens.astype(F32), x_prompt[0],
                         x_sample.reshape(ns, d)], axis=0)
    tab = _rope_table(lp, ns, past)
    zero_c = jnp.zeros((1, SSD_K - 1, SSD_CD), F32)
    zero_h = jnp.zeros((1, SSD_H, SSD_P, SSD_N), F32)
    zero_s = jnp.zeros((1, SC_K - 1, SC_D), F32)
    zero_f = jnp.zeros((1, FFN_K - 1, DFP), F32)
    padf = ((0, 0), (0, 0), (0, DFP - D_FF))

    h = _norm_cast(x, norm_mix_pre[0])
    outs_p, outs_s = [], []
    for i in range(depth):
        lw = _layer_weights(i, w_in, ssd_conv_w, ssd_conv_b, ssd_dt_bias, ssd_a_log, ssd_d, ssd_norm,
                            sc_conv_w, mla_q_norm, mla_w_uq, mla_kv_norm, mla_w_ukv, w_out, ffn_w_gate,
                            ffn_w_up, ffn_conv_w, ffn_w_down)
        u, dtf = _inproj(h, lw["win"])

        yp, yscp, cnew_p, hnew_p, scnew_p = _mixer(u, dtf, 0, 1, npc, NSKIP, zero_c, zero_h, zero_s, lw)
        y_ssd, y_sc, cnew_s, hnew_s, scnew_s = _mixer(u, dtf, npc, nb, 1, 0, state_ssd_conv[i], state_ssm[i],
                                                      state_sconv[i], lw, keep=(yp, yscp))

        q, ckv, kpe, kcat, v = _mla_proj(u, tab, lw)
        o0 = _attn0(q, kcat, v)
        o_real = _flash(q[CH:lp], kcat[CH:lp], v[CH:lp], kcat[:CH], v[:CH])
        o_s = _cached_attn(q, ckv, kpe, cache_kv_latent, cache_k_rope, i, npc, lw)
        y_mla = jnp.concatenate([o0, o_real, o_s], axis=0)

        mix = _outproj(y_ssd, y_sc, y_mla, lw["wout"])
        x, h = _resid_norm(x, mix, norm_mix_post[i], norm_ffn_pre[i])

        gate, up = _gateup(h, lw["wg"], lw["wu"])
        act, fnew_p = _act(gate, up, 0, 1, npc, NSKIP, zero_f, lw["fcw"])
        act, fnew_s = _act(gate, up, npc, nb, 1, 0, jnp.pad(state_ffn_conv[i], padf), lw["fcw"], act_in=act)
        f = _down(act, lw["wd"])
        x, h = _resid_norm(x, f, norm_ffn_post[i], norm_mix_pre[i + 1] if i + 1 < depth else None)

        outs_p.append((ckv[NSKIP:lp][None], kpe[NSKIP:lp, :ROPE][None], hnew_p, cnew_p, scnew_p,
                       fnew_p[:, :, :D_FF]))
        outs_s.append((ckv[lp:].reshape(nb, ls, KVL), kpe[lp:, :ROPE].reshape(nb, ls, ROPE), hnew_s, cnew_s,
                       scnew_s, fnew_s[:, :, :D_FF]))

    def stack(outs, j):
        return jnp.stack([o[j] for o in outs], axis=0)

    y_prompt = x[CH:lp][None]
    y_sample = x[lp:].reshape(nb, ls, d)
    return (y_prompt, y_sample,
            stack(outs_p, 0), stack(outs_p, 1), stack(outs_p, 2), stack(outs_p, 3), stack(outs_p, 4), stack(outs_p, 5),
            stack(outs_s, 0), stack(outs_s, 1), stack(outs_s, 2), stack(outs_s, 3), stack(outs_s, 4), stack(outs_s, 5))
```

```python
import functools

import jax
import jax.numpy as jnp
import numpy as np
from jax import lax
from jax.experimental import pallas as pl
from jax.experimental.pallas import tpu as pltpu

F32 = jnp.float32
BF16 = jnp.bfloat16

D_MODEL = 4096
N_META = 16
CH = 64
NSKIP = CH - N_META
EPS = 1e-6
SSD_P = 64
SSD_DI = 2048
SSD_H = 32
SSD_G = 8
SSD_N = 128
SSD_K = 4
SSD_CD = 4096
SC_D = 1024
SC_K = 3
MLA_H = 8
NOPE = 128
ROPE = 64
MLA_V = 128
QL = 768
KVL = 512
HK = 256
SCALE = (NOPE + ROPE) ** -0.5
QSCALE = SCALE * 1.4426950408889634
THETA = 10000.0
D_FF = 11008
DFP = 11264
FFN_K = 3
LANE = 128

C_Z, C_X, C_BC, C_SCB, C_SCC, C_SCH, C_MLA, NIN = 0, 2048, 4096, 6144, 7168, 8192, 9216, 10752
MLA_W = 1536

FLASH_TQ = 512
VMEM_LIMIT = 56 * 1024 * 1024
HI = lax.Precision.HIGHEST
NT = (((1,), (1,)), ((), ()))
TN = (((0,), (0,)), ((), ()))


def _cp(n):
    return pltpu.CompilerParams(dimension_semantics=("arbitrary",) * n, vmem_limit_bytes=VMEM_LIMIT)


def _pick(n, cands):
    for c in cands:
        if n % c == 0:
            return c
    raise ValueError(f"no tile for {n}")


MM_ROWS = (1088, 544, 512, 272, 256, 136, 128, 64)
EW_ROWS = (272, 256, 136, 128, 64)


def _rms(x, g):
    return x * lax.rsqrt(jnp.mean(x * x, axis=-1, keepdims=True) + EPS) * g


def _silu(x):
    return x * jax.nn.sigmoid(x)


def _norm_cast_kernel(x_ref, g_ref, h_ref):
    h_ref[...] = _rms(x_ref[...], g_ref[...]).astype(BF16)


def _norm_cast(x, g):
    r, d = x.shape
    bm = _pick(r, EW_ROWS)
    return pl.pallas_call(
        _norm_cast_kernel,
        grid=(r // bm,),
        in_specs=[pl.BlockSpec((bm, d), lambda i: (i, 0)), pl.BlockSpec((1, d), lambda i: (0, 0))],
        out_specs=pl.BlockSpec((bm, d), lambda i: (i, 0)),
        out_shape=jax.ShapeDtypeStruct((r, d), BF16),
        compiler_params=_cp(1), name="norm_cast",
    )(x, g.reshape(1, d))


def _resid_norm_kernel(x_ref, m_ref, gp_ref, gn_ref, x2_ref, h_ref):
    x2 = x_ref[...] + _rms(m_ref[...], gp_ref[...])
    x2_ref[...] = x2
    h_ref[...] = _rms(x2, gn_ref[...]).astype(BF16)


def _resid_kernel(x_ref, m_ref, gp_ref, x2_ref):
    x2_ref[...] = x_ref[...] + _rms(m_ref[...], gp_ref[...])


def _resid_norm(x, m, g_post, g_next):
    r, d = x.shape
    bm = _pick(r, EW_ROWS)
    row = pl.BlockSpec((bm, d), lambda i: (i, 0))
    vec = pl.BlockSpec((1, d), lambda i: (0, 0))
    if g_next is None:
        return pl.pallas_call(
            _resid_kernel, grid=(r // bm,), in_specs=[row, row, vec], out_specs=row,
            out_shape=jax.ShapeDtypeStruct((r, d), F32), compiler_params=_cp(1), name="resid",
        )(x, m, g_post.reshape(1, d)), None
    return pl.pallas_call(
        _resid_norm_kernel, grid=(r // bm,), in_specs=[row, row, vec, vec], out_specs=[row, row],
        out_shape=[jax.ShapeDtypeStruct((r, d), F32), jax.ShapeDtypeStruct((r, d), BF16)],
        compiler_params=_cp(1), name="resid_norm",
    )(x, m, g_post.reshape(1, d), g_next.reshape(1, d))


def _inproj_kernel(x_ref, w_ref, u_ref, dt_ref):
    acc = jnp.dot(x_ref[...], w_ref[...], preferred_element_type=F32)
    u_ref[...] = acc.astype(BF16)

    @pl.when(pl.program_id(1) == pl.num_programs(1) - 1)
    def _():
        dt_ref[...] = acc[:, acc.shape[1] - LANE:]


def _inproj(h, w):
    r, k = h.shape
    n = w.shape[1]
    bm = _pick(r, MM_ROWS)
    bn = 768
    return pl.pallas_call(
        _inproj_kernel,
        grid=(r // bm, n // bn),
        in_specs=[pl.BlockSpec((bm, k), lambda i, j: (i, 0)), pl.BlockSpec((k, bn), lambda i, j: (0, j))],
        out_specs=[pl.BlockSpec((bm, bn), lambda i, j: (i, j)), pl.BlockSpec((bm, LANE), lambda i, j: (i, 0))],
        out_shape=[jax.ShapeDtypeStruct((r, n), BF16), jax.ShapeDtypeStruct((r, LANE), F32)],
        compiler_params=_cp(2), name="inproj",
    )(h, w)


def _outproj_kernel(a_ref, b_ref, c_ref, wa_ref, wb_ref, wc_ref, o_ref):
    acc = jnp.dot(a_ref[...], wa_ref[...], preferred_element_type=F32)
    acc = acc + jnp.dot(b_ref[...], wb_ref[...], preferred_element_type=F32)
    acc = acc + jnp.dot(c_ref[...], wc_ref[...], preferred_element_type=F32)
    o_ref[...] = acc


def _outproj(y_ssd, y_sc, y_mla, w):
    r = y_ssd.shape[0]
    n = w.shape[1]
    bm = _pick(r, MM_ROWS)
    bn = 1024
    return pl.pallas_call(
        _outproj_kernel,
        grid=(r // bm, n // bn),
        in_specs=[pl.BlockSpec((bm, SSD_DI), lambda i, j: (i, 0)),
                  pl.BlockSpec((bm, SC_D), lambda i, j: (i, 0)),
                  pl.BlockSpec((bm, SC_D), lambda i, j: (i, 0)),
                  pl.BlockSpec((SSD_DI, bn), lambda i, j: (0, j)),
                  pl.BlockSpec((SC_D, bn), lambda i, j: (2, j)),
                  pl.BlockSpec((SC_D, bn), lambda i, j: (3, j))],
        out_specs=pl.BlockSpec((bm, bn), lambda i, j: (i, j)),
        out_shape=jax.ShapeDtypeStruct((r, n), F32),
        compiler_params=_cp(2), name="outproj",
    )(y_ssd, y_sc, y_mla, w, w, w)


def _gateup_kernel(x_ref, wg_ref, wu_ref, g_ref, u_ref):
    x = x_ref[...]
    g_ref[...] = jnp.dot(x, wg_ref[...], preferred_element_type=F32).astype(BF16)
    u_ref[...] = jnp.dot(x, wu_ref[...], preferred_element_type=F32).astype(BF16)


def _gateup(h, wg, wu):
    r, k = h.shape
    n = wg.shape[1]
    bm = _pick(r, MM_ROWS)
    bn = 512
    wspec = pl.BlockSpec((k, bn), lambda i, j: (0, j))
    ospec = pl.BlockSpec((bm, bn), lambda i, j: (i, j))
    return pl.pallas_call(
        _gateup_kernel,
        grid=(r // bm, n // bn),
        in_specs=[pl.BlockSpec((bm, k), lambda i, j: (i, 0)), wspec, wspec],
        out_specs=[ospec, ospec],
        out_shape=[jax.ShapeDtypeStruct((r, n), BF16)] * 2,
        compiler_params=_cp(2), name="gateup",
    )(h, wg, wu)


def _down_kernel(x_ref, w_ref, o_ref):
    p = jnp.dot(x_ref[...], w_ref[...], preferred_element_type=F32)

    @pl.when(pl.program_id(2) == 0)
    def _():
        o_ref[...] = p

    @pl.when(pl.program_id(2) > 0)
    def _():
        o_ref[...] += p


def _down(a, w):
    r, k = a.shape
    n = w.shape[1]
    bm = _pick(r, MM_ROWS)
    bn = 1024
    bk = k // 4
    return pl.pallas_call(
        _down_kernel,
        grid=(r // bm, n // bn, k // bk),
        in_specs=[pl.BlockSpec((bm, bk), lambda i, j, kk: (i, kk)), pl.BlockSpec((bk, bn), lambda i, j, kk: (kk, j))],
        out_specs=pl.BlockSpec((bm, bn), lambda i, j, kk: (i, j)),
        out_shape=jax.ShapeDtypeStruct((r, n), F32),
        compiler_params=_cp(3), name="down",
    )(a, w)


def _mixer_kernel(nskip, z_ref, x_ref, bc_ref, dt_ref, scb_ref, scc_ref, sch_ref,
                  cprev_ref, h0_ref, scprev_ref,
                  cw_ref, cb_ref, dtb_ref, alog_ref, dsk_ref, ng_ref, scw_ref,
                  y_ref, ysc_ref, cnew_ref, hnew_ref, scnew_ref,
                  buf, sbuf, hst):
    t = CH
    c = pl.program_id(1)
    last = pl.num_programs(1) - 1

    @pl.when(c == 0)
    def _init():
        buf[0:8, :] = jnp.zeros((8, SSD_CD), F32)
        buf[8 - (SSD_K - 1):8, :] = cprev_ref[0]
        sbuf[0:8, :] = jnp.zeros((8, SC_D), F32)
        sbuf[8 - (SC_K - 1):8, :] = scprev_ref[0]
        hst[...] = h0_ref[0]

    rows = lax.broadcasted_iota(jnp.int32, (t, 1), 0) + c * t
    valid = rows >= nskip

    xin = jnp.concatenate([x_ref[...], bc_ref[...]], axis=1).astype(F32)
    xin = jnp.where(valid, xin, 0.0)
    buf[8:8 + t, :] = xin
    acc = cb_ref[...] + cw_ref[0:1, :] * buf[5:5 + t, :]
    for i in range(1, SSD_K):
        acc = acc + cw_ref[i:i + 1, :] * buf[5 + i:5 + i + t, :]
    xbc = _silu(acc)
    buf[0:8, :] = buf[t:t + 8, :]

    @pl.when(c == last)
    def _():
        cnew_ref[0] = buf[8 - (SSD_K - 1):8, :]

    lane = lax.broadcasted_iota(jnp.int32, (t, LANE), 1)
    v = dt_ref[...] + dtb_ref[...]
    dt = jnp.maximum(v, 0.0) + jnp.log1p(jnp.exp(-jnp.abs(v)))
    dt = jnp.where(valid & (lane < SSD_H), dt, 0.0)
    adt = dt * (-jnp.exp(alog_ref[...]))
    ri = lax.broadcasted_iota(jnp.int32, (t, t), 0)
    ci = lax.broadcasted_iota(jnp.int32, (t, t), 1)
    tri = ri >= ci
    acs = jnp.dot(tri.astype(F32), adt, precision=HI, preferred_element_type=F32)

    hpg = SSD_H // SSD_G
    gw = hpg * SSD_P
    lane_g = lax.broadcasted_iota(jnp.int32, (t, gw), 1)
    row_g = lax.broadcasted_iota(jnp.int32, (t, gw), 0)
    seg = lane_g // SSD_P
    eye_g = row_g == lane_g % SSD_P
    tri_g = row_g >= lane_g % SSD_P
    blockdiag = (lax.broadcasted_iota(jnp.int32, (hpg * t, gw), 0) // t
                 == lax.broadcasted_iota(jnp.int32, (hpg * t, gw), 1) // SSD_P)

    def per_head_lanes(mat, g):
        out = jnp.broadcast_to(mat[:, g * hpg + hpg - 1:g * hpg + hpg], (t, gw))
        for r in range(hpg - 2, -1, -1):
            out = jnp.where(seg == r, jnp.broadcast_to(mat[:, g * hpg + r:g * hpg + r + 1], (t, gw)), out)
        return out

    for g in range(SSD_G):
        cols = slice(g * gw, (g + 1) * gw)
        bg = xbc[:, SSD_DI + g * SSD_N:SSD_DI + (g + 1) * SSD_N].astype(BF16)
        cg = xbc[:, SSD_DI + SSD_G * SSD_N + g * SSD_N:SSD_DI + SSD_G * SSD_N + (g + 1) * SSD_N].astype(BF16)
        acol = per_head_lanes(acs, g)
        dtx = per_head_lanes(dt, g)
        arow = jnp.sum(jnp.where(eye_g, acol, 0.0), axis=0, keepdims=True)
        decay = jnp.exp(jnp.where(tri_g, acol - arow, -jnp.inf))
        cb = lax.dot_general(cg, jnp.concatenate([bg] * hpg, axis=0), NT, preferred_element_type=F32)
        xg = xbc[:, cols]
        xdt = xg * dtx
        xdtb = xdt.astype(BF16)
        rhs = jnp.where(blockdiag, jnp.concatenate([xdtb] * hpg, axis=0), jnp.zeros((), BF16))
        ydiag = jnp.dot((cb * decay).astype(BF16), rhs, preferred_element_type=F32)
        hg = hst[:, cols]
        yoff = jnp.dot(cg, hg.astype(BF16), preferred_element_type=F32) * jnp.exp(acol)
        alast = acol[t - 1:t, :]
        snew = lax.dot_general(bg, (xdt * jnp.exp(alast - acol)).astype(BF16), TN, preferred_element_type=F32)
        hst[:, cols] = jnp.exp(alast) * hg + snew
        y = ydiag + yoff + dsk_ref[:, cols] * xg
        y = y * _silu(z_ref[:, cols].astype(F32))
        y_ref[:, cols] = _rms(y, ng_ref[:, cols]).astype(BF16)

    @pl.when(c == last)
    def _():
        hnew_ref[0] = hst[...]

    p = scc_ref[...].astype(F32) * sch_ref[...].astype(F32)
    p = jnp.where(valid, p, 0.0)
    sbuf[8:8 + t, :] = p
    conv = scw_ref[0:1, :] * sbuf[6:6 + t, :]
    for i in range(1, SC_K):
        conv = conv + scw_ref[i:i + 1, :] * sbuf[6 + i:6 + i + t, :]
    ysc_ref[...] = (scb_ref[...].astype(F32) * conv).astype(BF16)
    sbuf[0:8, :] = sbuf[t:t + 8, :]

    @pl.when(c == last)
    def _():
        scnew_ref[0] = sbuf[8 - (SC_K - 1):8, :]


def _mixer_alias_kernel(nskip, *refs):
    n_in = 17
    _mixer_kernel(nskip, *refs[:n_in], *refs[n_in + 2:])


def _mixer(u, dtf, rb0, nb, nch, nskip, cprev, h0, scprev, lw, keep=None):
    r = u.shape[0]
    t = CH

    def rowmap(cb):
        return lambda b, c: (rb0 + b * nch + c, cb)

    def stmap(b, c):
        return (b, 0, 0)

    def full(shape):
        return pl.BlockSpec(shape, lambda b, c: (0,) * len(shape))

    in_specs = [
        pl.BlockSpec((t, SSD_DI), rowmap(C_Z // SSD_DI)),
        pl.BlockSpec((t, SSD_DI), rowmap(C_X // SSD_DI)),
        pl.BlockSpec((t, SSD_DI), rowmap(C_BC // SSD_DI)),
        pl.BlockSpec((t, LANE), rowmap(0)),
        pl.BlockSpec((t, SC_D), rowmap(C_SCB // SC_D)),
        pl.BlockSpec((t, SC_D), rowmap(C_SCC // SC_D)),
        pl.BlockSpec((t, SC_D), rowmap(C_SCH // SC_D)),
        pl.BlockSpec((1, SSD_K - 1, SSD_CD), stmap),
        pl.BlockSpec((1, SSD_N, SSD_DI), stmap),
        pl.BlockSpec((1, SC_K - 1, SC_D), stmap),
        full((SSD_K, SSD_CD)), full((1, SSD_CD)), full((1, LANE)), full((1, LANE)),
        full((1, SSD_DI)), full((1, SSD_DI)), full((SC_K, SC_D)),
    ]
    out_specs = [
        pl.BlockSpec((t, SSD_DI), rowmap(0)),
        pl.BlockSpec((t, SC_D), rowmap(0)),
        pl.BlockSpec((1, SSD_K - 1, SSD_CD), stmap),
        pl.BlockSpec((1, SSD_N, SSD_DI), stmap),
        pl.BlockSpec((1, SC_K - 1, SC_D), stmap),
    ]
    out_shape = [
        jax.ShapeDtypeStruct((r, SSD_DI), BF16),
        jax.ShapeDtypeStruct((r, SC_D), BF16),
        jax.ShapeDtypeStruct((nb, SSD_K - 1, SSD_CD), F32),
        jax.ShapeDtypeStruct((nb, SSD_N, SSD_DI), F32),
        jax.ShapeDtypeStruct((nb, SC_K - 1, SC_D), F32),
    ]
    args = [u, u, u, dtf, u, u, u, cprev, h0, scprev,
            lw["cw"], lw["cb"], lw["dtb"], lw["alog"], lw["dsk"], lw["ng"], lw["scw"]]
    kern = functools.partial(_mixer_kernel, nskip)
    aliases = {}
    if keep is not None:
        aliases = {len(args): 0, len(args) + 1: 1}
        args += list(keep)
        in_specs += [pl.BlockSpec(memory_space=pl.ANY)] * 2
        kern = functools.partial(_mixer_alias_kernel, nskip)
    return pl.pallas_call(
        kern,
        grid=(nb, nch),
        in_specs=in_specs, out_specs=out_specs, out_shape=out_shape,
        scratch_shapes=[pltpu.VMEM((8 + t, SSD_CD), F32), pltpu.VMEM((8 + t, SC_D), F32),
                        pltpu.VMEM((SSD_N, SSD_DI), F32)],
        input_output_aliases=aliases,
        compiler_params=_cp(2), name="mixer",
    )(*args)


def _mla_proj_kernel(blk_ref, tab_ref, qn_ref, kvn_ref, wa_ref, wb_ref, wk_ref, wv_ref,
                     q_ref, ckv_ref, kpe_ref, kcat_ref, v_ref):
    blk = blk_ref[...]
    cq = _rms(blk[:, :QL].astype(F32), qn_ref[...]).astype(BF16)
    ckv = _rms(blk[:, QL:QL + KVL].astype(F32), kvn_ref[...])
    ckv_ref[...] = ckv
    tab = tab_ref[...]
    tabr = pltpu.roll(tab, ROPE, 1)
    prod = blk[:, QL + KVL:QL + KVL + LANE].astype(F32) * tab
    ksum = prod + pltpu.roll(prod, ROPE, 1)
    lane = lax.broadcasted_iota(jnp.int32, ksum.shape, 1)
    kpe = jnp.where(lane < ROPE, ksum, 0.0)
    kpe_ref[...] = kpe
    qa = jnp.dot(cq, wa_ref[...], preferred_element_type=F32)
    qb = jnp.dot(cq, wb_ref[...], preferred_element_type=F32)
    ckvb = ckv.astype(BF16)
    kn = jnp.dot(ckvb, wk_ref[...], preferred_element_type=F32)
    v_ref[...] = jnp.dot(ckvb, wv_ref[...], preferred_element_type=F32).astype(BF16)
    kpeb = kpe.astype(BF16)
    tab_q = tab * QSCALE
    tabr_q = tabr * QSCALE
    for h in range(MLA_H):
        q_ref[:, h * HK:h * HK + NOPE] = (qa[:, h * HK:h * HK + NOPE] * QSCALE).astype(BF16)
        q_ref[:, h * HK + NOPE:(h + 1) * HK] = (
            qa[:, h * HK + NOPE:(h + 1) * HK] * tab_q + qb[:, h * LANE:(h + 1) * LANE] * tabr_q).astype(BF16)
        kcat_ref[:, h * HK:h * HK + NOPE] = kn[:, h * NOPE:(h + 1) * NOPE].astype(BF16)
        kcat_ref[:, h * HK + NOPE:(h + 1) * HK] = kpeb


def _mla_proj(u, tab, lw):
    r = u.shape[0]
    bm = _pick(r, (544, 512, 272, 256, 136, 128, 64))

    def full(shape):
        return pl.BlockSpec(shape, lambda i: (0,) * len(shape))

    def row(w):
        return pl.BlockSpec((bm, w), lambda i: (i, 0))

    return pl.pallas_call(
        _mla_proj_kernel,
        grid=(r // bm,),
        in_specs=[pl.BlockSpec((bm, MLA_W), lambda i: (i, C_MLA // MLA_W)), row(LANE),
                  full((1, QL)), full((1, KVL)), full((QL, MLA_H * HK)), full((QL, MLA_H * LANE)),
                  full((KVL, MLA_H * NOPE)), full((KVL, MLA_H * MLA_V))],
        out_specs=[row(MLA_H * HK), row(KVL), row(LANE), row(MLA_H * HK), row(MLA_H * MLA_V)],
        out_shape=[jax.ShapeDtypeStruct((r, MLA_H * HK), BF16), jax.ShapeDtypeStruct((r, KVL), F32),
                   jax.ShapeDtypeStruct((r, LANE), F32), jax.ShapeDtypeStruct((r, MLA_H * HK), BF16),
                   jax.ShapeDtypeStruct((r, MLA_H * MLA_V), BF16)],
        compiler_params=_cp(1), name="mla_proj",
    )(u, tab, lw["qn"], lw["kvn"], lw["wa"], lw["wb"], lw["wk"], lw["wv"])


def _attn0_kernel(q_ref, k_ref, v_ref, o_ref):
    col = lax.broadcasted_iota(jnp.int32, (CH, CH), 1)
    for h in range(MLA_H):
        s = lax.dot_general(q_ref[:, h * HK:(h + 1) * HK], k_ref[:, h * HK:(h + 1) * HK], NT,
                            preferred_element_type=F32)
        s = jnp.where(col >= NSKIP, s, -jnp.inf)
        p = jnp.exp2(s - jnp.max(s, axis=-1, keepdims=True))
        o = jnp.dot(p.astype(BF16), v_ref[:, h * MLA_V:(h + 1) * MLA_V], preferred_element_type=F32)
        o_ref[:, h * MLA_V:(h + 1) * MLA_V] = (o / jnp.sum(p, axis=-1, keepdims=True)).astype(BF16)


def _attn0(q, kcat, v):
    return pl.pallas_call(
        _attn0_kernel,
        grid=(1,),
        in_specs=[pl.BlockSpec((CH, MLA_H * HK), lambda i: (0, 0)),
                  pl.BlockSpec((CH, MLA_H * HK), lambda i: (0, 0)),
                  pl.BlockSpec((CH, MLA_H * MLA_V), lambda i: (0, 0))],
        out_specs=pl.BlockSpec((CH, MLA_H * MLA_V), lambda i: (0, 0)),
        out_shape=jax.ShapeDtypeStruct((CH, MLA_H * MLA_V), BF16),
        compiler_params=_cp(1), name="attn0",
    )(q, kcat, v)


def _flash_kernel(it_ref, jt_ref, qt_ref, k_ref, vt_ref, k0_ref, v0t_ref, ot_ref, m_s, l_s, acc_s):
    tq = qt_ref.shape[1]
    tk = k_ref.shape[0]
    step = pl.program_id(0)
    i = it_ref[step]
    j = jt_ref[step]

    def update(h, st, vt, first):
        m_blk = jnp.max(st, axis=0, keepdims=True)
        if first:
            m_new = m_blk
        else:
            m_prev = m_s[h]
            m_new = jnp.maximum(m_prev, m_blk)
            alpha = jnp.exp2(m_prev - m_new)
        p = jnp.exp2(st - m_new)
        l_blk = jnp.sum(p, axis=0, keepdims=True)
        pv = jnp.dot(vt, p.astype(BF16), preferred_element_type=F32)
        rows = slice(h * MLA_V, (h + 1) * MLA_V)
        if first:
            l_s[h] = l_blk
            acc_s[rows, :] = pv
        else:
            l_s[h] = alpha * l_s[h] + l_blk
            acc_s[rows, :] = alpha * acc_s[rows, :] + pv
        m_s[h] = m_new

    @pl.when(j == 0)
    def _init():
        row = lax.broadcasted_iota(jnp.int32, (CH, tq), 0)
        for h in range(MLA_H):
            st = jnp.dot(k0_ref[:, h * HK:(h + 1) * HK], qt_ref[h * HK:(h + 1) * HK, :],
                         preferred_element_type=F32)
            st = jnp.where(row >= NSKIP, st, -jnp.inf)
            update(h, st, v0t_ref[h * MLA_V:(h + 1) * MLA_V, :], True)

    def tile(masked):
        if masked:
            kc = lax.broadcasted_iota(jnp.int32, (tk, tq), 0) // CH
            qc = lax.broadcasted_iota(jnp.int32, (tk, tq), 1) // CH
            keep = kc <= qc

        def scores(h):
            return jnp.dot(k_ref[:, h * HK:(h + 1) * HK], qt_ref[h * HK:(h + 1) * HK, :],
                           preferred_element_type=F32)

        s_next = scores(0)
        for h in range(MLA_H):
            st = s_next
            if h + 1 < MLA_H:
                s_next = scores(h + 1)
            if masked:
                st = jnp.where(keep, st, -jnp.inf)
            update(h, st, vt_ref[h * MLA_V:(h + 1) * MLA_V, :], False)

    @pl.when(j < i)
    def _():
        tile(False)

    @pl.when(j == i)
    def _():
        tile(True)
        for h in range(MLA_H):
            rows = slice(h * MLA_V, (h + 1) * MLA_V)
            ot_ref[rows, :] = (acc_s[rows, :] / l_s[h]).astype(BF16)


def _flash(qt, kcat, vt, k0, v0t):
    n = kcat.shape[0]
    tq = _pick(n, (FLASH_TQ, 256, 128))
    nq = n // tq
    it = np.concatenate([np.full((i + 1,), i, np.int32) for i in range(nq)])
    jt = np.concatenate([np.arange(i + 1, dtype=np.int32) for i in range(nq)])
    grid_spec = pltpu.PrefetchScalarGridSpec(
        num_scalar_prefetch=2,
        grid=(int(it.shape[0]),),
        in_specs=[pl.BlockSpec((MLA_H * HK, tq), lambda s, it, jt: (0, it[s])),
                  pl.BlockSpec((tq, MLA_H * HK), lambda s, it, jt: (jt[s], 0)),
                  pl.BlockSpec((MLA_H * MLA_V, tq), lambda s, it, jt: (0, jt[s])),
                  pl.BlockSpec((CH, MLA_H * HK), lambda s, it, jt: (0, 0)),
                  pl.BlockSpec((MLA_H * MLA_V, CH), lambda s, it, jt: (0, 0))],
        out_specs=pl.BlockSpec((MLA_H * MLA_V, tq), lambda s, it, jt: (0, it[s])),
        scratch_shapes=[pltpu.VMEM((MLA_H, 1, tq), F32), pltpu.VMEM((MLA_H, 1, tq), F32),
                        pltpu.VMEM((MLA_H * MLA_V, tq), F32)],
    )
    return pl.pallas_call(
        _flash_kernel,
        grid_spec=grid_spec,
        out_shape=jax.ShapeDtypeStruct((MLA_H * MLA_V, n), BF16),
        compiler_params=_cp(1), name="flash",
    )(jnp.asarray(it), jnp.asarray(jt), qt, kcat, vt, k0, v0t)


def _cached_attn_kernel(q_ref, cnew_ref, pnew_ref, ckv_ref, cpe_ref, wkt_ref, wv_ref, o_ref, call, peall):
    past = ckv_ref.shape[2]
    call[0:past, :] = ckv_ref[0, 0].astype(BF16)
    call[past:past + CH, :] = cnew_ref[...].astype(BF16)
    peall[...] = jnp.zeros(peall.shape, BF16)
    peall[0:past, 0:ROPE] = cpe_ref[0, 0].astype(BF16)
    peall[past:past + CH, :] = pnew_ref[...].astype(BF16)
    qlat = []
    qpe = []
    for h in range(MLA_H):
        qn = q_ref[:, h * HK:h * HK + NOPE]
        qlat.append(jnp.dot(qn, wkt_ref[h], preferred_element_type=F32).astype(BF16))
        qpe.append(q_ref[:, h * HK + NOPE:(h + 1) * HK])
    qlat = jnp.concatenate(qlat, axis=0)
    qpe = jnp.concatenate(qpe, axis=0)
    s = (lax.dot_general(qlat, call[...], NT, preferred_element_type=F32)
         + lax.dot_general(qpe, peall[...], NT, preferred_element_type=F32))
    p = jnp.exp2(s - jnp.max(s, axis=-1, keepdims=True))
    p = p / jnp.sum(p, axis=-1, keepdims=True)
    olat = jnp.dot(p.astype(BF16), call[...], preferred_element_type=F32).astype(BF16)
    for h in range(MLA_H):
        o_ref[:, h * MLA_V:(h + 1) * MLA_V] = jnp.dot(
            olat[h * CH:(h + 1) * CH, :], wv_ref[h], preferred_element_type=F32).astype(BF16)


def _cached_attn(q, ckv, kpe, cache_kv, cache_pe, layer, rb0, lw):
    nb, past = cache_kv.shape[1], cache_kv.shape[2]

    def rowmap(b):
        return (rb0 + b, 0)

    return pl.pallas_call(
        _cached_attn_kernel,
        grid=(nb,),
        in_specs=[pl.BlockSpec((CH, MLA_H * HK), rowmap),
                  pl.BlockSpec((CH, KVL), rowmap),
                  pl.BlockSpec((CH, LANE), rowmap),
                  pl.BlockSpec((1, 1, past, KVL), lambda b: (layer, b, 0, 0)),
                  pl.BlockSpec((1, 1, past, ROPE), lambda b: (layer, b, 0, 0)),
                  pl.BlockSpec((MLA_H, NOPE, KVL), lambda b: (0, 0, 0)),
                  pl.BlockSpec((MLA_H, KVL, MLA_V), lambda b: (0, 0, 0))],
        out_specs=pl.BlockSpec((CH, MLA_H * MLA_V), lambda b: (b, 0)),
        out_shape=jax.ShapeDtypeStruct((nb * CH, MLA_H * MLA_V), BF16),
        scratch_shapes=[pltpu.VMEM((past + CH, KVL), BF16), pltpu.VMEM((past + CH, LANE), BF16)],
        compiler_params=_cp(1), name="cached_attn",
    )(q, ckv, kpe, cache_kv, cache_pe, lw["wkt"], lw["wv3"])


def _act_kernel(nskip, g_ref, u_ref, prev_ref, w_ref, a_ref, new_ref, buf):
    t = CH
    c = pl.program_id(1)

    @pl.when(c == 0)
    def _():
        buf[0:8, :] = jnp.zeros((8, buf.shape[1]), F32)
        buf[8 - (FFN_K - 1):8, :] = prev_ref[0]

    rows = lax.broadcasted_iota(jnp.int32, (t, 1), 0) + c * t
    g = jnp.where(rows >= nskip, g_ref[...].astype(F32), 0.0)
    buf[8:8 + t, :] = g
    conv = w_ref[0:1, :] * buf[6:6 + t, :]
    for i in range(1, FFN_K):
        conv = conv + w_ref[i:i + 1, :] * buf[6 + i:6 + i + t, :]
    a_ref[...] = (_silu(conv) * u_ref[...].astype(F32)).astype(BF16)
    buf[0:8, :] = buf[t:t + 8, :]

    @pl.when(c == pl.num_programs(1) - 1)
    def _():
        new_ref[0] = buf[8 - (FFN_K - 1):8, :]


def _act(gate, up, rb0, nb, nch, nskip, prev, w, act_in=None):
    r, n = gate.shape
    t = CH

    def rowmap(b, c):
        return (rb0 + b * nch + c, 0)

    args = [gate, up, prev, w]
    in_specs = [pl.BlockSpec((t, n), rowmap), pl.BlockSpec((t, n), rowmap),
                pl.BlockSpec((1, FFN_K - 1, n), lambda b, c: (b, 0, 0)),
                pl.BlockSpec((FFN_K, n), lambda b, c: (0, 0))]
    kern = functools.partial(_act_kernel, nskip)
    aliases = {}
    if act_in is not None:
        args.append(act_in)
        in_specs.append(pl.BlockSpec(memory_space=pl.ANY))
        aliases = {4: 0}
        kern = functools.partial(_act_alias_kernel, nskip)
    return pl.pallas_call(
        kern,
        grid=(nb, nch),
        in_specs=in_specs,
        out_specs=[pl.BlockSpec((t, n), rowmap), pl.BlockSpec((1, FFN_K - 1, n), lambda b, c: (b, 0, 0))],
        out_shape=[jax.ShapeDtypeStruct((r, n), BF16), jax.ShapeDtypeStruct((nb, FFN_K - 1, n), F32)],
        scratch_shapes=[pltpu.VMEM((8 + t, n), F32)],
        input_output_aliases=aliases,
        compiler_params=_cp(2), name="act",
    )(*args)


def _act_alias_kernel(nskip, g_ref, u_ref, prev_ref, w_ref, keep_ref, a_ref, new_ref, buf):
    del keep_ref
    _act_kernel(nskip, g_ref, u_ref, prev_ref, w_ref, a_ref, new_ref, buf)


def _swap_half(w):
    half = w.shape[-1] // 2
    return jnp.concatenate([w[..., half:], w[..., :half]], axis=-1)


def _layer_weights(i, w_in, ssd_conv_w, ssd_conv_b, ssd_dt_bias, ssd_a_log, ssd_d, ssd_norm, sc_conv_w,
                   mla_q_norm, mla_w_uq, mla_kv_norm, mla_w_ukv, w_out, ffn_w_gate, ffn_w_up, ffn_conv_w,
                   ffn_w_down):
    w = w_in[i]
    o_dt = SSD_DI + SSD_CD
    o_sc = o_dt + SSD_H
    o_mla = o_sc + 3 * SC_D
    o_kr = o_mla + QL + KVL
    kr = w[:, o_kr:o_kr + ROPE]
    win = jnp.concatenate([p.astype(BF16) for p in (
        w[:, :o_dt], w[:, o_sc:o_mla], w[:, o_mla:o_kr], kr, _swap_half(kr), w[:, o_dt:o_sc],
        jnp.zeros((D_MODEL, LANE - SSD_H), F32))], axis=1)
    uq = mla_w_uq[i].reshape(QL, MLA_H, NOPE + ROPE)
    pe = uq[..., NOPE:]
    zq = jnp.zeros((QL, MLA_H, HK - NOPE - ROPE), F32)
    wa = jnp.concatenate([uq[..., :NOPE], pe, zq], axis=-1).reshape(QL, MLA_H * HK).astype(BF16)
    wb = jnp.concatenate([_swap_half(pe), zq], axis=-1).reshape(QL, MLA_H * LANE).astype(BF16)
    ukv = mla_w_ukv[i].reshape(KVL, MLA_H, NOPE + MLA_V)
    padf = ((0, 0), (0, DFP - D_FF))
    return dict(
        win=win,
        cw=ssd_conv_w[i], cb=ssd_conv_b[i].reshape(1, SSD_CD),
        dtb=jnp.pad(ssd_dt_bias[i], (0, LANE - SSD_H)).reshape(1, LANE),
        alog=jnp.pad(ssd_a_log[i], (0, LANE - SSD_H)).reshape(1, LANE),
        dsk=jnp.repeat(ssd_d[i], SSD_P).reshape(1, SSD_DI),
        ng=ssd_norm[i].reshape(1, SSD_DI),
        scw=sc_conv_w[i],
        qn=mla_q_norm[i].reshape(1, QL), kvn=mla_kv_norm[i].reshape(1, KVL),
        wa=wa, wb=wb,
        wk=ukv[..., :NOPE].reshape(KVL, MLA_H * NOPE).astype(BF16),
        wv=ukv[..., NOPE:].reshape(KVL, MLA_H * MLA_V).astype(BF16),
        wkt=jnp.transpose(ukv[..., :NOPE], (1, 2, 0)).astype(BF16),
        wv3=jnp.transpose(ukv[..., NOPE:], (1, 0, 2)).astype(BF16),
        wout=w_out[i].astype(BF16),
        wg=jnp.pad(ffn_w_gate[i].astype(BF16), padf),
        wu=jnp.pad(ffn_w_up[i].astype(BF16), padf),
        fcw=jnp.pad(ffn_conv_w[i], padf),
        wd=jnp.pad(ffn_w_down[i].astype(BF16), ((0, DFP - D_FF), (0, 0))),
    )


def _rope_table(lp, ns, past):
    half = ROPE // 2
    pos = jnp.concatenate([jnp.maximum(jnp.arange(lp, dtype=jnp.int32) - NSKIP, 0),
                           N_META + past + jnp.arange(ns, dtype=jnp.int32) % CH])
    inv = THETA ** (-jnp.arange(half, dtype=F32) / half)
    ang = pos.astype(F32)[:, None] * inv[None, :]
    cos, sin = jnp.cos(ang), jnp.sin(ang)
    return jnp.concatenate([cos, cos, -sin, sin], axis=1)


def kernel(x_prompt, x_sample, cache_kv_latent, cache_k_rope, state_ssm, state_ssd_conv, state_sconv, state_ffn_conv, meta_tokens, norm_mix_pre, norm_mix_post, norm_ffn_pre, norm_ffn_post, w_in, ssd_conv_w, ssd_conv_b, ssd_dt_bias, ssd_a_log, ssd_d, ssd_norm, sc_conv_w, mla_q_norm, mla_w_uq, mla_kv_norm, mla_w_ukv, w_out, ffn_w_gate, ffn_w_up, ffn_conv_w, ffn_w_down):
    bp, seq, d = x_prompt.shape
    nb, ls, _ = x_sample.shape
    depth, _, past, _ = cache_kv_latent.shape
    assert bp == 1 and ls == CH and seq % CH == 0 and d == D_MODEL
    lp = CH + seq
    ns = nb * ls
    npc = lp // CH

    x = jnp.concatenate([jnp.zeros((NSKIP, d), F32), meta_tokens.astype(F32), x_prompt[0],
                         x_sample.reshape(ns, d)], axis=0)
    tab = _rope_table(lp, ns, past)
    zero_c = jnp.zeros((1, SSD_K - 1, SSD_CD), F32)
    zero_h = jnp.zeros((1, SSD_N, SSD_DI), F32)

    def state_in(s):
        return jnp.transpose(s, (0, 3, 1, 2)).reshape(s.shape[0], SSD_N, SSD_DI)

    def state_out(s):
        return jnp.transpose(s.reshape(s.shape[0], SSD_N, SSD_H, SSD_P), (0, 2, 3, 1))
    zero_s = jnp.zeros((1, SC_K - 1, SC_D), F32)
    zero_f = jnp.zeros((1, FFN_K - 1, DFP), F32)
    padf = ((0, 0), (0, 0), (0, DFP - D_FF))

    h = _norm_cast(x, norm_mix_pre[0])
    outs_p, outs_s = [], []
    for i in range(depth):
        lw = _layer_weights(i, w_in, ssd_conv_w, ssd_conv_b, ssd_dt_bias, ssd_a_log, ssd_d, ssd_norm,
                            sc_conv_w, mla_q_norm, mla_w_uq, mla_kv_norm, mla_w_ukv, w_out, ffn_w_gate,
                            ffn_w_up, ffn_conv_w, ffn_w_down)
        u, dtf = _inproj(h, lw["win"])

        yp, yscp, cnew_p, hnew_p, scnew_p = _mixer(u, dtf, 0, 1, npc, NSKIP, zero_c, zero_h, zero_s, lw)
        y_ssd, y_sc, cnew_s, hnew_s, scnew_s = _mixer(u, dtf, npc, nb, 1, 0, state_ssd_conv[i],
                                                      state_in(state_ssm[i]), state_sconv[i], lw,
                                                      keep=(yp, yscp))
        hnew_p, hnew_s = state_out(hnew_p), state_out(hnew_s)

        q, ckv, kpe, kcat, v = _mla_proj(u, tab, lw)
        o0 = _attn0(q, kcat, v)
        o_real = _flash(q[CH:lp].T, kcat[CH:lp], v[CH:lp].T, kcat[:CH], v[:CH].T).T
        o_s = _cached_attn(q, ckv, kpe, cache_kv_latent, cache_k_rope, i, npc, lw)
        y_mla = jnp.concatenate([o0, o_real, o_s], axis=0)

        mix = _outproj(y_ssd, y_sc, y_mla, lw["wout"])
        x, h = _resid_norm(x, mix, norm_mix_post[i], norm_ffn_pre[i])

        gate, up = _gateup(h, lw["wg"], lw["wu"])
        act, fnew_p = _act(gate, up, 0, 1, npc, NSKIP, zero_f, lw["fcw"])
        act, fnew_s = _act(gate, up, npc, nb, 1, 0, jnp.pad(state_ffn_conv[i], padf), lw["fcw"], act_in=act)
        f = _down(act, lw["wd"])
        x, h = _resid_norm(x, f, norm_ffn_post[i], norm_mix_pre[i + 1] if i + 1 < depth else None)

        outs_p.append((ckv[NSKIP:lp][None], kpe[NSKIP:lp, :ROPE][None], hnew_p, cnew_p, scnew_p,
                       fnew_p[:, :, :D_FF]))
        outs_s.append((ckv[lp:].reshape(nb, ls, KVL), kpe[lp:, :ROPE].reshape(nb, ls, ROPE), hnew_s, cnew_s,
                       scnew_s, fnew_s[:, :, :D_FF]))

    def stack(outs, j):
        return jnp.stack([o[j] for o in outs], axis=0)

    y_prompt = x[CH:lp][None]
    y_sample = x[lp:].reshape(nb, ls, d)
    return (y_prompt, y_sample,
            stack(outs_p, 0), stack(outs_p, 1), stack(outs_p, 2), stack(outs_p, 3), stack(outs_p, 4), stack(outs_p, 5),
            stack(outs_s, 0), stack(outs_s, 1), stack(outs_s, 2), stack(outs_s, 3), stack(outs_s, 4), stack(outs_s, 5))
```

```python
import functools

import jax
import jax.numpy as jnp
import numpy as np
from jax import lax
from jax.experimental import pallas as pl
from jax.experimental.pallas import tpu as pltpu

F32 = jnp.float32
BF16 = jnp.bfloat16

D_MODEL = 4096
N_META = 16
CH = 64
NSKIP = CH - N_META
EPS = 1e-6
SSD_P = 64
SSD_DI = 2048
SSD_H = 32
SSD_G = 8
SSD_N = 128
SSD_K = 4
SSD_CD = 4096
SC_D = 1024
SC_K = 3
MLA_H = 8
NOPE = 128
ROPE = 64
MLA_V = 128
QL = 768
KVL = 512
HK = 256
SCALE = (NOPE + ROPE) ** -0.5
QSCALE = SCALE * 1.4426950408889634
THETA = 10000.0
D_FF = 11008
DFP = 11264
FFN_K = 3
LANE = 128

C_Z, C_X, C_BC, C_SCB, C_SCC, C_SCH, C_MLA, NIN = 0, 2048, 4096, 6144, 7168, 8192, 9216, 10752
MLA_W = 1536

FLASH_TQ = 512
VMEM_LIMIT = 56 * 1024 * 1024
HI = lax.Precision.HIGHEST
NT = (((1,), (1,)), ((), ()))
TN = (((0,), (0,)), ((), ()))


def _cp(n):
    return pltpu.CompilerParams(dimension_semantics=("arbitrary",) * n, vmem_limit_bytes=VMEM_LIMIT)


def _pick(n, cands):
    for c in cands:
        if n % c == 0:
            return c
    raise ValueError(f"no tile for {n}")


MM_ROWS = (1088, 544, 512, 272, 256, 136, 128, 64)
EW_ROWS = (272, 256, 136, 128, 64)


def _rms(x, g):
    return x * lax.rsqrt(jnp.mean(x * x, axis=-1, keepdims=True) + EPS) * g


def _silu(x):
    return x * jax.nn.sigmoid(x)


def _norm_cast_kernel(x_ref, g_ref, h_ref):
    h_ref[...] = _rms(x_ref[...], g_ref[...]).astype(BF16)


def _norm_cast(x, g):
    r, d = x.shape
    bm = _pick(r, EW_ROWS)
    return pl.pallas_call(
        _norm_cast_kernel,
        grid=(r // bm,),
        in_specs=[pl.BlockSpec((bm, d), lambda i: (i, 0)), pl.BlockSpec((1, d), lambda i: (0, 0))],
        out_specs=pl.BlockSpec((bm, d), lambda i: (i, 0)),
        out_shape=jax.ShapeDtypeStruct((r, d), BF16),
        compiler_params=_cp(1), name="norm_cast",
    )(x, g.reshape(1, d))


def _resid_norm_kernel(x_ref, m_ref, gp_ref, gn_ref, x2_ref, h_ref):
    x2 = x_ref[...] + _rms(m_ref[...], gp_ref[...])
    x2_ref[...] = x2
    h_ref[...] = _rms(x2, gn_ref[...]).astype(BF16)


def _resid_kernel(x_ref, m_ref, gp_ref, x2_ref):
    x2_ref[...] = x_ref[...] + _rms(m_ref[...], gp_ref[...])


def _resid_norm(x, m, g_post, g_next):
    r, d = x.shape
    bm = _pick(r, EW_ROWS)
    row = pl.BlockSpec((bm, d), lambda i: (i, 0))
    vec = pl.BlockSpec((1, d), lambda i: (0, 0))
    if g_next is None:
        return pl.pallas_call(
            _resid_kernel, grid=(r // bm,), in_specs=[row, row, vec], out_specs=row,
            out_shape=jax.ShapeDtypeStruct((r, d), F32), compiler_params=_cp(1), name="resid",
        )(x, m, g_post.reshape(1, d)), None
    return pl.pallas_call(
        _resid_norm_kernel, grid=(r // bm,), in_specs=[row, row, vec, vec], out_specs=[row, row],
        out_shape=[jax.ShapeDtypeStruct((r, d), F32), jax.ShapeDtypeStruct((r, d), BF16)],
        compiler_params=_cp(1), name="resid_norm",
    )(x, m, g_post.reshape(1, d), g_next.reshape(1, d))


def _inproj_kernel(x_ref, w_ref, u_ref, dt_ref):
    acc = jnp.dot(x_ref[...], w_ref[...], preferred_element_type=F32)
    u_ref[...] = acc.astype(BF16)

    @pl.when(pl.program_id(1) == pl.num_programs(1) - 1)
    def _():
        dt_ref[...] = acc[:, acc.shape[1] - LANE:]


def _inproj(h, w):
    r, k = h.shape
    n = w.shape[1]
    bm = _pick(r, MM_ROWS)
    bn = 768
    return pl.pallas_call(
        _inproj_kernel,
        grid=(r // bm, n // bn),
        in_specs=[pl.BlockSpec((bm, k), lambda i, j: (i, 0)), pl.BlockSpec((k, bn), lambda i, j: (0, j))],
        out_specs=[pl.BlockSpec((bm, bn), lambda i, j: (i, j)), pl.BlockSpec((bm, LANE), lambda i, j: (i, 0))],
        out_shape=[jax.ShapeDtypeStruct((r, n), BF16), jax.ShapeDtypeStruct((r, LANE), F32)],
        compiler_params=_cp(2), name="inproj",
    )(h, w)


def _outproj_kernel(a_ref, b_ref, c_ref, wa_ref, wb_ref, wc_ref, o_ref):
    acc = jnp.dot(a_ref[...], wa_ref[...], preferred_element_type=F32)
    acc = acc + jnp.dot(b_ref[...], wb_ref[...], preferred_element_type=F32)
    acc = acc + jnp.dot(c_ref[...], wc_ref[...], preferred_element_type=F32)
    o_ref[...] = acc


def _outproj(y_ssd, y_sc, y_mla, w):
    r = y_ssd.shape[0]
    n = w.shape[1]
    bm = _pick(r, MM_ROWS)
    bn = 1024
    return pl.pallas_call(
        _outproj_kernel,
        grid=(r // bm, n // bn),
        in_specs=[pl.BlockSpec((bm, SSD_DI), lambda i, j: (i, 0)),
                  pl.BlockSpec((bm, SC_D), lambda i, j: (i, 0)),
                  pl.BlockSpec((bm, SC_D), lambda i, j: (i, 0)),
                  pl.BlockSpec((SSD_DI, bn), lambda i, j: (0, j)),
                  pl.BlockSpec((SC_D, bn), lambda i, j: (2, j)),
                  pl.BlockSpec((SC_D, bn), lambda i, j: (3, j))],
        out_specs=pl.BlockSpec((bm, bn), lambda i, j: (i, j)),
        out_shape=jax.ShapeDtypeStruct((r, n), F32),
        compiler_params=_cp(2), name="outproj",
    )(y_ssd, y_sc, y_mla, w, w, w)


def _gateup_kernel(x_ref, wg_ref, wu_ref, g_ref, u_ref):
    x = x_ref[...]
    g_ref[...] = jnp.dot(x, wg_ref[...], preferred_element_type=F32).astype(BF16)
    u_ref[...] = jnp.dot(x, wu_ref[...], preferred_element_type=F32).astype(BF16)


def _gateup(h, wg, wu):
    r, k = h.shape
    n = wg.shape[1]
    bm = _pick(r, MM_ROWS)
    bn = 512
    wspec = pl.BlockSpec((k, bn), lambda i, j: (0, j))
    ospec = pl.BlockSpec((bm, bn), lambda i, j: (i, j))
    return pl.pallas_call(
        _gateup_kernel,
        grid=(r // bm, n // bn),
        in_specs=[pl.BlockSpec((bm, k), lambda i, j: (i, 0)), wspec, wspec],
        out_specs=[ospec, ospec],
        out_shape=[jax.ShapeDtypeStruct((r, n), BF16)] * 2,
        compiler_params=_cp(2), name="gateup",
    )(h, wg, wu)


def _down_kernel(x_ref, w_ref, o_ref):
    p = jnp.dot(x_ref[...], w_ref[...], preferred_element_type=F32)

    @pl.when(pl.program_id(2) == 0)
    def _():
        o_ref[...] = p

    @pl.when(pl.program_id(2) > 0)
    def _():
        o_ref[...] += p


def _down(a, w):
    r, k = a.shape
    n = w.shape[1]
    bm = _pick(r, MM_ROWS)
    bn = 1024
    bk = k // 4
    return pl.pallas_call(
        _down_kernel,
        grid=(r // bm, n // bn, k // bk),
        in_specs=[pl.BlockSpec((bm, bk), lambda i, j, kk: (i, kk)), pl.BlockSpec((bk, bn), lambda i, j, kk: (kk, j))],
        out_specs=pl.BlockSpec((bm, bn), lambda i, j, kk: (i, j)),
        out_shape=jax.ShapeDtypeStruct((r, n), F32),
        compiler_params=_cp(3), name="down",
    )(a, w)


def _mixer_kernel(nskip, z_ref, x_ref, bc_ref, dt_ref, scb_ref, scc_ref, sch_ref,
                  cprev_ref, h0_ref, scprev_ref,
                  cw_ref, cb_ref, dtb_ref, alog_ref, dsk_ref, ng_ref, scw_ref,
                  y_ref, ysc_ref, cnew_ref, hnew_ref, scnew_ref,
                  buf, sbuf, hst):
    t = CH
    c = pl.program_id(1)
    last = pl.num_programs(1) - 1

    @pl.when(c == 0)
    def _init():
        buf[0:8, :] = jnp.zeros((8, SSD_CD), F32)
        buf[8 - (SSD_K - 1):8, :] = cprev_ref[0]
        sbuf[0:8, :] = jnp.zeros((8, SC_D), F32)
        sbuf[8 - (SC_K - 1):8, :] = scprev_ref[0]
        hst[...] = h0_ref[0]

    rows = lax.broadcasted_iota(jnp.int32, (t, 1), 0) + c * t
    valid = rows >= nskip

    xin = jnp.concatenate([x_ref[...], bc_ref[...]], axis=1).astype(F32)
    xin = jnp.where(valid, xin, 0.0)
    buf[8:8 + t, :] = xin
    acc = cb_ref[...] + cw_ref[0:1, :] * buf[5:5 + t, :]
    for i in range(1, SSD_K):
        acc = acc + cw_ref[i:i + 1, :] * buf[5 + i:5 + i + t, :]
    xbc = _silu(acc)
    buf[0:8, :] = buf[t:t + 8, :]

    @pl.when(c == last)
    def _():
        cnew_ref[0] = buf[8 - (SSD_K - 1):8, :]

    lane = lax.broadcasted_iota(jnp.int32, (t, LANE), 1)
    v = dt_ref[...] + dtb_ref[...]
    dt = jnp.maximum(v, 0.0) + jnp.log1p(jnp.exp(-jnp.abs(v)))
    dt = jnp.where(valid & (lane < SSD_H), dt, 0.0)
    adt = dt * (-jnp.exp(alog_ref[...]))
    ri = lax.broadcasted_iota(jnp.int32, (t, t), 0)
    ci = lax.broadcasted_iota(jnp.int32, (t, t), 1)
    tri = ri >= ci
    acs = jnp.dot(tri.astype(F32), adt, precision=HI, preferred_element_type=F32)

    hpg = SSD_H // SSD_G
    gw = hpg * SSD_P
    lane_g = lax.broadcasted_iota(jnp.int32, (t, gw), 1)
    row_g = lax.broadcasted_iota(jnp.int32, (t, gw), 0)
    seg = lane_g // SSD_P
    eye_g = row_g == lane_g % SSD_P
    tri_g = row_g >= lane_g % SSD_P
    blockdiag = (lax.broadcasted_iota(jnp.int32, (hpg * t, gw), 0) // t
                 == lax.broadcasted_iota(jnp.int32, (hpg * t, gw), 1) // SSD_P)

    def per_head_lanes(mat, g):
        out = jnp.broadcast_to(mat[:, g * hpg + hpg - 1:g * hpg + hpg], (t, gw))
        for r in range(hpg - 2, -1, -1):
            out = jnp.where(seg == r, jnp.broadcast_to(mat[:, g * hpg + r:g * hpg + r + 1], (t, gw)), out)
        return out

    for g in range(SSD_G):
        cols = slice(g * gw, (g + 1) * gw)
        bg = xbc[:, SSD_DI + g * SSD_N:SSD_DI + (g + 1) * SSD_N].astype(BF16)
        cg = xbc[:, SSD_DI + SSD_G * SSD_N + g * SSD_N:SSD_DI + SSD_G * SSD_N + (g + 1) * SSD_N].astype(BF16)
        acol = per_head_lanes(acs, g)
        dtx = per_head_lanes(dt, g)
        arow = jnp.sum(jnp.where(eye_g, acol, 0.0), axis=0, keepdims=True)
        decay = jnp.exp(jnp.where(tri_g, acol - arow, -jnp.inf))
        cb = lax.dot_general(cg, jnp.concatenate([bg] * hpg, axis=0), NT, preferred_element_type=F32)
        xg = xbc[:, cols]
        xdt = xg * dtx
        xdtb = xdt.astype(BF16)
        rhs = jnp.where(blockdiag, jnp.concatenate([xdtb] * hpg, axis=0), jnp.zeros((), BF16))
        ydiag = jnp.dot((cb * decay).astype(BF16), rhs, preferred_element_type=F32)
        hg = hst[:, cols]
        yoff = jnp.dot(cg, hg.astype(BF16), preferred_element_type=F32) * jnp.exp(acol)
        alast = acol[t - 1:t, :]
        snew = lax.dot_general(bg, (xdt * jnp.exp(alast - acol)).astype(BF16), TN, preferred_element_type=F32)
        hst[:, cols] = jnp.exp(alast) * hg + snew
        y = ydiag + yoff + dsk_ref[:, cols] * xg
        y = y * _silu(z_ref[:, cols].astype(F32))
        y_ref[:, cols] = _rms(y, ng_ref[:, cols]).astype(BF16)

    @pl.when(c == last)
    def _():
        hnew_ref[0] = hst[...]

    p = scc_ref[...].astype(F32) * sch_ref[...].astype(F32)
    p = jnp.where(valid, p, 0.0)
    sbuf[8:8 + t, :] = p
    conv = scw_ref[0:1, :] * sbuf[6:6 + t, :]
    for i in range(1, SC_K):
        conv = conv + scw_ref[i:i + 1, :] * sbuf[6 + i:6 + i + t, :]
    ysc_ref[...] = (scb_ref[...].astype(F32) * conv).astype(BF16)
    sbuf[0:8, :] = sbuf[t:t + 8, :]

    @pl.when(c == last)
    def _():
        scnew_ref[0] = sbuf[8 - (SC_K - 1):8, :]


def _mixer_alias_kernel(nskip, *refs):
    n_in = 17
    _mixer_kernel(nskip, *refs[:n_in], *refs[n_in + 2:])


def _mixer(u, dtf, rb0, nb, nch, nskip, cprev, h0, scprev, lw, keep=None):
    r = u.shape[0]
    t = CH

    def rowmap(cb):
        return lambda b, c: (rb0 + b * nch + c, cb)

    def stmap(b, c):
        return (b, 0, 0)

    def full(shape):
        return pl.BlockSpec(shape, lambda b, c: (0,) * len(shape))

    in_specs = [
        pl.BlockSpec((t, SSD_DI), rowmap(C_Z // SSD_DI)),
        pl.BlockSpec((t, SSD_DI), rowmap(C_X // SSD_DI)),
        pl.BlockSpec((t, SSD_DI), rowmap(C_BC // SSD_DI)),
        pl.BlockSpec((t, LANE), rowmap(0)),
        pl.BlockSpec((t, SC_D), rowmap(C_SCB // SC_D)),
        pl.BlockSpec((t, SC_D), rowmap(C_SCC // SC_D)),
        pl.BlockSpec((t, SC_D), rowmap(C_SCH // SC_D)),
        pl.BlockSpec((1, SSD_K - 1, SSD_CD), stmap),
        pl.BlockSpec((1, SSD_N, SSD_DI), stmap),
        pl.BlockSpec((1, SC_K - 1, SC_D), stmap),
        full((SSD_K, SSD_CD)), full((1, SSD_CD)), full((1, LANE)), full((1, LANE)),
        full((1, SSD_DI)), full((1, SSD_DI)), full((SC_K, SC_D)),
    ]
    out_specs = [
        pl.BlockSpec((t, SSD_DI), rowmap(0)),
        pl.BlockSpec((t, SC_D), rowmap(0)),
        pl.BlockSpec((1, SSD_K - 1, SSD_CD), stmap),
        pl.BlockSpec((1, SSD_N, SSD_DI), stmap),
        pl.BlockSpec((1, SC_K - 1, SC_D), stmap),
    ]
    out_shape = [
        jax.ShapeDtypeStruct((r, SSD_DI), BF16),
        jax.ShapeDtypeStruct((r, SC_D), BF16),
        jax.ShapeDtypeStruct((nb, SSD_K - 1, SSD_CD), F32),
        jax.ShapeDtypeStruct((nb, SSD_N, SSD_DI), F32),
        jax.ShapeDtypeStruct((nb, SC_K - 1, SC_D), F32),
    ]
    args = [u, u, u, dtf, u, u, u, cprev, h0, scprev,
            lw["cw"], lw["cb"], lw["dtb"], lw["alog"], lw["dsk"], lw["ng"], lw["scw"]]
    kern = functools.partial(_mixer_kernel, nskip)
    aliases = {}
    if keep is not None:
        aliases = {len(args): 0, len(args) + 1: 1}
        args += list(keep)
        in_specs += [pl.BlockSpec(memory_space=pl.ANY)] * 2
        kern = functools.partial(_mixer_alias_kernel, nskip)
    return pl.pallas_call(
        kern,
        grid=(nb, nch),
        in_specs=in_specs, out_specs=out_specs, out_shape=out_shape,
        scratch_shapes=[pltpu.VMEM((8 + t, SSD_CD), F32), pltpu.VMEM((8 + t, SC_D), F32),
                        pltpu.VMEM((SSD_N, SSD_DI), F32)],
        input_output_aliases=aliases,
        compiler_params=_cp(2), name="mixer",
    )(*args)


def _mla_proj_kernel(blk_ref, tab_ref, qn_ref, kvn_ref, wa_ref, wb_ref, wk_ref, wv_ref,
                     q_ref, ckv_ref, kpe_ref, kcat_ref, v_ref):
    blk = blk_ref[...]
    cq = _rms(blk[:, :QL].astype(F32), qn_ref[...]).astype(BF16)
    ckv = _rms(blk[:, QL:QL + KVL].astype(F32), kvn_ref[...])
    ckv_ref[...] = ckv
    tab = tab_ref[...]
    tabr = pltpu.roll(tab, ROPE, 1)
    prod = blk[:, QL + KVL:QL + KVL + LANE].astype(F32) * tab
    ksum = prod + pltpu.roll(prod, ROPE, 1)
    lane = lax.broadcasted_iota(jnp.int32, ksum.shape, 1)
    kpe = jnp.where(lane < ROPE, ksum, 0.0)
    kpe_ref[...] = kpe
    qa = jnp.dot(cq, wa_ref[...], preferred_element_type=F32)
    qb = jnp.dot(cq, wb_ref[...], preferred_element_type=F32)
    ckvb = ckv.astype(BF16)
    kn = jnp.dot(ckvb, wk_ref[...], preferred_element_type=F32)
    v_ref[...] = jnp.dot(ckvb, wv_ref[...], preferred_element_type=F32).astype(BF16)
    kpeb = kpe.astype(BF16)
    tab_q = tab * QSCALE
    tabr_q = tabr * QSCALE
    for h in range(MLA_H):
        q_ref[:, h * HK:h * HK + NOPE] = (qa[:, h * HK:h * HK + NOPE] * QSCALE).astype(BF16)
        q_ref[:, h * HK + NOPE:(h + 1) * HK] = (
            qa[:, h * HK + NOPE:(h + 1) * HK] * tab_q + qb[:, h * LANE:(h + 1) * LANE] * tabr_q).astype(BF16)
        kcat_ref[:, h * HK:h * HK + NOPE] = kn[:, h * NOPE:(h + 1) * NOPE].astype(BF16)
        kcat_ref[:, h * HK + NOPE:(h + 1) * HK] = kpeb


def _mla_proj(u, tab, lw):
    r = u.shape[0]
    bm = _pick(r, (544, 512, 272, 256, 136, 128, 64))

    def full(shape):
        return pl.BlockSpec(shape, lambda i: (0,) * len(shape))

    def row(w):
        return pl.BlockSpec((bm, w), lambda i: (i, 0))

    return pl.pallas_call(
        _mla_proj_kernel,
        grid=(r // bm,),
        in_specs=[pl.BlockSpec((bm, MLA_W), lambda i: (i, C_MLA // MLA_W)), row(LANE),
                  full((1, QL)), full((1, KVL)), full((QL, MLA_H * HK)), full((QL, MLA_H * LANE)),
                  full((KVL, MLA_H * NOPE)), full((KVL, MLA_H * MLA_V))],
        out_specs=[row(MLA_H * HK), row(KVL), row(LANE), row(MLA_H * HK), row(MLA_H * MLA_V)],
        out_shape=[jax.ShapeDtypeStruct((r, MLA_H * HK), BF16), jax.ShapeDtypeStruct((r, KVL), F32),
                   jax.ShapeDtypeStruct((r, LANE), F32), jax.ShapeDtypeStruct((r, MLA_H * HK), BF16),
                   jax.ShapeDtypeStruct((r, MLA_H * MLA_V), BF16)],
        compiler_params=_cp(1), name="mla_proj",
    )(u, tab, lw["qn"], lw["kvn"], lw["wa"], lw["wb"], lw["wk"], lw["wv"])


def _flash_kernel(it_ref, jt_ref, q_ref, k_ref, v_ref, o_ref, qt_s, m_s, l_s, acc_s):
    tq = q_ref.shape[0]
    tk = k_ref.shape[0]
    step = pl.program_id(0)
    i = it_ref[step]
    j = jt_ref[step]

    @pl.when(j == 0)
    def _init():
        for h in range(MLA_H):
            qt_s[h * HK:(h + 1) * HK, :] = q_ref[:, h * HK:(h + 1) * HK].T
        m_s[...] = jnp.full(m_s.shape, -jnp.inf, F32)
        l_s[...] = jnp.zeros(l_s.shape, F32)
        acc_s[...] = jnp.zeros(acc_s.shape, F32)

    def tile(masked):
        if masked:
            krow = j * tk + lax.broadcasted_iota(jnp.int32, (tk, tq), 0)
            qrow = i * tq + lax.broadcasted_iota(jnp.int32, (tk, tq), 1)
            keep = (krow // CH <= qrow // CH) & (krow >= NSKIP)

        def scores(h):
            return jnp.dot(k_ref[:, h * HK:(h + 1) * HK], qt_s[h * HK:(h + 1) * HK, :],
                           preferred_element_type=F32)

        s_next = scores(0)
        for h in range(MLA_H):
            st = s_next
            if h + 1 < MLA_H:
                s_next = scores(h + 1)
            if masked:
                st = jnp.where(keep, st, -jnp.inf)
            m_prev = m_s[h]
            m_new = jnp.maximum(m_prev, jnp.max(st, axis=0, keepdims=True))
            alpha = jnp.exp2(m_prev - m_new)
            p = jnp.exp2(st - m_new)
            l_s[h] = alpha * l_s[h] + jnp.sum(p, axis=0, keepdims=True)
            rows = slice(h * MLA_V, (h + 1) * MLA_V)
            acc_s[rows, :] = alpha * acc_s[rows, :] + lax.dot_general(
                v_ref[:, rows], p.astype(BF16), TN, preferred_element_type=F32)
            m_s[h] = m_new

    @pl.when((j > 0) & (j < i))
    def _():
        tile(False)

    @pl.when((j == 0) | (j == i))
    def _():
        tile(True)

    @pl.when(j == i)
    def _():
        for h in range(MLA_H):
            rows = slice(h * MLA_V, (h + 1) * MLA_V)
            o_ref[:, rows] = (acc_s[rows, :] / l_s[h]).T.astype(BF16)


def _flash(q, kcat, v, lp):
    r = q.shape[0]
    tq = FLASH_TQ
    nq = -(-lp // tq)
    assert nq * tq <= r
    it = np.concatenate([np.full((i + 1,), i, np.int32) for i in range(nq)])
    jt = np.concatenate([np.arange(i + 1, dtype=np.int32) for i in range(nq)])
    grid_spec = pltpu.PrefetchScalarGridSpec(
        num_scalar_prefetch=2,
        grid=(int(it.shape[0]),),
        in_specs=[pl.BlockSpec((tq, MLA_H * HK), lambda s, it, jt: (it[s], 0)),
                  pl.BlockSpec((tq, MLA_H * HK), lambda s, it, jt: (jt[s], 0)),
                  pl.BlockSpec((tq, MLA_H * MLA_V), lambda s, it, jt: (jt[s], 0))],
        out_specs=pl.BlockSpec((tq, MLA_H * MLA_V), lambda s, it, jt: (it[s], 0)),
        scratch_shapes=[pltpu.VMEM((MLA_H * HK, tq), BF16),
                        pltpu.VMEM((MLA_H, 1, tq), F32), pltpu.VMEM((MLA_H, 1, tq), F32),
                        pltpu.VMEM((MLA_H * MLA_V, tq), F32)],
    )
    return pl.pallas_call(
        _flash_kernel,
        grid_spec=grid_spec,
        out_shape=jax.ShapeDtypeStruct((r, MLA_H * MLA_V), BF16),
        compiler_params=_cp(1), name="flash",
    )(jnp.asarray(it), jnp.asarray(jt), q, kcat, v)


def _cached_attn_kernel(q_ref, cnew_ref, pnew_ref, ckv_ref, cpe_ref, wkt_ref, wv_ref, o_ref, call, peall):
    past = ckv_ref.shape[2]
    call[0:past, :] = ckv_ref[0, 0].astype(BF16)
    call[past:past + CH, :] = cnew_ref[...].astype(BF16)
    peall[...] = jnp.zeros(peall.shape, BF16)
    peall[0:past, 0:ROPE] = cpe_ref[0, 0].astype(BF16)
    peall[past:past + CH, :] = pnew_ref[...].astype(BF16)
    qlat = []
    qpe = []
    for h in range(MLA_H):
        qn = q_ref[:, h * HK:h * HK + NOPE]
        qlat.append(jnp.dot(qn, wkt_ref[h], preferred_element_type=F32).astype(BF16))
        qpe.append(q_ref[:, h * HK + NOPE:(h + 1) * HK])
    qlat = jnp.concatenate(qlat, axis=0)
    qpe = jnp.concatenate(qpe, axis=0)
    s = (lax.dot_general(qlat, call[...], NT, preferred_element_type=F32)
         + lax.dot_general(qpe, peall[...], NT, preferred_element_type=F32))
    p = jnp.exp2(s - jnp.max(s, axis=-1, keepdims=True))
    p = p / jnp.sum(p, axis=-1, keepdims=True)
    olat = jnp.dot(p.astype(BF16), call[...], preferred_element_type=F32).astype(BF16)
    for h in range(MLA_H):
        o_ref[:, h * MLA_V:(h + 1) * MLA_V] = jnp.dot(
            olat[h * CH:(h + 1) * CH, :], wv_ref[h], preferred_element_type=F32).astype(BF16)


def _cached_attn_alias_kernel(q_ref, cnew_ref, pnew_ref, ckv_ref, cpe_ref, wkt_ref, wv_ref, keep_ref, o_ref,
                              call, peall):
    del keep_ref
    _cached_attn_kernel(q_ref, cnew_ref, pnew_ref, ckv_ref, cpe_ref, wkt_ref, wv_ref, o_ref, call, peall)


def _cached_attn(q, ckv, kpe, cache_kv, cache_pe, layer, rb0, lw, o_prompt):
    nb, past = cache_kv.shape[1], cache_kv.shape[2]

    def rowmap(b):
        return (rb0 + b, 0)

    return pl.pallas_call(
        _cached_attn_alias_kernel,
        grid=(nb,),
        in_specs=[pl.BlockSpec((CH, MLA_H * HK), rowmap),
                  pl.BlockSpec((CH, KVL), rowmap),
                  pl.BlockSpec((CH, LANE), rowmap),
                  pl.BlockSpec((1, 1, past, KVL), lambda b: (layer, b, 0, 0)),
                  pl.BlockSpec((1, 1, past, ROPE), lambda b: (layer, b, 0, 0)),
                  pl.BlockSpec((MLA_H, NOPE, KVL), lambda b: (0, 0, 0)),
                  pl.BlockSpec((MLA_H, KVL, MLA_V), lambda b: (0, 0, 0)),
                  pl.BlockSpec(memory_space=pl.ANY)],
        out_specs=pl.BlockSpec((CH, MLA_H * MLA_V), rowmap),
        out_shape=jax.ShapeDtypeStruct(o_prompt.shape, BF16),
        scratch_shapes=[pltpu.VMEM((past + CH, KVL), BF16), pltpu.VMEM((past + CH, LANE), BF16)],
        input_output_aliases={7: 0},
        compiler_params=_cp(1), name="cached_attn",
    )(q, ckv, kpe, cache_kv, cache_pe, lw["wkt"], lw["wv3"], o_prompt)


def _act_kernel(nskip, g_ref, u_ref, prev_ref, w_ref, a_ref, new_ref, buf):
    t = CH
    c = pl.program_id(1)

    @pl.when(c == 0)
    def _():
        buf[0:8, :] = jnp.zeros((8, buf.shape[1]), F32)
        buf[8 - (FFN_K - 1):8, :] = prev_ref[0]

    rows = lax.broadcasted_iota(jnp.int32, (t, 1), 0) + c * t
    g = jnp.where(rows >= nskip, g_ref[...].astype(F32), 0.0)
    buf[8:8 + t, :] = g
    conv = w_ref[0:1, :] * buf[6:6 + t, :]
    for i in range(1, FFN_K):
        conv = conv + w_ref[i:i + 1, :] * buf[6 + i:6 + i + t, :]
    a_ref[...] = (_silu(conv) * u_ref[...].astype(F32)).astype(BF16)
    buf[0:8, :] = buf[t:t + 8, :]

    @pl.when(c == pl.num_programs(1) - 1)
    def _():
        new_ref[0] = buf[8 - (FFN_K - 1):8, :]


def _act(gate, up, rb0, nb, nch, nskip, prev, w, act_in=None):
    r, n = gate.shape
    t = CH

    def rowmap(b, c):
        return (rb0 + b * nch + c, 0)

    args = [gate, up, prev, w]
    in_specs = [pl.BlockSpec((t, n), rowmap), pl.BlockSpec((t, n), rowmap),
                pl.BlockSpec((1, FFN_K - 1, n), lambda b, c: (b, 0, 0)),
                pl.BlockSpec((FFN_K, n), lambda b, c: (0, 0))]
    kern = functools.partial(_act_kernel, nskip)
    aliases = {}
    if act_in is not None:
        args.append(act_in)
        in_specs.append(pl.BlockSpec(memory_space=pl.ANY))
        aliases = {4: 0}
        kern = functools.partial(_act_alias_kernel, nskip)
    return pl.pallas_call(
        kern,
        grid=(nb, nch),
        in_specs=in_specs,
        out_specs=[pl.BlockSpec((t, n), rowmap), pl.BlockSpec((1, FFN_K - 1, n), lambda b, c: (b, 0, 0))],
        out_shape=[jax.ShapeDtypeStruct((r, n), BF16), jax.ShapeDtypeStruct((nb, FFN_K - 1, n), F32)],
        scratch_shapes=[pltpu.VMEM((8 + t, n), F32)],
        input_output_aliases=aliases,
        compiler_params=_cp(2), name="act",
    )(*args)


def _act_alias_kernel(nskip, g_ref, u_ref, prev_ref, w_ref, keep_ref, a_ref, new_ref, buf):
    del keep_ref
    _act_kernel(nskip, g_ref, u_ref, prev_ref, w_ref, a_ref, new_ref, buf)


def _cast_cols_kernel(x_ref, o_ref):
    n = x_ref.shape[1]
    o_ref[:, :n] = x_ref[...].astype(BF16)
    if o_ref.shape[1] > n:
        o_ref[:, n:] = jnp.zeros((o_ref.shape[0], o_ref.shape[1] - n), BF16)


def _cast_pad_cols(w, layer, n_out):
    _, k, n = w.shape
    tr = _pick(k, (128, 64))
    return pl.pallas_call(
        _cast_cols_kernel,
        grid=(k // tr,),
        in_specs=[pl.BlockSpec((None, tr, n), lambda i: (layer, i, 0))],
        out_specs=pl.BlockSpec((tr, n_out), lambda i: (i, 0)),
        out_shape=jax.ShapeDtypeStruct((k, n_out), BF16),
        compiler_params=_cp(1), name="cast_cols",
    )(w)


def _cast_rows_kernel(n_full, x_ref, o_ref):
    @pl.when(pl.program_id(0) < n_full)
    def _():
        o_ref[...] = x_ref[...].astype(BF16)

    @pl.when(pl.program_id(0) >= n_full)
    def _():
        o_ref[...] = jnp.zeros(o_ref.shape, BF16)


def _cast_pad_rows(w, layer, k_out):
    _, k, n = w.shape
    tr = 256
    assert k % tr == 0 and k_out % tr == 0
    n_full = k // tr
    return pl.pallas_call(
        functools.partial(_cast_rows_kernel, n_full),
        grid=(k_out // tr,),
        in_specs=[pl.BlockSpec((None, tr, n), lambda i: (layer, jnp.minimum(i, n_full - 1), 0))],
        out_specs=pl.BlockSpec((tr, n), lambda i: (i, 0)),
        out_shape=jax.ShapeDtypeStruct((k_out, n), BF16),
        compiler_params=_cp(1), name="cast_rows",
    )(w)


def _swap_half(w):
    half = w.shape[-1] // 2
    return jnp.concatenate([w[..., half:], w[..., :half]], axis=-1)


def _layer_weights(i, w_in, ssd_conv_w, ssd_conv_b, ssd_dt_bias, ssd_a_log, ssd_d, ssd_norm, sc_conv_w,
                   mla_q_norm, mla_w_uq, mla_kv_norm, mla_w_ukv, w_out, ffn_w_gate, ffn_w_up, ffn_conv_w,
                   ffn_w_down):
    w = w_in[i]
    o_dt = SSD_DI + SSD_CD
    o_sc = o_dt + SSD_H
    o_mla = o_sc + 3 * SC_D
    o_kr = o_mla + QL + KVL
    kr = w[:, o_kr:o_kr + ROPE]
    win = jnp.concatenate([p.astype(BF16) for p in (
        w[:, :o_dt], w[:, o_sc:o_mla], w[:, o_mla:o_kr], kr, _swap_half(kr), w[:, o_dt:o_sc],
        jnp.zeros((D_MODEL, LANE - SSD_H), F32))], axis=1)
    uq = mla_w_uq[i].reshape(QL, MLA_H, NOPE + ROPE)
    pe = uq[..., NOPE:]
    zq = jnp.zeros((QL, MLA_H, HK - NOPE - ROPE), F32)
    wa = jnp.concatenate([uq[..., :NOPE], pe, zq], axis=-1).reshape(QL, MLA_H * HK).astype(BF16)
    wb = jnp.concatenate([_swap_half(pe), zq], axis=-1).reshape(QL, MLA_H * LANE).astype(BF16)
    ukv = mla_w_ukv[i].reshape(KVL, MLA_H, NOPE + MLA_V)
    padf = ((0, 0), (0, DFP - D_FF))
    return dict(
        win=win,
        cw=ssd_conv_w[i], cb=ssd_conv_b[i].reshape(1, SSD_CD),
        dtb=jnp.pad(ssd_dt_bias[i], (0, LANE - SSD_H)).reshape(1, LANE),
        alog=jnp.pad(ssd_a_log[i], (0, LANE - SSD_H)).reshape(1, LANE),
        dsk=jnp.repeat(ssd_d[i], SSD_P).reshape(1, SSD_DI),
        ng=ssd_norm[i].reshape(1, SSD_DI),
        scw=sc_conv_w[i],
        qn=mla_q_norm[i].reshape(1, QL), kvn=mla_kv_norm[i].reshape(1, KVL),
        wa=wa, wb=wb,
        wk=ukv[..., :NOPE].reshape(KVL, MLA_H * NOPE).astype(BF16),
        wv=ukv[..., NOPE:].reshape(KVL, MLA_H * MLA_V).astype(BF16),
        wkt=jnp.transpose(ukv[..., :NOPE], (1, 2, 0)).astype(BF16),
        wv3=jnp.transpose(ukv[..., NOPE:], (1, 0, 2)).astype(BF16),
        wout=_cast_pad_cols(w_out, i, D_MODEL),
        wg=_cast_pad_cols(ffn_w_gate, i, DFP),
        wu=_cast_pad_cols(ffn_w_up, i, DFP),
        fcw=jnp.pad(ffn_conv_w[i], padf),
        wd=_cast_pad_rows(ffn_w_down, i, DFP),
    )


def _rope_table(lp, ns, past):
    half = ROPE // 2
    pos = jnp.concatenate([jnp.maximum(jnp.arange(lp, dtype=jnp.int32) - NSKIP, 0),
                           N_META + past + jnp.arange(ns, dtype=jnp.int32) % CH])
    inv = THETA ** (-jnp.arange(half, dtype=F32) / half)
    ang = pos.astype(F32)[:, None] * inv[None, :]
    cos, sin = jnp.cos(ang), jnp.sin(ang)
    return jnp.concatenate([cos, cos, -sin, sin], axis=1)


def kernel(x_prompt, x_sample, cache_kv_latent, cache_k_rope, state_ssm, state_ssd_conv, state_sconv, state_ffn_conv, meta_tokens, norm_mix_pre, norm_mix_post, norm_ffn_pre, norm_ffn_post, w_in, ssd_conv_w, ssd_conv_b, ssd_dt_bias, ssd_a_log, ssd_d, ssd_norm, sc_conv_w, mla_q_norm, mla_w_uq, mla_kv_norm, mla_w_ukv, w_out, ffn_w_gate, ffn_w_up, ffn_conv_w, ffn_w_down):
    bp, seq, d = x_prompt.shape
    nb, ls, _ = x_sample.shape
    depth, _, past, _ = cache_kv_latent.shape
    assert bp == 1 and ls == CH and seq % CH == 0 and d == D_MODEL
    lp = CH + seq
    ns = nb * ls
    npc = lp // CH

    x = jnp.concatenate([jnp.zeros((NSKIP, d), F32), meta_tokens.astype(F32), x_prompt[0],
                         x_sample.reshape(ns, d)], axis=0)
    tab = _rope_table(lp, ns, past)
    zero_c = jnp.zeros((1, SSD_K - 1, SSD_CD), F32)
    zero_h = jnp.zeros((1, SSD_N, SSD_DI), F32)

    def state_in(s):
        return jnp.transpose(s, (0, 3, 1, 2)).reshape(s.shape[0], SSD_N, SSD_DI)

    def state_out(s):
        return jnp.transpose(s.reshape(s.shape[0], SSD_N, SSD_H, SSD_P), (0, 2, 3, 1))
    zero_s = jnp.zeros((1, SC_K - 1, SC_D), F32)
    zero_f = jnp.zeros((1, FFN_K - 1, DFP), F32)
    padf = ((0, 0), (0, 0), (0, DFP - D_FF))

    h = _norm_cast(x, norm_mix_pre[0])
    outs_p, outs_s = [], []
    for i in range(depth):
        lw = _layer_weights(i, w_in, ssd_conv_w, ssd_conv_b, ssd_dt_bias, ssd_a_log, ssd_d, ssd_norm,
                            sc_conv_w, mla_q_norm, mla_w_uq, mla_kv_norm, mla_w_ukv, w_out, ffn_w_gate,
                            ffn_w_up, ffn_conv_w, ffn_w_down)
        u, dtf = _inproj(h, lw["win"])

        yp, yscp, cnew_p, hnew_p, scnew_p = _mixer(u, dtf, 0, 1, npc, NSKIP, zero_c, zero_h, zero_s, lw)
        y_ssd, y_sc, cnew_s, hnew_s, scnew_s = _mixer(u, dtf, npc, nb, 1, 0, state_ssd_conv[i],
                                                      state_in(state_ssm[i]), state_sconv[i], lw,
                                                      keep=(yp, yscp))
        hnew_p, hnew_s = state_out(hnew_p), state_out(hnew_s)

        q, ckv, kpe, kcat, v = _mla_proj(u, tab, lw)
        y_mla = _cached_attn(q, ckv, kpe, cache_kv_latent, cache_k_rope, i, npc, lw, _flash(q, kcat, v, lp))

        mix = _outproj(y_ssd, y_sc, y_mla, lw["wout"])
        x, h = _resid_norm(x, mix, norm_mix_post[i], norm_ffn_pre[i])

        gate, up = _gateup(h, lw["wg"], lw["wu"])
        act, fnew_p = _act(gate, up, 0, 1, npc, NSKIP, zero_f, lw["fcw"])
        act, fnew_s = _act(gate, up, npc, nb, 1, 0, jnp.pad(state_ffn_conv[i], padf), lw["fcw"], act_in=act)
        f = _down(act, lw["wd"])
        x, h = _resid_norm(x, f, norm_ffn_post[i], norm_mix_pre[i + 1] if i + 1 < depth else None)

        outs_p.append((ckv[NSKIP:lp][None], kpe[NSKIP:lp, :ROPE][None], hnew_p, cnew_p, scnew_p,
                       fnew_p[:, :, :D_FF]))
        outs_s.append((ckv[lp:].reshape(nb, ls, KVL), kpe[lp:, :ROPE].reshape(nb, ls, ROPE), hnew_s, cnew_s,
                       scnew_s, fnew_s[:, :, :D_FF]))

    def stack(outs, j):
        return jnp.stack([o[j] for o in outs], axis=0)

    y_prompt = x[CH:lp][None]
    y_sample = x[lp:].reshape(nb, ls, d)
    return (y_prompt, y_sample,
            stack(outs_p, 0), stack(outs_p, 1), stack(outs_p, 2), stack(outs_p, 3), stack(outs_p, 4), stack(outs_p, 5),
            stack(outs_s, 0), stack(outs_s, 1), stack(outs_s, 2), stack(outs_s, 3), stack(outs_s, 4), stack(outs_s, 5))
```

```python
import functools

import jax
import jax.numpy as jnp
import numpy as np
from jax import lax
from jax.experimental import pallas as pl
from jax.experimental.pallas import tpu as pltpu

F32 = jnp.float32
BF16 = jnp.bfloat16

D_MODEL = 4096
N_META = 16
CH = 64
NSKIP = CH - N_META
EPS = 1e-6
SSD_P = 64
SSD_DI = 2048
SSD_H = 32
SSD_G = 8
SSD_N = 128
SSD_K = 4
SSD_CD = 4096
SC_D = 1024
SC_K = 3
MLA_H = 8
NOPE = 128
ROPE = 64
MLA_V = 128
QL = 768
KVL = 512
HK = 256
SCALE = (NOPE + ROPE) ** -0.5
QSCALE = SCALE * 1.4426950408889634
THETA = 10000.0
D_FF = 11008
DFP = 11264
FFN_K = 3
LANE = 128

C_Z, C_X, C_BC, C_SCB, C_SCC, C_SCH, C_MLA, NIN = 0, 2048, 4096, 6144, 7168, 8192, 9216, 10752
MLA_W = 1536

FLASH_TQ = 512
VMEM_LIMIT = 56 * 1024 * 1024
HI = lax.Precision.HIGHEST
NT = (((1,), (1,)), ((), ()))
TN = (((0,), (0,)), ((), ()))


def _cp(n, flags=None):
    return pltpu.CompilerParams(dimension_semantics=("arbitrary",) * n, vmem_limit_bytes=VMEM_LIMIT, flags=flags)


def _pick(n, cands):
    for c in cands:
        if n % c == 0:
            return c
    raise ValueError(f"no tile for {n}")


MM_ROWS = (1088, 544, 512, 272, 256, 136, 128, 64)
EW_ROWS = (272, 256, 136, 128, 64)


def _rms(x, g):
    return x * lax.rsqrt(jnp.mean(x * x, axis=-1, keepdims=True) + EPS) * g


def _silu(x):
    return x * jax.nn.sigmoid(x)


def _norm_cast_kernel(x_ref, g_ref, h_ref):
    h_ref[...] = _rms(x_ref[...], g_ref[...]).astype(BF16)


def _norm_cast(x, g):
    r, d = x.shape
    bm = _pick(r, EW_ROWS)
    return pl.pallas_call(
        _norm_cast_kernel,
        grid=(r // bm,),
        in_specs=[pl.BlockSpec((bm, d), lambda i: (i, 0)), pl.BlockSpec((1, d), lambda i: (0, 0))],
        out_specs=pl.BlockSpec((bm, d), lambda i: (i, 0)),
        out_shape=jax.ShapeDtypeStruct((r, d), BF16),
        compiler_params=_cp(1), name="norm_cast",
    )(x, g.reshape(1, d))


def _resid_norm_kernel(x_ref, m_ref, gp_ref, gn_ref, x2_ref, h_ref):
    x2 = x_ref[...] + _rms(m_ref[...], gp_ref[...])
    x2_ref[...] = x2
    h_ref[...] = _rms(x2, gn_ref[...]).astype(BF16)


def _resid_kernel(x_ref, m_ref, gp_ref, x2_ref):
    x2_ref[...] = x_ref[...] + _rms(m_ref[...], gp_ref[...])


def _resid_norm(x, m, g_post, g_next):
    r, d = x.shape
    bm = _pick(r, EW_ROWS)
    row = pl.BlockSpec((bm, d), lambda i: (i, 0))
    vec = pl.BlockSpec((1, d), lambda i: (0, 0))
    if g_next is None:
        return pl.pallas_call(
            _resid_kernel, grid=(r // bm,), in_specs=[row, row, vec], out_specs=row,
            out_shape=jax.ShapeDtypeStruct((r, d), F32), compiler_params=_cp(1), name="resid",
        )(x, m, g_post.reshape(1, d)), None
    return pl.pallas_call(
        _resid_norm_kernel, grid=(r // bm,), in_specs=[row, row, vec, vec], out_specs=[row, row],
        out_shape=[jax.ShapeDtypeStruct((r, d), F32), jax.ShapeDtypeStruct((r, d), BF16)],
        compiler_params=_cp(1), name="resid_norm",
    )(x, m, g_post.reshape(1, d), g_next.reshape(1, d))


def _inproj_kernel(x_ref, w_ref, u_ref, dt_ref):
    acc = jnp.dot(x_ref[...], w_ref[...], preferred_element_type=F32)
    u_ref[...] = acc.astype(BF16)

    @pl.when(pl.program_id(1) == pl.num_programs(1) - 1)
    def _():
        dt_ref[...] = acc[:, acc.shape[1] - LANE:]


def _inproj(h, w):
    r, k = h.shape
    n = w.shape[1]
    bm = _pick(r, MM_ROWS)
    bn = 768
    return pl.pallas_call(
        _inproj_kernel,
        grid=(r // bm, n // bn),
        in_specs=[pl.BlockSpec((bm, k), lambda i, j: (i, 0)), pl.BlockSpec((k, bn), lambda i, j: (0, j))],
        out_specs=[pl.BlockSpec((bm, bn), lambda i, j: (i, j)), pl.BlockSpec((bm, LANE), lambda i, j: (i, 0))],
        out_shape=[jax.ShapeDtypeStruct((r, n), BF16), jax.ShapeDtypeStruct((r, LANE), F32)],
        compiler_params=_cp(2), name="inproj",
    )(h, w)


def _outproj_kernel(a_ref, b_ref, c_ref, wa_ref, wb_ref, wc_ref, o_ref):
    acc = jnp.dot(a_ref[...], wa_ref[...], preferred_element_type=F32)
    acc = acc + jnp.dot(b_ref[...], wb_ref[...], preferred_element_type=F32)
    acc = acc + jnp.dot(c_ref[...], wc_ref[...], preferred_element_type=F32)
    o_ref[...] = acc


def _outproj(y_ssd, y_sc, y_mla, w):
    r = y_ssd.shape[0]
    n = w.shape[1]
    bm = _pick(r, MM_ROWS)
    bn = 1024
    return pl.pallas_call(
        _outproj_kernel,
        grid=(r // bm, n // bn),
        in_specs=[pl.BlockSpec((bm, SSD_DI), lambda i, j: (i, 0)),
                  pl.BlockSpec((bm, SC_D), lambda i, j: (i, 0)),
                  pl.BlockSpec((bm, SC_D), lambda i, j: (i, 0)),
                  pl.BlockSpec((SSD_DI, bn), lambda i, j: (0, j)),
                  pl.BlockSpec((SC_D, bn), lambda i, j: (2, j)),
                  pl.BlockSpec((SC_D, bn), lambda i, j: (3, j))],
        out_specs=pl.BlockSpec((bm, bn), lambda i, j: (i, j)),
        out_shape=jax.ShapeDtypeStruct((r, n), F32),
        compiler_params=_cp(2), name="outproj",
    )(y_ssd, y_sc, y_mla, w, w, w)


def _gateup_kernel(x_ref, wg_ref, wu_ref, g_ref, u_ref):
    x = x_ref[...]
    g_ref[...] = jnp.dot(x, wg_ref[...], preferred_element_type=F32).astype(BF16)
    u_ref[...] = jnp.dot(x, wu_ref[...], preferred_element_type=F32).astype(BF16)


def _gateup(h, wg, wu):
    r, k = h.shape
    n = wg.shape[1]
    bm = _pick(r, MM_ROWS)
    bn = 512
    wspec = pl.BlockSpec((k, bn), lambda i, j: (0, j))
    ospec = pl.BlockSpec((bm, bn), lambda i, j: (i, j))
    return pl.pallas_call(
        _gateup_kernel,
        grid=(r // bm, n // bn),
        in_specs=[pl.BlockSpec((bm, k), lambda i, j: (i, 0)), wspec, wspec],
        out_specs=[ospec, ospec],
        out_shape=[jax.ShapeDtypeStruct((r, n), BF16)] * 2,
        compiler_params=_cp(2), name="gateup",
    )(h, wg, wu)


def _down_kernel(x_ref, w_ref, o_ref):
    p = jnp.dot(x_ref[...], w_ref[...], preferred_element_type=F32)

    @pl.when(pl.program_id(2) == 0)
    def _():
        o_ref[...] = p

    @pl.when(pl.program_id(2) > 0)
    def _():
        o_ref[...] += p


def _down(a, w):
    r, k = a.shape
    n = w.shape[1]
    bm = _pick(r, MM_ROWS)
    bn = 1024
    bk = k // 4
    return pl.pallas_call(
        _down_kernel,
        grid=(r // bm, n // bn, k // bk),
        in_specs=[pl.BlockSpec((bm, bk), lambda i, j, kk: (i, kk)), pl.BlockSpec((bk, bn), lambda i, j, kk: (kk, j))],
        out_specs=pl.BlockSpec((bm, bn), lambda i, j, kk: (i, j)),
        out_shape=jax.ShapeDtypeStruct((r, n), F32),
        compiler_params=_cp(3), name="down",
    )(a, w)


def _mixer_kernel(nskip, z_ref, x_ref, bc_ref, dt_ref, scb_ref, scc_ref, sch_ref,
                  cprev_ref, h0_ref, scprev_ref,
                  cw_ref, cb_ref, dtb_ref, alog_ref, dsk_ref, ng_ref, scw_ref, pm_ref,
                  y_ref, ysc_ref, cnew_ref, hnew_ref, scnew_ref,
                  buf, sbuf, hst):
    t = CH
    c = pl.program_id(1)
    last = pl.num_programs(1) - 1

    ns = SSD_K - 1

    @pl.when(c == 0)
    def _init():
        _load_tail(buf, cprev_ref[0])
        sbuf[0:8, :] = jnp.zeros((8, SC_D), F32)
        sbuf[8 - (SC_K - 1):8, :] = scprev_ref[0]
        hst[...] = h0_ref[0]

    rows = lax.broadcasted_iota(jnp.int32, (t, 1), 0) + c * t
    valid = rows >= nskip

    xin = jnp.concatenate([x_ref[...], bc_ref[...]], axis=1)
    xin = jnp.where(valid, xin, jnp.zeros((), xin.dtype))
    delayed = _delayed(pm_ref, buf, xin, ns)
    acc = cb_ref[...] + cw_ref[ns:ns + 1, :] * xin.astype(F32)
    for d in range(1, ns + 1):
        acc = acc + cw_ref[ns - d:ns - d + 1, :] * delayed[d - 1]
    xbc = _silu(acc)

    @pl.when(c == last)
    def _():
        for r in range(ns):
            cnew_ref[0, r:r + 1, :] = buf[3 * r:3 * r + 1, :]

    lane = lax.broadcasted_iota(jnp.int32, (t, LANE), 1)
    v = dt_ref[...] + dtb_ref[...]
    dt = jnp.maximum(v, 0.0) + jnp.log1p(jnp.exp(-jnp.abs(v)))
    dt = jnp.where(valid & (lane < SSD_H), dt, 0.0)
    adt = dt * (-jnp.exp(alog_ref[...]))
    ri = lax.broadcasted_iota(jnp.int32, (t, t), 0)
    ci = lax.broadcasted_iota(jnp.int32, (t, t), 1)
    tri = ri >= ci
    acs = jnp.dot(tri.astype(F32), adt, precision=HI, preferred_element_type=F32)

    hpg = SSD_H // SSD_G
    gw = hpg * SSD_P
    lane_g = lax.broadcasted_iota(jnp.int32, (t, gw), 1)
    row_g = lax.broadcasted_iota(jnp.int32, (t, gw), 0)
    seg = lane_g // SSD_P
    eye_g = row_g == lane_g % SSD_P
    tri_g = row_g >= lane_g % SSD_P
    blockdiag = (lax.broadcasted_iota(jnp.int32, (hpg * t, gw), 0) // t
                 == lax.broadcasted_iota(jnp.int32, (hpg * t, gw), 1) // SSD_P)

    def per_head_lanes(mat, g):
        out = jnp.broadcast_to(mat[:, g * hpg + hpg - 1:g * hpg + hpg], (t, gw))
        for r in range(hpg - 2, -1, -1):
            out = jnp.where(seg == r, jnp.broadcast_to(mat[:, g * hpg + r:g * hpg + r + 1], (t, gw)), out)
        return out

    for g in range(SSD_G):
        cols = slice(g * gw, (g + 1) * gw)
        bg = xbc[:, SSD_DI + g * SSD_N:SSD_DI + (g + 1) * SSD_N].astype(BF16)
        cg = xbc[:, SSD_DI + SSD_G * SSD_N + g * SSD_N:SSD_DI + SSD_G * SSD_N + (g + 1) * SSD_N].astype(BF16)
        acol = per_head_lanes(acs, g)
        dtx = per_head_lanes(dt, g)
        arow = jnp.sum(jnp.where(eye_g, acol, 0.0), axis=0, keepdims=True)
        decay = jnp.exp(jnp.where(tri_g, acol - arow, -jnp.inf))
        cb = lax.dot_general(cg, jnp.concatenate([bg] * hpg, axis=0), NT, preferred_element_type=F32)
        xg = xbc[:, cols]
        xdt = xg * dtx
        xdtb = xdt.astype(BF16)
        rhs = jnp.where(blockdiag, jnp.concatenate([xdtb] * hpg, axis=0), jnp.zeros((), BF16))
        ydiag = jnp.dot((cb * decay).astype(BF16), rhs, preferred_element_type=F32)
        hg = hst[:, cols]
        yoff = jnp.dot(cg, hg.astype(BF16), preferred_element_type=F32) * jnp.exp(acol)
        alast = acol[t - 1:t, :]
        snew = lax.dot_general(bg, (xdt * jnp.exp(alast - acol)).astype(BF16), TN, preferred_element_type=F32)
        hst[:, cols] = jnp.exp(alast) * hg + snew
        y = ydiag + yoff + dsk_ref[:, cols] * xg
        y = y * _silu(z_ref[:, cols].astype(F32))
        y_ref[:, cols] = _rms(y, ng_ref[:, cols]).astype(BF16)

    @pl.when(c == last)
    def _():
        hnew_ref[0] = hst[...]

    p = scc_ref[...].astype(F32) * sch_ref[...].astype(F32)
    p = jnp.where(valid, p, 0.0)
    sbuf[8:8 + t, :] = p
    conv = scw_ref[0:1, :] * sbuf[6:6 + t, :]
    for i in range(1, SC_K):
        conv = conv + scw_ref[i:i + 1, :] * sbuf[6 + i:6 + i + t, :]
    ysc_ref[...] = (scb_ref[...].astype(F32) * conv).astype(BF16)
    sbuf[0:8, :] = sbuf[t:t + 8, :]

    @pl.when(c == last)
    def _():
        scnew_ref[0] = sbuf[8 - (SC_K - 1):8, :]


def _mixer_alias_kernel(nskip, *refs):
    n_in = 18
    _mixer_kernel(nskip, *refs[:n_in], *refs[n_in + 2:])


def _mixer(u, dtf, rb0, nb, nch, nskip, cprev, h0, scprev, lw, keep=None):
    r = u.shape[0]
    t = CH

    def rowmap(cb):
        return lambda b, c: (rb0 + b * nch + c, cb)

    def stmap(b, c):
        return (b, 0, 0)

    def full(shape):
        return pl.BlockSpec(shape, lambda b, c: (0,) * len(shape))

    in_specs = [
        pl.BlockSpec((t, SSD_DI), rowmap(C_Z // SSD_DI)),
        pl.BlockSpec((t, SSD_DI), rowmap(C_X // SSD_DI)),
        pl.BlockSpec((t, SSD_DI), rowmap(C_BC // SSD_DI)),
        pl.BlockSpec((t, LANE), rowmap(0)),
        pl.BlockSpec((t, SC_D), rowmap(C_SCB // SC_D)),
        pl.BlockSpec((t, SC_D), rowmap(C_SCC // SC_D)),
        pl.BlockSpec((t, SC_D), rowmap(C_SCH // SC_D)),
        pl.BlockSpec((1, SSD_K - 1, SSD_CD), stmap),
        pl.BlockSpec((1, SSD_N, SSD_DI), stmap),
        pl.BlockSpec((1, SC_K - 1, SC_D), stmap),
        full((SSD_K, SSD_CD)), full((1, SSD_CD)), full((1, LANE)), full((1, LANE)),
        full((1, SSD_DI)), full((1, SSD_DI)), full((SC_K, SC_D)),
        full((CH * (SSD_K - 1) + TAIL, TAIL + CH)),
    ]
    out_specs = [
        pl.BlockSpec((t, SSD_DI), rowmap(0)),
        pl.BlockSpec((t, SC_D), rowmap(0)),
        pl.BlockSpec((1, SSD_K - 1, SSD_CD), stmap),
        pl.BlockSpec((1, SSD_N, SSD_DI), stmap),
        pl.BlockSpec((1, SC_K - 1, SC_D), stmap),
    ]
    out_shape = [
        jax.ShapeDtypeStruct((r, SSD_DI), BF16),
        jax.ShapeDtypeStruct((r, SC_D), BF16),
        jax.ShapeDtypeStruct((nb, SSD_K - 1, SSD_CD), F32),
        jax.ShapeDtypeStruct((nb, SSD_N, SSD_DI), F32),
        jax.ShapeDtypeStruct((nb, SC_K - 1, SC_D), F32),
    ]
    args = [u, u, u, dtf, u, u, u, cprev, h0, scprev,
            lw["cw"], lw["cb"], lw["dtb"], lw["alog"], lw["dsk"], lw["ng"], lw["scw"],
            jnp.asarray(_shift_matrix(SSD_K), BF16)]
    kern = functools.partial(_mixer_kernel, nskip)
    aliases = {}
    if keep is not None:
        aliases = {len(args): 0, len(args) + 1: 1}
        args += list(keep)
        in_specs += [pl.BlockSpec(memory_space=pl.ANY)] * 2
        kern = functools.partial(_mixer_alias_kernel, nskip)
    return pl.pallas_call(
        kern,
        grid=(nb, nch),
        in_specs=in_specs, out_specs=out_specs, out_shape=out_shape,
        scratch_shapes=[pltpu.VMEM((TAIL, SSD_CD), F32), pltpu.VMEM((8 + t, SC_D), F32),
                        pltpu.VMEM((SSD_N, SSD_DI), F32)],
        input_output_aliases=aliases,
        compiler_params=_cp(2), name="mixer",
    )(*args)


def _mla_proj_kernel(blk_ref, tab_ref, qn_ref, kvn_ref, wa_ref, wb_ref, wk_ref, wv_ref,
                     q_ref, ckv_ref, kpe_ref, kcat_ref, v_ref):
    blk = blk_ref[...]
    cq = _rms(blk[:, :QL].astype(F32), qn_ref[...]).astype(BF16)
    ckv = _rms(blk[:, QL:QL + KVL].astype(F32), kvn_ref[...])
    ckv_ref[...] = ckv
    tab = tab_ref[...]
    tabr = pltpu.roll(tab, ROPE, 1)
    prod = blk[:, QL + KVL:QL + KVL + LANE].astype(F32) * tab
    ksum = prod + pltpu.roll(prod, ROPE, 1)
    lane = lax.broadcasted_iota(jnp.int32, ksum.shape, 1)
    kpe = jnp.where(lane < ROPE, ksum, 0.0)
    kpe_ref[...] = kpe
    qa = jnp.dot(cq, wa_ref[...], preferred_element_type=F32)
    qb = jnp.dot(cq, wb_ref[...], preferred_element_type=F32)
    ckvb = ckv.astype(BF16)
    kn = jnp.dot(ckvb, wk_ref[...], preferred_element_type=F32)
    v_ref[...] = jnp.dot(ckvb, wv_ref[...], preferred_element_type=F32).astype(BF16)
    kpeb = kpe.astype(BF16)
    tab_q = tab * QSCALE
    tabr_q = tabr * QSCALE
    for h in range(MLA_H):
        q_ref[:, h * HK:h * HK + NOPE] = (qa[:, h * HK:h * HK + NOPE] * QSCALE).astype(BF16)
        q_ref[:, h * HK + NOPE:(h + 1) * HK] = (
            qa[:, h * HK + NOPE:(h + 1) * HK] * tab_q + qb[:, h * LANE:(h + 1) * LANE] * tabr_q).astype(BF16)
        kcat_ref[:, h * HK:h * HK + NOPE] = kn[:, h * NOPE:(h + 1) * NOPE].astype(BF16)
        kcat_ref[:, h * HK + NOPE:(h + 1) * HK] = kpeb


def _mla_proj(u, tab, lw):
    r = u.shape[0]
    bm = _pick(r, (544, 512, 272, 256, 136, 128, 64))

    def full(shape):
        return pl.BlockSpec(shape, lambda i: (0,) * len(shape))

    def row(w):
        return pl.BlockSpec((bm, w), lambda i: (i, 0))

    return pl.pallas_call(
        _mla_proj_kernel,
        grid=(r // bm,),
        in_specs=[pl.BlockSpec((bm, MLA_W), lambda i: (i, C_MLA // MLA_W)), row(LANE),
                  full((1, QL)), full((1, KVL)), full((QL, MLA_H * HK)), full((QL, MLA_H * LANE)),
                  full((KVL, MLA_H * NOPE)), full((KVL, MLA_H * MLA_V))],
        out_specs=[row(MLA_H * HK), row(KVL), row(LANE), row(MLA_H * HK), row(MLA_H * MLA_V)],
        out_shape=[jax.ShapeDtypeStruct((r, MLA_H * HK), BF16), jax.ShapeDtypeStruct((r, KVL), F32),
                   jax.ShapeDtypeStruct((r, LANE), F32), jax.ShapeDtypeStruct((r, MLA_H * HK), BF16),
                   jax.ShapeDtypeStruct((r, MLA_H * MLA_V), BF16)],
        compiler_params=_cp(1), name="mla_proj",
    )(u, tab, lw["qn"], lw["kvn"], lw["wa"], lw["wb"], lw["wk"], lw["wv"])


def _flash_kernel(it_ref, jt_ref, mt_ref, q_ref, k_ref, v_ref, mask_ref, o_ref,
                  qt_s, m_s, l_s, acc_s, s_scr, p_scr, a_s):
    tq = q_ref.shape[0]
    tk = k_ref.shape[0]
    step = pl.program_id(0)
    i = it_ref[step]
    j = jt_ref[step]

    @pl.when(j == 0)
    def _init():
        for h in range(MLA_H):
            qt_s[h * HK:(h + 1) * HK, :] = q_ref[:, h * HK:(h + 1) * HK].T
        m_s[...] = jnp.full(m_s.shape, -jnp.inf, F32)
        l_s[...] = jnp.zeros(l_s.shape, F32)
        acc_s[...] = jnp.zeros(acc_s.shape, F32)

    def tile(masked):
        nkb = tk // CH

        def block(sref, kb):
            blk = sref[kb * CH:(kb + 1) * CH, :]
            if masked:
                blk = blk + mask_ref[0, kb * CH:(kb + 1) * CH, :]
            return blk

        def scores(h):
            s_scr[h % 2] = jnp.dot(k_ref[:, h * HK:(h + 1) * HK], qt_s[h * HK:(h + 1) * HK, :],
                                   preferred_element_type=F32)

        def softmax(h):
            sref = s_scr.at[h % 2]
            pref = p_scr.at[h % 2]
            mx = block(sref, 0).reshape(CH // 8, 8, tq).max(axis=0)
            for kb in range(1, nkb):
                mx = jnp.maximum(mx, block(sref, kb).reshape(CH // 8, 8, tq).max(axis=0))
            m_prev = m_s[h]
            m_new = jnp.maximum(m_prev, jnp.max(mx, axis=0, keepdims=True))
            alpha = jnp.exp2(m_prev - m_new)
            part = jnp.zeros((8, tq), F32)
            for kb in range(nkb):
                e = jnp.exp2(block(sref, kb) - m_new)
                part = part + jnp.sum(e.reshape(CH // 8, 8, tq), axis=0)
                pref[kb * CH:(kb + 1) * CH, :] = e.astype(BF16)
            l_s[h] = alpha * l_s[h] + jnp.sum(part, axis=0, keepdims=True)
            m_s[h] = m_new
            a_s[h % 2] = alpha

        def weighted_values(h):
            rows = slice(h * MLA_V, (h + 1) * MLA_V)
            acc_s[rows, :] = a_s[h % 2] * acc_s[rows, :] + lax.dot_general(
                v_ref[:, rows], p_scr[h % 2], TN, preferred_element_type=F32)

        scores(0)
        for h in range(MLA_H + 1):
            if h + 1 < MLA_H:
                scores(h + 1)
            if h >= 1:
                weighted_values(h - 1)
            if h < MLA_H:
                softmax(h)

    @pl.when((j > 0) & (j < i))
    def _():
        tile(False)

    @pl.when((j == 0) | (j == i))
    def _():
        tile(True)

    @pl.when(j == i)
    def _():
        for h in range(MLA_H):
            rows = slice(h * MLA_V, (h + 1) * MLA_V)
            o_ref[:, rows] = (acc_s[rows, :] / l_s[h]).T.astype(BF16)


def _flash(q, kcat, v, lp):
    r = q.shape[0]
    tq = FLASH_TQ
    nq = -(-lp // tq)
    assert nq * tq <= r
    it = np.concatenate([np.full((i + 1,), i, np.int32) for i in range(nq)])
    jt = np.concatenate([np.arange(i + 1, dtype=np.int32) for i in range(nq)])
    kk = np.arange(tq)[:, None]
    qq = np.arange(tq)[None, :]
    pad = np.where(kk + 0 * qq < NSKIP, -np.inf, 0.0)
    diag = np.where(kk // CH > qq // CH, -np.inf, 0.0)
    masks = np.stack([pad, diag, pad + diag]).astype(np.float32)
    mt = np.where(jt == it, np.where(jt == 0, 2, 1), 0).astype(np.int32)
    grid_spec = pltpu.PrefetchScalarGridSpec(
        num_scalar_prefetch=3,
        grid=(int(it.shape[0]),),
        in_specs=[pl.BlockSpec((tq, MLA_H * HK), lambda s, it, jt, mt: (it[s], 0)),
                  pl.BlockSpec((tq, MLA_H * HK), lambda s, it, jt, mt: (jt[s], 0)),
                  pl.BlockSpec((tq, MLA_H * MLA_V), lambda s, it, jt, mt: (jt[s], 0)),
                  pl.BlockSpec((1, tq, tq), lambda s, it, jt, mt: (mt[s], 0, 0))],
        out_specs=pl.BlockSpec((tq, MLA_H * MLA_V), lambda s, it, jt, mt: (it[s], 0)),
        scratch_shapes=[pltpu.VMEM((MLA_H * HK, tq), BF16),
                        pltpu.VMEM((MLA_H, 1, tq), F32), pltpu.VMEM((MLA_H, 1, tq), F32),
                        pltpu.VMEM((MLA_H * MLA_V, tq), F32),
                        pltpu.VMEM((2, tq, tq), F32), pltpu.VMEM((2, tq, tq), BF16),
                        pltpu.VMEM((2, 1, tq), F32)],
    )
    return pl.pallas_call(
        _flash_kernel,
        grid_spec=grid_spec,
        out_shape=jax.ShapeDtypeStruct((r, MLA_H * MLA_V), BF16),
        compiler_params=_cp(1), name="flash",
    )(jnp.asarray(it), jnp.asarray(jt), jnp.asarray(mt), q, kcat, v, jnp.asarray(masks))


def _cached_attn_kernel(q_ref, cnew_ref, pnew_ref, ckv_ref, cpe_ref, wkt_ref, wv_ref, o_ref, call, peall):
    past = ckv_ref.shape[2]
    call[0:past, :] = ckv_ref[0, 0].astype(BF16)
    call[past:past + CH, :] = cnew_ref[...].astype(BF16)
    peall[...] = jnp.zeros(peall.shape, BF16)
    peall[0:past, 0:ROPE] = cpe_ref[0, 0].astype(BF16)
    peall[past:past + CH, :] = pnew_ref[...].astype(BF16)
    qlat = []
    qpe = []
    for h in range(MLA_H):
        qn = q_ref[:, h * HK:h * HK + NOPE]
        qlat.append(jnp.dot(qn, wkt_ref[h], preferred_element_type=F32).astype(BF16))
        qpe.append(q_ref[:, h * HK + NOPE:(h + 1) * HK])
    qlat = jnp.concatenate(qlat, axis=0)
    qpe = jnp.concatenate(qpe, axis=0)
    s = (lax.dot_general(qlat, call[...], NT, preferred_element_type=F32)
         + lax.dot_general(qpe, peall[...], NT, preferred_element_type=F32))
    p = jnp.exp2(s - jnp.max(s, axis=-1, keepdims=True))
    p = p / jnp.sum(p, axis=-1, keepdims=True)
    olat = jnp.dot(p.astype(BF16), call[...], preferred_element_type=F32).astype(BF16)
    for h in range(MLA_H):
        o_ref[:, h * MLA_V:(h + 1) * MLA_V] = jnp.dot(
            olat[h * CH:(h + 1) * CH, :], wv_ref[h], preferred_element_type=F32).astype(BF16)


def _cached_attn_alias_kernel(q_ref, cnew_ref, pnew_ref, ckv_ref, cpe_ref, wkt_ref, wv_ref, keep_ref, o_ref,
                              call, peall):
    del keep_ref
    _cached_attn_kernel(q_ref, cnew_ref, pnew_ref, ckv_ref, cpe_ref, wkt_ref, wv_ref, o_ref, call, peall)


def _cached_attn(q, ckv, kpe, cache_kv, cache_pe, layer, rb0, lw, o_prompt):
    nb, past = cache_kv.shape[1], cache_kv.shape[2]

    def rowmap(b):
        return (rb0 + b, 0)

    return pl.pallas_call(
        _cached_attn_alias_kernel,
        grid=(nb,),
        in_specs=[pl.BlockSpec((CH, MLA_H * HK), rowmap),
                  pl.BlockSpec((CH, KVL), rowmap),
                  pl.BlockSpec((CH, LANE), rowmap),
                  pl.BlockSpec((1, 1, past, KVL), lambda b: (layer, b, 0, 0)),
                  pl.BlockSpec((1, 1, past, ROPE), lambda b: (layer, b, 0, 0)),
                  pl.BlockSpec((MLA_H, NOPE, KVL), lambda b: (0, 0, 0)),
                  pl.BlockSpec((MLA_H, KVL, MLA_V), lambda b: (0, 0, 0)),
                  pl.BlockSpec(memory_space=pl.ANY)],
        out_specs=pl.BlockSpec((CH, MLA_H * MLA_V), rowmap),
        out_shape=jax.ShapeDtypeStruct(o_prompt.shape, BF16),
        scratch_shapes=[pltpu.VMEM((past + CH, KVL), BF16), pltpu.VMEM((past + CH, LANE), BF16)],
        input_output_aliases={7: 0},
        compiler_params=_cp(1), name="cached_attn",
    )(q, ckv, kpe, cache_kv, cache_pe, lw["wkt"], lw["wv3"], o_prompt)


TAIL = 16


def _shift_matrix(k):
    ns = k - 1
    pm = np.zeros((CH * ns + TAIL, TAIL + CH), np.float32)
    for d in range(1, ns + 1):
        for t in range(CH):
            if t - d >= 0:
                pm[(d - 1) * CH + t, TAIL + t - d] = 1.0
            else:
                r = ns + t - d
                pm[(d - 1) * CH + t, 3 * r:3 * r + 3] = 1.0
    for r in range(ns):
        pm[CH * ns + 3 * r, TAIL + CH - ns + r] = 1.0
    return pm


def _load_tail(tail_s, prev):
    hi = prev.astype(BF16).astype(F32)
    rest = prev - hi
    mid = rest.astype(BF16).astype(F32)
    lo = (rest - mid).astype(BF16).astype(F32)
    tail_s[...] = jnp.zeros(tail_s.shape, F32)
    for r in range(prev.shape[0]):
        tail_s[3 * r:3 * r + 1, :] = hi[r:r + 1]
        tail_s[3 * r + 1:3 * r + 2, :] = mid[r:r + 1]
        tail_s[3 * r + 2:3 * r + 3, :] = lo[r:r + 1]


def _delayed(pm_ref, tail_s, x, ns):
    ext = jnp.concatenate([tail_s[...].astype(BF16), x], axis=0)
    out = jnp.dot(pm_ref[...], ext, preferred_element_type=F32)
    tail_s[...] = out[CH * ns:, :]
    return [out[(d - 1) * CH:d * CH, :] for d in range(1, ns + 1)]


def _act_kernel(nskip, g_ref, u_ref, prev_ref, w_ref, pm_ref, a_ref, new_ref, tail_s):
    t = CH
    ns = FFN_K - 1
    c = pl.program_id(1)

    @pl.when(c == 0)
    def _():
        _load_tail(tail_s, prev_ref[0])

    rows = lax.broadcasted_iota(jnp.int32, (t, 1), 0) + c * t
    g = g_ref[...]
    g = jnp.where(rows >= nskip, g, jnp.zeros((), g.dtype))
    delayed = _delayed(pm_ref, tail_s, g, ns)
    conv = w_ref[ns:ns + 1, :] * g.astype(F32)
    for d in range(1, ns + 1):
        conv = conv + w_ref[ns - d:ns - d + 1, :] * delayed[d - 1]
    a_ref[...] = (_silu(conv) * u_ref[...].astype(F32)).astype(BF16)

    @pl.when(c == pl.num_programs(1) - 1)
    def _():
        for r in range(ns):
            new_ref[0, r:r + 1, :] = tail_s[3 * r:3 * r + 1, :]


def _act(gate, up, rb0, nb, nch, nskip, prev, w, act_in=None):
    r, n = gate.shape
    t = CH

    def rowmap(b, c):
        return (rb0 + b * nch + c, 0)

    pm = jnp.asarray(_shift_matrix(FFN_K), BF16)
    args = [gate, up, prev, w, pm]
    in_specs = [pl.BlockSpec((t, n), rowmap), pl.BlockSpec((t, n), rowmap),
                pl.BlockSpec((1, FFN_K - 1, n), lambda b, c: (b, 0, 0)),
                pl.BlockSpec((FFN_K, n), lambda b, c: (0, 0)),
                pl.BlockSpec(pm.shape, lambda b, c: (0, 0))]
    kern = functools.partial(_act_kernel, nskip)
    aliases = {}
    if act_in is not None:
        aliases = {len(args): 0}
        args.append(act_in)
        in_specs.append(pl.BlockSpec(memory_space=pl.ANY))
        kern = functools.partial(_act_alias_kernel, nskip)
    return pl.pallas_call(
        kern,
        grid=(nb, nch),
        in_specs=in_specs,
        out_specs=[pl.BlockSpec((t, n), rowmap), pl.BlockSpec((1, FFN_K - 1, n), lambda b, c: (b, 0, 0))],
        out_shape=[jax.ShapeDtypeStruct((r, n), BF16), jax.ShapeDtypeStruct((nb, FFN_K - 1, n), F32)],
        scratch_shapes=[pltpu.VMEM((TAIL, n), F32)],
        input_output_aliases=aliases,
        compiler_params=_cp(2), name="act",
    )(*args)


def _act_alias_kernel(nskip, g_ref, u_ref, prev_ref, w_ref, pm_ref, keep_ref, a_ref, new_ref, tail_s):
    del keep_ref
    _act_kernel(nskip, g_ref, u_ref, prev_ref, w_ref, pm_ref, a_ref, new_ref, tail_s)


def _cast_cols_kernel(x_ref, o_ref):
    n = x_ref.shape[1]
    o_ref[:, :n] = x_ref[...].astype(BF16)
    if o_ref.shape[1] > n:
        o_ref[:, n:] = jnp.zeros((o_ref.shape[0], o_ref.shape[1] - n), BF16)


def _cast_pad_cols(w, layer, n_out):
    _, k, n = w.shape
    tr = _pick(k, (128, 64))
    return pl.pallas_call(
        _cast_cols_kernel,
        grid=(k // tr,),
        in_specs=[pl.BlockSpec((None, tr, n), lambda i: (layer, i, 0))],
        out_specs=pl.BlockSpec((tr, n_out), lambda i: (i, 0)),
        out_shape=jax.ShapeDtypeStruct((k, n_out), BF16),
        compiler_params=_cp(1), name="cast_cols",
    )(w)


def _cast_rows_kernel(n_full, x_ref, o_ref):
    @pl.when(pl.program_id(0) < n_full)
    def _():
        o_ref[...] = x_ref[...].astype(BF16)

    @pl.when(pl.program_id(0) >= n_full)
    def _():
        o_ref[...] = jnp.zeros(o_ref.shape, BF16)


def _cast_pad_rows(w, layer, k_out):
    _, k, n = w.shape
    tr = 256
    assert k % tr == 0 and k_out % tr == 0
    n_full = k // tr
    return pl.pallas_call(
        functools.partial(_cast_rows_kernel, n_full),
        grid=(k_out // tr,),
        in_specs=[pl.BlockSpec((None, tr, n), lambda i: (layer, jnp.minimum(i, n_full - 1), 0))],
        out_specs=pl.BlockSpec((tr, n), lambda i: (i, 0)),
        out_shape=jax.ShapeDtypeStruct((k_out, n), BF16),
        compiler_params=_cp(1), name="cast_rows",
    )(w)


def _swap_half(w):
    half = w.shape[-1] // 2
    return jnp.concatenate([w[..., half:], w[..., :half]], axis=-1)


def _layer_weights(i, w_in, ssd_conv_w, ssd_conv_b, ssd_dt_bias, ssd_a_log, ssd_d, ssd_norm, sc_conv_w,
                   mla_q_norm, mla_w_uq, mla_kv_norm, mla_w_ukv, w_out, ffn_w_gate, ffn_w_up, ffn_conv_w,
                   ffn_w_down):
    w = w_in[i]
    o_dt = SSD_DI + SSD_CD
    o_sc = o_dt + SSD_H
    o_mla = o_sc + 3 * SC_D
    o_kr = o_mla + QL + KVL
    kr = w[:, o_kr:o_kr + ROPE]
    win = jnp.concatenate([p.astype(BF16) for p in (
        w[:, :o_dt], w[:, o_sc:o_mla], w[:, o_mla:o_kr], kr, _swap_half(kr), w[:, o_dt:o_sc],
        jnp.zeros((D_MODEL, LANE - SSD_H), F32))], axis=1)
    uq = mla_w_uq[i].reshape(QL, MLA_H, NOPE + ROPE)
    pe = uq[..., NOPE:]
    zq = jnp.zeros((QL, MLA_H, HK - NOPE - ROPE), F32)
    wa = jnp.concatenate([uq[..., :NOPE], pe, zq], axis=-1).reshape(QL, MLA_H * HK).astype(BF16)
    wb = jnp.concatenate([_swap_half(pe), zq], axis=-1).reshape(QL, MLA_H * LANE).astype(BF16)
    ukv = mla_w_ukv[i].reshape(KVL, MLA_H, NOPE + MLA_V)
    padf = ((0, 0), (0, DFP - D_FF))
    return dict(
        win=win,
        cw=ssd_conv_w[i], cb=ssd_conv_b[i].reshape(1, SSD_CD),
        dtb=jnp.pad(ssd_dt_bias[i], (0, LANE - SSD_H)).reshape(1, LANE),
        alog=jnp.pad(ssd_a_log[i], (0, LANE - SSD_H)).reshape(1, LANE),
        dsk=jnp.repeat(ssd_d[i], SSD_P).reshape(1, SSD_DI),
        ng=ssd_norm[i].reshape(1, SSD_DI),
        scw=sc_conv_w[i],
        qn=mla_q_norm[i].reshape(1, QL), kvn=mla_kv_norm[i].reshape(1, KVL),
        wa=wa, wb=wb,
        wk=ukv[..., :NOPE].reshape(KVL, MLA_H * NOPE).astype(BF16),
        wv=ukv[..., NOPE:].reshape(KVL, MLA_H * MLA_V).astype(BF16),
        wkt=jnp.transpose(ukv[..., :NOPE], (1, 2, 0)).astype(BF16),
        wv3=jnp.transpose(ukv[..., NOPE:], (1, 0, 2)).astype(BF16),
        wout=_cast_pad_cols(w_out, i, D_MODEL),
        wg=_cast_pad_cols(ffn_w_gate, i, DFP),
        wu=_cast_pad_cols(ffn_w_up, i, DFP),
        fcw=jnp.pad(ffn_conv_w[i], padf),
        wd=_cast_pad_rows(ffn_w_down, i, DFP),
    )


def _rope_table(lp, ns, past):
    half = ROPE // 2
    pos = jnp.concatenate([jnp.maximum(jnp.arange(lp, dtype=jnp.int32) - NSKIP, 0),
                           N_META + past + jnp.arange(ns, dtype=jnp.int32) % CH])
    inv = THETA ** (-jnp.arange(half, dtype=F32) / half)
    ang = pos.astype(F32)[:, None] * inv[None, :]
    cos, sin = jnp.cos(ang), jnp.sin(ang)
    return jnp.concatenate([cos, cos, -sin, sin], axis=1)


def kernel(x_prompt, x_sample, cache_kv_latent, cache_k_rope, state_ssm, state_ssd_conv, state_sconv, state_ffn_conv, meta_tokens, norm_mix_pre, norm_mix_post, norm_ffn_pre, norm_ffn_post, w_in, ssd_conv_w, ssd_conv_b, ssd_dt_bias, ssd_a_log, ssd_d, ssd_norm, sc_conv_w, mla_q_norm, mla_w_uq, mla_kv_norm, mla_w_ukv, w_out, ffn_w_gate, ffn_w_up, ffn_conv_w, ffn_w_down):
    bp, seq, d = x_prompt.shape
    nb, ls, _ = x_sample.shape
    depth, _, past, _ = cache_kv_latent.shape
    assert bp == 1 and ls == CH and seq % CH == 0 and d == D_MODEL
    lp = CH + seq
    ns = nb * ls
    npc = lp // CH

    x = jnp.concatenate([jnp.zeros((NSKIP, d), F32), meta_tokens.astype(F32), x_prompt[0],
                         x_sample.reshape(ns, d)], axis=0)
    tab = _rope_table(lp, ns, past)
    zero_c = jnp.zeros((1, SSD_K - 1, SSD_CD), F32)
    zero_h = jnp.zeros((1, SSD_N, SSD_DI), F32)

    def state_in(s):
        return jnp.transpose(s, (0, 3, 1, 2)).reshape(s.shape[0], SSD_N, SSD_DI)

    def state_out(s):
        return jnp.transpose(s.reshape(s.shape[0], SSD_N, SSD_H, SSD_P), (0, 2, 3, 1))
    zero_s = jnp.zeros((1, SC_K - 1, SC_D), F32)
    zero_f = jnp.zeros((1, FFN_K - 1, DFP), F32)
    padf = ((0, 0), (0, 0), (0, DFP - D_FF))

    h = _norm_cast(x, norm_mix_pre[0])
    outs_p, outs_s = [], []
    for i in range(depth):
        lw = _layer_weights(i, w_in, ssd_conv_w, ssd_conv_b, ssd_dt_bias, ssd_a_log, ssd_d, ssd_norm,
                            sc_conv_w, mla_q_norm, mla_w_uq, mla_kv_norm, mla_w_ukv, w_out, ffn_w_gate,
                            ffn_w_up, ffn_conv_w, ffn_w_down)
        u, dtf = _inproj(h, lw["win"])

        yp, yscp, cnew_p, hnew_p, scnew_p = _mixer(u, dtf, 0, 1, npc, NSKIP, zero_c, zero_h, zero_s, lw)
        y_ssd, y_sc, cnew_s, hnew_s, scnew_s = _mixer(u, dtf, npc, nb, 1, 0, state_ssd_conv[i],
                                                      state_in(state_ssm[i]), state_sconv[i], lw,
                                                      keep=(yp, yscp))
        hnew_p, hnew_s = state_out(hnew_p), state_out(hnew_s)

        q, ckv, kpe, kcat, v = _mla_proj(u, tab, lw)
        y_mla = _cached_attn(q, ckv, kpe, cache_kv_latent, cache_k_rope, i, npc, lw, _flash(q, kcat, v, lp))

        mix = _outproj(y_ssd, y_sc, y_mla, lw["wout"])
        x, h = _resid_norm(x, mix, norm_mix_post[i], norm_ffn_pre[i])

        gate, up = _gateup(h, lw["wg"], lw["wu"])
        act, fnew_p = _act(gate, up, 0, 1, npc, NSKIP, zero_f, lw["fcw"])
        act, fnew_s = _act(gate, up, npc, nb, 1, 0, jnp.pad(state_ffn_conv[i], padf), lw["fcw"], act_in=act)
        f = _down(act, lw["wd"])
        x, h = _resid_norm(x, f, norm_ffn_post[i], norm_mix_pre[i + 1] if i + 1 < depth else None)

        outs_p.append((ckv[NSKIP:lp][None], kpe[NSKIP:lp, :ROPE][None], hnew_p, cnew_p, scnew_p,
                       fnew_p[:, :, :D_FF]))
        outs_s.append((ckv[lp:].reshape(nb, ls, KVL), kpe[lp:, :ROPE].reshape(nb, ls, ROPE), hnew_s, cnew_s,
                       scnew_s, fnew_s[:, :, :D_FF]))

    def stack(outs, j):
        return jnp.stack([o[j] for o in outs], axis=0)

    y_prompt = x[CH:lp][None]
    y_sample = x[lp:].reshape(nb, ls, d)
    return (y_prompt, y_sample,
            stack(outs_p, 0), stack(outs_p, 1), stack(outs_p, 2), stack(outs_p, 3), stack(outs_p, 4), stack(outs_p, 5),
            stack(outs_s, 0), stack(outs_s, 1), stack(outs_s, 2), stack(outs_s, 3), stack(outs_s, 4), stack(outs_s, 5))
```

```python
import functools

import jax
import jax.numpy as jnp
import numpy as np
from jax import lax
from jax.experimental import pallas as pl
from jax.experimental.pallas import tpu as pltpu

F32 = jnp.float32
BF16 = jnp.bfloat16

D_MODEL = 4096
N_META = 16
CH = 64
NSKIP = CH - N_META
EPS = 1e-6
SSD_P = 64
SSD_DI = 2048
SSD_H = 32
SSD_G = 8
SSD_N = 128
SSD_K = 4
SSD_CD = 4096
SC_D = 1024
SC_K = 3
MLA_H = 8
NOPE = 128
ROPE = 64
MLA_V = 128
QL = 768
KVL = 512
HK = 256
SCALE = (NOPE + ROPE) ** -0.5
QSCALE = SCALE * 1.4426950408889634
THETA = 10000.0
D_FF = 11008
DFP = 11264
FFN_K = 3
LANE = 128

C_Z, C_X, C_BC, C_SCB, C_SCC, C_SCH, C_MLA, NIN = 0, 2048, 4096, 6144, 7168, 8192, 9216, 10752
MLA_W = 1536

FLASH_TQ = 512
VMEM_LIMIT = 56 * 1024 * 1024
HI = lax.Precision.HIGHEST
NT = (((1,), (1,)), ((), ()))
TN = (((0,), (0,)), ((), ()))


def _cp(n, flags=None):
    return pltpu.CompilerParams(dimension_semantics=("arbitrary",) * n, vmem_limit_bytes=VMEM_LIMIT, flags=flags)


def _pick(n, cands):
    for c in cands:
        if n % c == 0:
            return c
    raise ValueError(f"no tile for {n}")


MM_ROWS = (1088, 544, 512, 272, 256, 136, 128, 64)
EW_ROWS = (272, 256, 136, 128, 64)


def _stored_chunk(c, nch):
    return (c + nch - 1) % nch


def _rms(x, g):
    return x * lax.rsqrt(jnp.mean(x * x, axis=-1, keepdims=True) + EPS) * g


def _silu(x):
    return x * jax.nn.sigmoid(x)


def _norm_cast_kernel(x_ref, g_ref, h_ref):
    h_ref[...] = _rms(x_ref[...], g_ref[...]).astype(BF16)


def _norm_cast(x, g):
    r, d = x.shape
    bm = _pick(r, EW_ROWS)
    return pl.pallas_call(
        _norm_cast_kernel,
        grid=(r // bm,),
        in_specs=[pl.BlockSpec((bm, d), lambda i: (i, 0)), pl.BlockSpec((1, d), lambda i: (0, 0))],
        out_specs=pl.BlockSpec((bm, d), lambda i: (i, 0)),
        out_shape=jax.ShapeDtypeStruct((r, d), BF16),
        compiler_params=_cp(1), name="norm_cast",
    )(x, g.reshape(1, d))


def _resid_norm_kernel(x_ref, m_ref, gp_ref, gn_ref, x2_ref, h_ref):
    x2 = x_ref[...] + _rms(m_ref[...], gp_ref[...])
    x2_ref[...] = x2
    h_ref[...] = _rms(x2, gn_ref[...]).astype(BF16)


def _resid_kernel(x_ref, m_ref, gp_ref, x2_ref):
    x2_ref[...] = x_ref[...] + _rms(m_ref[...], gp_ref[...])


def _resid_norm(x, m, g_post, g_next):
    r, d = x.shape
    bm = _pick(r, EW_ROWS)
    row = pl.BlockSpec((bm, d), lambda i: (i, 0))
    vec = pl.BlockSpec((1, d), lambda i: (0, 0))
    if g_next is None:
        return pl.pallas_call(
            _resid_kernel, grid=(r // bm,), in_specs=[row, row, vec], out_specs=row,
            out_shape=jax.ShapeDtypeStruct((r, d), F32), compiler_params=_cp(1), name="resid",
        )(x, m, g_post.reshape(1, d)), None
    return pl.pallas_call(
        _resid_norm_kernel, grid=(r // bm,), in_specs=[row, row, vec, vec], out_specs=[row, row],
        out_shape=[jax.ShapeDtypeStruct((r, d), F32), jax.ShapeDtypeStruct((r, d), BF16)],
        compiler_params=_cp(1), name="resid_norm",
    )(x, m, g_post.reshape(1, d), g_next.reshape(1, d))


def _inproj_kernel(x_ref, w_ref, u_ref, dt_ref):
    acc = jnp.dot(x_ref[...], w_ref[...], preferred_element_type=F32)
    u_ref[...] = acc.astype(BF16)

    @pl.when(pl.program_id(1) == pl.num_programs(1) - 1)
    def _():
        dt_ref[...] = acc[:, acc.shape[1] - LANE:]


def _inproj(h, w):
    r, k = h.shape
    n = w.shape[1]
    bm = _pick(r, MM_ROWS)
    bn = 768
    return pl.pallas_call(
        _inproj_kernel,
        grid=(r // bm, n // bn),
        in_specs=[pl.BlockSpec((bm, k), lambda i, j: (i, 0)), pl.BlockSpec((k, bn), lambda i, j: (0, j))],
        out_specs=[pl.BlockSpec((bm, bn), lambda i, j: (i, j)), pl.BlockSpec((bm, LANE), lambda i, j: (i, 0))],
        out_shape=[jax.ShapeDtypeStruct((r, n), BF16), jax.ShapeDtypeStruct((r, LANE), F32)],
        compiler_params=_cp(2), name="inproj",
    )(h, w)


def _outproj_kernel(a_ref, b_ref, c_ref, wa_ref, wb_ref, wc_ref, o_ref):
    acc = jnp.dot(a_ref[...], wa_ref[...], preferred_element_type=F32)
    acc = acc + jnp.dot(b_ref[...], wb_ref[...], preferred_element_type=F32)
    acc = acc + jnp.dot(c_ref[...], wc_ref[...], preferred_element_type=F32)
    o_ref[...] = acc


def _outproj(y_ssd, y_sc, y_mla, w):
    r = y_ssd.shape[0]
    n = w.shape[1]
    bm = _pick(r, MM_ROWS)
    bn = 1024
    return pl.pallas_call(
        _outproj_kernel,
        grid=(r // bm, n // bn),
        in_specs=[pl.BlockSpec((bm, SSD_DI), lambda i, j: (i, 0)),
                  pl.BlockSpec((bm, SC_D), lambda i, j: (i, 0)),
                  pl.BlockSpec((bm, SC_D), lambda i, j: (i, 0)),
                  pl.BlockSpec((SSD_DI, bn), lambda i, j: (0, j)),
                  pl.BlockSpec((SC_D, bn), lambda i, j: (2, j)),
                  pl.BlockSpec((SC_D, bn), lambda i, j: (3, j))],
        out_specs=pl.BlockSpec((bm, bn), lambda i, j: (i, j)),
        out_shape=jax.ShapeDtypeStruct((r, n), F32),
        compiler_params=_cp(2), name="outproj",
    )(y_ssd, y_sc, y_mla, w, w, w)


def _gateup_kernel(x_ref, wg_ref, wu_ref, g_ref, u_ref):
    x = x_ref[...]
    g_ref[...] = jnp.dot(x, wg_ref[...], preferred_element_type=F32).astype(BF16)
    u_ref[...] = jnp.dot(x, wu_ref[...], preferred_element_type=F32).astype(BF16)


def _gateup(h, wg, wu):
    r, k = h.shape
    n = wg.shape[1]
    bm = _pick(r, MM_ROWS)
    bn = 512
    wspec = pl.BlockSpec((k, bn), lambda i, j: (0, j))
    ospec = pl.BlockSpec((bm, bn), lambda i, j: (i, j))
    return pl.pallas_call(
        _gateup_kernel,
        grid=(r // bm, n // bn),
        in_specs=[pl.BlockSpec((bm, k), lambda i, j: (i, 0)), wspec, wspec],
        out_specs=[ospec, ospec],
        out_shape=[jax.ShapeDtypeStruct((r, n), BF16)] * 2,
        compiler_params=_cp(2), name="gateup",
    )(h, wg, wu)


def _down_kernel(x_ref, w_ref, o_ref):
    p = jnp.dot(x_ref[...], w_ref[...], preferred_element_type=F32)

    @pl.when(pl.program_id(2) == 0)
    def _():
        o_ref[...] = p

    @pl.when(pl.program_id(2) > 0)
    def _():
        o_ref[...] += p


def _down(a, w):
    r, k = a.shape
    n = w.shape[1]
    bm = _pick(r, MM_ROWS)
    bn = 1024
    bk = k // 4
    return pl.pallas_call(
        _down_kernel,
        grid=(r // bm, n // bn, k // bk),
        in_specs=[pl.BlockSpec((bm, bk), lambda i, j, kk: (i, kk)), pl.BlockSpec((bk, bn), lambda i, j, kk: (kk, j))],
        out_specs=pl.BlockSpec((bm, bn), lambda i, j, kk: (i, j)),
        out_shape=jax.ShapeDtypeStruct((r, n), F32),
        compiler_params=_cp(3), name="down",
    )(a, w)


def _mixer_kernel(nskip, z_ref, x_ref, bc_ref, dt_ref, scb_ref, scc_ref, sch_ref,
                  cprev_ref, h0_ref, scprev_ref,
                  cw_ref, cb_ref, dtb_ref, alog_ref, dsk_ref, ng_ref, scw_ref, pm_ref,
                  y_ref, ysc_ref, cnew_ref, hnew_ref, scnew_ref,
                  buf, sbuf, hst):
    t = CH
    c = pl.program_id(1)
    last = pl.num_programs(1) - 1

    ns = SSD_K - 1

    @pl.when(c == 0)
    def _init():
        _load_tail(buf, cprev_ref[0])
        sbuf[0:8, :] = jnp.zeros((8, SC_D), F32)
        sbuf[8 - (SC_K - 1):8, :] = scprev_ref[0]
        hst[...] = h0_ref[0]

    rows = lax.broadcasted_iota(jnp.int32, (t, 1), 0) + c * t
    valid = rows >= nskip

    xin = jnp.concatenate([x_ref[...], bc_ref[...]], axis=1)
    xin = jnp.where(valid, xin, jnp.zeros((), xin.dtype))
    delayed = _delayed(pm_ref, buf, xin, ns)
    acc = cb_ref[...] + cw_ref[ns:ns + 1, :] * xin.astype(F32)
    for d in range(1, ns + 1):
        acc = acc + cw_ref[ns - d:ns - d + 1, :] * delayed[d - 1]
    xbc = _silu(acc)

    @pl.when(c == last)
    def _():
        for r in range(ns):
            cnew_ref[0, r:r + 1, :] = buf[3 * r:3 * r + 1, :]

    lane = lax.broadcasted_iota(jnp.int32, (t, LANE), 1)
    v = dt_ref[...] + dtb_ref[...]
    dt = jnp.maximum(v, 0.0) + jnp.log1p(jnp.exp(-jnp.abs(v)))
    dt = jnp.where(valid & (lane < SSD_H), dt, 0.0)
    adt = dt * (-jnp.exp(alog_ref[...]))
    ri = lax.broadcasted_iota(jnp.int32, (t, t), 0)
    ci = lax.broadcasted_iota(jnp.int32, (t, t), 1)
    tri = ri >= ci
    acs = jnp.dot(tri.astype(F32), adt, precision=HI, preferred_element_type=F32)

    hpg = SSD_H // SSD_G
    gw = hpg * SSD_P
    lane_g = lax.broadcasted_iota(jnp.int32, (t, gw), 1)
    row_g = lax.broadcasted_iota(jnp.int32, (t, gw), 0)
    seg = lane_g // SSD_P
    eye_g = row_g == lane_g % SSD_P
    tri_g = row_g >= lane_g % SSD_P
    blockdiag = (lax.broadcasted_iota(jnp.int32, (hpg * t, gw), 0) // t
                 == lax.broadcasted_iota(jnp.int32, (hpg * t, gw), 1) // SSD_P)

    def per_head_lanes(mat, g):
        out = jnp.broadcast_to(mat[:, g * hpg + hpg - 1:g * hpg + hpg], (t, gw))
        for r in range(hpg - 2, -1, -1):
            out = jnp.where(seg == r, jnp.broadcast_to(mat[:, g * hpg + r:g * hpg + r + 1], (t, gw)), out)
        return out

    for g in range(SSD_G):
        cols = slice(g * gw, (g + 1) * gw)
        bg = xbc[:, SSD_DI + g * SSD_N:SSD_DI + (g + 1) * SSD_N].astype(BF16)
        cg = xbc[:, SSD_DI + SSD_G * SSD_N + g * SSD_N:SSD_DI + SSD_G * SSD_N + (g + 1) * SSD_N].astype(BF16)
        acol = per_head_lanes(acs, g)
        dtx = per_head_lanes(dt, g)
        arow = jnp.sum(jnp.where(eye_g, acol, 0.0), axis=0, keepdims=True)
        decay = jnp.exp(jnp.where(tri_g, acol - arow, -jnp.inf))
        cb = lax.dot_general(cg, jnp.concatenate([bg] * hpg, axis=0), NT, preferred_element_type=F32)
        xg = xbc[:, cols]
        xdt = xg * dtx
        xdtb = xdt.astype(BF16)
        rhs = jnp.where(blockdiag, jnp.concatenate([xdtb] * hpg, axis=0), jnp.zeros((), BF16))
        ydiag = jnp.dot((cb * decay).astype(BF16), rhs, preferred_element_type=F32)
        hg = hst[:, cols]
        yoff = jnp.dot(cg, hg.astype(BF16), preferred_element_type=F32) * jnp.exp(acol)
        alast = acol[t - 1:t, :]
        snew = lax.dot_general(bg, (xdt * jnp.exp(alast - acol)).astype(BF16), TN, preferred_element_type=F32)
        hst[:, cols] = jnp.exp(alast) * hg + snew
        y = ydiag + yoff + dsk_ref[:, cols] * xg
        y = y * _silu(z_ref[:, cols].astype(F32))
        y_ref[:, cols] = _rms(y, ng_ref[:, cols]).astype(BF16)

    @pl.when(c == last)
    def _():
        hnew_ref[0] = hst[...]

    p = scc_ref[...].astype(F32) * sch_ref[...].astype(F32)
    p = jnp.where(valid, p, 0.0)
    sbuf[8:8 + t, :] = p
    conv = scw_ref[0:1, :] * sbuf[6:6 + t, :]
    for i in range(1, SC_K):
        conv = conv + scw_ref[i:i + 1, :] * sbuf[6 + i:6 + i + t, :]
    ysc_ref[...] = (scb_ref[...].astype(F32) * conv).astype(BF16)
    sbuf[0:8, :] = sbuf[t:t + 8, :]

    @pl.when(c == last)
    def _():
        scnew_ref[0] = sbuf[8 - (SC_K - 1):8, :]


def _mixer_alias_kernel(nskip, *refs):
    n_in = 18
    _mixer_kernel(nskip, *refs[:n_in], *refs[n_in + 2:])


def _mixer(u, dtf, rb0, nb, nch, nskip, cprev, h0, scprev, lw, keep=None):
    r = u.shape[0]
    t = CH

    def rowmap(cb):
        return lambda b, c: (rb0 + b * nch + _stored_chunk(c, nch), cb)

    def stmap(b, c):
        return (b, 0, 0)

    def full(shape):
        return pl.BlockSpec(shape, lambda b, c: (0,) * len(shape))

    in_specs = [
        pl.BlockSpec((t, SSD_DI), rowmap(C_Z // SSD_DI)),
        pl.BlockSpec((t, SSD_DI), rowmap(C_X // SSD_DI)),
        pl.BlockSpec((t, SSD_DI), rowmap(C_BC // SSD_DI)),
        pl.BlockSpec((t, LANE), rowmap(0)),
        pl.BlockSpec((t, SC_D), rowmap(C_SCB // SC_D)),
        pl.BlockSpec((t, SC_D), rowmap(C_SCC // SC_D)),
        pl.BlockSpec((t, SC_D), rowmap(C_SCH // SC_D)),
        pl.BlockSpec((1, SSD_K - 1, SSD_CD), stmap),
        pl.BlockSpec((1, SSD_N, SSD_DI), stmap),
        pl.BlockSpec((1, SC_K - 1, SC_D), stmap),
        full((SSD_K, SSD_CD)), full((1, SSD_CD)), full((1, LANE)), full((1, LANE)),
        full((1, SSD_DI)), full((1, SSD_DI)), full((SC_K, SC_D)),
        full((CH * (SSD_K - 1) + TAIL, TAIL + CH)),
    ]
    out_specs = [
        pl.BlockSpec((t, SSD_DI), rowmap(0)),
        pl.BlockSpec((t, SC_D), rowmap(0)),
        pl.BlockSpec((1, SSD_K - 1, SSD_CD), stmap),
        pl.BlockSpec((1, SSD_N, SSD_DI), stmap),
        pl.BlockSpec((1, SC_K - 1, SC_D), stmap),
    ]
    out_shape = [
        jax.ShapeDtypeStruct((r, SSD_DI), BF16),
        jax.ShapeDtypeStruct((r, SC_D), BF16),
        jax.ShapeDtypeStruct((nb, SSD_K - 1, SSD_CD), F32),
        jax.ShapeDtypeStruct((nb, SSD_N, SSD_DI), F32),
        jax.ShapeDtypeStruct((nb, SC_K - 1, SC_D), F32),
    ]
    args = [u, u, u, dtf, u, u, u, cprev, h0, scprev,
            lw["cw"], lw["cb"], lw["dtb"], lw["alog"], lw["dsk"], lw["ng"], lw["scw"],
            jnp.asarray(_shift_matrix(SSD_K), BF16)]
    kern = functools.partial(_mixer_kernel, nskip)
    aliases = {}
    if keep is not None:
        aliases = {len(args): 0, len(args) + 1: 1}
        args += list(keep)
        in_specs += [pl.BlockSpec(memory_space=pl.ANY)] * 2
        kern = functools.partial(_mixer_alias_kernel, nskip)
    return pl.pallas_call(
        kern,
        grid=(nb, nch),
        in_specs=in_specs, out_specs=out_specs, out_shape=out_shape,
        scratch_shapes=[pltpu.VMEM((TAIL, SSD_CD), F32), pltpu.VMEM((8 + t, SC_D), F32),
                        pltpu.VMEM((SSD_N, SSD_DI), F32)],
        input_output_aliases=aliases,
        compiler_params=_cp(2), name="mixer",
    )(*args)


def _mla_proj_kernel(blk_ref, tab_ref, qn_ref, kvn_ref, wa_ref, wb_ref, wk_ref, wv_ref,
                     q_ref, ckv_ref, kpe_ref, kcat_ref, v_ref):
    blk = blk_ref[...]
    cq = _rms(blk[:, :QL].astype(F32), qn_ref[...]).astype(BF16)
    ckv = _rms(blk[:, QL:QL + KVL].astype(F32), kvn_ref[...])
    ckv_ref[...] = ckv
    tab = tab_ref[...]
    tabr = pltpu.roll(tab, ROPE, 1)
    prod = blk[:, QL + KVL:QL + KVL + LANE].astype(F32) * tab
    ksum = prod + pltpu.roll(prod, ROPE, 1)
    lane = lax.broadcasted_iota(jnp.int32, ksum.shape, 1)
    kpe = jnp.where(lane < ROPE, ksum, 0.0)
    kpe_ref[...] = kpe
    qa = jnp.dot(cq, wa_ref[...], preferred_element_type=F32)
    qb = jnp.dot(cq, wb_ref[...], preferred_element_type=F32)
    ckvb = ckv.astype(BF16)
    kn = jnp.dot(ckvb, wk_ref[...], preferred_element_type=F32)
    v_ref[...] = jnp.dot(ckvb, wv_ref[...], preferred_element_type=F32).astype(BF16)
    kpeb = kpe.astype(BF16)
    tab_q = tab * QSCALE
    tabr_q = tabr * QSCALE
    for h in range(MLA_H):
        q_ref[:, h * HK:h * HK + NOPE] = (qa[:, h * HK:h * HK + NOPE] * QSCALE).astype(BF16)
        q_ref[:, h * HK + NOPE:(h + 1) * HK] = (
            qa[:, h * HK + NOPE:(h + 1) * HK] * tab_q + qb[:, h * LANE:(h + 1) * LANE] * tabr_q).astype(BF16)
        kcat_ref[:, h * HK:h * HK + NOPE] = kn[:, h * NOPE:(h + 1) * NOPE].astype(BF16)
        kcat_ref[:, h * HK + NOPE:(h + 1) * HK] = kpeb


def _mla_proj(u, tab, lw):
    r = u.shape[0]
    bm = _pick(r, (544, 512, 272, 256, 136, 128, 64))

    def full(shape):
        return pl.BlockSpec(shape, lambda i: (0,) * len(shape))

    def row(w):
        return pl.BlockSpec((bm, w), lambda i: (i, 0))

    return pl.pallas_call(
        _mla_proj_kernel,
        grid=(r // bm,),
        in_specs=[pl.BlockSpec((bm, MLA_W), lambda i: (i, C_MLA // MLA_W)), row(LANE),
                  full((1, QL)), full((1, KVL)), full((QL, MLA_H * HK)), full((QL, MLA_H * LANE)),
                  full((KVL, MLA_H * NOPE)), full((KVL, MLA_H * MLA_V))],
        out_specs=[row(MLA_H * HK), row(KVL), row(LANE), row(MLA_H * HK), row(MLA_H * MLA_V)],
        out_shape=[jax.ShapeDtypeStruct((r, MLA_H * HK), BF16), jax.ShapeDtypeStruct((r, KVL), F32),
                   jax.ShapeDtypeStruct((r, LANE), F32), jax.ShapeDtypeStruct((r, MLA_H * HK), BF16),
                   jax.ShapeDtypeStruct((r, MLA_H * MLA_V), BF16)],
        compiler_params=_cp(1), name="mla_proj",
    )(u, tab, lw["qn"], lw["kvn"], lw["wa"], lw["wb"], lw["wk"], lw["wv"])


def _flash_kernel(it_ref, jt_ref, q_ref, k_ref, v_ref, k0_ref, v0_ref, mask_ref, pad_ref, o_ref,
                  qt_s, m_s, l_s, acc_s, s_scr, p_scr, a_s):
    tq = q_ref.shape[0]
    tk = k_ref.shape[0]
    step = pl.program_id(0)
    i = it_ref[step]
    j = jt_ref[step]

    @pl.when(j == 0)
    def _init():
        for h in range(MLA_H):
            qt_s[h * HK:(h + 1) * HK, :] = q_ref[:, h * HK:(h + 1) * HK].T
        for h in range(MLA_H):
            s0 = jnp.dot(k0_ref[:, h * HK:(h + 1) * HK], qt_s[h * HK:(h + 1) * HK, :],
                         preferred_element_type=F32) + pad_ref[...]
            m0 = jnp.max(s0, axis=0, keepdims=True)
            p0 = jnp.exp2(s0 - m0)
            rows = slice(h * MLA_V, (h + 1) * MLA_V)
            m_s[h] = m0
            l_s[h] = jnp.sum(p0, axis=0, keepdims=True)
            acc_s[rows, :] = lax.dot_general(v0_ref[:, rows], p0.astype(BF16), TN, preferred_element_type=F32)

    def tile(masked):
        nkb = tk // CH

        def block(sref, kb):
            blk = sref[kb * CH:(kb + 1) * CH, :]
            if masked:
                blk = blk + mask_ref[kb * CH:(kb + 1) * CH, :]
            return blk

        def scores(h):
            s_scr[h % 2] = jnp.dot(k_ref[:, h * HK:(h + 1) * HK], qt_s[h * HK:(h + 1) * HK, :],
                                   preferred_element_type=F32)

        def softmax(h):
            sref = s_scr.at[h % 2]
            pref = p_scr.at[h % 2]
            mx = block(sref, 0).reshape(CH // 8, 8, tq).max(axis=0)
            for kb in range(1, nkb):
                mx = jnp.maximum(mx, block(sref, kb).reshape(CH // 8, 8, tq).max(axis=0))
            m_prev = m_s[h]
            m_new = jnp.maximum(m_prev, jnp.max(mx, axis=0, keepdims=True))
            alpha = jnp.exp2(m_prev - m_new)
            part = jnp.zeros((8, tq), F32)
            for kb in range(nkb):
                e = jnp.exp2(block(sref, kb) - m_new)
                part = part + jnp.sum(e.reshape(CH // 8, 8, tq), axis=0)
                pref[kb * CH:(kb + 1) * CH, :] = e.astype(BF16)
            l_s[h] = alpha * l_s[h] + jnp.sum(part, axis=0, keepdims=True)
            m_s[h] = m_new
            a_s[h % 2] = alpha

        def weighted_values(h):
            rows = slice(h * MLA_V, (h + 1) * MLA_V)
            acc_s[rows, :] = a_s[h % 2] * acc_s[rows, :] + lax.dot_general(
                v_ref[:, rows], p_scr[h % 2], TN, preferred_element_type=F32)

        scores(0)
        for h in range(MLA_H + 1):
            if h + 1 < MLA_H:
                scores(h + 1)
            if h >= 1:
                weighted_values(h - 1)
            if h < MLA_H:
                softmax(h)

    @pl.when(j < i)
    def _():
        tile(False)

    @pl.when(j == i)
    def _():
        tile(True)
        for h in range(MLA_H):
            rows = slice(h * MLA_V, (h + 1) * MLA_V)
            o_ref[:, rows] = (acc_s[rows, :] / l_s[h]).T.astype(BF16)


def _pad_mask(n):
    return jnp.asarray(np.where(np.arange(CH)[:, None] + 0 * np.arange(n)[None, :] < NSKIP, -np.inf, 0.0),
                       F32)


def _flash(q, kcat, v, seq):
    r = q.shape[0]
    tq = _pick(seq, (FLASH_TQ, 256, 128))
    nq = seq // tq
    c0 = seq // CH
    it = np.concatenate([np.full((i + 1,), i, np.int32) for i in range(nq)])
    jt = np.concatenate([np.arange(i + 1, dtype=np.int32) for i in range(nq)])
    kk = np.arange(tq)[:, None]
    qq = np.arange(tq)[None, :]
    diag = jnp.asarray(np.where(kk // CH > qq // CH, -np.inf, 0.0), F32)
    grid_spec = pltpu.PrefetchScalarGridSpec(
        num_scalar_prefetch=2,
        grid=(int(it.shape[0]),),
        in_specs=[pl.BlockSpec((tq, MLA_H * HK), lambda s, it, jt: (it[s], 0)),
                  pl.BlockSpec((tq, MLA_H * HK), lambda s, it, jt: (jt[s], 0)),
                  pl.BlockSpec((tq, MLA_H * MLA_V), lambda s, it, jt: (jt[s], 0)),
                  pl.BlockSpec((CH, MLA_H * HK), lambda s, it, jt: (c0, 0)),
                  pl.BlockSpec((CH, MLA_H * MLA_V), lambda s, it, jt: (c0, 0)),
                  pl.BlockSpec((tq, tq), lambda s, it, jt: (0, 0)),
                  pl.BlockSpec((CH, tq), lambda s, it, jt: (0, 0))],
        out_specs=pl.BlockSpec((tq, MLA_H * MLA_V), lambda s, it, jt: (it[s], 0)),
        scratch_shapes=[pltpu.VMEM((MLA_H * HK, tq), BF16),
                        pltpu.VMEM((MLA_H, 1, tq), F32), pltpu.VMEM((MLA_H, 1, tq), F32),
                        pltpu.VMEM((MLA_H * MLA_V, tq), F32),
                        pltpu.VMEM((2, tq, tq), F32), pltpu.VMEM((2, tq, tq), BF16),
                        pltpu.VMEM((2, 1, tq), F32)],
    )
    return pl.pallas_call(
        _flash_kernel,
        grid_spec=grid_spec,
        out_shape=jax.ShapeDtypeStruct((r, MLA_H * MLA_V), BF16),
        compiler_params=_cp(1), name="flash",
    )(jnp.asarray(it), jnp.asarray(jt), q, kcat, v, kcat, v, diag, _pad_mask(tq))


def _attn0_kernel(q_ref, k_ref, v_ref, keep_ref, o_ref):
    del keep_ref
    col = lax.broadcasted_iota(jnp.int32, (CH, CH), 1)
    for h in range(MLA_H):
        s = lax.dot_general(q_ref[:, h * HK:(h + 1) * HK], k_ref[:, h * HK:(h + 1) * HK], NT,
                            preferred_element_type=F32)
        s = jnp.where(col >= NSKIP, s, -jnp.inf)
        p = jnp.exp2(s - jnp.max(s, axis=-1, keepdims=True))
        o = jnp.dot(p.astype(BF16), v_ref[:, h * MLA_V:(h + 1) * MLA_V], preferred_element_type=F32)
        o_ref[:, h * MLA_V:(h + 1) * MLA_V] = (o / jnp.sum(p, axis=-1, keepdims=True)).astype(BF16)


def _attn0(q, kcat, v, c0, o_frames):
    def blk(w):
        return pl.BlockSpec((CH, w), lambda i: (c0, 0))

    return pl.pallas_call(
        _attn0_kernel,
        grid=(1,),
        in_specs=[blk(MLA_H * HK), blk(MLA_H * HK), blk(MLA_H * MLA_V), pl.BlockSpec(memory_space=pl.ANY)],
        out_specs=blk(MLA_H * MLA_V),
        out_shape=jax.ShapeDtypeStruct(o_frames.shape, BF16),
        input_output_aliases={3: 0},
        compiler_params=_cp(1), name="attn0",
    )(q, kcat, v, o_frames)


def _cached_attn_kernel(q_ref, cnew_ref, pnew_ref, ckv_ref, cpe_ref, wkt_ref, wv_ref, o_ref, call, peall):
    past = ckv_ref.shape[2]
    call[0:past, :] = ckv_ref[0, 0].astype(BF16)
    call[past:past + CH, :] = cnew_ref[...].astype(BF16)
    peall[...] = jnp.zeros(peall.shape, BF16)
    peall[0:past, 0:ROPE] = cpe_ref[0, 0].astype(BF16)
    peall[past:past + CH, :] = pnew_ref[...].astype(BF16)
    qlat = []
    qpe = []
    for h in range(MLA_H):
        qn = q_ref[:, h * HK:h * HK + NOPE]
        qlat.append(jnp.dot(qn, wkt_ref[h], preferred_element_type=F32).astype(BF16))
        qpe.append(q_ref[:, h * HK + NOPE:(h + 1) * HK])
    qlat = jnp.concatenate(qlat, axis=0)
    qpe = jnp.concatenate(qpe, axis=0)
    s = (lax.dot_general(qlat, call[...], NT, preferred_element_type=F32)
         + lax.dot_general(qpe, peall[...], NT, preferred_element_type=F32))
    p = jnp.exp2(s - jnp.max(s, axis=-1, keepdims=True))
    p = p / jnp.sum(p, axis=-1, keepdims=True)
    olat = jnp.dot(p.astype(BF16), call[...], preferred_element_type=F32).astype(BF16)
    for h in range(MLA_H):
        o_ref[:, h * MLA_V:(h + 1) * MLA_V] = jnp.dot(
            olat[h * CH:(h + 1) * CH, :], wv_ref[h], preferred_element_type=F32).astype(BF16)


def _cached_attn_alias_kernel(q_ref, cnew_ref, pnew_ref, ckv_ref, cpe_ref, wkt_ref, wv_ref, keep_ref, o_ref,
                              call, peall):
    del keep_ref
    _cached_attn_kernel(q_ref, cnew_ref, pnew_ref, ckv_ref, cpe_ref, wkt_ref, wv_ref, o_ref, call, peall)


def _cached_attn(q, ckv, kpe, cache_kv, cache_pe, layer, rb0, lw, o_prompt):
    nb, past = cache_kv.shape[1], cache_kv.shape[2]

    def rowmap(b):
        return (rb0 + b, 0)

    return pl.pallas_call(
        _cached_attn_alias_kernel,
        grid=(nb,),
        in_specs=[pl.BlockSpec((CH, MLA_H * HK), rowmap),
                  pl.BlockSpec((CH, KVL), rowmap),
                  pl.BlockSpec((CH, LANE), rowmap),
                  pl.BlockSpec((1, 1, past, KVL), lambda b: (layer, b, 0, 0)),
                  pl.BlockSpec((1, 1, past, ROPE), lambda b: (layer, b, 0, 0)),
                  pl.BlockSpec((MLA_H, NOPE, KVL), lambda b: (0, 0, 0)),
                  pl.BlockSpec((MLA_H, KVL, MLA_V), lambda b: (0, 0, 0)),
                  pl.BlockSpec(memory_space=pl.ANY)],
        out_specs=pl.BlockSpec((CH, MLA_H * MLA_V), rowmap),
        out_shape=jax.ShapeDtypeStruct(o_prompt.shape, BF16),
        scratch_shapes=[pltpu.VMEM((past + CH, KVL), BF16), pltpu.VMEM((past + CH, LANE), BF16)],
        input_output_aliases={7: 0},
        compiler_params=_cp(1), name="cached_attn",
    )(q, ckv, kpe, cache_kv, cache_pe, lw["wkt"], lw["wv3"], o_prompt)


TAIL = 16


def _shift_matrix(k):
    ns = k - 1
    pm = np.zeros((CH * ns + TAIL, TAIL + CH), np.float32)
    for d in range(1, ns + 1):
        for t in range(CH):
            if t - d >= 0:
                pm[(d - 1) * CH + t, TAIL + t - d] = 1.0
            else:
                r = ns + t - d
                pm[(d - 1) * CH + t, 3 * r:3 * r + 3] = 1.0
    for r in range(ns):
        pm[CH * ns + 3 * r, TAIL + CH - ns + r] = 1.0
    return pm


def _load_tail(tail_s, prev):
    hi = prev.astype(BF16).astype(F32)
    rest = prev - hi
    mid = rest.astype(BF16).astype(F32)
    lo = (rest - mid).astype(BF16).astype(F32)
    tail_s[...] = jnp.zeros(tail_s.shape, F32)
    for r in range(prev.shape[0]):
        tail_s[3 * r:3 * r + 1, :] = hi[r:r + 1]
        tail_s[3 * r + 1:3 * r + 2, :] = mid[r:r + 1]
        tail_s[3 * r + 2:3 * r + 3, :] = lo[r:r + 1]


def _delayed(pm_ref, tail_s, x, ns):
    ext = jnp.concatenate([tail_s[...].astype(BF16), x], axis=0)
    out = jnp.dot(pm_ref[...], ext, preferred_element_type=F32)
    tail_s[...] = out[CH * ns:, :]
    return [out[(d - 1) * CH:d * CH, :] for d in range(1, ns + 1)]


def _act_kernel(nskip, g_ref, u_ref, prev_ref, w_ref, pm_ref, a_ref, new_ref, tail_s):
    t = CH
    ns = FFN_K - 1
    c = pl.program_id(1)

    @pl.when(c == 0)
    def _():
        _load_tail(tail_s, prev_ref[0])

    rows = lax.broadcasted_iota(jnp.int32, (t, 1), 0) + c * t
    g = g_ref[...]
    g = jnp.where(rows >= nskip, g, jnp.zeros((), g.dtype))
    delayed = _delayed(pm_ref, tail_s, g, ns)
    conv = w_ref[ns:ns + 1, :] * g.astype(F32)
    for d in range(1, ns + 1):
        conv = conv + w_ref[ns - d:ns - d + 1, :] * delayed[d - 1]
    a_ref[...] = (_silu(conv) * u_ref[...].astype(F32)).astype(BF16)

    @pl.when(c == pl.num_programs(1) - 1)
    def _():
        for r in range(ns):
            new_ref[0, r:r + 1, :] = tail_s[3 * r:3 * r + 1, :]


def _act(gate, up, rb0, nb, nch, nskip, prev, w, act_in=None):
    r, n = gate.shape
    t = CH

    def rowmap(b, c):
        return (rb0 + b * nch + _stored_chunk(c, nch), 0)

    pm = jnp.asarray(_shift_matrix(FFN_K), BF16)
    args = [gate, up, prev, w, pm]
    in_specs = [pl.BlockSpec((t, n), rowmap), pl.BlockSpec((t, n), rowmap),
                pl.BlockSpec((1, FFN_K - 1, n), lambda b, c: (b, 0, 0)),
                pl.BlockSpec((FFN_K, n), lambda b, c: (0, 0)),
                pl.BlockSpec(pm.shape, lambda b, c: (0, 0))]
    kern = functools.partial(_act_kernel, nskip)
    aliases = {}
    if act_in is not None:
        aliases = {len(args): 0}
        args.append(act_in)
        in_specs.append(pl.BlockSpec(memory_space=pl.ANY))
        kern = functools.partial(_act_alias_kernel, nskip)
    return pl.pallas_call(
        kern,
        grid=(nb, nch),
        in_specs=in_specs,
        out_specs=[pl.BlockSpec((t, n), rowmap), pl.BlockSpec((1, FFN_K - 1, n), lambda b, c: (b, 0, 0))],
        out_shape=[jax.ShapeDtypeStruct((r, n), BF16), jax.ShapeDtypeStruct((nb, FFN_K - 1, n), F32)],
        scratch_shapes=[pltpu.VMEM((TAIL, n), F32)],
        input_output_aliases=aliases,
        compiler_params=_cp(2), name="act",
    )(*args)


def _act_alias_kernel(nskip, g_ref, u_ref, prev_ref, w_ref, pm_ref, keep_ref, a_ref, new_ref, tail_s):
    del keep_ref
    _act_kernel(nskip, g_ref, u_ref, prev_ref, w_ref, pm_ref, a_ref, new_ref, tail_s)


def _cast_cols_kernel(x_ref, o_ref):
    n = x_ref.shape[1]
    o_ref[:, :n] = x_ref[...].astype(BF16)
    if o_ref.shape[1] > n:
        o_ref[:, n:] = jnp.zeros((o_ref.shape[0], o_ref.shape[1] - n), BF16)


def _cast_pad_cols(w, layer, n_out):
    _, k, n = w.shape
    tr = _pick(k, (128, 64))
    return pl.pallas_call(
        _cast_cols_kernel,
        grid=(k // tr,),
        in_specs=[pl.BlockSpec((None, tr, n), lambda i: (layer, i, 0))],
        out_specs=pl.BlockSpec((tr, n_out), lambda i: (i, 0)),
        out_shape=jax.ShapeDtypeStruct((k, n_out), BF16),
        compiler_params=_cp(1), name="cast_cols",
    )(w)


def _cast_rows_kernel(n_full, x_ref, o_ref):
    @pl.when(pl.program_id(0) < n_full)
    def _():
        o_ref[...] = x_ref[...].astype(BF16)

    @pl.when(pl.program_id(0) >= n_full)
    def _():
        o_ref[...] = jnp.zeros(o_ref.shape, BF16)


def _cast_pad_rows(w, layer, k_out):
    _, k, n = w.shape
    tr = 256
    assert k % tr == 0 and k_out % tr == 0
    n_full = k // tr
    return pl.pallas_call(
        functools.partial(_cast_rows_kernel, n_full),
        grid=(k_out // tr,),
        in_specs=[pl.BlockSpec((None, tr, n), lambda i: (layer, jnp.minimum(i, n_full - 1), 0))],
        out_specs=pl.BlockSpec((tr, n), lambda i: (i, 0)),
        out_shape=jax.ShapeDtypeStruct((k_out, n), BF16),
        compiler_params=_cp(1), name="cast_rows",
    )(w)


def _swap_half(w):
    half = w.shape[-1] // 2
    return jnp.concatenate([w[..., half:], w[..., :half]], axis=-1)


def _layer_weights(i, w_in, ssd_conv_w, ssd_conv_b, ssd_dt_bias, ssd_a_log, ssd_d, ssd_norm, sc_conv_w,
                   mla_q_norm, mla_w_uq, mla_kv_norm, mla_w_ukv, w_out, ffn_w_gate, ffn_w_up, ffn_conv_w,
                   ffn_w_down):
    w = w_in[i]
    o_dt = SSD_DI + SSD_CD
    o_sc = o_dt + SSD_H
    o_mla = o_sc + 3 * SC_D
    o_kr = o_mla + QL + KVL
    kr = w[:, o_kr:o_kr + ROPE]
    win = jnp.concatenate([p.astype(BF16) for p in (
        w[:, :o_dt], w[:, o_sc:o_mla], w[:, o_mla:o_kr], kr, _swap_half(kr), w[:, o_dt:o_sc],
        jnp.zeros((D_MODEL, LANE - SSD_H), F32))], axis=1)
    uq = mla_w_uq[i].reshape(QL, MLA_H, NOPE + ROPE)
    pe = uq[..., NOPE:]
    zq = jnp.zeros((QL, MLA_H, HK - NOPE - ROPE), F32)
    wa = jnp.concatenate([uq[..., :NOPE], pe, zq], axis=-1).reshape(QL, MLA_H * HK).astype(BF16)
    wb = jnp.concatenate([_swap_half(pe), zq], axis=-1).reshape(QL, MLA_H * LANE).astype(BF16)
    ukv = mla_w_ukv[i].reshape(KVL, MLA_H, NOPE + MLA_V)
    padf = ((0, 0), (0, DFP - D_FF))
    return dict(
        win=win,
        cw=ssd_conv_w[i], cb=ssd_conv_b[i].reshape(1, SSD_CD),
        dtb=jnp.pad(ssd_dt_bias[i], (0, LANE - SSD_H)).reshape(1, LANE),
        alog=jnp.pad(ssd_a_log[i], (0, LANE - SSD_H)).reshape(1, LANE),
        dsk=jnp.repeat(ssd_d[i], SSD_P).reshape(1, SSD_DI),
        ng=ssd_norm[i].reshape(1, SSD_DI),
        scw=sc_conv_w[i],
        qn=mla_q_norm[i].reshape(1, QL), kvn=mla_kv_norm[i].reshape(1, KVL),
        wa=wa, wb=wb,
        wk=ukv[..., :NOPE].reshape(KVL, MLA_H * NOPE).astype(BF16),
        wv=ukv[..., NOPE:].reshape(KVL, MLA_H * MLA_V).astype(BF16),
        wkt=jnp.transpose(ukv[..., :NOPE], (1, 2, 0)).astype(BF16),
        wv3=jnp.transpose(ukv[..., NOPE:], (1, 0, 2)).astype(BF16),
        wout=_cast_pad_cols(w_out, i, D_MODEL),
        wg=_cast_pad_cols(ffn_w_gate, i, DFP),
        wu=_cast_pad_cols(ffn_w_up, i, DFP),
        fcw=jnp.pad(ffn_conv_w[i], padf),
        wd=_cast_pad_rows(ffn_w_down, i, DFP),
    )


def _rope_table(seq, ns, past):
    half = ROPE // 2
    pos = jnp.concatenate([N_META + jnp.arange(seq, dtype=jnp.int32),
                           jnp.maximum(jnp.arange(CH, dtype=jnp.int32) - NSKIP, 0),
                           N_META + past + jnp.arange(ns, dtype=jnp.int32) % CH])
    inv = THETA ** (-jnp.arange(half, dtype=F32) / half)
    ang = pos.astype(F32)[:, None] * inv[None, :]
    cos, sin = jnp.cos(ang), jnp.sin(ang)
    return jnp.concatenate([cos, cos, -sin, sin], axis=1)


def kernel(x_prompt, x_sample, cache_kv_latent, cache_k_rope, state_ssm, state_ssd_conv, state_sconv, state_ffn_conv, meta_tokens, norm_mix_pre, norm_mix_post, norm_ffn_pre, norm_ffn_post, w_in, ssd_conv_w, ssd_conv_b, ssd_dt_bias, ssd_a_log, ssd_d, ssd_norm, sc_conv_w, mla_q_norm, mla_w_uq, mla_kv_norm, mla_w_ukv, w_out, ffn_w_gate, ffn_w_up, ffn_conv_w, ffn_w_down):
    bp, seq, d = x_prompt.shape
    nb, ls, _ = x_sample.shape
    depth, _, past, _ = cache_kv_latent.shape
    assert bp == 1 and ls == CH and seq % CH == 0 and d == D_MODEL
    lp = CH + seq
    ns = nb * ls
    npc = lp // CH
    c0 = seq // CH

    x = jnp.concatenate([x_prompt[0], jnp.zeros((NSKIP, d), F32), meta_tokens.astype(F32),
                         x_sample.reshape(ns, d)], axis=0)
    tab = _rope_table(seq, ns, past)
    zero_c = jnp.zeros((1, SSD_K - 1, SSD_CD), F32)
    zero_h = jnp.zeros((1, SSD_N, SSD_DI), F32)

    def state_in(s):
        return jnp.transpose(s, (0, 3, 1, 2)).reshape(s.shape[0], SSD_N, SSD_DI)

    def state_out(s):
        return jnp.transpose(s.reshape(s.shape[0], SSD_N, SSD_H, SSD_P), (0, 2, 3, 1))
    zero_s = jnp.zeros((1, SC_K - 1, SC_D), F32)
    zero_f = jnp.zeros((1, FFN_K - 1, DFP), F32)
    padf = ((0, 0), (0, 0), (0, DFP - D_FF))

    h = _norm_cast(x, norm_mix_pre[0])
    outs_p, outs_s = [], []
    for i in range(depth):
        lw = _layer_weights(i, w_in, ssd_conv_w, ssd_conv_b, ssd_dt_bias, ssd_a_log, ssd_d, ssd_norm,
                            sc_conv_w, mla_q_norm, mla_w_uq, mla_kv_norm, mla_w_ukv, w_out, ffn_w_gate,
                            ffn_w_up, ffn_conv_w, ffn_w_down)
        u, dtf = _inproj(h, lw["win"])

        yp, yscp, cnew_p, hnew_p, scnew_p = _mixer(u, dtf, 0, 1, npc, NSKIP, zero_c, zero_h, zero_s, lw)
        y_ssd, y_sc, cnew_s, hnew_s, scnew_s = _mixer(u, dtf, npc, nb, 1, 0, state_ssd_conv[i],
                                                      state_in(state_ssm[i]), state_sconv[i], lw,
                                                      keep=(yp, yscp))
        hnew_p, hnew_s = state_out(hnew_p), state_out(hnew_s)

        q, ckv, kpe, kcat, v = _mla_proj(u, tab, lw)
        y_mla = _attn0(q, kcat, v, c0, _flash(q, kcat, v, seq))
        y_mla = _cached_attn(q, ckv, kpe, cache_kv_latent, cache_k_rope, i, npc, lw, y_mla)

        mix = _outproj(y_ssd, y_sc, y_mla, lw["wout"])
        x, h = _resid_norm(x, mix, norm_mix_post[i], norm_ffn_pre[i])

        gate, up = _gateup(h, lw["wg"], lw["wu"])
        act, fnew_p = _act(gate, up, 0, 1, npc, NSKIP, zero_f, lw["fcw"])
        act, fnew_s = _act(gate, up, npc, nb, 1, 0, jnp.pad(state_ffn_conv[i], padf), lw["fcw"], act_in=act)
        f = _down(act, lw["wd"])
        x, h = _resid_norm(x, f, norm_ffn_post[i], norm_mix_pre[i + 1] if i + 1 < depth else None)

        meta_rows = slice(seq + NSKIP, lp)
        outs_p.append((jnp.concatenate([ckv[meta_rows], ckv[:seq]], axis=0)[None],
                       jnp.concatenate([kpe[meta_rows, :ROPE], kpe[:seq, :ROPE]], axis=0)[None],
                       hnew_p, cnew_p, scnew_p, fnew_p[:, :, :D_FF]))
        outs_s.append((ckv[lp:].reshape(nb, ls, KVL), kpe[lp:, :ROPE].reshape(nb, ls, ROPE), hnew_s, cnew_s,
                       scnew_s, fnew_s[:, :, :D_FF]))

    def stack(outs, j):
        return jnp.stack([o[j] for o in outs], axis=0)

    y_prompt = x[:seq][None]
    y_sample = x[lp:].reshape(nb, ls, d)
    return (y_prompt, y_sample,
            stack(outs_p, 0), stack(outs_p, 1), stack(outs_p, 2), stack(outs_p, 3), stack(outs_p, 4), stack(outs_p, 5),
            stack(outs_s, 0), stack(outs_s, 1), stack(outs_s, 2), stack(outs_s, 3), stack(outs_s, 4), stack(outs_s, 5))
```

```python
import functools

import jax
import jax.numpy as jnp
import numpy as np
from jax import lax
from jax.experimental import pallas as pl
from jax.experimental.pallas import tpu as pltpu

F32 = jnp.float32
BF16 = jnp.bfloat16

D_MODEL = 4096
N_META = 16
CH = 64
NSKIP = CH - N_META
EPS = 1e-6
SSD_P = 64
SSD_DI = 2048
SSD_H = 32
SSD_G = 8
SSD_N = 128
SSD_K = 4
SSD_CD = 4096
SC_D = 1024
SC_K = 3
MLA_H = 8
NOPE = 128
ROPE = 64
MLA_V = 128
QL = 768
KVL = 512
HK = 256
SCALE = (NOPE + ROPE) ** -0.5
QSCALE = SCALE * 1.4426950408889634
THETA = 10000.0
D_FF = 11008
DFP = 11264
FFN_K = 3
LANE = 128

C_Z, C_X, C_BC, C_SCB, C_SCC, C_SCH, C_MLA, NIN = 0, 2048, 4096, 6144, 7168, 8192, 9216, 10752
MLA_W = 1536

FLASH_TQ = 512
ACT_ROWS = 128
VMEM_LIMIT = 56 * 1024 * 1024
HI = lax.Precision.HIGHEST
NT = (((1,), (1,)), ((), ()))
TN = (((0,), (0,)), ((), ()))


def _cp(n, flags=None):
    return pltpu.CompilerParams(dimension_semantics=("arbitrary",) * n, vmem_limit_bytes=VMEM_LIMIT, flags=flags)


def _pick(n, cands):
    for c in cands:
        if n % c == 0:
            return c
    raise ValueError(f"no tile for {n}")


MM_ROWS = (1088, 544, 512, 272, 256, 136, 128, 64)
EW_ROWS = (272, 256, 136, 128, 64)


def _stored_chunk(c, nch):
    return (c + nch - 1) % nch


def _rms(x, g):
    return x * lax.rsqrt(jnp.mean(x * x, axis=-1, keepdims=True) + EPS) * g


def _silu(x):
    return x * jax.nn.sigmoid(x)


def _assemble_kernel(nf, xp_ref, xs_ref, meta_ref, g_ref, x_ref, h_ref):
    c = pl.program_id(0)

    def emit(x):
        x_ref[...] = x
        h_ref[...] = _rms(x, g_ref[...]).astype(BF16)

    @pl.when(c < nf)
    def _():
        emit(xp_ref[...])

    @pl.when(c == nf)
    def _():
        emit(jnp.concatenate([jnp.zeros((NSKIP, x_ref.shape[1]), F32), meta_ref[...]], axis=0))

    @pl.when(c > nf)
    def _():
        emit(xs_ref[...])


def _assemble(x_prompt, x_sample, meta, g):
    seq, d = x_prompt.shape
    ns = x_sample.shape[0]
    nf = seq // CH
    r = seq + CH + ns
    blk = pl.BlockSpec((CH, d), lambda c: (c, 0))
    return pl.pallas_call(
        functools.partial(_assemble_kernel, nf),
        grid=(r // CH,),
        in_specs=[pl.BlockSpec((CH, d), lambda c: (jnp.minimum(c, nf - 1), 0)),
                  pl.BlockSpec((CH, d), lambda c: (jnp.maximum(c - nf - 1, 0), 0)),
                  pl.BlockSpec((N_META, d), lambda c: (0, 0)),
                  pl.BlockSpec((1, d), lambda c: (0, 0))],
        out_specs=[blk, blk],
        out_shape=[jax.ShapeDtypeStruct((r, d), F32), jax.ShapeDtypeStruct((r, d), BF16)],
        compiler_params=_cp(1), name="assemble",
    )(x_prompt, x_sample, meta, g.reshape(1, d))


def _final_kernel(nf, x_ref, m_ref, gp_ref, yp_ref, ys_ref):
    c = pl.program_id(0)
    x2 = x_ref[...] + _rms(m_ref[...], gp_ref[...])

    @pl.when(c < nf)
    def _():
        yp_ref[...] = x2

    @pl.when(c > nf)
    def _():
        ys_ref[...] = x2


def _final(x, m, g_post, seq):
    r, d = x.shape
    nf = seq // CH
    ns = r - seq - CH
    blk = pl.BlockSpec((CH, d), lambda c: (c, 0))
    return pl.pallas_call(
        functools.partial(_final_kernel, nf),
        grid=(r // CH,),
        in_specs=[blk, blk, pl.BlockSpec((1, d), lambda c: (0, 0))],
        out_specs=[pl.BlockSpec((CH, d), lambda c: (jnp.minimum(c, nf - 1), 0)),
                   pl.BlockSpec((CH, d), lambda c: (jnp.maximum(c - nf - 1, 0), 0))],
        out_shape=[jax.ShapeDtypeStruct((seq, d), F32), jax.ShapeDtypeStruct((ns, d), F32)],
        compiler_params=_cp(1), name="final",
    )(x, m, g_post.reshape(1, d))


def _resid_norm_kernel(x_ref, m_ref, gp_ref, gn_ref, x2_ref, h_ref):
    x2 = x_ref[...] + _rms(m_ref[...], gp_ref[...])
    x2_ref[...] = x2
    h_ref[...] = _rms(x2, gn_ref[...]).astype(BF16)


def _resid_norm(x, m, g_post, g_next):
    r, d = x.shape
    bm = _pick(r, EW_ROWS)
    row = pl.BlockSpec((bm, d), lambda i: (i, 0))
    vec = pl.BlockSpec((1, d), lambda i: (0, 0))
    return pl.pallas_call(
        _resid_norm_kernel, grid=(r // bm,), in_specs=[row, row, vec, vec], out_specs=[row, row],
        out_shape=[jax.ShapeDtypeStruct((r, d), F32), jax.ShapeDtypeStruct((r, d), BF16)],
        compiler_params=_cp(1), name="resid_norm",
    )(x, m, g_post.reshape(1, d), g_next.reshape(1, d))


def _inproj_kernel(x_ref, w_ref, u_ref, dt_ref):
    acc = jnp.dot(x_ref[...], w_ref[...], preferred_element_type=F32)
    u_ref[...] = acc.astype(BF16)

    @pl.when(pl.program_id(1) == pl.num_programs(1) - 1)
    def _():
        dt_ref[...] = acc[:, acc.shape[1] - LANE:]


def _inproj(h, w):
    r, k = h.shape
    n = w.shape[1]
    bm = _pick(r, MM_ROWS)
    bn = 768
    return pl.pallas_call(
        _inproj_kernel,
        grid=(r // bm, n // bn),
        in_specs=[pl.BlockSpec((bm, k), lambda i, j: (i, 0)), pl.BlockSpec((k, bn), lambda i, j: (0, j))],
        out_specs=[pl.BlockSpec((bm, bn), lambda i, j: (i, j)), pl.BlockSpec((bm, LANE), lambda i, j: (i, 0))],
        out_shape=[jax.ShapeDtypeStruct((r, n), BF16), jax.ShapeDtypeStruct((r, LANE), F32)],
        compiler_params=_cp(2), name="inproj",
    )(h, w)


def _outproj_kernel(a_ref, b_ref, c_ref, wa_ref, wb_ref, wc_ref, o_ref):
    acc = jnp.dot(a_ref[...], wa_ref[...], preferred_element_type=F32)
    acc = acc + jnp.dot(b_ref[...], wb_ref[...], preferred_element_type=F32)
    acc = acc + jnp.dot(c_ref[...], wc_ref[...], preferred_element_type=F32)
    o_ref[...] = acc


def _outproj(y_ssd, y_sc, y_mla, w):
    r = y_ssd.shape[0]
    n = w.shape[1]
    bm = _pick(r, MM_ROWS)
    bn = 1024
    return pl.pallas_call(
        _outproj_kernel,
        grid=(r // bm, n // bn),
        in_specs=[pl.BlockSpec((bm, SSD_DI), lambda i, j: (i, 0)),
                  pl.BlockSpec((bm, SC_D), lambda i, j: (i, 0)),
                  pl.BlockSpec((bm, SC_D), lambda i, j: (i, 0)),
                  pl.BlockSpec((SSD_DI, bn), lambda i, j: (0, j)),
                  pl.BlockSpec((SC_D, bn), lambda i, j: (2, j)),
                  pl.BlockSpec((SC_D, bn), lambda i, j: (3, j))],
        out_specs=pl.BlockSpec((bm, bn), lambda i, j: (i, j)),
        out_shape=jax.ShapeDtypeStruct((r, n), F32),
        compiler_params=_cp(2), name="outproj",
    )(y_ssd, y_sc, y_mla, w, w, w)


def _gateup_kernel(x_ref, wg_ref, wu_ref, g_ref, u_ref):
    x = x_ref[...]
    g_ref[...] = jnp.dot(x, wg_ref[...], preferred_element_type=F32).astype(BF16)
    u_ref[...] = jnp.dot(x, wu_ref[...], preferred_element_type=F32).astype(BF16)


def _gateup(h, wg, wu):
    r, k = h.shape
    n = wg.shape[1]
    bm = _pick(r, MM_ROWS)
    bn = 512
    wspec = pl.BlockSpec((k, bn), lambda i, j: (0, j))
    ospec = pl.BlockSpec((bm, bn), lambda i, j: (i, j))
    return pl.pallas_call(
        _gateup_kernel,
        grid=(r // bm, n // bn),
        in_specs=[pl.BlockSpec((bm, k), lambda i, j: (i, 0)), wspec, wspec],
        out_specs=[ospec, ospec],
        out_shape=[jax.ShapeDtypeStruct((r, n), BF16)] * 2,
        compiler_params=_cp(2), name="gateup",
    )(h, wg, wu)


def _down_kernel(x_ref, w_ref, o_ref):
    p = jnp.dot(x_ref[...], w_ref[...], preferred_element_type=F32)

    @pl.when(pl.program_id(2) == 0)
    def _():
        o_ref[...] = p

    @pl.when(pl.program_id(2) > 0)
    def _():
        o_ref[...] += p


def _down(a, w):
    r, k = a.shape
    n = w.shape[1]
    bm = _pick(r, MM_ROWS)
    bn = 1024
    bk = k // 4
    return pl.pallas_call(
        _down_kernel,
        grid=(r // bm, n // bn, k // bk),
        in_specs=[pl.BlockSpec((bm, bk), lambda i, j, kk: (i, kk)), pl.BlockSpec((bk, bn), lambda i, j, kk: (kk, j))],
        out_specs=pl.BlockSpec((bm, bn), lambda i, j, kk: (i, j)),
        out_shape=jax.ShapeDtypeStruct((r, n), F32),
        compiler_params=_cp(3), name="down",
    )(a, w)


def _mixer_kernel(nskip, z_ref, x_ref, bc_ref, dt_ref, scb_ref, scc_ref, sch_ref,
                  cprev_ref, h0_ref, scprev_ref,
                  cw_ref, cb_ref, dtb_ref, alog_ref, dsk_ref, ng_ref, scw_ref, pm_ref,
                  y_ref, ysc_ref, cnew_ref, hnew_ref, scnew_ref,
                  buf, sbuf, hst):
    t = CH
    c = pl.program_id(1)
    last = pl.num_programs(1) - 1

    ns = SSD_K - 1

    @pl.when(c == 0)
    def _init():
        _load_tail(buf, cprev_ref[0])
        sbuf[0:8, :] = jnp.zeros((8, SC_D), F32)
        sbuf[8 - (SC_K - 1):8, :] = scprev_ref[0]
        hst[...] = h0_ref[0]

    rows = lax.broadcasted_iota(jnp.int32, (t, 1), 0) + c * t
    valid = rows >= nskip

    xin = jnp.concatenate([x_ref[...], bc_ref[...]], axis=1)
    xin = jnp.where(valid, xin, jnp.zeros((), xin.dtype))
    delayed = _delayed(pm_ref, buf, xin, ns)
    acc = cb_ref[...] + cw_ref[ns:ns + 1, :] * xin.astype(F32)
    for d in range(1, ns + 1):
        acc = acc + cw_ref[ns - d:ns - d + 1, :] * delayed[d - 1]
    xbc = _silu(acc)

    @pl.when(c == last)
    def _():
        for r in range(ns):
            cnew_ref[0, r:r + 1, :] = buf[3 * r:3 * r + 1, :]

    lane = lax.broadcasted_iota(jnp.int32, (t, LANE), 1)
    v = dt_ref[...] + dtb_ref[...]
    dt = jnp.maximum(v, 0.0) + jnp.log1p(jnp.exp(-jnp.abs(v)))
    dt = jnp.where(valid & (lane < SSD_H), dt, 0.0)
    adt = dt * (-jnp.exp(alog_ref[...]))
    ri = lax.broadcasted_iota(jnp.int32, (t, t), 0)
    ci = lax.broadcasted_iota(jnp.int32, (t, t), 1)
    tri = ri >= ci
    acs = jnp.dot(tri.astype(F32), adt, precision=HI, preferred_element_type=F32)

    hpg = SSD_H // SSD_G
    gw = hpg * SSD_P
    lane_g = lax.broadcasted_iota(jnp.int32, (t, gw), 1)
    row_g = lax.broadcasted_iota(jnp.int32, (t, gw), 0)
    seg = lane_g // SSD_P
    eye_g = row_g == lane_g % SSD_P
    tri_g = row_g >= lane_g % SSD_P
    blockdiag = (lax.broadcasted_iota(jnp.int32, (hpg * t, gw), 0) // t
                 == lax.broadcasted_iota(jnp.int32, (hpg * t, gw), 1) // SSD_P)

    def per_head_lanes(mat, g):
        out = jnp.broadcast_to(mat[:, g * hpg + hpg - 1:g * hpg + hpg], (t, gw))
        for r in range(hpg - 2, -1, -1):
            out = jnp.where(seg == r, jnp.broadcast_to(mat[:, g * hpg + r:g * hpg + r + 1], (t, gw)), out)
        return out

    for g in range(SSD_G):
        cols = slice(g * gw, (g + 1) * gw)
        bg = xbc[:, SSD_DI + g * SSD_N:SSD_DI + (g + 1) * SSD_N].astype(BF16)
        cg = xbc[:, SSD_DI + SSD_G * SSD_N + g * SSD_N:SSD_DI + SSD_G * SSD_N + (g + 1) * SSD_N].astype(BF16)
        acol = per_head_lanes(acs, g)
        dtx = per_head_lanes(dt, g)
        arow = jnp.sum(jnp.where(eye_g, acol, 0.0), axis=0, keepdims=True)
        decay = jnp.exp(jnp.where(tri_g, acol - arow, -jnp.inf))
        cb = lax.dot_general(cg, jnp.concatenate([bg] * hpg, axis=0), NT, preferred_element_type=F32)
        xg = xbc[:, cols]
        xdt = xg * dtx
        xdtb = xdt.astype(BF16)
        rhs = jnp.where(blockdiag, jnp.concatenate([xdtb] * hpg, axis=0), jnp.zeros((), BF16))
        ydiag = jnp.dot((cb * decay).astype(BF16), rhs, preferred_element_type=F32)
        hg = hst[:, cols]
        yoff = jnp.dot(cg, hg.astype(BF16), preferred_element_type=F32) * jnp.exp(acol)
        alast = acol[t - 1:t, :]
        snew = lax.dot_general(bg, (xdt * jnp.exp(alast - acol)).astype(BF16), TN, preferred_element_type=F32)
        hst[:, cols] = jnp.exp(alast) * hg + snew
        y = ydiag + yoff + dsk_ref[:, cols] * xg
        y = y * _silu(z_ref[:, cols].astype(F32))
        y_ref[:, cols] = _rms(y, ng_ref[:, cols]).astype(BF16)

    @pl.when(c == last)
    def _():
        hnew_ref[0] = hst[...]

    p = scc_ref[...].astype(F32) * sch_ref[...].astype(F32)
    p = jnp.where(valid, p, 0.0)
    sbuf[8:8 + t, :] = p
    conv = scw_ref[0:1, :] * sbuf[6:6 + t, :]
    for i in range(1, SC_K):
        conv = conv + scw_ref[i:i + 1, :] * sbuf[6 + i:6 + i + t, :]
    ysc_ref[...] = (scb_ref[...].astype(F32) * conv).astype(BF16)
    sbuf[0:8, :] = sbuf[t:t + 8, :]

    @pl.when(c == last)
    def _():
        scnew_ref[0] = sbuf[8 - (SC_K - 1):8, :]


def _mixer_alias_kernel(nskip, *refs):
    n_in = 18
    _mixer_kernel(nskip, *refs[:n_in], *refs[n_in + 2:])


def _mixer(u, dtf, rb0, nb, nch, nskip, cprev, h0, scprev, lw, keep=None):
    r = u.shape[0]
    t = CH

    def rowmap(cb):
        return lambda b, c: (rb0 + b * nch + _stored_chunk(c, nch), cb)

    def stmap(b, c):
        return (b, 0, 0)

    def full(shape):
        return pl.BlockSpec(shape, lambda b, c: (0,) * len(shape))

    in_specs = [
        pl.BlockSpec((t, SSD_DI), rowmap(C_Z // SSD_DI)),
        pl.BlockSpec((t, SSD_DI), rowmap(C_X // SSD_DI)),
        pl.BlockSpec((t, SSD_DI), rowmap(C_BC // SSD_DI)),
        pl.BlockSpec((t, LANE), rowmap(0)),
        pl.BlockSpec((t, SC_D), rowmap(C_SCB // SC_D)),
        pl.BlockSpec((t, SC_D), rowmap(C_SCC // SC_D)),
        pl.BlockSpec((t, SC_D), rowmap(C_SCH // SC_D)),
        pl.BlockSpec((1, SSD_K - 1, SSD_CD), stmap),
        pl.BlockSpec((1, SSD_N, SSD_DI), stmap),
        pl.BlockSpec((1, SC_K - 1, SC_D), stmap),
        full((SSD_K, SSD_CD)), full((1, SSD_CD)), full((1, LANE)), full((1, LANE)),
        full((1, SSD_DI)), full((1, SSD_DI)), full((SC_K, SC_D)),
        full((CH * (SSD_K - 1) + TAIL, TAIL + CH)),
    ]
    out_specs = [
        pl.BlockSpec((t, SSD_DI), rowmap(0)),
        pl.BlockSpec((t, SC_D), rowmap(0)),
        pl.BlockSpec((1, SSD_K - 1, SSD_CD), stmap),
        pl.BlockSpec((1, SSD_N, SSD_DI), stmap),
        pl.BlockSpec((1, SC_K - 1, SC_D), stmap),
    ]
    out_shape = [
        jax.ShapeDtypeStruct((r, SSD_DI), BF16),
        jax.ShapeDtypeStruct((r, SC_D), BF16),
        jax.ShapeDtypeStruct((nb, SSD_K - 1, SSD_CD), F32),
        jax.ShapeDtypeStruct((nb, SSD_N, SSD_DI), F32),
        jax.ShapeDtypeStruct((nb, SC_K - 1, SC_D), F32),
    ]
    args = [u, u, u, dtf, u, u, u, cprev, h0, scprev,
            lw["cw"], lw["cb"], lw["dtb"], lw["alog"], lw["dsk"], lw["ng"], lw["scw"],
            jnp.asarray(_shift_matrix(SSD_K), BF16)]
    kern = functools.partial(_mixer_kernel, nskip)
    aliases = {}
    if keep is not None:
        aliases = {len(args): 0, len(args) + 1: 1}
        args += list(keep)
        in_specs += [pl.BlockSpec(memory_space=pl.ANY)] * 2
        kern = functools.partial(_mixer_alias_kernel, nskip)
    return pl.pallas_call(
        kern,
        grid=(nb, nch),
        in_specs=in_specs, out_specs=out_specs, out_shape=out_shape,
        scratch_shapes=[pltpu.VMEM((TAIL, SSD_CD), F32), pltpu.VMEM((8 + t, SC_D), F32),
                        pltpu.VMEM((SSD_N, SSD_DI), F32)],
        input_output_aliases=aliases,
        compiler_params=_cp(2), name="mixer",
    )(*args)


def _mla_proj_kernel(blk_ref, tab_ref, qn_ref, kvn_ref, wa_ref, wb_ref, wk_ref, wv_ref,
                     q_ref, ckv_ref, kpe_ref, kcat_ref, v_ref):
    blk = blk_ref[...]
    cq = _rms(blk[:, :QL].astype(F32), qn_ref[...]).astype(BF16)
    ckv = _rms(blk[:, QL:QL + KVL].astype(F32), kvn_ref[...])
    ckv_ref[...] = ckv
    tab = tab_ref[...]
    tabr = pltpu.roll(tab, ROPE, 1)
    prod = blk[:, QL + KVL:QL + KVL + LANE].astype(F32) * tab
    ksum = prod + pltpu.roll(prod, ROPE, 1)
    lane = lax.broadcasted_iota(jnp.int32, ksum.shape, 1)
    kpe = jnp.where(lane < ROPE, ksum, 0.0)
    kpe_ref[...] = kpe
    qa = jnp.dot(cq, wa_ref[...], preferred_element_type=F32)
    qb = jnp.dot(cq, wb_ref[...], preferred_element_type=F32)
    ckvb = ckv.astype(BF16)
    kn = jnp.dot(ckvb, wk_ref[...], preferred_element_type=F32)
    v_ref[...] = jnp.dot(ckvb, wv_ref[...], preferred_element_type=F32).astype(BF16)
    kpeb = kpe.astype(BF16)
    tab_q = tab * QSCALE
    tabr_q = tabr * QSCALE
    for h in range(MLA_H):
        q_ref[:, h * HK:h * HK + NOPE] = (qa[:, h * HK:h * HK + NOPE] * QSCALE).astype(BF16)
        q_ref[:, h * HK + NOPE:(h + 1) * HK] = (
            qa[:, h * HK + NOPE:(h + 1) * HK] * tab_q + qb[:, h * LANE:(h + 1) * LANE] * tabr_q).astype(BF16)
        kcat_ref[:, h * HK:h * HK + NOPE] = kn[:, h * NOPE:(h + 1) * NOPE].astype(BF16)
        kcat_ref[:, h * HK + NOPE:(h + 1) * HK] = kpeb


def _mla_proj(u, tab, lw):
    r = u.shape[0]
    bm = _pick(r, (544, 512, 272, 256, 136, 128, 64))

    def full(shape):
        return pl.BlockSpec(shape, lambda i: (0,) * len(shape))

    def row(w):
        return pl.BlockSpec((bm, w), lambda i: (i, 0))

    return pl.pallas_call(
        _mla_proj_kernel,
        grid=(r // bm,),
        in_specs=[pl.BlockSpec((bm, MLA_W), lambda i: (i, C_MLA // MLA_W)), row(LANE),
                  full((1, QL)), full((1, KVL)), full((QL, MLA_H * HK)), full((QL, MLA_H * LANE)),
                  full((KVL, MLA_H * NOPE)), full((KVL, MLA_H * MLA_V))],
        out_specs=[row(MLA_H * HK), row(KVL), row(LANE), row(MLA_H * HK), row(MLA_H * MLA_V)],
        out_shape=[jax.ShapeDtypeStruct((r, MLA_H * HK), BF16), jax.ShapeDtypeStruct((r, KVL), F32),
                   jax.ShapeDtypeStruct((r, LANE), F32), jax.ShapeDtypeStruct((r, MLA_H * HK), BF16),
                   jax.ShapeDtypeStruct((r, MLA_H * MLA_V), BF16)],
        compiler_params=_cp(1), name="mla_proj",
    )(u, tab, lw["qn"], lw["kvn"], lw["wa"], lw["wb"], lw["wk"], lw["wv"])


def _flash_kernel(it_ref, jt_ref, q_ref, k_ref, v_ref, k0_ref, v0_ref, mask_ref, pad_ref, o_ref,
                  qt_s, m_s, l_s, acc_s, s_scr, p_scr, a_s):
    tq = q_ref.shape[0]
    tk = k_ref.shape[0]
    step = pl.program_id(0)
    i = it_ref[step]
    j = jt_ref[step]

    @pl.when(j == 0)
    def _init():
        for h in range(MLA_H):
            qt_s[h * HK:(h + 1) * HK, :] = q_ref[:, h * HK:(h + 1) * HK].T
        for h in range(MLA_H):
            s0 = jnp.dot(k0_ref[:, h * HK:(h + 1) * HK], qt_s[h * HK:(h + 1) * HK, :],
                         preferred_element_type=F32) + pad_ref[...]
            m0 = jnp.max(s0, axis=0, keepdims=True)
            p0 = jnp.exp2(s0 - m0)
            rows = slice(h * MLA_V, (h + 1) * MLA_V)
            m_s[h] = m0
            l_s[h] = jnp.sum(p0, axis=0, keepdims=True)
            acc_s[rows, :] = lax.dot_general(v0_ref[:, rows], p0.astype(BF16), TN, preferred_element_type=F32)

    def tile(masked):
        nkb = tk // CH

        def block(sref, kb):
            blk = sref[kb * CH:(kb + 1) * CH, :]
            if masked:
                blk = blk + mask_ref[kb * CH:(kb + 1) * CH, :]
            return blk

        def scores(h):
            s_scr[h % 2] = jnp.dot(k_ref[:, h * HK:(h + 1) * HK], qt_s[h * HK:(h + 1) * HK, :],
                                   preferred_element_type=F32)

        def softmax(h):
            sref = s_scr.at[h % 2]
            pref = p_scr.at[h % 2]
            mx = block(sref, 0).reshape(CH // 8, 8, tq).max(axis=0)
            for kb in range(1, nkb):
                mx = jnp.maximum(mx, block(sref, kb).reshape(CH // 8, 8, tq).max(axis=0))
            m_prev = m_s[h]
            m_new = jnp.maximum(m_prev, jnp.max(mx, axis=0, keepdims=True))
            alpha = jnp.exp2(m_prev - m_new)
            part = jnp.zeros((8, tq), F32)
            for kb in range(nkb):
                e = jnp.exp2(block(sref, kb) - m_new)
                part = part + jnp.sum(e.reshape(CH // 8, 8, tq), axis=0)
                pref[kb * CH:(kb + 1) * CH, :] = e.astype(BF16)
            l_s[h] = alpha * l_s[h] + jnp.sum(part, axis=0, keepdims=True)
            m_s[h] = m_new
            a_s[h % 2] = alpha

        def weighted_values(h):
            rows = slice(h * MLA_V, (h + 1) * MLA_V)
            acc_s[rows, :] = a_s[h % 2] * acc_s[rows, :] + lax.dot_general(
                v_ref[:, rows], p_scr[h % 2], TN, preferred_element_type=F32)

        scores(0)
        for h in range(MLA_H + 1):
            if h + 1 < MLA_H:
                scores(h + 1)
            if h >= 1:
                weighted_values(h - 1)
            if h < MLA_H:
                softmax(h)

    @pl.when(j < i)
    def _():
        tile(False)

    @pl.when(j == i)
    def _():
        tile(True)
        for h in range(MLA_H):
            rows = slice(h * MLA_V, (h + 1) * MLA_V)
            o_ref[:, rows] = (acc_s[rows, :] / l_s[h]).T.astype(BF16)


def _pad_mask(n):
    return jnp.asarray(np.where(np.arange(CH)[:, None] + 0 * np.arange(n)[None, :] < NSKIP, -np.inf, 0.0),
                       F32)


def _flash(q, kcat, v, seq):
    r = q.shape[0]
    tq = _pick(seq, (FLASH_TQ, 256, 128))
    nq = seq // tq
    c0 = seq // CH
    it = np.concatenate([np.full((i + 1,), i, np.int32) for i in range(nq)])
    jt = np.concatenate([np.arange(i + 1, dtype=np.int32) for i in range(nq)])
    kk = np.arange(tq)[:, None]
    qq = np.arange(tq)[None, :]
    diag = jnp.asarray(np.where(kk // CH > qq // CH, -np.inf, 0.0), F32)
    grid_spec = pltpu.PrefetchScalarGridSpec(
        num_scalar_prefetch=2,
        grid=(int(it.shape[0]),),
        in_specs=[pl.BlockSpec((tq, MLA_H * HK), lambda s, it, jt: (it[s], 0)),
                  pl.BlockSpec((tq, MLA_H * HK), lambda s, it, jt: (jt[s], 0)),
                  pl.BlockSpec((tq, MLA_H * MLA_V), lambda s, it, jt: (jt[s], 0)),
                  pl.BlockSpec((CH, MLA_H * HK), lambda s, it, jt: (c0, 0)),
                  pl.BlockSpec((CH, MLA_H * MLA_V), lambda s, it, jt: (c0, 0)),
                  pl.BlockSpec((tq, tq), lambda s, it, jt: (0, 0)),
                  pl.BlockSpec((CH, tq), lambda s, it, jt: (0, 0))],
        out_specs=pl.BlockSpec((tq, MLA_H * MLA_V), lambda s, it, jt: (it[s], 0)),
        scratch_shapes=[pltpu.VMEM((MLA_H * HK, tq), BF16),
                        pltpu.VMEM((MLA_H, 1, tq), F32), pltpu.VMEM((MLA_H, 1, tq), F32),
                        pltpu.VMEM((MLA_H * MLA_V, tq), F32),
                        pltpu.VMEM((2, tq, tq), F32), pltpu.VMEM((2, tq, tq), BF16),
                        pltpu.VMEM((2, 1, tq), F32)],
    )
    return pl.pallas_call(
        _flash_kernel,
        grid_spec=grid_spec,
        out_shape=jax.ShapeDtypeStruct((r, MLA_H * MLA_V), BF16),
        compiler_params=_cp(1), name="flash",
    )(jnp.asarray(it), jnp.asarray(jt), q, kcat, v, kcat, v, diag, _pad_mask(tq))


def _attn0_kernel(q_ref, k_ref, v_ref, keep_ref, o_ref):
    del keep_ref
    col = lax.broadcasted_iota(jnp.int32, (CH, CH), 1)
    for h in range(MLA_H):
        s = lax.dot_general(q_ref[:, h * HK:(h + 1) * HK], k_ref[:, h * HK:(h + 1) * HK], NT,
                            preferred_element_type=F32)
        s = jnp.where(col >= NSKIP, s, -jnp.inf)
        p = jnp.exp2(s - jnp.max(s, axis=-1, keepdims=True))
        o = jnp.dot(p.astype(BF16), v_ref[:, h * MLA_V:(h + 1) * MLA_V], preferred_element_type=F32)
        o_ref[:, h * MLA_V:(h + 1) * MLA_V] = (o / jnp.sum(p, axis=-1, keepdims=True)).astype(BF16)


def _attn0(q, kcat, v, c0, o_frames):
    def blk(w):
        return pl.BlockSpec((CH, w), lambda i: (c0, 0))

    return pl.pallas_call(
        _attn0_kernel,
        grid=(1,),
        in_specs=[blk(MLA_H * HK), blk(MLA_H * HK), blk(MLA_H * MLA_V), pl.BlockSpec(memory_space=pl.ANY)],
        out_specs=blk(MLA_H * MLA_V),
        out_shape=jax.ShapeDtypeStruct(o_frames.shape, BF16),
        input_output_aliases={3: 0},
        compiler_params=_cp(1), name="attn0",
    )(q, kcat, v, o_frames)


def _cached_attn_kernel(q_ref, cnew_ref, pnew_ref, ckv_ref, cpe_ref, wkt_ref, wv_ref, o_ref, call, peall):
    past = ckv_ref.shape[2]
    call[0:past, :] = ckv_ref[0, 0].astype(BF16)
    call[past:past + CH, :] = cnew_ref[...].astype(BF16)
    peall[...] = jnp.zeros(peall.shape, BF16)
    peall[0:past, 0:ROPE] = cpe_ref[0, 0].astype(BF16)
    peall[past:past + CH, :] = pnew_ref[...].astype(BF16)
    qlat = []
    qpe = []
    for h in range(MLA_H):
        qn = q_ref[:, h * HK:h * HK + NOPE]
        qlat.append(jnp.dot(qn, wkt_ref[h], preferred_element_type=F32).astype(BF16))
        qpe.append(q_ref[:, h * HK + NOPE:(h + 1) * HK])
    qlat = jnp.concatenate(qlat, axis=0)
    qpe = jnp.concatenate(qpe, axis=0)
    s = (lax.dot_general(qlat, call[...], NT, preferred_element_type=F32)
         + lax.dot_general(qpe, peall[...], NT, preferred_element_type=F32))
    p = jnp.exp2(s - jnp.max(s, axis=-1, keepdims=True))
    p = p / jnp.sum(p, axis=-1, keepdims=True)
    olat = jnp.dot(p.astype(BF16), call[...], preferred_element_type=F32).astype(BF16)
    for h in range(MLA_H):
        o_ref[:, h * MLA_V:(h + 1) * MLA_V] = jnp.dot(
            olat[h * CH:(h + 1) * CH, :], wv_ref[h], preferred_element_type=F32).astype(BF16)


def _cached_attn_alias_kernel(q_ref, cnew_ref, pnew_ref, ckv_ref, cpe_ref, wkt_ref, wv_ref, keep_ref, o_ref,
                              call, peall):
    del keep_ref
    _cached_attn_kernel(q_ref, cnew_ref, pnew_ref, ckv_ref, cpe_ref, wkt_ref, wv_ref, o_ref, call, peall)


def _cached_attn(q, ckv, kpe, cache_kv, cache_pe, layer, rb0, lw, o_prompt):
    nb, past = cache_kv.shape[1], cache_kv.shape[2]

    def rowmap(b):
        return (rb0 + b, 0)

    return pl.pallas_call(
        _cached_attn_alias_kernel,
        grid=(nb,),
        in_specs=[pl.BlockSpec((CH, MLA_H * HK), rowmap),
                  pl.BlockSpec((CH, KVL), rowmap),
                  pl.BlockSpec((CH, LANE), rowmap),
                  pl.BlockSpec((1, 1, past, KVL), lambda b: (layer, b, 0, 0)),
                  pl.BlockSpec((1, 1, past, ROPE), lambda b: (layer, b, 0, 0)),
                  pl.BlockSpec((MLA_H, NOPE, KVL), lambda b: (0, 0, 0)),
                  pl.BlockSpec((MLA_H, KVL, MLA_V), lambda b: (0, 0, 0)),
                  pl.BlockSpec(memory_space=pl.ANY)],
        out_specs=pl.BlockSpec((CH, MLA_H * MLA_V), rowmap),
        out_shape=jax.ShapeDtypeStruct(o_prompt.shape, BF16),
        scratch_shapes=[pltpu.VMEM((past + CH, KVL), BF16), pltpu.VMEM((past + CH, LANE), BF16)],
        input_output_aliases={7: 0},
        compiler_params=_cp(1), name="cached_attn",
    )(q, ckv, kpe, cache_kv, cache_pe, lw["wkt"], lw["wv3"], o_prompt)


TAIL = 16


def _shift_matrix(k, rows=CH):
    ns = k - 1
    pm = np.zeros((rows * ns + TAIL, TAIL + rows), np.float32)
    for d in range(1, ns + 1):
        for t in range(rows):
            if t - d >= 0:
                pm[(d - 1) * rows + t, TAIL + t - d] = 1.0
            else:
                r = ns + t - d
                pm[(d - 1) * rows + t, 3 * r:3 * r + 3] = 1.0
    for r in range(ns):
        pm[rows * ns + 3 * r, TAIL + rows - ns + r] = 1.0
    return pm


def _load_tail(tail_s, prev):
    hi = prev.astype(BF16).astype(F32)
    rest = prev - hi
    mid = rest.astype(BF16).astype(F32)
    lo = (rest - mid).astype(BF16).astype(F32)
    tail_s[...] = jnp.zeros(tail_s.shape, F32)
    for r in range(prev.shape[0]):
        tail_s[3 * r:3 * r + 1, :] = hi[r:r + 1]
        tail_s[3 * r + 1:3 * r + 2, :] = mid[r:r + 1]
        tail_s[3 * r + 2:3 * r + 3, :] = lo[r:r + 1]


def _delayed(pm_ref, tail_s, x, ns):
    t = x.shape[0]
    ext = jnp.concatenate([tail_s[...].astype(BF16), x], axis=0)
    out = jnp.dot(pm_ref[...], ext, preferred_element_type=F32)
    tail_s[...] = out[t * ns:, :]
    return [out[(d - 1) * t:d * t, :] for d in range(1, ns + 1)]


def _act_kernel(nskip, g_ref, u_ref, prev_ref, w_ref, pm_ref, a_ref, new_ref, tail_s):
    t = g_ref.shape[0]
    ns = FFN_K - 1
    c = pl.program_id(1)

    @pl.when(c == 0)
    def _():
        _load_tail(tail_s, prev_ref[0])

    rows = lax.broadcasted_iota(jnp.int32, (t, 1), 0) + c * t
    g = g_ref[...]
    g = jnp.where(rows >= nskip, g, jnp.zeros((), g.dtype))
    delayed = _delayed(pm_ref, tail_s, g, ns)
    conv = w_ref[ns:ns + 1, :] * g.astype(F32)
    for d in range(1, ns + 1):
        conv = conv + w_ref[ns - d:ns - d + 1, :] * delayed[d - 1]
    a_ref[...] = (_silu(conv) * u_ref[...].astype(F32)).astype(BF16)

    @pl.when(c == pl.num_programs(1) - 1)
    def _():
        for r in range(ns):
            new_ref[0, r:r + 1, :] = tail_s[3 * r:3 * r + 1, :]


def _act(gate, up, rb0, nb, nch, nskip, prev, w, act_in=None, t=CH):
    r, n = gate.shape

    def rowmap(b, c):
        return (rb0 + b * nch + c, 0)

    pm = jnp.asarray(_shift_matrix(FFN_K, t), BF16)
    args = [gate, up, prev, w, pm]
    in_specs = [pl.BlockSpec((t, n), rowmap), pl.BlockSpec((t, n), rowmap),
                pl.BlockSpec((1, FFN_K - 1, n), lambda b, c: (b, 0, 0)),
                pl.BlockSpec((FFN_K, n), lambda b, c: (0, 0)),
                pl.BlockSpec(pm.shape, lambda b, c: (0, 0))]
    kern = functools.partial(_act_kernel, nskip)
    aliases = {}
    if act_in is not None:
        aliases = {len(args): 0}
        args.append(act_in)
        in_specs.append(pl.BlockSpec(memory_space=pl.ANY))
        kern = functools.partial(_act_alias_kernel, nskip)
    return pl.pallas_call(
        kern,
        grid=(nb, nch),
        in_specs=in_specs,
        out_specs=[pl.BlockSpec((t, n), rowmap), pl.BlockSpec((1, FFN_K - 1, n), lambda b, c: (b, 0, 0))],
        out_shape=[jax.ShapeDtypeStruct((r, n), BF16), jax.ShapeDtypeStruct((nb, FFN_K - 1, n), F32)],
        scratch_shapes=[pltpu.VMEM((TAIL, n), F32)],
        input_output_aliases=aliases,
        compiler_params=_cp(2), name="act",
    )(*args)


def _act_alias_kernel(nskip, g_ref, u_ref, prev_ref, w_ref, pm_ref, keep_ref, a_ref, new_ref, tail_s):
    del keep_ref
    _act_kernel(nskip, g_ref, u_ref, prev_ref, w_ref, pm_ref, a_ref, new_ref, tail_s)


def _cast_cols_kernel(x_ref, o_ref):
    n = x_ref.shape[1]
    o_ref[:, :n] = x_ref[...].astype(BF16)
    if o_ref.shape[1] > n:
        o_ref[:, n:] = jnp.zeros((o_ref.shape[0], o_ref.shape[1] - n), BF16)


def _cast_pad_cols(w, layer, n_out):
    _, k, n = w.shape
    tr = _pick(k, (128, 64))
    return pl.pallas_call(
        _cast_cols_kernel,
        grid=(k // tr,),
        in_specs=[pl.BlockSpec((None, tr, n), lambda i: (layer, i, 0))],
        out_specs=pl.BlockSpec((tr, n_out), lambda i: (i, 0)),
        out_shape=jax.ShapeDtypeStruct((k, n_out), BF16),
        compiler_params=_cp(1), name="cast_cols",
    )(w)


def _cast_rows_kernel(n_full, x_ref, o_ref):
    @pl.when(pl.program_id(0) < n_full)
    def _():
        o_ref[...] = x_ref[...].astype(BF16)

    @pl.when(pl.program_id(0) >= n_full)
    def _():
        o_ref[...] = jnp.zeros(o_ref.shape, BF16)


def _cast_pad_rows(w, layer, k_out):
    _, k, n = w.shape
    tr = 256
    assert k % tr == 0 and k_out % tr == 0
    n_full = k // tr
    return pl.pallas_call(
        functools.partial(_cast_rows_kernel, n_full),
        grid=(k_out // tr,),
        in_specs=[pl.BlockSpec((None, tr, n), lambda i: (layer, jnp.minimum(i, n_full - 1), 0))],
        out_specs=pl.BlockSpec((tr, n), lambda i: (i, 0)),
        out_shape=jax.ShapeDtypeStruct((k_out, n), BF16),
        compiler_params=_cp(1), name="cast_rows",
    )(w)


def _swap_half(w):
    half = w.shape[-1] // 2
    return jnp.concatenate([w[..., half:], w[..., :half]], axis=-1)


def _layer_weights(i, w_in, ssd_conv_w, ssd_conv_b, ssd_dt_bias, ssd_a_log, ssd_d, ssd_norm, sc_conv_w,
                   mla_q_norm, mla_w_uq, mla_kv_norm, mla_w_ukv, w_out, ffn_w_gate, ffn_w_up, ffn_conv_w,
                   ffn_w_down):
    w = w_in[i]
    o_dt = SSD_DI + SSD_CD
    o_sc = o_dt + SSD_H
    o_mla = o_sc + 3 * SC_D
    o_kr = o_mla + QL + KVL
    kr = w[:, o_kr:o_kr + ROPE]
    win = jnp.concatenate([p.astype(BF16) for p in (
        w[:, :o_dt], w[:, o_sc:o_mla], w[:, o_mla:o_kr], kr, _swap_half(kr), w[:, o_dt:o_sc],
        jnp.zeros((D_MODEL, LANE - SSD_H), F32))], axis=1)
    uq = mla_w_uq[i].reshape(QL, MLA_H, NOPE + ROPE)
    pe = uq[..., NOPE:]
    zq = jnp.zeros((QL, MLA_H, HK - NOPE - ROPE), F32)
    wa = jnp.concatenate([uq[..., :NOPE], pe, zq], axis=-1).reshape(QL, MLA_H * HK).astype(BF16)
    wb = jnp.concatenate([_swap_half(pe), zq], axis=-1).reshape(QL, MLA_H * LANE).astype(BF16)
    ukv = mla_w_ukv[i].reshape(KVL, MLA_H, NOPE + MLA_V)
    padf = ((0, 0), (0, DFP - D_FF))
    return dict(
        win=win,
        cw=ssd_conv_w[i], cb=ssd_conv_b[i].reshape(1, SSD_CD),
        dtb=jnp.pad(ssd_dt_bias[i], (0, LANE - SSD_H)).reshape(1, LANE),
        alog=jnp.pad(ssd_a_log[i], (0, LANE - SSD_H)).reshape(1, LANE),
        dsk=jnp.repeat(ssd_d[i], SSD_P).reshape(1, SSD_DI),
        ng=ssd_norm[i].reshape(1, SSD_DI),
        scw=sc_conv_w[i],
        qn=mla_q_norm[i].reshape(1, QL), kvn=mla_kv_norm[i].reshape(1, KVL),
        wa=wa, wb=wb,
        wk=ukv[..., :NOPE].reshape(KVL, MLA_H * NOPE).astype(BF16),
        wv=ukv[..., NOPE:].reshape(KVL, MLA_H * MLA_V).astype(BF16),
        wkt=jnp.transpose(ukv[..., :NOPE], (1, 2, 0)).astype(BF16),
        wv3=jnp.transpose(ukv[..., NOPE:], (1, 0, 2)).astype(BF16),
        wout=_cast_pad_cols(w_out, i, D_MODEL),
        wg=_cast_pad_cols(ffn_w_gate, i, DFP),
        wu=_cast_pad_cols(ffn_w_up, i, DFP),
        fcw=jnp.pad(ffn_conv_w[i], padf),
        wd=_cast_pad_rows(ffn_w_down, i, DFP),
    )


def _rope_table(seq, ns, past):
    half = ROPE // 2
    pos = jnp.concatenate([N_META + jnp.arange(seq, dtype=jnp.int32),
                           jnp.maximum(jnp.arange(CH, dtype=jnp.int32) - NSKIP, 0),
                           N_META + past + jnp.arange(ns, dtype=jnp.int32) % CH])
    inv = THETA ** (-jnp.arange(half, dtype=F32) / half)
    ang = pos.astype(F32)[:, None] * inv[None, :]
    cos, sin = jnp.cos(ang), jnp.sin(ang)
    return jnp.concatenate([cos, cos, -sin, sin], axis=1)


def kernel(x_prompt, x_sample, cache_kv_latent, cache_k_rope, state_ssm, state_ssd_conv, state_sconv, state_ffn_conv, meta_tokens, norm_mix_pre, norm_mix_post, norm_ffn_pre, norm_ffn_post, w_in, ssd_conv_w, ssd_conv_b, ssd_dt_bias, ssd_a_log, ssd_d, ssd_norm, sc_conv_w, mla_q_norm, mla_w_uq, mla_kv_norm, mla_w_ukv, w_out, ffn_w_gate, ffn_w_up, ffn_conv_w, ffn_w_down):
    bp, seq, d = x_prompt.shape
    nb, ls, _ = x_sample.shape
    depth, _, past, _ = cache_kv_latent.shape
    assert bp == 1 and ls == CH and seq % CH == 0 and d == D_MODEL
    lp = CH + seq
    ns = nb * ls
    npc = lp // CH
    c0 = seq // CH

    x, h = _assemble(x_prompt[0], x_sample.reshape(ns, d), meta_tokens.astype(F32), norm_mix_pre[0])
    tab = _rope_table(seq, ns, past)
    zero_c = jnp.zeros((1, SSD_K - 1, SSD_CD), F32)
    zero_h = jnp.zeros((1, SSD_N, SSD_DI), F32)

    def state_in(s):
        return jnp.transpose(s, (0, 3, 1, 2)).reshape(s.shape[0], SSD_N, SSD_DI)

    def state_out(s):
        return jnp.transpose(s.reshape(s.shape[0], SSD_N, SSD_H, SSD_P), (0, 2, 3, 1))
    zero_s = jnp.zeros((1, SC_K - 1, SC_D), F32)
    zero_f = jnp.zeros((1, FFN_K - 1, DFP), F32)
    padf = ((0, 0), (0, 0), (0, DFP - D_FF))

    outs_p, outs_s = [], []
    for i in range(depth):
        lw = _layer_weights(i, w_in, ssd_conv_w, ssd_conv_b, ssd_dt_bias, ssd_a_log, ssd_d, ssd_norm,
                            sc_conv_w, mla_q_norm, mla_w_uq, mla_kv_norm, mla_w_ukv, w_out, ffn_w_gate,
                            ffn_w_up, ffn_conv_w, ffn_w_down)
        u, dtf = _inproj(h, lw["win"])

        yp, yscp, cnew_p, hnew_p, scnew_p = _mixer(u, dtf, 0, 1, npc, NSKIP, zero_c, zero_h, zero_s, lw)
        y_ssd, y_sc, cnew_s, hnew_s, scnew_s = _mixer(u, dtf, npc, nb, 1, 0, state_ssd_conv[i],
                                                      state_in(state_ssm[i]), state_sconv[i], lw,
                                                      keep=(yp, yscp))
        hnew_p, hnew_s = state_out(hnew_p), state_out(hnew_s)

        q, ckv, kpe, kcat, v = _mla_proj(u, tab, lw)
        y_mla = _attn0(q, kcat, v, c0, _flash(q, kcat, v, seq))
        y_mla = _cached_attn(q, ckv, kpe, cache_kv_latent, cache_k_rope, i, npc, lw, y_mla)

        mix = _outproj(y_ssd, y_sc, y_mla, lw["wout"])
        x, h = _resid_norm(x, mix, norm_mix_post[i], norm_ffn_pre[i])

        gate, up = _gateup(h, lw["wg"], lw["wu"])
        act, f0 = _act(gate, up, c0, 1, 1, NSKIP, zero_f, lw["fcw"])
        act, fnew_p = _act(gate, up, 0, 1, seq // ACT_ROWS, 0, f0, lw["fcw"], act_in=act, t=ACT_ROWS)
        act, fnew_s = _act(gate, up, npc, nb, 1, 0, jnp.pad(state_ffn_conv[i], padf), lw["fcw"], act_in=act)
        f = _down(act, lw["wd"])
        if i + 1 < depth:
            x, h = _resid_norm(x, f, norm_ffn_post[i], norm_mix_pre[i + 1])
        else:
            y_prompt, y_sample = _final(x, f, norm_ffn_post[i], seq)

        meta_rows = slice(seq + NSKIP, lp)
        outs_p.append((jnp.concatenate([ckv[meta_rows], ckv[:seq]], axis=0)[None],
                       jnp.concatenate([kpe[meta_rows, :ROPE], kpe[:seq, :ROPE]], axis=0)[None],
                       hnew_p, cnew_p, scnew_p, fnew_p[:, :, :D_FF]))
        outs_s.append((ckv[lp:].reshape(nb, ls, KVL), kpe[lp:, :ROPE].reshape(nb, ls, ROPE), hnew_s, cnew_s,
                       scnew_s, fnew_s[:, :, :D_FF]))

    def stack(outs, j):
        return jnp.stack([o[j] for o in outs], axis=0)

    return (y_prompt[None], y_sample.reshape(nb, ls, d),
            stack(outs_p, 0), stack(outs_p, 1), stack(outs_p, 2), stack(outs_p, 3), stack(outs_p, 4), stack(outs_p, 5),
            stack(outs_s, 0), stack(outs_s, 1), stack(outs_s, 2), stack(outs_s, 3), stack(outs_s, 4), stack(outs_s, 5))
```

```python
import functools

import jax
import jax.numpy as jnp
import numpy as np
from jax import lax
from jax.experimental import pallas as pl
from jax.experimental.pallas import tpu as pltpu

F32 = jnp.float32
BF16 = jnp.bfloat16

D_MODEL = 4096
N_META = 16
CH = 64
NSKIP = CH - N_META
EPS = 1e-6
SSD_P = 64
SSD_DI = 2048
SSD_H = 32
SSD_G = 8
SSD_N = 128
SSD_K = 4
SSD_CD = 4096
SC_D = 1024
SC_K = 3
MLA_H = 8
NOPE = 128
ROPE = 64
MLA_V = 128
QL = 768
KVL = 512
HK = 256
SCALE = (NOPE + ROPE) ** -0.5
QSCALE = SCALE * 1.4426950408889634
THETA = 10000.0
D_FF = 11008
DFP = 11264
FFN_K = 3
LANE = 128

C_Z, C_X, C_BC, C_SCB, C_SCC, C_SCH, C_MLA, NIN = 0, 2048, 4096, 6144, 7168, 8192, 9216, 10752
MLA_W = 1536

FLASH_TQ = 512
ACT_ROWS = 128
VMEM_LIMIT = 56 * 1024 * 1024
HI = lax.Precision.HIGHEST
NT = (((1,), (1,)), ((), ()))
TN = (((0,), (0,)), ((), ()))


def _cp(n, flags=None):
    return pltpu.CompilerParams(dimension_semantics=("arbitrary",) * n, vmem_limit_bytes=VMEM_LIMIT, flags=flags)


def _pick(n, cands):
    for c in cands:
        if n % c == 0:
            return c
    raise ValueError(f"no tile for {n}")


MM_ROWS = (1088, 544, 512, 272, 256, 136, 128, 64)
EW_ROWS = (272, 256, 136, 128, 64)


def _stored_chunk(c, nch):
    return (c + nch - 1) % nch


def _rms(x, g):
    return x * lax.rsqrt(jnp.mean(x * x, axis=-1, keepdims=True) + EPS) * g


def _silu(x):
    return x * jax.nn.sigmoid(x)


def _norm_rows_kernel(x_in_ref, g_ref, x_ref, h_ref):
    x = x_in_ref[...]
    x_ref[...] = x
    h_ref[...] = _rms(x, g_ref[...]).astype(BF16)


def _tail_rows_kernel(xs_ref, meta_ref, g_ref, keep_x, keep_h, x_ref, h_ref):
    del keep_x, keep_h
    c = pl.program_id(0)

    def emit(x):
        x_ref[...] = x
        h_ref[...] = _rms(x, g_ref[...]).astype(BF16)

    @pl.when(c == 0)
    def _():
        emit(jnp.concatenate([jnp.zeros((NSKIP, x_ref.shape[1]), F32), meta_ref[...]], axis=0))

    @pl.when(c > 0)
    def _():
        emit(xs_ref[...])


def _assemble(x_prompt, x_sample, meta, g):
    seq, d = x_prompt.shape
    ns = x_sample.shape[0]
    r = seq + CH + ns
    bm = _pick(seq, (512, 256, 128, 64))
    vec = pl.BlockSpec((1, d), lambda c: (0, 0))
    out_shape = [jax.ShapeDtypeStruct((r, d), F32), jax.ShapeDtypeStruct((r, d), BF16)]
    big = pl.BlockSpec((bm, d), lambda c: (c, 0))
    x, h = pl.pallas_call(
        _norm_rows_kernel, grid=(seq // bm,), in_specs=[big, vec], out_specs=[big, big],
        out_shape=out_shape, compiler_params=_cp(1), name="assemble_frames",
    )(x_prompt, g.reshape(1, d))
    c0 = seq // CH
    small = pl.BlockSpec((CH, d), lambda c: (c0 + c, 0))
    anyspec = pl.BlockSpec(memory_space=pl.ANY)
    return pl.pallas_call(
        _tail_rows_kernel, grid=(1 + ns // CH,),
        in_specs=[pl.BlockSpec((CH, d), lambda c: (jnp.maximum(c - 1, 0), 0)),
                  pl.BlockSpec((N_META, d), lambda c: (0, 0)), vec, anyspec, anyspec],
        out_specs=[small, small], out_shape=out_shape, input_output_aliases={3: 0, 4: 1},
        compiler_params=_cp(1), name="assemble_tail",
    )(x_sample, meta, g.reshape(1, d), x, h)


def _resid_out_kernel(x_ref, m_ref, gp_ref, y_ref):
    y_ref[...] = x_ref[...] + _rms(m_ref[...], gp_ref[...])


def _resid_out(x, m, g_post, row0, rows):
    d = x.shape[1]
    bm = next(b for b in (512, 256, 128, 64) if rows % b == 0 and row0 % b == 0)
    src = pl.BlockSpec((bm, d), lambda c: (row0 // bm + c, 0))
    return pl.pallas_call(
        _resid_out_kernel, grid=(rows // bm,),
        in_specs=[src, src, pl.BlockSpec((1, d), lambda c: (0, 0))],
        out_specs=pl.BlockSpec((bm, d), lambda c: (c, 0)),
        out_shape=jax.ShapeDtypeStruct((rows, d), F32),
        compiler_params=_cp(1), name="resid_out",
    )(x, m, g_post.reshape(1, d))


def _resid_norm_kernel(x_ref, m_ref, gp_ref, gn_ref, x2_ref, h_ref):
    x2 = x_ref[...] + _rms(m_ref[...], gp_ref[...])
    x2_ref[...] = x2
    h_ref[...] = _rms(x2, gn_ref[...]).astype(BF16)


def _resid_norm(x, m, g_post, g_next):
    r, d = x.shape
    bm = _pick(r, EW_ROWS)
    row = pl.BlockSpec((bm, d), lambda i: (i, 0))
    vec = pl.BlockSpec((1, d), lambda i: (0, 0))
    return pl.pallas_call(
        _resid_norm_kernel, grid=(r // bm,), in_specs=[row, row, vec, vec], out_specs=[row, row],
        out_shape=[jax.ShapeDtypeStruct((r, d), F32), jax.ShapeDtypeStruct((r, d), BF16)],
        compiler_params=_cp(1), name="resid_norm",
    )(x, m, g_post.reshape(1, d), g_next.reshape(1, d))


def _inproj_kernel(x_ref, w_ref, u_ref, dt_ref):
    acc = jnp.dot(x_ref[...], w_ref[...], preferred_element_type=F32)
    u_ref[...] = acc.astype(BF16)

    @pl.when(pl.program_id(1) == pl.num_programs(1) - 1)
    def _():
        dt_ref[...] = acc[:, acc.shape[1] - LANE:]


def _inproj(h, w):
    r, k = h.shape
    n = w.shape[1]
    bm = _pick(r, MM_ROWS)
    bn = 768
    return pl.pallas_call(
        _inproj_kernel,
        grid=(r // bm, n // bn),
        in_specs=[pl.BlockSpec((bm, k), lambda i, j: (i, 0)), pl.BlockSpec((k, bn), lambda i, j: (0, j))],
        out_specs=[pl.BlockSpec((bm, bn), lambda i, j: (i, j)), pl.BlockSpec((bm, LANE), lambda i, j: (i, 0))],
        out_shape=[jax.ShapeDtypeStruct((r, n), BF16), jax.ShapeDtypeStruct((r, LANE), F32)],
        compiler_params=_cp(2), name="inproj",
    )(h, w)


def _outproj_kernel(a_ref, b_ref, c_ref, wa_ref, wb_ref, wc_ref, o_ref):
    acc = jnp.dot(a_ref[...], wa_ref[...], preferred_element_type=F32)
    acc = acc + jnp.dot(b_ref[...], wb_ref[...], preferred_element_type=F32)
    acc = acc + jnp.dot(c_ref[...], wc_ref[...], preferred_element_type=F32)
    o_ref[...] = acc


def _outproj(y_ssd, y_sc, y_mla, w):
    r = y_ssd.shape[0]
    n = w.shape[1]
    bm = _pick(r, MM_ROWS)
    bn = 1024
    return pl.pallas_call(
        _outproj_kernel,
        grid=(r // bm, n // bn),
        in_specs=[pl.BlockSpec((bm, SSD_DI), lambda i, j: (i, 0)),
                  pl.BlockSpec((bm, SC_D), lambda i, j: (i, 0)),
                  pl.BlockSpec((bm, SC_D), lambda i, j: (i, 0)),
                  pl.BlockSpec((SSD_DI, bn), lambda i, j: (0, j)),
                  pl.BlockSpec((SC_D, bn), lambda i, j: (2, j)),
                  pl.BlockSpec((SC_D, bn), lambda i, j: (3, j))],
        out_specs=pl.BlockSpec((bm, bn), lambda i, j: (i, j)),
        out_shape=jax.ShapeDtypeStruct((r, n), F32),
        compiler_params=_cp(2), name="outproj",
    )(y_ssd, y_sc, y_mla, w, w, w)


def _gateup_kernel(x_ref, wg_ref, wu_ref, g_ref, u_ref):
    x = x_ref[...]
    g_ref[...] = jnp.dot(x, wg_ref[...], preferred_element_type=F32).astype(BF16)
    u_ref[...] = jnp.dot(x, wu_ref[...], preferred_element_type=F32).astype(BF16)


def _gateup(h, wg, wu):
    r, k = h.shape
    n = wg.shape[1]
    bm = _pick(r, MM_ROWS)
    bn = 512
    wspec = pl.BlockSpec((k, bn), lambda i, j: (0, j))
    ospec = pl.BlockSpec((bm, bn), lambda i, j: (i, j))
    return pl.pallas_call(
        _gateup_kernel,
        grid=(r // bm, n // bn),
        in_specs=[pl.BlockSpec((bm, k), lambda i, j: (i, 0)), wspec, wspec],
        out_specs=[ospec, ospec],
        out_shape=[jax.ShapeDtypeStruct((r, n), BF16)] * 2,
        compiler_params=_cp(2), name="gateup",
    )(h, wg, wu)


def _down_kernel(x_ref, w_ref, o_ref):
    p = jnp.dot(x_ref[...], w_ref[...], preferred_element_type=F32)

    @pl.when(pl.program_id(2) == 0)
    def _():
        o_ref[...] = p

    @pl.when(pl.program_id(2) > 0)
    def _():
        o_ref[...] += p


def _down(a, w):
    r, k = a.shape
    n = w.shape[1]
    bm = _pick(r, MM_ROWS)
    bn = 1024
    bk = k // 4
    return pl.pallas_call(
        _down_kernel,
        grid=(r // bm, n // bn, k // bk),
        in_specs=[pl.BlockSpec((bm, bk), lambda i, j, kk: (i, kk)), pl.BlockSpec((bk, bn), lambda i, j, kk: (kk, j))],
        out_specs=pl.BlockSpec((bm, bn), lambda i, j, kk: (i, j)),
        out_shape=jax.ShapeDtypeStruct((r, n), F32),
        compiler_params=_cp(3), name="down",
    )(a, w)


def _mixer_kernel(nskip, z_ref, x_ref, bc_ref, dt_ref, scb_ref, scc_ref, sch_ref,
                  cprev_ref, h0_ref, scprev_ref,
                  cw_ref, cb_ref, dtb_ref, alog_ref, dsk_ref, ng_ref, scw_ref, pm_ref,
                  y_ref, ysc_ref, cnew_ref, hnew_ref, scnew_ref,
                  buf, sbuf, hst):
    t = CH
    c = pl.program_id(1)
    last = pl.num_programs(1) - 1

    ns = SSD_K - 1

    @pl.when(c == 0)
    def _init():
        _load_tail(buf, cprev_ref[0])
        sbuf[0:8, :] = jnp.zeros((8, SC_D), F32)
        sbuf[8 - (SC_K - 1):8, :] = scprev_ref[0]
        hst[...] = h0_ref[0]

    rows = lax.broadcasted_iota(jnp.int32, (t, 1), 0) + c * t
    valid = rows >= nskip

    xin = jnp.concatenate([x_ref[...], bc_ref[...]], axis=1)
    xin = jnp.where(valid, xin, jnp.zeros((), xin.dtype))
    delayed = _delayed(pm_ref, buf, xin, ns)
    acc = cb_ref[...] + cw_ref[ns:ns + 1, :] * xin.astype(F32)
    for d in range(1, ns + 1):
        acc = acc + cw_ref[ns - d:ns - d + 1, :] * delayed[d - 1]
    xbc = _silu(acc)

    @pl.when(c == last)
    def _():
        for r in range(ns):
            cnew_ref[0, r:r + 1, :] = buf[3 * r:3 * r + 1, :]

    lane = lax.broadcasted_iota(jnp.int32, (t, LANE), 1)
    v = dt_ref[...] + dtb_ref[...]
    dt = jnp.maximum(v, 0.0) + jnp.log1p(jnp.exp(-jnp.abs(v)))
    dt = jnp.where(valid & (lane < SSD_H), dt, 0.0)
    adt = dt * (-jnp.exp(alog_ref[...]))
    ri = lax.broadcasted_iota(jnp.int32, (t, t), 0)
    ci = lax.broadcasted_iota(jnp.int32, (t, t), 1)
    tri = ri >= ci
    acs = jnp.dot(tri.astype(F32), adt, precision=HI, preferred_element_type=F32)

    hpg = SSD_H // SSD_G
    gw = hpg * SSD_P
    lane_g = lax.broadcasted_iota(jnp.int32, (t, gw), 1)
    row_g = lax.broadcasted_iota(jnp.int32, (t, gw), 0)
    seg = lane_g // SSD_P
    eye_g = row_g == lane_g % SSD_P
    tri_g = row_g >= lane_g % SSD_P
    blockdiag = (lax.broadcasted_iota(jnp.int32, (hpg * t, gw), 0) // t
                 == lax.broadcasted_iota(jnp.int32, (hpg * t, gw), 1) // SSD_P)

    def per_head_lanes(mat, g):
        out = jnp.broadcast_to(mat[:, g * hpg + hpg - 1:g * hpg + hpg], (t, gw))
        for r in range(hpg - 2, -1, -1):
            out = jnp.where(seg == r, jnp.broadcast_to(mat[:, g * hpg + r:g * hpg + r + 1], (t, gw)), out)
        return out

    for g in range(SSD_G):
        cols = slice(g * gw, (g + 1) * gw)
        bg = xbc[:, SSD_DI + g * SSD_N:SSD_DI + (g + 1) * SSD_N].astype(BF16)
        cg = xbc[:, SSD_DI + SSD_G * SSD_N + g * SSD_N:SSD_DI + SSD_G * SSD_N + (g + 1) * SSD_N].astype(BF16)
        acol = per_head_lanes(acs, g)
        dtx = per_head_lanes(dt, g)
        arow = jnp.sum(jnp.where(eye_g, acol, 0.0), axis=0, keepdims=True)
        decay = jnp.exp(jnp.where(tri_g, acol - arow, -jnp.inf))
        cb = lax.dot_general(cg, jnp.concatenate([bg] * hpg, axis=0), NT, preferred_element_type=F32)
        xg = xbc[:, cols]
        xdt = xg * dtx
        xdtb = xdt.astype(BF16)
        rhs = jnp.where(blockdiag, jnp.concatenate([xdtb] * hpg, axis=0), jnp.zeros((), BF16))
        ydiag = jnp.dot((cb * decay).astype(BF16), rhs, preferred_element_type=F32)
        hg = hst[:, cols]
        yoff = jnp.dot(cg, hg.astype(BF16), preferred_element_type=F32) * jnp.exp(acol)
        alast = acol[t - 1:t, :]
        snew = lax.dot_general(bg, (xdt * jnp.exp(alast - acol)).astype(BF16), TN, preferred_element_type=F32)
        hst[:, cols] = jnp.exp(alast) * hg + snew
        y = ydiag + yoff + dsk_ref[:, cols] * xg
        y = y * _silu(z_ref[:, cols].astype(F32))
        y_ref[:, cols] = _rms(y, ng_ref[:, cols]).astype(BF16)

    @pl.when(c == last)
    def _():
        hnew_ref[0] = hst[...]

    p = scc_ref[...].astype(F32) * sch_ref[...].astype(F32)
    p = jnp.where(valid, p, 0.0)
    sbuf[8:8 + t, :] = p
    conv = scw_ref[0:1, :] * sbuf[6:6 + t, :]
    for i in range(1, SC_K):
        conv = conv + scw_ref[i:i + 1, :] * sbuf[6 + i:6 + i + t, :]
    ysc_ref[...] = (scb_ref[...].astype(F32) * conv).astype(BF16)
    sbuf[0:8, :] = sbuf[t:t + 8, :]

    @pl.when(c == last)
    def _():
        scnew_ref[0] = sbuf[8 - (SC_K - 1):8, :]


def _mixer_alias_kernel(nskip, *refs):
    n_in = 18
    _mixer_kernel(nskip, *refs[:n_in], *refs[n_in + 2:])


def _mixer(u, dtf, rb0, nb, nch, nskip, cprev, h0, scprev, lw, keep=None):
    r = u.shape[0]
    t = CH

    def rowmap(cb):
        return lambda b, c: (rb0 + b * nch + _stored_chunk(c, nch), cb)

    def stmap(b, c):
        return (b, 0, 0)

    def full(shape):
        return pl.BlockSpec(shape, lambda b, c: (0,) * len(shape))

    in_specs = [
        pl.BlockSpec((t, SSD_DI), rowmap(C_Z // SSD_DI)),
        pl.BlockSpec((t, SSD_DI), rowmap(C_X // SSD_DI)),
        pl.BlockSpec((t, SSD_DI), rowmap(C_BC // SSD_DI)),
        pl.BlockSpec((t, LANE), rowmap(0)),
        pl.BlockSpec((t, SC_D), rowmap(C_SCB // SC_D)),
        pl.BlockSpec((t, SC_D), rowmap(C_SCC // SC_D)),
        pl.BlockSpec((t, SC_D), rowmap(C_SCH // SC_D)),
        pl.BlockSpec((1, SSD_K - 1, SSD_CD), stmap),
        pl.BlockSpec((1, SSD_N, SSD_DI), stmap),
        pl.BlockSpec((1, SC_K - 1, SC_D), stmap),
        full((SSD_K, SSD_CD)), full((1, SSD_CD)), full((1, LANE)), full((1, LANE)),
        full((1, SSD_DI)), full((1, SSD_DI)), full((SC_K, SC_D)),
        full((CH * (SSD_K - 1) + TAIL, TAIL + CH)),
    ]
    out_specs = [
        pl.BlockSpec((t, SSD_DI), rowmap(0)),
        pl.BlockSpec((t, SC_D), rowmap(0)),
        pl.BlockSpec((1, SSD_K - 1, SSD_CD), stmap),
        pl.BlockSpec((1, SSD_N, SSD_DI), stmap),
        pl.BlockSpec((1, SC_K - 1, SC_D), stmap),
    ]
    out_shape = [
        jax.ShapeDtypeStruct((r, SSD_DI), BF16),
        jax.ShapeDtypeStruct((r, SC_D), BF16),
        jax.ShapeDtypeStruct((nb, SSD_K - 1, SSD_CD), F32),
        jax.ShapeDtypeStruct((nb, SSD_N, SSD_DI), F32),
        jax.ShapeDtypeStruct((nb, SC_K - 1, SC_D), F32),
    ]
    args = [u, u, u, dtf, u, u, u, cprev, h0, scprev,
            lw["cw"], lw["cb"], lw["dtb"], lw["alog"], lw["dsk"], lw["ng"], lw["scw"],
            jnp.asarray(_shift_matrix(SSD_K), BF16)]
    kern = functools.partial(_mixer_kernel, nskip)
    aliases = {}
    if keep is not None:
        aliases = {len(args): 0, len(args) + 1: 1}
        args += list(keep)
        in_specs += [pl.BlockSpec(memory_space=pl.ANY)] * 2
        kern = functools.partial(_mixer_alias_kernel, nskip)
    return pl.pallas_call(
        kern,
        grid=(nb, nch),
        in_specs=in_specs, out_specs=out_specs, out_shape=out_shape,
        scratch_shapes=[pltpu.VMEM((TAIL, SSD_CD), F32), pltpu.VMEM((8 + t, SC_D), F32),
                        pltpu.VMEM((SSD_N, SSD_DI), F32)],
        input_output_aliases=aliases,
        compiler_params=_cp(2), name="mixer",
    )(*args)


def _mla_proj_kernel(blk_ref, tab_ref, qn_ref, kvn_ref, wa_ref, wb_ref, wk_ref, wv_ref,
                     q_ref, ckv_ref, kpe_ref, kcat_ref, v_ref):
    blk = blk_ref[...]
    cq = _rms(blk[:, :QL].astype(F32), qn_ref[...]).astype(BF16)
    ckv = _rms(blk[:, QL:QL + KVL].astype(F32), kvn_ref[...])
    ckv_ref[...] = ckv
    tab = tab_ref[...]
    tabr = pltpu.roll(tab, ROPE, 1)
    prod = blk[:, QL + KVL:QL + KVL + LANE].astype(F32) * tab
    ksum = prod + pltpu.roll(prod, ROPE, 1)
    lane = lax.broadcasted_iota(jnp.int32, ksum.shape, 1)
    kpe = jnp.where(lane < ROPE, ksum, 0.0)
    kpe_ref[...] = kpe
    qa = jnp.dot(cq, wa_ref[...], preferred_element_type=F32)
    qb = jnp.dot(cq, wb_ref[...], preferred_element_type=F32)
    ckvb = ckv.astype(BF16)
    kn = jnp.dot(ckvb, wk_ref[...], preferred_element_type=F32)
    v_ref[...] = jnp.dot(ckvb, wv_ref[...], preferred_element_type=F32).astype(BF16)
    kpeb = kpe.astype(BF16)
    tab_q = tab * QSCALE
    tabr_q = tabr * QSCALE
    for h in range(MLA_H):
        q_ref[:, h * HK:h * HK + NOPE] = (qa[:, h * HK:h * HK + NOPE] * QSCALE).astype(BF16)
        q_ref[:, h * HK + NOPE:(h + 1) * HK] = (
            qa[:, h * HK + NOPE:(h + 1) * HK] * tab_q + qb[:, h * LANE:(h + 1) * LANE] * tabr_q).astype(BF16)
        kcat_ref[:, h * HK:h * HK + NOPE] = kn[:, h * NOPE:(h + 1) * NOPE].astype(BF16)
        kcat_ref[:, h * HK + NOPE:(h + 1) * HK] = kpeb


def _mla_proj(u, tab, lw):
    r = u.shape[0]
    bm = _pick(r, (544, 512, 272, 256, 136, 128, 64))

    def full(shape):
        return pl.BlockSpec(shape, lambda i: (0,) * len(shape))

    def row(w):
        return pl.BlockSpec((bm, w), lambda i: (i, 0))

    return pl.pallas_call(
        _mla_proj_kernel,
        grid=(r // bm,),
        in_specs=[pl.BlockSpec((bm, MLA_W), lambda i: (i, C_MLA // MLA_W)), row(LANE),
                  full((1, QL)), full((1, KVL)), full((QL, MLA_H * HK)), full((QL, MLA_H * LANE)),
                  full((KVL, MLA_H * NOPE)), full((KVL, MLA_H * MLA_V))],
        out_specs=[row(MLA_H * HK), row(KVL), row(LANE), row(MLA_H * HK), row(MLA_H * MLA_V)],
        out_shape=[jax.ShapeDtypeStruct((r, MLA_H * HK), BF16), jax.ShapeDtypeStruct((r, KVL), F32),
                   jax.ShapeDtypeStruct((r, LANE), F32), jax.ShapeDtypeStruct((r, MLA_H * HK), BF16),
                   jax.ShapeDtypeStruct((r, MLA_H * MLA_V), BF16)],
        compiler_params=_cp(1), name="mla_proj",
    )(u, tab, lw["qn"], lw["kvn"], lw["wa"], lw["wb"], lw["wk"], lw["wv"])


def _flash_kernel(it_ref, jt_ref, q_ref, k_ref, v_ref, k0_ref, v0_ref, mask_ref, pad_ref, o_ref,
                  qt_s, m_s, l_s, acc_s, s_scr, p_scr, a_s):
    tq = q_ref.shape[0]
    tk = k_ref.shape[0]
    step = pl.program_id(0)
    i = it_ref[step]
    j = jt_ref[step]

    @pl.when(j == 0)
    def _init():
        for h in range(MLA_H):
            qt_s[h * HK:(h + 1) * HK, :] = q_ref[:, h * HK:(h + 1) * HK].T
        for h in range(MLA_H):
            s0 = jnp.dot(k0_ref[:, h * HK:(h + 1) * HK], qt_s[h * HK:(h + 1) * HK, :],
                         preferred_element_type=F32) + pad_ref[...]
            m0 = jnp.max(s0, axis=0, keepdims=True)
            p0 = jnp.exp2(s0 - m0)
            rows = slice(h * MLA_V, (h + 1) * MLA_V)
            m_s[h] = m0
            l_s[h] = jnp.sum(p0, axis=0, keepdims=True)
            acc_s[rows, :] = lax.dot_general(v0_ref[:, rows], p0.astype(BF16), TN, preferred_element_type=F32)

    def tile(masked):
        nkb = tk // CH

        def block(sref, kb):
            blk = sref[kb * CH:(kb + 1) * CH, :]
            if masked:
                blk = blk + mask_ref[kb * CH:(kb + 1) * CH, :]
            return blk

        def scores(h):
            s_scr[h % 2] = jnp.dot(k_ref[:, h * HK:(h + 1) * HK], qt_s[h * HK:(h + 1) * HK, :],
                                   preferred_element_type=F32)

        def softmax(h):
            sref = s_scr.at[h % 2]
            pref = p_scr.at[h % 2]
            mx = block(sref, 0).reshape(CH // 8, 8, tq).max(axis=0)
            for kb in range(1, nkb):
                mx = jnp.maximum(mx, block(sref, kb).reshape(CH // 8, 8, tq).max(axis=0))
            m_prev = m_s[h]
            m_new = jnp.maximum(m_prev, jnp.max(mx, axis=0, keepdims=True))
            alpha = jnp.exp2(m_prev - m_new)
            part = jnp.zeros((8, tq), F32)
            for kb in range(nkb):
                e = jnp.exp2(block(sref, kb) - m_new)
                part = part + jnp.sum(e.reshape(CH // 8, 8, tq), axis=0)
                pref[kb * CH:(kb + 1) * CH, :] = e.astype(BF16)
            l_s[h] = alpha * l_s[h] + jnp.sum(part, axis=0, keepdims=True)
            m_s[h] = m_new
            a_s[h % 2] = alpha

        def weighted_values(h):
            rows = slice(h * MLA_V, (h + 1) * MLA_V)
            acc_s[rows, :] = a_s[h % 2] * acc_s[rows, :] + lax.dot_general(
                v_ref[:, rows], p_scr[h % 2], TN, preferred_element_type=F32)

        scores(0)
        for h in range(MLA_H + 1):
            if h + 1 < MLA_H:
                scores(h + 1)
            if h >= 1:
                weighted_values(h - 1)
            if h < MLA_H:
                softmax(h)

    @pl.when(j < i)
    def _():
        tile(False)

    @pl.when(j == i)
    def _():
        tile(True)
        for h in range(MLA_H):
            rows = slice(h * MLA_V, (h + 1) * MLA_V)
            o_ref[:, rows] = (acc_s[rows, :] / l_s[h]).T.astype(BF16)


def _pad_mask(n):
    return jnp.asarray(np.where(np.arange(CH)[:, None] + 0 * np.arange(n)[None, :] < NSKIP, -np.inf, 0.0),
                       F32)


def _flash(q, kcat, v, seq):
    r = q.shape[0]
    tq = _pick(seq, (FLASH_TQ, 256, 128))
    nq = seq // tq
    c0 = seq // CH
    it = np.concatenate([np.full((i + 1,), i, np.int32) for i in range(nq)])
    jt = np.concatenate([np.arange(i + 1, dtype=np.int32) for i in range(nq)])
    kk = np.arange(tq)[:, None]
    qq = np.arange(tq)[None, :]
    diag = jnp.asarray(np.where(kk // CH > qq // CH, -np.inf, 0.0), F32)
    grid_spec = pltpu.PrefetchScalarGridSpec(
        num_scalar_prefetch=2,
        grid=(int(it.shape[0]),),
        in_specs=[pl.BlockSpec((tq, MLA_H * HK), lambda s, it, jt: (it[s], 0)),
                  pl.BlockSpec((tq, MLA_H * HK), lambda s, it, jt: (jt[s], 0)),
                  pl.BlockSpec((tq, MLA_H * MLA_V), lambda s, it, jt: (jt[s], 0)),
                  pl.BlockSpec((CH, MLA_H * HK), lambda s, it, jt: (c0, 0)),
                  pl.BlockSpec((CH, MLA_H * MLA_V), lambda s, it, jt: (c0, 0)),
                  pl.BlockSpec((tq, tq), lambda s, it, jt: (0, 0)),
                  pl.BlockSpec((CH, tq), lambda s, it, jt: (0, 0))],
        out_specs=pl.BlockSpec((tq, MLA_H * MLA_V), lambda s, it, jt: (it[s], 0)),
        scratch_shapes=[pltpu.VMEM((MLA_H * HK, tq), BF16),
                        pltpu.VMEM((MLA_H, 1, tq), F32), pltpu.VMEM((MLA_H, 1, tq), F32),
                        pltpu.VMEM((MLA_H * MLA_V, tq), F32),
                        pltpu.VMEM((2, tq, tq), F32), pltpu.VMEM((2, tq, tq), BF16),
                        pltpu.VMEM((2, 1, tq), F32)],
    )
    return pl.pallas_call(
        _flash_kernel,
        grid_spec=grid_spec,
        out_shape=jax.ShapeDtypeStruct((r, MLA_H * MLA_V), BF16),
        compiler_params=_cp(1), name="flash",
    )(jnp.asarray(it), jnp.asarray(jt), q, kcat, v, kcat, v, diag, _pad_mask(tq))


def _attn0_kernel(q_ref, k_ref, v_ref, keep_ref, o_ref):
    del keep_ref
    col = lax.broadcasted_iota(jnp.int32, (CH, CH), 1)
    for h in range(MLA_H):
        s = lax.dot_general(q_ref[:, h * HK:(h + 1) * HK], k_ref[:, h * HK:(h + 1) * HK], NT,
                            preferred_element_type=F32)
        s = jnp.where(col >= NSKIP, s, -jnp.inf)
        p = jnp.exp2(s - jnp.max(s, axis=-1, keepdims=True))
        o = jnp.dot(p.astype(BF16), v_ref[:, h * MLA_V:(h + 1) * MLA_V], preferred_element_type=F32)
        o_ref[:, h * MLA_V:(h + 1) * MLA_V] = (o / jnp.sum(p, axis=-1, keepdims=True)).astype(BF16)


def _attn0(q, kcat, v, c0, o_frames):
    def blk(w):
        return pl.BlockSpec((CH, w), lambda i: (c0, 0))

    return pl.pallas_call(
        _attn0_kernel,
        grid=(1,),
        in_specs=[blk(MLA_H * HK), blk(MLA_H * HK), blk(MLA_H * MLA_V), pl.BlockSpec(memory_space=pl.ANY)],
        out_specs=blk(MLA_H * MLA_V),
        out_shape=jax.ShapeDtypeStruct(o_frames.shape, BF16),
        input_output_aliases={3: 0},
        compiler_params=_cp(1), name="attn0",
    )(q, kcat, v, o_frames)


def _cached_attn_kernel(q_ref, cnew_ref, pnew_ref, ckv_ref, cpe_ref, wkt_ref, wv_ref, o_ref, call, peall):
    past = ckv_ref.shape[2]
    call[0:past, :] = ckv_ref[0, 0].astype(BF16)
    call[past:past + CH, :] = cnew_ref[...].astype(BF16)
    peall[...] = jnp.zeros(peall.shape, BF16)
    peall[0:past, 0:ROPE] = cpe_ref[0, 0].astype(BF16)
    peall[past:past + CH, :] = pnew_ref[...].astype(BF16)
    qlat = []
    qpe = []
    for h in range(MLA_H):
        qn = q_ref[:, h * HK:h * HK + NOPE]
        qlat.append(jnp.dot(qn, wkt_ref[h], preferred_element_type=F32).astype(BF16))
        qpe.append(q_ref[:, h * HK + NOPE:(h + 1) * HK])
    qlat = jnp.concatenate(qlat, axis=0)
    qpe = jnp.concatenate(qpe, axis=0)
    s = (lax.dot_general(qlat, call[...], NT, preferred_element_type=F32)
         + lax.dot_general(qpe, peall[...], NT, preferred_element_type=F32))
    p = jnp.exp2(s - jnp.max(s, axis=-1, keepdims=True))
    p = p / jnp.sum(p, axis=-1, keepdims=True)
    olat = jnp.dot(p.astype(BF16), call[...], preferred_element_type=F32).astype(BF16)
    for h in range(MLA_H):
        o_ref[:, h * MLA_V:(h + 1) * MLA_V] = jnp.dot(
            olat[h * CH:(h + 1) * CH, :], wv_ref[h], preferred_element_type=F32).astype(BF16)


def _cached_attn_alias_kernel(q_ref, cnew_ref, pnew_ref, ckv_ref, cpe_ref, wkt_ref, wv_ref, keep_ref, o_ref,
                              call, peall):
    del keep_ref
    _cached_attn_kernel(q_ref, cnew_ref, pnew_ref, ckv_ref, cpe_ref, wkt_ref, wv_ref, o_ref, call, peall)


def _cached_attn(q, ckv, kpe, cache_kv, cache_pe, layer, rb0, lw, o_prompt):
    nb, past = cache_kv.shape[1], cache_kv.shape[2]

    def rowmap(b):
        return (rb0 + b, 0)

    return pl.pallas_call(
        _cached_attn_alias_kernel,
        grid=(nb,),
        in_specs=[pl.BlockSpec((CH, MLA_H * HK), rowmap),
                  pl.BlockSpec((CH, KVL), rowmap),
                  pl.BlockSpec((CH, LANE), rowmap),
                  pl.BlockSpec((1, 1, past, KVL), lambda b: (layer, b, 0, 0)),
                  pl.BlockSpec((1, 1, past, ROPE), lambda b: (layer, b, 0, 0)),
                  pl.BlockSpec((MLA_H, NOPE, KVL), lambda b: (0, 0, 0)),
                  pl.BlockSpec((MLA_H, KVL, MLA_V), lambda b: (0, 0, 0)),
                  pl.BlockSpec(memory_space=pl.ANY)],
        out_specs=pl.BlockSpec((CH, MLA_H * MLA_V), rowmap),
        out_shape=jax.ShapeDtypeStruct(o_prompt.shape, BF16),
        scratch_shapes=[pltpu.VMEM((past + CH, KVL), BF16), pltpu.VMEM((past + CH, LANE), BF16)],
        input_output_aliases={7: 0},
        compiler_params=_cp(1), name="cached_attn",
    )(q, ckv, kpe, cache_kv, cache_pe, lw["wkt"], lw["wv3"], o_prompt)


TAIL = 16


def _shift_matrix(k, rows=CH):
    ns = k - 1
    pm = np.zeros((rows * ns + TAIL, TAIL + rows), np.float32)
    for d in range(1, ns + 1):
        for t in range(rows):
            if t - d >= 0:
                pm[(d - 1) * rows + t, TAIL + t - d] = 1.0
            else:
                r = ns + t - d
                pm[(d - 1) * rows + t, 3 * r:3 * r + 3] = 1.0
    for r in range(ns):
        pm[rows * ns + 3 * r, TAIL + rows - ns + r] = 1.0
    return pm


def _load_tail(tail_s, prev):
    hi = prev.astype(BF16).astype(F32)
    rest = prev - hi
    mid = rest.astype(BF16).astype(F32)
    lo = (rest - mid).astype(BF16).astype(F32)
    tail_s[...] = jnp.zeros(tail_s.shape, F32)
    for r in range(prev.shape[0]):
        tail_s[3 * r:3 * r + 1, :] = hi[r:r + 1]
        tail_s[3 * r + 1:3 * r + 2, :] = mid[r:r + 1]
        tail_s[3 * r + 2:3 * r + 3, :] = lo[r:r + 1]


def _delayed(pm_ref, tail_s, x, ns):
    t = x.shape[0]
    ext = jnp.concatenate([tail_s[...].astype(BF16), x], axis=0)
    out = jnp.dot(pm_ref[...], ext, preferred_element_type=F32)
    tail_s[...] = out[t * ns:, :]
    return [out[(d - 1) * t:d * t, :] for d in range(1, ns + 1)]


def _act_kernel(nskip, g_ref, u_ref, prev_ref, w_ref, pm_ref, a_ref, new_ref, tail_s):
    t = g_ref.shape[0]
    ns = FFN_K - 1
    c = pl.program_id(1)

    @pl.when(c == 0)
    def _():
        _load_tail(tail_s, prev_ref[0])

    rows = lax.broadcasted_iota(jnp.int32, (t, 1), 0) + c * t
    g = g_ref[...]
    g = jnp.where(rows >= nskip, g, jnp.zeros((), g.dtype))
    delayed = _delayed(pm_ref, tail_s, g, ns)
    conv = w_ref[ns:ns + 1, :] * g.astype(F32)
    for d in range(1, ns + 1):
        conv = conv + w_ref[ns - d:ns - d + 1, :] * delayed[d - 1]
    a_ref[...] = (_silu(conv) * u_ref[...].astype(F32)).astype(BF16)

    @pl.when(c == pl.num_programs(1) - 1)
    def _():
        for r in range(ns):
            new_ref[0, r:r + 1, :] = tail_s[3 * r:3 * r + 1, :]


def _act(gate, up, rb0, nb, nch, nskip, prev, w, act_in=None, t=CH):
    r, n = gate.shape

    def rowmap(b, c):
        return (rb0 + b * nch + c, 0)

    pm = jnp.asarray(_shift_matrix(FFN_K, t), BF16)
    args = [gate, up, prev, w, pm]
    in_specs = [pl.BlockSpec((t, n), rowmap), pl.BlockSpec((t, n), rowmap),
                pl.BlockSpec((1, FFN_K - 1, n), lambda b, c: (b, 0, 0)),
                pl.BlockSpec((FFN_K, n), lambda b, c: (0, 0)),
                pl.BlockSpec(pm.shape, lambda b, c: (0, 0))]
    kern = functools.partial(_act_kernel, nskip)
    aliases = {}
    if act_in is not None:
        aliases = {len(args): 0}
        args.append(act_in)
        in_specs.append(pl.BlockSpec(memory_space=pl.ANY))
        kern = functools.partial(_act_alias_kernel, nskip)
    return pl.pallas_call(
        kern,
        grid=(nb, nch),
        in_specs=in_specs,
        out_specs=[pl.BlockSpec((t, n), rowmap), pl.BlockSpec((1, FFN_K - 1, n), lambda b, c: (b, 0, 0))],
        out_shape=[jax.ShapeDtypeStruct((r, n), BF16), jax.ShapeDtypeStruct((nb, FFN_K - 1, n), F32)],
        scratch_shapes=[pltpu.VMEM((TAIL, n), F32)],
        input_output_aliases=aliases,
        compiler_params=_cp(2), name="act",
    )(*args)


def _act_alias_kernel(nskip, g_ref, u_ref, prev_ref, w_ref, pm_ref, keep_ref, a_ref, new_ref, tail_s):
    del keep_ref
    _act_kernel(nskip, g_ref, u_ref, prev_ref, w_ref, pm_ref, a_ref, new_ref, tail_s)


def _cast_cols_kernel(x_ref, o_ref):
    n = x_ref.shape[1]
    o_ref[:, :n] = x_ref[...].astype(BF16)
    if o_ref.shape[1] > n:
        o_ref[:, n:] = jnp.zeros((o_ref.shape[0], o_ref.shape[1] - n), BF16)


def _cast_pad_cols(w, layer, n_out):
    _, k, n = w.shape
    tr = _pick(k, (128, 64))
    return pl.pallas_call(
        _cast_cols_kernel,
        grid=(k // tr,),
        in_specs=[pl.BlockSpec((None, tr, n), lambda i: (layer, i, 0))],
        out_specs=pl.BlockSpec((tr, n_out), lambda i: (i, 0)),
        out_shape=jax.ShapeDtypeStruct((k, n_out), BF16),
        compiler_params=_cp(1), name="cast_cols",
    )(w)


def _cast_rows_kernel(n_full, x_ref, o_ref):
    @pl.when(pl.program_id(0) < n_full)
    def _():
        o_ref[...] = x_ref[...].astype(BF16)

    @pl.when(pl.program_id(0) >= n_full)
    def _():
        o_ref[...] = jnp.zeros(o_ref.shape, BF16)


def _cast_pad_rows(w, layer, k_out):
    _, k, n = w.shape
    tr = 256
    assert k % tr == 0 and k_out % tr == 0
    n_full = k // tr
    return pl.pallas_call(
        functools.partial(_cast_rows_kernel, n_full),
        grid=(k_out // tr,),
        in_specs=[pl.BlockSpec((None, tr, n), lambda i: (layer, jnp.minimum(i, n_full - 1), 0))],
        out_specs=pl.BlockSpec((tr, n), lambda i: (i, 0)),
        out_shape=jax.ShapeDtypeStruct((k_out, n), BF16),
        compiler_params=_cp(1), name="cast_rows",
    )(w)


def _pack_win_kernel(w_ref, o_ref):
    w = w_ref[...]
    o_dt = SSD_DI + SSD_CD
    o_sc = o_dt + SSD_H
    o_kr = o_sc + 3 * SC_D + QL + KVL
    half = ROPE // 2
    o_ref[:, 0:o_dt] = w[:, 0:o_dt].astype(BF16)
    o_ref[:, o_dt:o_dt + (o_kr - o_sc)] = w[:, o_sc:o_kr].astype(BF16)
    c = o_dt + (o_kr - o_sc)
    kr = w[:, o_kr:o_kr + ROPE]
    o_ref[:, c:c + ROPE] = kr.astype(BF16)
    o_ref[:, c + ROPE:c + ROPE + half] = kr[:, half:].astype(BF16)
    o_ref[:, c + ROPE + half:c + 2 * ROPE] = kr[:, :half].astype(BF16)
    o_ref[:, c + 2 * ROPE:c + 2 * ROPE + SSD_H] = w[:, o_dt:o_sc].astype(BF16)
    o_ref[:, c + 2 * ROPE + SSD_H:] = jnp.zeros((w.shape[0], LANE - SSD_H), BF16)


def _pack_win(w_in, layer):
    _, k, n = w_in.shape
    tr = 128
    return pl.pallas_call(
        _pack_win_kernel,
        grid=(k // tr,),
        in_specs=[pl.BlockSpec((None, tr, n), lambda i: (layer, i, 0))],
        out_specs=pl.BlockSpec((tr, NIN), lambda i: (i, 0)),
        out_shape=jax.ShapeDtypeStruct((k, NIN), BF16),
        compiler_params=_cp(1), name="pack_win",
    )(w_in)


def _swap_half(w):
    half = w.shape[-1] // 2
    return jnp.concatenate([w[..., half:], w[..., :half]], axis=-1)


def _layer_weights(i, w_in, ssd_conv_w, ssd_conv_b, ssd_dt_bias, ssd_a_log, ssd_d, ssd_norm, sc_conv_w,
                   mla_q_norm, mla_w_uq, mla_kv_norm, mla_w_ukv, w_out, ffn_w_gate, ffn_w_up, ffn_conv_w,
                   ffn_w_down):
    win = _pack_win(w_in, i)
    uq = mla_w_uq[i].reshape(QL, MLA_H, NOPE + ROPE)
    pe = uq[..., NOPE:]
    zq = jnp.zeros((QL, MLA_H, HK - NOPE - ROPE), F32)
    wa = jnp.concatenate([uq[..., :NOPE], pe, zq], axis=-1).reshape(QL, MLA_H * HK).astype(BF16)
    wb = jnp.concatenate([_swap_half(pe), zq], axis=-1).reshape(QL, MLA_H * LANE).astype(BF16)
    ukv = mla_w_ukv[i].reshape(KVL, MLA_H, NOPE + MLA_V)
    padf = ((0, 0), (0, DFP - D_FF))
    return dict(
        win=win,
        cw=ssd_conv_w[i], cb=ssd_conv_b[i].reshape(1, SSD_CD),
        dtb=jnp.pad(ssd_dt_bias[i], (0, LANE - SSD_H)).reshape(1, LANE),
        alog=jnp.pad(ssd_a_log[i], (0, LANE - SSD_H)).reshape(1, LANE),
        dsk=jnp.repeat(ssd_d[i], SSD_P).reshape(1, SSD_DI),
        ng=ssd_norm[i].reshape(1, SSD_DI),
        scw=sc_conv_w[i],
        qn=mla_q_norm[i].reshape(1, QL), kvn=mla_kv_norm[i].reshape(1, KVL),
        wa=wa, wb=wb,
        wk=ukv[..., :NOPE].reshape(KVL, MLA_H * NOPE).astype(BF16),
        wv=ukv[..., NOPE:].reshape(KVL, MLA_H * MLA_V).astype(BF16),
        wkt=jnp.transpose(ukv[..., :NOPE], (1, 2, 0)).astype(BF16),
        wv3=jnp.transpose(ukv[..., NOPE:], (1, 0, 2)).astype(BF16),
        wout=_cast_pad_cols(w_out, i, D_MODEL),
        wg=_cast_pad_cols(ffn_w_gate, i, DFP),
        wu=_cast_pad_cols(ffn_w_up, i, DFP),
        fcw=jnp.pad(ffn_conv_w[i], padf),
        wd=_cast_pad_rows(ffn_w_down, i, DFP),
    )


def _rope_table(seq, ns, past):
    half = ROPE // 2
    pos = jnp.concatenate([N_META + jnp.arange(seq, dtype=jnp.int32),
                           jnp.maximum(jnp.arange(CH, dtype=jnp.int32) - NSKIP, 0),
                           N_META + past + jnp.arange(ns, dtype=jnp.int32) % CH])
    inv = THETA ** (-jnp.arange(half, dtype=F32) / half)
    ang = pos.astype(F32)[:, None] * inv[None, :]
    cos, sin = jnp.cos(ang), jnp.sin(ang)
    return jnp.concatenate([cos, cos, -sin, sin], axis=1)


def kernel(x_prompt, x_sample, cache_kv_latent, cache_k_rope, state_ssm, state_ssd_conv, state_sconv, state_ffn_conv, meta_tokens, norm_mix_pre, norm_mix_post, norm_ffn_pre, norm_ffn_post, w_in, ssd_conv_w, ssd_conv_b, ssd_dt_bias, ssd_a_log, ssd_d, ssd_norm, sc_conv_w, mla_q_norm, mla_w_uq, mla_kv_norm, mla_w_ukv, w_out, ffn_w_gate, ffn_w_up, ffn_conv_w, ffn_w_down):
    bp, seq, d = x_prompt.shape
    nb, ls, _ = x_sample.shape
    depth, _, past, _ = cache_kv_latent.shape
    assert bp == 1 and ls == CH and seq % CH == 0 and d == D_MODEL
    lp = CH + seq
    ns = nb * ls
    npc = lp // CH
    c0 = seq // CH

    x, h = _assemble(x_prompt[0], x_sample.reshape(ns, d), meta_tokens.astype(F32), norm_mix_pre[0])
    tab = _rope_table(seq, ns, past)
    zero_c = jnp.zeros((1, SSD_K - 1, SSD_CD), F32)
    zero_h = jnp.zeros((1, SSD_N, SSD_DI), F32)

    def state_in(s):
        return jnp.transpose(s, (0, 3, 1, 2)).reshape(s.shape[0], SSD_N, SSD_DI)

    def state_out(s):
        return jnp.transpose(s.reshape(s.shape[0], SSD_N, SSD_H, SSD_P), (0, 2, 3, 1))
    zero_s = jnp.zeros((1, SC_K - 1, SC_D), F32)
    zero_f = jnp.zeros((1, FFN_K - 1, DFP), F32)
    padf = ((0, 0), (0, 0), (0, DFP - D_FF))

    outs_p, outs_s = [], []
    for i in range(depth):
        lw = _layer_weights(i, w_in, ssd_conv_w, ssd_conv_b, ssd_dt_bias, ssd_a_log, ssd_d, ssd_norm,
                            sc_conv_w, mla_q_norm, mla_w_uq, mla_kv_norm, mla_w_ukv, w_out, ffn_w_gate,
                            ffn_w_up, ffn_conv_w, ffn_w_down)
        u, dtf = _inproj(h, lw["win"])

        yp, yscp, cnew_p, hnew_p, scnew_p = _mixer(u, dtf, 0, 1, npc, NSKIP, zero_c, zero_h, zero_s, lw)
        y_ssd, y_sc, cnew_s, hnew_s, scnew_s = _mixer(u, dtf, npc, nb, 1, 0, state_ssd_conv[i],
                                                      state_in(state_ssm[i]), state_sconv[i], lw,
                                                      keep=(yp, yscp))
        hnew_p, hnew_s = state_out(hnew_p), state_out(hnew_s)

        q, ckv, kpe, kcat, v = _mla_proj(u, tab, lw)
        y_mla = _attn0(q, kcat, v, c0, _flash(q, kcat, v, seq))
        y_mla = _cached_attn(q, ckv, kpe, cache_kv_latent, cache_k_rope, i, npc, lw, y_mla)

        mix = _outproj(y_ssd, y_sc, y_mla, lw["wout"])
        x, h = _resid_norm(x, mix, norm_mix_post[i], norm_ffn_pre[i])

        gate, up = _gateup(h, lw["wg"], lw["wu"])
        act, f0 = _act(gate, up, c0, 1, 1, NSKIP, zero_f, lw["fcw"])
        act, fnew_p = _act(gate, up, 0, 1, seq // ACT_ROWS, 0, f0, lw["fcw"], act_in=act, t=ACT_ROWS)
        act, fnew_s = _act(gate, up, npc, nb, 1, 0, jnp.pad(state_ffn_conv[i], padf), lw["fcw"], act_in=act)
        f = _down(act, lw["wd"])
        if i + 1 < depth:
            x, h = _resid_norm(x, f, norm_ffn_post[i], norm_mix_pre[i + 1])
        else:
            y_prompt = _resid_out(x, f, norm_ffn_post[i], 0, seq)
            y_sample = _resid_out(x, f, norm_ffn_post[i], lp, ns)

        meta_rows = slice(seq + NSKIP, lp)
        outs_p.append((jnp.concatenate([ckv[meta_rows], ckv[:seq]], axis=0)[None],
                       jnp.concatenate([kpe[meta_rows, :ROPE], kpe[:seq, :ROPE]], axis=0)[None],
                       hnew_p, cnew_p, scnew_p, fnew_p[:, :, :D_FF]))
        outs_s.append((ckv[lp:].reshape(nb, ls, KVL), kpe[lp:, :ROPE].reshape(nb, ls, ROPE), hnew_s, cnew_s,
                       scnew_s, fnew_s[:, :, :D_FF]))

    def stack(outs, j):
        return jnp.stack([o[j] for o in outs], axis=0)

    return (y_prompt[None], y_sample.reshape(nb, ls, d),
            stack(outs_p, 0), stack(outs_p, 1), stack(outs_p, 2), stack(outs_p, 3), stack(outs_p, 4), stack(outs_p, 5),
            stack(outs_s, 0), stack(outs_s, 1), stack(outs_s, 2), stack(outs_s, 3), stack(outs_s, 4), stack(outs_s, 5))
```

```python
import functools

import jax
import jax.numpy as jnp
import numpy as np
from jax import lax
from jax.experimental import pallas as pl
from jax.experimental.pallas import tpu as pltpu

F32 = jnp.float32
BF16 = jnp.bfloat16

D_MODEL = 4096
N_META = 16
CH = 64
NSKIP = CH - N_META
EPS = 1e-6
SSD_P = 64
SSD_DI = 2048
SSD_H = 32
SSD_G = 8
SSD_N = 128
SSD_K = 4
SSD_CD = 4096
SC_D = 1024
SC_K = 3
MLA_H = 8
NOPE = 128
ROPE = 64
MLA_V = 128
QL = 768
KVL = 512
HK = 256
SCALE = (NOPE + ROPE) ** -0.5
QSCALE = SCALE * 1.4426950408889634
THETA = 10000.0
D_FF = 11008
DFP = 11264
FFN_K = 3
LANE = 128

C_Z, C_X, C_BC, C_SCB, C_SCC, C_SCH, C_MLA, NIN = 0, 2048, 4096, 6144, 7168, 8192, 9216, 10752
MLA_W = 1536

FLASH_TQ = 512
ACT_ROWS = 128
VMEM_LIMIT = 56 * 1024 * 1024
HI = lax.Precision.HIGHEST
NT = (((1,), (1,)), ((), ()))
TN = (((0,), (0,)), ((), ()))


def _cp(n, flags=None):
    return pltpu.CompilerParams(dimension_semantics=("arbitrary",) * n, vmem_limit_bytes=VMEM_LIMIT, flags=flags)


def _pick(n, cands):
    for c in cands:
        if n % c == 0:
            return c
    raise ValueError(f"no tile for {n}")


MM_ROWS = (1088, 544, 512, 272, 256, 136, 128, 64)
EW_ROWS = (272, 256, 136, 128, 64)


def _stored_chunk(c, nch):
    return (c + nch - 1) % nch


def _rms(x, g):
    return x * lax.rsqrt(jnp.mean(x * x, axis=-1, keepdims=True) + EPS) * g


def _silu(x):
    return x * jax.nn.sigmoid(x)


def _norm_rows_kernel(x_in_ref, g_ref, x_ref, h_ref):
    x = x_in_ref[...]
    x_ref[...] = x
    h_ref[...] = _rms(x, g_ref[...]).astype(BF16)


def _tail_rows_kernel(xs_ref, meta_ref, g_ref, keep_x, keep_h, x_ref, h_ref):
    del keep_x, keep_h
    c = pl.program_id(0)

    def emit(x):
        x_ref[...] = x
        h_ref[...] = _rms(x, g_ref[...]).astype(BF16)

    @pl.when(c == 0)
    def _():
        emit(jnp.concatenate([jnp.zeros((NSKIP, x_ref.shape[1]), F32), meta_ref[...]], axis=0))

    @pl.when(c > 0)
    def _():
        emit(xs_ref[...])


def _assemble(x_prompt, x_sample, meta, g):
    seq, d = x_prompt.shape
    ns = x_sample.shape[0]
    r = seq + CH + ns
    bm = _pick(seq, (512, 256, 128, 64))
    vec = pl.BlockSpec((1, d), lambda c: (0, 0))
    out_shape = [jax.ShapeDtypeStruct((r, d), F32), jax.ShapeDtypeStruct((r, d), BF16)]
    big = pl.BlockSpec((bm, d), lambda c: (c, 0))
    x, h = pl.pallas_call(
        _norm_rows_kernel, grid=(seq // bm,), in_specs=[big, vec], out_specs=[big, big],
        out_shape=out_shape, compiler_params=_cp(1), name="assemble_frames",
    )(x_prompt, g.reshape(1, d))
    c0 = seq // CH
    small = pl.BlockSpec((CH, d), lambda c: (c0 + c, 0))
    anyspec = pl.BlockSpec(memory_space=pl.ANY)
    return pl.pallas_call(
        _tail_rows_kernel, grid=(1 + ns // CH,),
        in_specs=[pl.BlockSpec((CH, d), lambda c: (jnp.maximum(c - 1, 0), 0)),
                  pl.BlockSpec((N_META, d), lambda c: (0, 0)), vec, anyspec, anyspec],
        out_specs=[small, small], out_shape=out_shape, input_output_aliases={3: 0, 4: 1},
        compiler_params=_cp(1), name="assemble_tail",
    )(x_sample, meta, g.reshape(1, d), x, h)


def _resid_out_kernel(x_ref, m_ref, gp_ref, y_ref):
    y_ref[...] = x_ref[...] + _rms(m_ref[...], gp_ref[...])


def _resid_out(x, m, g_post, row0, rows):
    d = x.shape[1]
    bm = next(b for b in (512, 256, 128, 64) if rows % b == 0 and row0 % b == 0)
    src = pl.BlockSpec((bm, d), lambda c: (row0 // bm + c, 0))
    return pl.pallas_call(
        _resid_out_kernel, grid=(rows // bm,),
        in_specs=[src, src, pl.BlockSpec((1, d), lambda c: (0, 0))],
        out_specs=pl.BlockSpec((bm, d), lambda c: (c, 0)),
        out_shape=jax.ShapeDtypeStruct((rows, d), F32),
        compiler_params=_cp(1), name="resid_out",
    )(x, m, g_post.reshape(1, d))


def _resid_norm_kernel(x_ref, m_ref, gp_ref, gn_ref, x2_ref, h_ref):
    x2 = x_ref[...] + _rms(m_ref[...], gp_ref[...])
    x2_ref[...] = x2
    h_ref[...] = _rms(x2, gn_ref[...]).astype(BF16)


def _resid_norm(x, m, g_post, g_next):
    r, d = x.shape
    bm = _pick(r, EW_ROWS)
    row = pl.BlockSpec((bm, d), lambda i: (i, 0))
    vec = pl.BlockSpec((1, d), lambda i: (0, 0))
    return pl.pallas_call(
        _resid_norm_kernel, grid=(r // bm,), in_specs=[row, row, vec, vec], out_specs=[row, row],
        out_shape=[jax.ShapeDtypeStruct((r, d), F32), jax.ShapeDtypeStruct((r, d), BF16)],
        compiler_params=_cp(1), name="resid_norm",
    )(x, m, g_post.reshape(1, d), g_next.reshape(1, d))


def _inproj_kernel(x_ref, w_ref, u_ref, dt_ref):
    acc = jnp.dot(x_ref[...], w_ref[...], preferred_element_type=F32)
    u_ref[...] = acc.astype(BF16)

    @pl.when(pl.program_id(1) == pl.num_programs(1) - 1)
    def _():
        dt_ref[...] = acc[:, acc.shape[1] - LANE:]


def _inproj(h, w):
    r, k = h.shape
    n = w.shape[1]
    bm = _pick(r, MM_ROWS)
    bn = 768
    return pl.pallas_call(
        _inproj_kernel,
        grid=(r // bm, n // bn),
        in_specs=[pl.BlockSpec((bm, k), lambda i, j: (i, 0)), pl.BlockSpec((k, bn), lambda i, j: (0, j))],
        out_specs=[pl.BlockSpec((bm, bn), lambda i, j: (i, j)), pl.BlockSpec((bm, LANE), lambda i, j: (i, 0))],
        out_shape=[jax.ShapeDtypeStruct((r, n), BF16), jax.ShapeDtypeStruct((r, LANE), F32)],
        compiler_params=_cp(2), name="inproj",
    )(h, w)


def _outproj_kernel(a_ref, b_ref, c_ref, wa_ref, wb_ref, wc_ref, o_ref):
    acc = jnp.dot(a_ref[...], wa_ref[...], preferred_element_type=F32)
    acc = acc + jnp.dot(b_ref[...], wb_ref[...], preferred_element_type=F32)
    acc = acc + jnp.dot(c_ref[...], wc_ref[...], preferred_element_type=F32)
    o_ref[...] = acc


def _outproj(y_ssd, y_sc, y_mla, w):
    r = y_ssd.shape[0]
    n = w.shape[1]
    bm = _pick(r, MM_ROWS)
    bn = 1024
    return pl.pallas_call(
        _outproj_kernel,
        grid=(r // bm, n // bn),
        in_specs=[pl.BlockSpec((bm, SSD_DI), lambda i, j: (i, 0)),
                  pl.BlockSpec((bm, SC_D), lambda i, j: (i, 0)),
                  pl.BlockSpec((bm, SC_D), lambda i, j: (i, 0)),
                  pl.BlockSpec((SSD_DI, bn), lambda i, j: (0, j)),
                  pl.BlockSpec((SC_D, bn), lambda i, j: (2, j)),
                  pl.BlockSpec((SC_D, bn), lambda i, j: (3, j))],
        out_specs=pl.BlockSpec((bm, bn), lambda i, j: (i, j)),
        out_shape=jax.ShapeDtypeStruct((r, n), F32),
        compiler_params=_cp(2), name="outproj",
    )(y_ssd, y_sc, y_mla, w, w, w)


def _gateup_kernel(x_ref, wg_ref, wu_ref, g_ref, u_ref):
    x = x_ref[...]
    g_ref[...] = jnp.dot(x, wg_ref[...], preferred_element_type=F32).astype(BF16)
    u_ref[...] = jnp.dot(x, wu_ref[...], preferred_element_type=F32).astype(BF16)


def _gateup(h, wg, wu):
    r, k = h.shape
    n = wg.shape[1]
    bm = _pick(r, MM_ROWS)
    bn = 512
    wspec = pl.BlockSpec((k, bn), lambda i, j: (0, j))
    ospec = pl.BlockSpec((bm, bn), lambda i, j: (i, j))
    return pl.pallas_call(
        _gateup_kernel,
        grid=(r // bm, n // bn),
        in_specs=[pl.BlockSpec((bm, k), lambda i, j: (i, 0)), wspec, wspec],
        out_specs=[ospec, ospec],
        out_shape=[jax.ShapeDtypeStruct((r, n), BF16)] * 2,
        compiler_params=_cp(2), name="gateup",
    )(h, wg, wu)


def _down_kernel(x_ref, w_ref, o_ref):
    p = jnp.dot(x_ref[...], w_ref[...], preferred_element_type=F32)

    @pl.when(pl.program_id(2) == 0)
    def _():
        o_ref[...] = p

    @pl.when(pl.program_id(2) > 0)
    def _():
        o_ref[...] += p


def _down(a, w):
    r, k = a.shape
    n = w.shape[1]
    bm = _pick(r, MM_ROWS)
    bn = 1024
    bk = k // 4
    return pl.pallas_call(
        _down_kernel,
        grid=(r // bm, n // bn, k // bk),
        in_specs=[pl.BlockSpec((bm, bk), lambda i, j, kk: (i, kk)), pl.BlockSpec((bk, bn), lambda i, j, kk: (kk, j))],
        out_specs=pl.BlockSpec((bm, bn), lambda i, j, kk: (i, j)),
        out_shape=jax.ShapeDtypeStruct((r, n), F32),
        compiler_params=_cp(3), name="down",
    )(a, w)


def _mixer_kernel(nskip, z_ref, x_ref, bc_ref, dt_ref, scb_ref, scc_ref, sch_ref,
                  cprev_ref, h0_ref, scprev_ref,
                  cw_ref, cb_ref, dtb_ref, alog_ref, dsk_ref, ng_ref, scw_ref, pm_ref,
                  y_ref, ysc_ref, cnew_ref, hnew_ref, scnew_ref,
                  buf, sbuf, hst):
    t = CH
    c = pl.program_id(1)
    last = pl.num_programs(1) - 1

    ns = SSD_K - 1

    @pl.when(c == 0)
    def _init():
        _load_tail(buf, cprev_ref[0])
        sbuf[0:8, :] = jnp.zeros((8, SC_D), F32)
        sbuf[8 - (SC_K - 1):8, :] = scprev_ref[0]
        hst[...] = h0_ref[0]

    rows = lax.broadcasted_iota(jnp.int32, (t, 1), 0) + c * t
    valid = rows >= nskip

    xin = jnp.concatenate([x_ref[...], bc_ref[...]], axis=1)
    xin = jnp.where(valid, xin, jnp.zeros((), xin.dtype))
    delayed = _delayed(pm_ref, buf, xin, ns)
    acc = cb_ref[...] + cw_ref[ns:ns + 1, :] * xin.astype(F32)
    for d in range(1, ns + 1):
        acc = acc + cw_ref[ns - d:ns - d + 1, :] * delayed[d - 1]
    xbc = _silu(acc)

    lane = lax.broadcasted_iota(jnp.int32, (t, LANE), 1)
    v = dt_ref[...] + dtb_ref[...]
    dt = jnp.maximum(v, 0.0) + jnp.log1p(jnp.exp(-jnp.abs(v)))
    dt = jnp.where(valid & (lane < SSD_H), dt, 0.0)
    adt = dt * (-jnp.exp(alog_ref[...]))
    ri = lax.broadcasted_iota(jnp.int32, (t, t), 0)
    ci = lax.broadcasted_iota(jnp.int32, (t, t), 1)
    tri = ri >= ci
    acs = jnp.dot(tri.astype(F32), adt, precision=HI, preferred_element_type=F32)

    hpg = SSD_H // SSD_G
    gw = hpg * SSD_P
    lane_g = lax.broadcasted_iota(jnp.int32, (t, gw), 1)
    row_g = lax.broadcasted_iota(jnp.int32, (t, gw), 0)
    seg = lane_g // SSD_P
    eye_g = row_g == lane_g % SSD_P
    tri_g = row_g >= lane_g % SSD_P
    blockdiag = (lax.broadcasted_iota(jnp.int32, (hpg * t, gw), 0) // t
                 == lax.broadcasted_iota(jnp.int32, (hpg * t, gw), 1) // SSD_P)

    def per_head_lanes(mat, g):
        out = jnp.broadcast_to(mat[:, g * hpg + hpg - 1:g * hpg + hpg], (t, gw))
        for r in range(hpg - 2, -1, -1):
            out = jnp.where(seg == r, jnp.broadcast_to(mat[:, g * hpg + r:g * hpg + r + 1], (t, gw)), out)
        return out

    for g in range(SSD_G):
        cols = slice(g * gw, (g + 1) * gw)
        bg = xbc[:, SSD_DI + g * SSD_N:SSD_DI + (g + 1) * SSD_N].astype(BF16)
        cg = xbc[:, SSD_DI + SSD_G * SSD_N + g * SSD_N:SSD_DI + SSD_G * SSD_N + (g + 1) * SSD_N].astype(BF16)
        acol = per_head_lanes(acs, g)
        dtx = per_head_lanes(dt, g)
        arow = jnp.sum(jnp.where(eye_g, acol, 0.0), axis=0, keepdims=True)
        decay = jnp.exp(jnp.where(tri_g, acol - arow, -jnp.inf))
        cb = lax.dot_general(cg, jnp.concatenate([bg] * hpg, axis=0), NT, preferred_element_type=F32)
        xg = xbc[:, cols]
        xdt = xg * dtx
        xdtb = xdt.astype(BF16)
        rhs = jnp.where(blockdiag, jnp.concatenate([xdtb] * hpg, axis=0), jnp.zeros((), BF16))
        ydiag = jnp.dot((cb * decay).astype(BF16), rhs, preferred_element_type=F32)
        hg = hst[:, cols]
        yoff = jnp.dot(cg, hg.astype(BF16), preferred_element_type=F32) * jnp.exp(acol)
        alast = acol[t - 1:t, :]
        snew = lax.dot_general(bg, (xdt * jnp.exp(alast - acol)).astype(BF16), TN, preferred_element_type=F32)
        hst[:, cols] = jnp.exp(alast) * hg + snew
        y = ydiag + yoff + dsk_ref[:, cols] * xg
        y = y * _silu(z_ref[:, cols].astype(F32))
        y_ref[:, cols] = _rms(y, ng_ref[:, cols]).astype(BF16)

    p = scc_ref[...].astype(F32) * sch_ref[...].astype(F32)
    p = jnp.where(valid, p, 0.0)
    sbuf[8:8 + t, :] = p
    conv = scw_ref[0:1, :] * sbuf[6:6 + t, :]
    for i in range(1, SC_K):
        conv = conv + scw_ref[i:i + 1, :] * sbuf[6 + i:6 + i + t, :]
    ysc_ref[...] = (scb_ref[...].astype(F32) * conv).astype(BF16)
    sbuf[0:8, :] = sbuf[t:t + 8, :]

    @pl.when(c == last)
    def _():
        for r in range(ns):
            cnew_ref[0, r:r + 1, :] = buf[3 * r:3 * r + 1, :]
        hnew_ref[0] = hst[...]
        scnew_ref[0] = sbuf[8 - (SC_K - 1):8, :]


def _mixer_alias_kernel(nskip, *refs):
    n_in = 18
    _mixer_kernel(nskip, *refs[:n_in], *refs[n_in + 2:])


def _mixer(u, dtf, rb0, nb, nch, nskip, cprev, h0, scprev, lw, keep=None):
    r = u.shape[0]
    t = CH

    def rowmap(cb):
        return lambda b, c: (rb0 + b * nch + _stored_chunk(c, nch), cb)

    def stmap(b, c):
        return (b, 0, 0)

    def full(shape):
        return pl.BlockSpec(shape, lambda b, c: (0,) * len(shape))

    in_specs = [
        pl.BlockSpec((t, SSD_DI), rowmap(C_Z // SSD_DI)),
        pl.BlockSpec((t, SSD_DI), rowmap(C_X // SSD_DI)),
        pl.BlockSpec((t, SSD_DI), rowmap(C_BC // SSD_DI)),
        pl.BlockSpec((t, LANE), rowmap(0)),
        pl.BlockSpec((t, SC_D), rowmap(C_SCB // SC_D)),
        pl.BlockSpec((t, SC_D), rowmap(C_SCC // SC_D)),
        pl.BlockSpec((t, SC_D), rowmap(C_SCH // SC_D)),
        pl.BlockSpec((1, SSD_K - 1, SSD_CD), stmap),
        pl.BlockSpec((1, SSD_N, SSD_DI), stmap),
        pl.BlockSpec((1, SC_K - 1, SC_D), stmap),
        full((SSD_K, SSD_CD)), full((1, SSD_CD)), full((1, LANE)), full((1, LANE)),
        full((1, SSD_DI)), full((1, SSD_DI)), full((SC_K, SC_D)),
        full((CH * (SSD_K - 1) + TAIL, TAIL + CH)),
    ]
    out_specs = [
        pl.BlockSpec((t, SSD_DI), rowmap(0)),
        pl.BlockSpec((t, SC_D), rowmap(0)),
        pl.BlockSpec((1, SSD_K - 1, SSD_CD), stmap),
        pl.BlockSpec((1, SSD_N, SSD_DI), stmap),
        pl.BlockSpec((1, SC_K - 1, SC_D), stmap),
    ]
    out_shape = [
        jax.ShapeDtypeStruct((r, SSD_DI), BF16),
        jax.ShapeDtypeStruct((r, SC_D), BF16),
        jax.ShapeDtypeStruct((nb, SSD_K - 1, SSD_CD), F32),
        jax.ShapeDtypeStruct((nb, SSD_N, SSD_DI), F32),
        jax.ShapeDtypeStruct((nb, SC_K - 1, SC_D), F32),
    ]
    args = [u, u, u, dtf, u, u, u, cprev, h0, scprev,
            lw["cw"], lw["cb"], lw["dtb"], lw["alog"], lw["dsk"], lw["ng"], lw["scw"],
            jnp.asarray(_shift_matrix(SSD_K), BF16)]
    kern = functools.partial(_mixer_kernel, nskip)
    aliases = {}
    if keep is not None:
        aliases = {len(args): 0, len(args) + 1: 1}
        args += list(keep)
        in_specs += [pl.BlockSpec(memory_space=pl.ANY)] * 2
        kern = functools.partial(_mixer_alias_kernel, nskip)
    return pl.pallas_call(
        kern,
        grid=(nb, nch),
        in_specs=in_specs, out_specs=out_specs, out_shape=out_shape,
        scratch_shapes=[pltpu.VMEM((TAIL, SSD_CD), F32), pltpu.VMEM((8 + t, SC_D), F32),
                        pltpu.VMEM((SSD_N, SSD_DI), F32)],
        input_output_aliases=aliases,
        compiler_params=_cp(2), name="mixer",
    )(*args)


def _mla_proj_kernel(blk_ref, tab_ref, qn_ref, kvn_ref, wa_ref, wb_ref, wk_ref, wv_ref,
                     q_ref, ckv_ref, kpe_ref, kcat_ref, v_ref):
    blk = blk_ref[...]
    cq = _rms(blk[:, :QL].astype(F32), qn_ref[...]).astype(BF16)
    ckv = _rms(blk[:, QL:QL + KVL].astype(F32), kvn_ref[...])
    ckv_ref[...] = ckv
    tab = tab_ref[...]
    tabr = pltpu.roll(tab, ROPE, 1)
    prod = blk[:, QL + KVL:QL + KVL + LANE].astype(F32) * tab
    ksum = prod + pltpu.roll(prod, ROPE, 1)
    lane = lax.broadcasted_iota(jnp.int32, ksum.shape, 1)
    kpe = jnp.where(lane < ROPE, ksum, 0.0)
    kpe_ref[...] = kpe
    qa = jnp.dot(cq, wa_ref[...], preferred_element_type=F32)
    qb = jnp.dot(cq, wb_ref[...], preferred_element_type=F32)
    ckvb = ckv.astype(BF16)
    kn = jnp.dot(ckvb, wk_ref[...], preferred_element_type=F32)
    v_ref[...] = jnp.dot(ckvb, wv_ref[...], preferred_element_type=F32).astype(BF16)
    kpeb = kpe.astype(BF16)
    tab_q = tab * QSCALE
    tabr_q = tabr * QSCALE
    for h in range(MLA_H):
        q_ref[:, h * HK:h * HK + NOPE] = (qa[:, h * HK:h * HK + NOPE] * QSCALE).astype(BF16)
        q_ref[:, h * HK + NOPE:(h + 1) * HK] = (
            qa[:, h * HK + NOPE:(h + 1) * HK] * tab_q + qb[:, h * LANE:(h + 1) * LANE] * tabr_q).astype(BF16)
        kcat_ref[:, h * HK:h * HK + NOPE] = kn[:, h * NOPE:(h + 1) * NOPE].astype(BF16)
        kcat_ref[:, h * HK + NOPE:(h + 1) * HK] = kpeb


def _mla_proj(u, tab, lw):
    r = u.shape[0]
    bm = _pick(r, (544, 512, 272, 256, 136, 128, 64))

    def full(shape):
        return pl.BlockSpec(shape, lambda i: (0,) * len(shape))

    def row(w):
        return pl.BlockSpec((bm, w), lambda i: (i, 0))

    return pl.pallas_call(
        _mla_proj_kernel,
        grid=(r // bm,),
        in_specs=[pl.BlockSpec((bm, MLA_W), lambda i: (i, C_MLA // MLA_W)), row(LANE),
                  full((1, QL)), full((1, KVL)), full((QL, MLA_H * HK)), full((QL, MLA_H * LANE)),
                  full((KVL, MLA_H * NOPE)), full((KVL, MLA_H * MLA_V))],
        out_specs=[row(MLA_H * HK), row(KVL), row(LANE), row(MLA_H * HK), row(MLA_H * MLA_V)],
        out_shape=[jax.ShapeDtypeStruct((r, MLA_H * HK), BF16), jax.ShapeDtypeStruct((r, KVL), F32),
                   jax.ShapeDtypeStruct((r, LANE), F32), jax.ShapeDtypeStruct((r, MLA_H * HK), BF16),
                   jax.ShapeDtypeStruct((r, MLA_H * MLA_V), BF16)],
        compiler_params=_cp(1), name="mla_proj",
    )(u, tab, lw["qn"], lw["kvn"], lw["wa"], lw["wb"], lw["wk"], lw["wv"])


def _flash_kernel(it_ref, jt_ref, q_ref, k_ref, v_ref, k0_ref, v0_ref, mask_ref, pad_ref, o_ref,
                  qt_s, m_s, l_s, acc_s, s_scr, p_scr, a_s):
    tq = q_ref.shape[0]
    tk = k_ref.shape[0]
    step = pl.program_id(0)
    i = it_ref[step]
    j = jt_ref[step]

    @pl.when(j == 0)
    def _init():
        for h in range(MLA_H):
            qt_s[h * HK:(h + 1) * HK, :] = q_ref[:, h * HK:(h + 1) * HK].T
        for h in range(MLA_H):
            s0 = jnp.dot(k0_ref[:, h * HK:(h + 1) * HK], qt_s[h * HK:(h + 1) * HK, :],
                         preferred_element_type=F32) + pad_ref[...]
            m0 = jnp.max(s0, axis=0, keepdims=True)
            p0 = jnp.exp2(s0 - m0)
            rows = slice(h * MLA_V, (h + 1) * MLA_V)
            m_s[h] = m0
            l_s[h] = jnp.sum(p0, axis=0, keepdims=True)
            acc_s[rows, :] = lax.dot_general(v0_ref[:, rows], p0.astype(BF16), TN, preferred_element_type=F32)

    def tile(masked):
        nkb = tk // CH

        def block(sref, kb):
            blk = sref[kb * CH:(kb + 1) * CH, :]
            if masked:
                blk = blk + mask_ref[kb * CH:(kb + 1) * CH, :]
            return blk

        def scores(h):
            s_scr[h % 2] = jnp.dot(k_ref[:, h * HK:(h + 1) * HK], qt_s[h * HK:(h + 1) * HK, :],
                                   preferred_element_type=F32)

        def softmax(h):
            sref = s_scr.at[h % 2]
            pref = p_scr.at[h % 2]
            mx = block(sref, 0).reshape(CH // 8, 8, tq).max(axis=0)
            for kb in range(1, nkb):
                mx = jnp.maximum(mx, block(sref, kb).reshape(CH // 8, 8, tq).max(axis=0))
            m_prev = m_s[h]
            m_new = jnp.maximum(m_prev, jnp.max(mx, axis=0, keepdims=True))
            alpha = jnp.exp2(m_prev - m_new)
            part = jnp.zeros((8, tq), F32)
            for kb in range(nkb):
                e = jnp.exp2(block(sref, kb) - m_new)
                part = part + jnp.sum(e.reshape(CH // 8, 8, tq), axis=0)
                pref[kb * CH:(kb + 1) * CH, :] = e.astype(BF16)
            l_s[h] = alpha * l_s[h] + jnp.sum(part, axis=0, keepdims=True)
            m_s[h] = m_new
            a_s[h % 2] = alpha

        def weighted_values(h):
            rows = slice(h * MLA_V, (h + 1) * MLA_V)
            acc_s[rows, :] = a_s[h % 2] * acc_s[rows, :] + lax.dot_general(
                v_ref[:, rows], p_scr[h % 2], TN, preferred_element_type=F32)

        scores(0)
        for h in range(MLA_H + 1):
            if h + 1 < MLA_H:
                scores(h + 1)
            if h >= 1:
                weighted_values(h - 1)
            if h < MLA_H:
                softmax(h)

    @pl.when(j < i)
    def _():
        tile(False)

    @pl.when(j == i)
    def _():
        tile(True)
        for h in range(MLA_H):
            rows = slice(h * MLA_V, (h + 1) * MLA_V)
            o_ref[:, rows] = (acc_s[rows, :] / l_s[h]).T.astype(BF16)


def _pad_mask(n):
    return jnp.asarray(np.where(np.arange(CH)[:, None] + 0 * np.arange(n)[None, :] < NSKIP, -np.inf, 0.0),
                       F32)


def _flash(q, kcat, v, seq):
    r = q.shape[0]
    tq = _pick(seq, (FLASH_TQ, 256, 128))
    nq = seq // tq
    c0 = seq // CH
    it = np.concatenate([np.full((i + 1,), i, np.int32) for i in range(nq)])
    jt = np.concatenate([np.arange(i + 1, dtype=np.int32) for i in range(nq)])
    kk = np.arange(tq)[:, None]
    qq = np.arange(tq)[None, :]
    diag = jnp.asarray(np.where(kk // CH > qq // CH, -np.inf, 0.0), F32)
    grid_spec = pltpu.PrefetchScalarGridSpec(
        num_scalar_prefetch=2,
        grid=(int(it.shape[0]),),
        in_specs=[pl.BlockSpec((tq, MLA_H * HK), lambda s, it, jt: (it[s], 0)),
                  pl.BlockSpec((tq, MLA_H * HK), lambda s, it, jt: (jt[s], 0)),
                  pl.BlockSpec((tq, MLA_H * MLA_V), lambda s, it, jt: (jt[s], 0)),
                  pl.BlockSpec((CH, MLA_H * HK), lambda s, it, jt: (c0, 0)),
                  pl.BlockSpec((CH, MLA_H * MLA_V), lambda s, it, jt: (c0, 0)),
                  pl.BlockSpec((tq, tq), lambda s, it, jt: (0, 0)),
                  pl.BlockSpec((CH, tq), lambda s, it, jt: (0, 0))],
        out_specs=pl.BlockSpec((tq, MLA_H * MLA_V), lambda s, it, jt: (it[s], 0)),
        scratch_shapes=[pltpu.VMEM((MLA_H * HK, tq), BF16),
                        pltpu.VMEM((MLA_H, 1, tq), F32), pltpu.VMEM((MLA_H, 1, tq), F32),
                        pltpu.VMEM((MLA_H * MLA_V, tq), F32),
                        pltpu.VMEM((2, tq, tq), F32), pltpu.VMEM((2, tq, tq), BF16),
                        pltpu.VMEM((2, 1, tq), F32)],
    )
    return pl.pallas_call(
        _flash_kernel,
        grid_spec=grid_spec,
        out_shape=jax.ShapeDtypeStruct((r, MLA_H * MLA_V), BF16),
        compiler_params=_cp(1), name="flash",
    )(jnp.asarray(it), jnp.asarray(jt), q, kcat, v, kcat, v, diag, _pad_mask(tq))


def _attn0_kernel(q_ref, k_ref, v_ref, keep_ref, o_ref):
    del keep_ref
    col = lax.broadcasted_iota(jnp.int32, (CH, CH), 1)
    for h in range(MLA_H):
        s = lax.dot_general(q_ref[:, h * HK:(h + 1) * HK], k_ref[:, h * HK:(h + 1) * HK], NT,
                            preferred_element_type=F32)
        s = jnp.where(col >= NSKIP, s, -jnp.inf)
        p = jnp.exp2(s - jnp.max(s, axis=-1, keepdims=True))
        o = jnp.dot(p.astype(BF16), v_ref[:, h * MLA_V:(h + 1) * MLA_V], preferred_element_type=F32)
        o_ref[:, h * MLA_V:(h + 1) * MLA_V] = (o / jnp.sum(p, axis=-1, keepdims=True)).astype(BF16)


def _attn0(q, kcat, v, c0, o_frames):
    def blk(w):
        return pl.BlockSpec((CH, w), lambda i: (c0, 0))

    return pl.pallas_call(
        _attn0_kernel,
        grid=(1,),
        in_specs=[blk(MLA_H * HK), blk(MLA_H * HK), blk(MLA_H * MLA_V), pl.BlockSpec(memory_space=pl.ANY)],
        out_specs=blk(MLA_H * MLA_V),
        out_shape=jax.ShapeDtypeStruct(o_frames.shape, BF16),
        input_output_aliases={3: 0},
        compiler_params=_cp(1), name="attn0",
    )(q, kcat, v, o_frames)


def _cached_attn_kernel(q_ref, cnew_ref, pnew_ref, ckv_ref, cpe_ref, wkt_ref, wv_ref, o_ref, call, peall):
    past = ckv_ref.shape[2]
    call[0:past, :] = ckv_ref[0, 0].astype(BF16)
    call[past:past + CH, :] = cnew_ref[...].astype(BF16)
    peall[...] = jnp.zeros(peall.shape, BF16)
    peall[0:past, 0:ROPE] = cpe_ref[0, 0].astype(BF16)
    peall[past:past + CH, :] = pnew_ref[...].astype(BF16)
    qlat = []
    qpe = []
    for h in range(MLA_H):
        qn = q_ref[:, h * HK:h * HK + NOPE]
        qlat.append(jnp.dot(qn, wkt_ref[h], preferred_element_type=F32).astype(BF16))
        qpe.append(q_ref[:, h * HK + NOPE:(h + 1) * HK])
    qlat = jnp.concatenate(qlat, axis=0)
    qpe = jnp.concatenate(qpe, axis=0)
    s = (lax.dot_general(qlat, call[...], NT, preferred_element_type=F32)
         + lax.dot_general(qpe, peall[...], NT, preferred_element_type=F32))
    p = jnp.exp2(s - jnp.max(s, axis=-1, keepdims=True))
    p = p / jnp.sum(p, axis=-1, keepdims=True)
    olat = jnp.dot(p.astype(BF16), call[...], preferred_element_type=F32).astype(BF16)
    for h in range(MLA_H):
        o_ref[:, h * MLA_V:(h + 1) * MLA_V] = jnp.dot(
            olat[h * CH:(h + 1) * CH, :], wv_ref[h], preferred_element_type=F32).astype(BF16)


def _cached_attn_alias_kernel(q_ref, cnew_ref, pnew_ref, ckv_ref, cpe_ref, wkt_ref, wv_ref, keep_ref, o_ref,
                              call, peall):
    del keep_ref
    _cached_attn_kernel(q_ref, cnew_ref, pnew_ref, ckv_ref, cpe_ref, wkt_ref, wv_ref, o_ref, call, peall)


def _cached_attn(q, ckv, kpe, cache_kv, cache_pe, layer, rb0, lw, o_prompt):
    nb, past = cache_kv.shape[1], cache_kv.shape[2]

    def rowmap(b):
        return (rb0 + b, 0)

    return pl.pallas_call(
        _cached_attn_alias_kernel,
        grid=(nb,),
        in_specs=[pl.BlockSpec((CH, MLA_H * HK), rowmap),
                  pl.BlockSpec((CH, KVL), rowmap),
                  pl.BlockSpec((CH, LANE), rowmap),
                  pl.BlockSpec((1, 1, past, KVL), lambda b: (layer, b, 0, 0)),
                  pl.BlockSpec((1, 1, past, ROPE), lambda b: (layer, b, 0, 0)),
                  pl.BlockSpec((MLA_H, NOPE, KVL), lambda b: (0, 0, 0)),
                  pl.BlockSpec((MLA_H, KVL, MLA_V), lambda b: (0, 0, 0)),
                  pl.BlockSpec(memory_space=pl.ANY)],
        out_specs=pl.BlockSpec((CH, MLA_H * MLA_V), rowmap),
        out_shape=jax.ShapeDtypeStruct(o_prompt.shape, BF16),
        scratch_shapes=[pltpu.VMEM((past + CH, KVL), BF16), pltpu.VMEM((past + CH, LANE), BF16)],
        input_output_aliases={7: 0},
        compiler_params=_cp(1), name="cached_attn",
    )(q, ckv, kpe, cache_kv, cache_pe, lw["wkt"], lw["wv3"], o_prompt)


TAIL = 16


def _shift_matrix(k, rows=CH):
    ns = k - 1
    pm = np.zeros((rows * ns + TAIL, TAIL + rows), np.float32)
    for d in range(1, ns + 1):
        for t in range(rows):
            if t - d >= 0:
                pm[(d - 1) * rows + t, TAIL + t - d] = 1.0
            else:
                r = ns + t - d
                pm[(d - 1) * rows + t, 3 * r:3 * r + 3] = 1.0
    for r in range(ns):
        pm[rows * ns + 3 * r, TAIL + rows - ns + r] = 1.0
    return pm


def _load_tail(tail_s, prev):
    hi = prev.astype(BF16).astype(F32)
    rest = prev - hi
    mid = rest.astype(BF16).astype(F32)
    lo = (rest - mid).astype(BF16).astype(F32)
    tail_s[...] = jnp.zeros(tail_s.shape, F32)
    for r in range(prev.shape[0]):
        tail_s[3 * r:3 * r + 1, :] = hi[r:r + 1]
        tail_s[3 * r + 1:3 * r + 2, :] = mid[r:r + 1]
        tail_s[3 * r + 2:3 * r + 3, :] = lo[r:r + 1]


def _delayed(pm_ref, tail_s, x, ns):
    t = x.shape[0]
    ext = jnp.concatenate([tail_s[...].astype(BF16), x], axis=0)
    out = jnp.dot(pm_ref[...], ext, preferred_element_type=F32)
    tail_s[...] = out[t * ns:, :]
    return [out[(d - 1) * t:d * t, :] for d in range(1, ns + 1)]


def _act_kernel(nskip, g_ref, u_ref, prev_ref, w_ref, pm_ref, a_ref, new_ref, tail_s):
    t = g_ref.shape[0]
    ns = FFN_K - 1
    c = pl.program_id(1)

    @pl.when(c == 0)
    def _():
        _load_tail(tail_s, prev_ref[0])

    rows = lax.broadcasted_iota(jnp.int32, (t, 1), 0) + c * t
    g = g_ref[...]
    g = jnp.where(rows >= nskip, g, jnp.zeros((), g.dtype))
    delayed = _delayed(pm_ref, tail_s, g, ns)
    conv = w_ref[ns:ns + 1, :] * g.astype(F32)
    for d in range(1, ns + 1):
        conv = conv + w_ref[ns - d:ns - d + 1, :] * delayed[d - 1]
    a_ref[...] = (_silu(conv) * u_ref[...].astype(F32)).astype(BF16)

    @pl.when(c == pl.num_programs(1) - 1)
    def _():
        for r in range(ns):
            new_ref[0, r:r + 1, :] = tail_s[3 * r:3 * r + 1, :]


def _act(gate, up, rb0, nb, nch, nskip, prev, w, act_in=None, t=CH):
    r, n = gate.shape

    def rowmap(b, c):
        return (rb0 + b * nch + c, 0)

    pm = jnp.asarray(_shift_matrix(FFN_K, t), BF16)
    args = [gate, up, prev, w, pm]
    in_specs = [pl.BlockSpec((t, n), rowmap), pl.BlockSpec((t, n), rowmap),
                pl.BlockSpec((1, FFN_K - 1, n), lambda b, c: (b, 0, 0)),
                pl.BlockSpec((FFN_K, n), lambda b, c: (0, 0)),
                pl.BlockSpec(pm.shape, lambda b, c: (0, 0))]
    kern = functools.partial(_act_kernel, nskip)
    aliases = {}
    if act_in is not None:
        aliases = {len(args): 0}
        args.append(act_in)
        in_specs.append(pl.BlockSpec(memory_space=pl.ANY))
        kern = functools.partial(_act_alias_kernel, nskip)
    return pl.pallas_call(
        kern,
        grid=(nb, nch),
        in_specs=in_specs,
        out_specs=[pl.BlockSpec((t, n), rowmap), pl.BlockSpec((1, FFN_K - 1, n), lambda b, c: (b, 0, 0))],
        out_shape=[jax.ShapeDtypeStruct((r, n), BF16), jax.ShapeDtypeStruct((nb, FFN_K - 1, n), F32)],
        scratch_shapes=[pltpu.VMEM((TAIL, n), F32)],
        input_output_aliases=aliases,
        compiler_params=_cp(2), name="act",
    )(*args)


def _act_alias_kernel(nskip, g_ref, u_ref, prev_ref, w_ref, pm_ref, keep_ref, a_ref, new_ref, tail_s):
    del keep_ref
    _act_kernel(nskip, g_ref, u_ref, prev_ref, w_ref, pm_ref, a_ref, new_ref, tail_s)


def _cast_cols_kernel(x_ref, o_ref):
    n = x_ref.shape[1]
    o_ref[:, :n] = x_ref[...].astype(BF16)
    if o_ref.shape[1] > n:
        o_ref[:, n:] = jnp.zeros((o_ref.shape[0], o_ref.shape[1] - n), BF16)


def _cast_pad_cols(w, layer, n_out):
    _, k, n = w.shape
    tr = _pick(k, (128, 64))
    return pl.pallas_call(
        _cast_cols_kernel,
        grid=(k // tr,),
        in_specs=[pl.BlockSpec((None, tr, n), lambda i: (layer, i, 0))],
        out_specs=pl.BlockSpec((tr, n_out), lambda i: (i, 0)),
        out_shape=jax.ShapeDtypeStruct((k, n_out), BF16),
        compiler_params=_cp(1), name="cast_cols",
    )(w)


def _cast_rows_kernel(n_full, x_ref, o_ref):
    @pl.when(pl.program_id(0) < n_full)
    def _():
        o_ref[...] = x_ref[...].astype(BF16)

    @pl.when(pl.program_id(0) >= n_full)
    def _():
        o_ref[...] = jnp.zeros(o_ref.shape, BF16)


def _cast_pad_rows(w, layer, k_out):
    _, k, n = w.shape
    tr = 256
    assert k % tr == 0 and k_out % tr == 0
    n_full = k // tr
    return pl.pallas_call(
        functools.partial(_cast_rows_kernel, n_full),
        grid=(k_out // tr,),
        in_specs=[pl.BlockSpec((None, tr, n), lambda i: (layer, jnp.minimum(i, n_full - 1), 0))],
        out_specs=pl.BlockSpec((tr, n), lambda i: (i, 0)),
        out_shape=jax.ShapeDtypeStruct((k_out, n), BF16),
        compiler_params=_cp(1), name="cast_rows",
    )(w)


def _pack_win_kernel(w_ref, o_ref):
    w = w_ref[...]
    o_dt = SSD_DI + SSD_CD
    o_sc = o_dt + SSD_H
    o_kr = o_sc + 3 * SC_D + QL + KVL
    half = ROPE // 2
    o_ref[:, 0:o_dt] = w[:, 0:o_dt].astype(BF16)
    o_ref[:, o_dt:o_dt + (o_kr - o_sc)] = w[:, o_sc:o_kr].astype(BF16)
    c = o_dt + (o_kr - o_sc)
    kr = w[:, o_kr:o_kr + ROPE]
    o_ref[:, c:c + ROPE] = kr.astype(BF16)
    o_ref[:, c + ROPE:c + ROPE + half] = kr[:, half:].astype(BF16)
    o_ref[:, c + ROPE + half:c + 2 * ROPE] = kr[:, :half].astype(BF16)
    o_ref[:, c + 2 * ROPE:c + 2 * ROPE + SSD_H] = w[:, o_dt:o_sc].astype(BF16)
    o_ref[:, c + 2 * ROPE + SSD_H:] = jnp.zeros((w.shape[0], LANE - SSD_H), BF16)


def _pack_win(w_in, layer):
    _, k, n = w_in.shape
    tr = 128
    return pl.pallas_call(
        _pack_win_kernel,
        grid=(k // tr,),
        in_specs=[pl.BlockSpec((None, tr, n), lambda i: (layer, i, 0))],
        out_specs=pl.BlockSpec((tr, NIN), lambda i: (i, 0)),
        out_shape=jax.ShapeDtypeStruct((k, NIN), BF16),
        compiler_params=_cp(1), name="pack_win",
    )(w_in)


def _swap_half(w):
    half = w.shape[-1] // 2
    return jnp.concatenate([w[..., half:], w[..., :half]], axis=-1)


def _layer_weights(i, w_in, ssd_conv_w, ssd_conv_b, ssd_dt_bias, ssd_a_log, ssd_d, ssd_norm, sc_conv_w,
                   mla_q_norm, mla_w_uq, mla_kv_norm, mla_w_ukv, w_out, ffn_w_gate, ffn_w_up, ffn_conv_w,
                   ffn_w_down):
    win = _pack_win(w_in, i)
    uq = mla_w_uq[i].reshape(QL, MLA_H, NOPE + ROPE)
    pe = uq[..., NOPE:]
    zq = jnp.zeros((QL, MLA_H, HK - NOPE - ROPE), F32)
    wa = jnp.concatenate([uq[..., :NOPE], pe, zq], axis=-1).reshape(QL, MLA_H * HK).astype(BF16)
    wb = jnp.concatenate([_swap_half(pe), zq], axis=-1).reshape(QL, MLA_H * LANE).astype(BF16)
    ukv = mla_w_ukv[i].reshape(KVL, MLA_H, NOPE + MLA_V)
    padf = ((0, 0), (0, DFP - D_FF))
    return dict(
        win=win,
        cw=ssd_conv_w[i], cb=ssd_conv_b[i].reshape(1, SSD_CD),
        dtb=jnp.pad(ssd_dt_bias[i], (0, LANE - SSD_H)).reshape(1, LANE),
        alog=jnp.pad(ssd_a_log[i], (0, LANE - SSD_H)).reshape(1, LANE),
        dsk=jnp.repeat(ssd_d[i], SSD_P).reshape(1, SSD_DI),
        ng=ssd_norm[i].reshape(1, SSD_DI),
        scw=sc_conv_w[i],
        qn=mla_q_norm[i].reshape(1, QL), kvn=mla_kv_norm[i].reshape(1, KVL),
        wa=wa, wb=wb,
        wk=ukv[..., :NOPE].reshape(KVL, MLA_H * NOPE).astype(BF16),
        wv=ukv[..., NOPE:].reshape(KVL, MLA_H * MLA_V).astype(BF16),
        wkt=jnp.transpose(ukv[..., :NOPE], (1, 2, 0)).astype(BF16),
        wv3=jnp.transpose(ukv[..., NOPE:], (1, 0, 2)).astype(BF16),
        wout=_cast_pad_cols(w_out, i, D_MODEL),
        wg=_cast_pad_cols(ffn_w_gate, i, DFP),
        wu=_cast_pad_cols(ffn_w_up, i, DFP),
        fcw=jnp.pad(ffn_conv_w[i], padf),
        wd=_cast_pad_rows(ffn_w_down, i, DFP),
    )


def _rope_table(seq, ns, past):
    half = ROPE // 2
    pos = jnp.concatenate([N_META + jnp.arange(seq, dtype=jnp.int32),
                           jnp.maximum(jnp.arange(CH, dtype=jnp.int32) - NSKIP, 0),
                           N_META + past + jnp.arange(ns, dtype=jnp.int32) % CH])
    inv = THETA ** (-jnp.arange(half, dtype=F32) / half)
    ang = pos.astype(F32)[:, None] * inv[None, :]
    cos, sin = jnp.cos(ang), jnp.sin(ang)
    return jnp.concatenate([cos, cos, -sin, sin], axis=1)


def kernel(x_prompt, x_sample, cache_kv_latent, cache_k_rope, state_ssm, state_ssd_conv, state_sconv, state_ffn_conv, meta_tokens, norm_mix_pre, norm_mix_post, norm_ffn_pre, norm_ffn_post, w_in, ssd_conv_w, ssd_conv_b, ssd_dt_bias, ssd_a_log, ssd_d, ssd_norm, sc_conv_w, mla_q_norm, mla_w_uq, mla_kv_norm, mla_w_ukv, w_out, ffn_w_gate, ffn_w_up, ffn_conv_w, ffn_w_down):
    bp, seq, d = x_prompt.shape
    nb, ls, _ = x_sample.shape
    depth, _, past, _ = cache_kv_latent.shape
    assert bp == 1 and ls == CH and seq % CH == 0 and d == D_MODEL
    lp = CH + seq
    ns = nb * ls
    npc = lp // CH
    c0 = seq // CH

    x, h = _assemble(x_prompt[0], x_sample.reshape(ns, d), meta_tokens.astype(F32), norm_mix_pre[0])
    tab = _rope_table(seq, ns, past)
    zero_c = jnp.zeros((1, SSD_K - 1, SSD_CD), F32)
    zero_h = jnp.zeros((1, SSD_N, SSD_DI), F32)

    def state_in(s):
        return jnp.transpose(s, (0, 3, 1, 2)).reshape(s.shape[0], SSD_N, SSD_DI)

    def state_out(s):
        return jnp.transpose(s.reshape(s.shape[0], SSD_N, SSD_H, SSD_P), (0, 2, 3, 1))
    zero_s = jnp.zeros((1, SC_K - 1, SC_D), F32)
    zero_f = jnp.zeros((1, FFN_K - 1, DFP), F32)
    padf = ((0, 0), (0, 0), (0, DFP - D_FF))

    outs_p, outs_s = [], []
    for i in range(depth):
        lw = _layer_weights(i, w_in, ssd_conv_w, ssd_conv_b, ssd_dt_bias, ssd_a_log, ssd_d, ssd_norm,
                            sc_conv_w, mla_q_norm, mla_w_uq, mla_kv_norm, mla_w_ukv, w_out, ffn_w_gate,
                            ffn_w_up, ffn_conv_w, ffn_w_down)
        u, dtf = _inproj(h, lw["win"])

        yp, yscp, cnew_p, hnew_p, scnew_p = _mixer(u, dtf, 0, 1, npc, NSKIP, zero_c, zero_h, zero_s, lw)
        y_ssd, y_sc, cnew_s, hnew_s, scnew_s = _mixer(u, dtf, npc, nb, 1, 0, state_ssd_conv[i],
                                                      state_in(state_ssm[i]), state_sconv[i], lw,
                                                      keep=(yp, yscp))
        hnew_p, hnew_s = state_out(hnew_p), state_out(hnew_s)

        q, ckv, kpe, kcat, v = _mla_proj(u, tab, lw)
        y_mla = _attn0(q, kcat, v, c0, _flash(q, kcat, v, seq))
        y_mla = _cached_attn(q, ckv, kpe, cache_kv_latent, cache_k_rope, i, npc, lw, y_mla)

        mix = _outproj(y_ssd, y_sc, y_mla, lw["wout"])
        x, h = _resid_norm(x, mix, norm_mix_post[i], norm_ffn_pre[i])

        gate, up = _gateup(h, lw["wg"], lw["wu"])
        act, f0 = _act(gate, up, c0, 1, 1, NSKIP, zero_f, lw["fcw"])
        act, fnew_p = _act(gate, up, 0, 1, seq // ACT_ROWS, 0, f0, lw["fcw"], act_in=act, t=ACT_ROWS)
        act, fnew_s = _act(gate, up, npc, nb, 1, 0, jnp.pad(state_ffn_conv[i], padf), lw["fcw"], act_in=act)
        f = _down(act, lw["wd"])
        if i + 1 < depth:
            x, h = _resid_norm(x, f, norm_ffn_post[i], norm_mix_pre[i + 1])
        else:
            y_prompt = _resid_out(x, f, norm_ffn_post[i], 0, seq)
            y_sample = _resid_out(x, f, norm_ffn_post[i], lp, ns)

        meta_rows = slice(seq + NSKIP, lp)
        outs_p.append((jnp.concatenate([ckv[meta_rows], ckv[:seq]], axis=0)[None],
                       jnp.concatenate([kpe[meta_rows, :ROPE], kpe[:seq, :ROPE]], axis=0)[None],
                       hnew_p, cnew_p, scnew_p, fnew_p[:, :, :D_FF]))
        outs_s.append((ckv[lp:].reshape(nb, ls, KVL), kpe[lp:, :ROPE].reshape(nb, ls, ROPE), hnew_s, cnew_s,
                       scnew_s, fnew_s[:, :, :D_FF]))

    def stack(outs, j):
        return jnp.stack([o[j] for o in outs], axis=0)

    return (y_prompt[None], y_sample.reshape(nb, ls, d),
            stack(outs_p, 0), stack(outs_p, 1), stack(outs_p, 2), stack(outs_p, 3), stack(outs_p, 4), stack(outs_p, 5),
            stack(outs_s, 0), stack(outs_s, 1), stack(outs_s, 2), stack(outs_s, 3), stack(outs_s, 4), stack(outs_s, 5))
```

```python
import functools

import jax
import jax.numpy as jnp
import numpy as np
from jax import lax
from jax.experimental import pallas as pl
from jax.experimental.pallas import tpu as pltpu

F32 = jnp.float32
BF16 = jnp.bfloat16

D_MODEL = 4096
N_META = 16
CH = 64
NSKIP = CH - N_META
EPS = 1e-6
SSD_P = 64
SSD_DI = 2048
SSD_H = 32
SSD_G = 8
SSD_N = 128
SSD_K = 4
SSD_CD = 4096
SC_D = 1024
SC_K = 3
MLA_H = 8
NOPE = 128
ROPE = 64
MLA_V = 128
QL = 768
KVL = 512
HK = 256
SCALE = (NOPE + ROPE) ** -0.5
QSCALE = SCALE * 1.4426950408889634
THETA = 10000.0
D_FF = 11008
DFP = 11264
FFN_K = 3
LANE = 128

C_Z, C_X, C_BC, C_SCB, C_SCC, C_SCH, C_MLA, NIN = 0, 2048, 4096, 6144, 7168, 8192, 9216, 10752
MLA_W = 1536

FLASH_TQ = 512
ACT_ROWS = 128
VMEM_LIMIT = 56 * 1024 * 1024
HI = lax.Precision.HIGHEST
NT = (((1,), (1,)), ((), ()))
TN = (((0,), (0,)), ((), ()))


def _cp(n, flags=None):
    return pltpu.CompilerParams(dimension_semantics=("arbitrary",) * n, vmem_limit_bytes=VMEM_LIMIT, flags=flags)


def _pick(n, cands):
    for c in cands:
        if n % c == 0:
            return c
    raise ValueError(f"no tile for {n}")


MM_ROWS = (1088, 544, 512, 272, 256, 136, 128, 64)
EW_ROWS = (272, 256, 136, 128, 64)


def _stored_chunk(c, nch):
    return (c + nch - 1) % nch


def _rms(x, g):
    return x * lax.rsqrt(jnp.mean(x * x, axis=-1, keepdims=True) + EPS) * g


def _silu(x):
    return x * jax.nn.sigmoid(x)


def _norm_rows_kernel(x_in_ref, g_ref, x_ref, h_ref):
    x = x_in_ref[...]
    x_ref[...] = x
    h_ref[...] = _rms(x, g_ref[...]).astype(BF16)


def _tail_rows_kernel(xs_ref, meta_ref, g_ref, keep_x, keep_h, x_ref, h_ref):
    del keep_x, keep_h
    c = pl.program_id(0)

    def emit(x):
        x_ref[...] = x
        h_ref[...] = _rms(x, g_ref[...]).astype(BF16)

    @pl.when(c == 0)
    def _():
        emit(jnp.concatenate([jnp.zeros((NSKIP, x_ref.shape[1]), F32), meta_ref[...]], axis=0))

    @pl.when(c > 0)
    def _():
        emit(xs_ref[...])


def _assemble(x_prompt, x_sample, meta, g):
    seq, d = x_prompt.shape
    ns = x_sample.shape[0]
    r = seq + CH + ns
    bm = _pick(seq, (512, 256, 128, 64))
    vec = pl.BlockSpec((1, d), lambda c: (0, 0))
    out_shape = [jax.ShapeDtypeStruct((r, d), F32), jax.ShapeDtypeStruct((r, d), BF16)]
    big = pl.BlockSpec((bm, d), lambda c: (c, 0))
    x, h = pl.pallas_call(
        _norm_rows_kernel, grid=(seq // bm,), in_specs=[big, vec], out_specs=[big, big],
        out_shape=out_shape, compiler_params=_cp(1), name="assemble_frames",
    )(x_prompt, g.reshape(1, d))
    c0 = seq // CH
    small = pl.BlockSpec((CH, d), lambda c: (c0 + c, 0))
    anyspec = pl.BlockSpec(memory_space=pl.ANY)
    return pl.pallas_call(
        _tail_rows_kernel, grid=(1 + ns // CH,),
        in_specs=[pl.BlockSpec((CH, d), lambda c: (jnp.maximum(c - 1, 0), 0)),
                  pl.BlockSpec((N_META, d), lambda c: (0, 0)), vec, anyspec, anyspec],
        out_specs=[small, small], out_shape=out_shape, input_output_aliases={3: 0, 4: 1},
        compiler_params=_cp(1), name="assemble_tail",
    )(x_sample, meta, g.reshape(1, d), x, h)


def _resid_out_kernel(x_ref, m_ref, gp_ref, y_ref):
    y_ref[...] = x_ref[...] + _rms(m_ref[...], gp_ref[...])


def _resid_out(x, m, g_post, row0, rows):
    d = x.shape[1]
    bm = next(b for b in (512, 256, 128, 64) if rows % b == 0 and row0 % b == 0)
    src = pl.BlockSpec((bm, d), lambda c: (row0 // bm + c, 0))
    return pl.pallas_call(
        _resid_out_kernel, grid=(rows // bm,),
        in_specs=[src, src, pl.BlockSpec((1, d), lambda c: (0, 0))],
        out_specs=pl.BlockSpec((bm, d), lambda c: (c, 0)),
        out_shape=jax.ShapeDtypeStruct((rows, d), F32),
        compiler_params=_cp(1), name="resid_out",
    )(x, m, g_post.reshape(1, d))


def _resid_norm_kernel(x_ref, m_ref, gp_ref, gn_ref, x2_ref, h_ref):
    x2 = x_ref[...] + _rms(m_ref[...], gp_ref[...])
    x2_ref[...] = x2
    h_ref[...] = _rms(x2, gn_ref[...]).astype(BF16)


def _resid_norm(x, m, g_post, g_next):
    r, d = x.shape
    bm = _pick(r, EW_ROWS)
    row = pl.BlockSpec((bm, d), lambda i: (i, 0))
    vec = pl.BlockSpec((1, d), lambda i: (0, 0))
    return pl.pallas_call(
        _resid_norm_kernel, grid=(r // bm,), in_specs=[row, row, vec, vec], out_specs=[row, row],
        out_shape=[jax.ShapeDtypeStruct((r, d), F32), jax.ShapeDtypeStruct((r, d), BF16)],
        compiler_params=_cp(1), name="resid_norm",
    )(x, m, g_post.reshape(1, d), g_next.reshape(1, d))


def _inproj_kernel(x_ref, w_ref, u_ref, dt_ref):
    acc = jnp.dot(x_ref[...], w_ref[...], preferred_element_type=F32)
    u_ref[...] = acc.astype(BF16)

    @pl.when(pl.program_id(1) == pl.num_programs(1) - 1)
    def _():
        dt_ref[...] = acc[:, acc.shape[1] - LANE:]


def _inproj(h, w):
    r, k = h.shape
    n = w.shape[1]
    bm = _pick(r, MM_ROWS)
    bn = 768
    return pl.pallas_call(
        _inproj_kernel,
        grid=(r // bm, n // bn),
        in_specs=[pl.BlockSpec((bm, k), lambda i, j: (i, 0)), pl.BlockSpec((k, bn), lambda i, j: (0, j))],
        out_specs=[pl.BlockSpec((bm, bn), lambda i, j: (i, j)), pl.BlockSpec((bm, LANE), lambda i, j: (i, 0))],
        out_shape=[jax.ShapeDtypeStruct((r, n), BF16), jax.ShapeDtypeStruct((r, LANE), F32)],
        compiler_params=_cp(2), name="inproj",
    )(h, w)


def _outproj_kernel(a_ref, b_ref, c_ref, wa_ref, wb_ref, wc_ref, o_ref):
    acc = jnp.dot(a_ref[...], wa_ref[...], preferred_element_type=F32)
    acc = acc + jnp.dot(b_ref[...], wb_ref[...], preferred_element_type=F32)
    acc = acc + jnp.dot(c_ref[...], wc_ref[...], preferred_element_type=F32)
    o_ref[...] = acc


def _outproj(y_ssd, y_sc, y_mla, w):
    r = y_ssd.shape[0]
    n = w.shape[1]
    bm = _pick(r, MM_ROWS)
    bn = 1024
    return pl.pallas_call(
        _outproj_kernel,
        grid=(r // bm, n // bn),
        in_specs=[pl.BlockSpec((bm, SSD_DI), lambda i, j: (i, 0)),
                  pl.BlockSpec((bm, SC_D), lambda i, j: (i, 0)),
                  pl.BlockSpec((bm, SC_D), lambda i, j: (i, 0)),
                  pl.BlockSpec((SSD_DI, bn), lambda i, j: (0, j)),
                  pl.BlockSpec((SC_D, bn), lambda i, j: (2, j)),
                  pl.BlockSpec((SC_D, bn), lambda i, j: (3, j))],
        out_specs=pl.BlockSpec((bm, bn), lambda i, j: (i, j)),
        out_shape=jax.ShapeDtypeStruct((r, n), F32),
        compiler_params=_cp(2), name="outproj",
    )(y_ssd, y_sc, y_mla, w, w, w)


def _gateup_kernel(x_ref, wg_ref, wu_ref, g_ref, u_ref):
    x = x_ref[...]
    g_ref[...] = jnp.dot(x, wg_ref[...], preferred_element_type=F32).astype(BF16)
    u_ref[...] = jnp.dot(x, wu_ref[...], preferred_element_type=F32).astype(BF16)


def _gateup(h, wg, wu):
    r, k = h.shape
    n = wg.shape[1]
    bm = _pick(r, MM_ROWS)
    bn = 512
    wspec = pl.BlockSpec((k, bn), lambda i, j: (0, j))
    ospec = pl.BlockSpec((bm, bn), lambda i, j: (i, j))
    return pl.pallas_call(
        _gateup_kernel,
        grid=(r // bm, n // bn),
        in_specs=[pl.BlockSpec((bm, k), lambda i, j: (i, 0)), wspec, wspec],
        out_specs=[ospec, ospec],
        out_shape=[jax.ShapeDtypeStruct((r, n), BF16)] * 2,
        compiler_params=_cp(2), name="gateup",
    )(h, wg, wu)


def _down_kernel(x_ref, w_ref, o_ref):
    p = jnp.dot(x_ref[...], w_ref[...], preferred_element_type=F32)

    @pl.when(pl.program_id(2) == 0)
    def _():
        o_ref[...] = p

    @pl.when(pl.program_id(2) > 0)
    def _():
        o_ref[...] += p


def _down(a, w):
    r, k = a.shape
    n = w.shape[1]
    bm = _pick(r, MM_ROWS)
    bn = 1024
    bk = k // 4
    return pl.pallas_call(
        _down_kernel,
        grid=(r // bm, n // bn, k // bk),
        in_specs=[pl.BlockSpec((bm, bk), lambda i, j, kk: (i, kk)), pl.BlockSpec((bk, bn), lambda i, j, kk: (kk, j))],
        out_specs=pl.BlockSpec((bm, bn), lambda i, j, kk: (i, j)),
        out_shape=jax.ShapeDtypeStruct((r, n), F32),
        compiler_params=_cp(3), name="down",
    )(a, w)


def _mixer_kernel(nskip, z_ref, x_ref, bc_ref, dt_ref, scb_ref, scc_ref, sch_ref,
                  cprev_ref, h0_ref, scprev_ref,
                  cw_ref, cb_ref, dtb_ref, alog_ref, dsk_ref, ng_ref, scw_ref, pm_ref,
                  y_ref, ysc_ref, cnew_ref, hnew_ref, scnew_ref,
                  buf, sbuf, hst):
    t = CH
    c = pl.program_id(1)
    last = pl.num_programs(1) - 1

    ns = SSD_K - 1

    @pl.when(c == 0)
    def _init():
        _load_tail(buf, cprev_ref[0])
        sbuf[0:8, :] = jnp.zeros((8, SC_D), F32)
        sbuf[8 - (SC_K - 1):8, :] = scprev_ref[0]
        hst[...] = h0_ref[0]

    rows = lax.broadcasted_iota(jnp.int32, (t, 1), 0) + c * t
    valid = rows >= nskip

    xin = jnp.concatenate([x_ref[...], bc_ref[...]], axis=1)
    xin = jnp.where(valid, xin, jnp.zeros((), xin.dtype))
    delayed = _delayed(pm_ref, buf, xin, ns)
    acc = cb_ref[...] + cw_ref[ns:ns + 1, :] * xin.astype(F32)
    for d in range(1, ns + 1):
        acc = acc + cw_ref[ns - d:ns - d + 1, :] * delayed[d - 1]
    xbc = _silu(acc)

    lane = lax.broadcasted_iota(jnp.int32, (t, LANE), 1)
    v = dt_ref[...] + dtb_ref[...]
    dt = jnp.maximum(v, 0.0) + jnp.log1p(jnp.exp(-jnp.abs(v)))
    dt = jnp.where(valid & (lane < SSD_H), dt, 0.0)
    adt = dt * (-jnp.exp(alog_ref[...]))
    ri = lax.broadcasted_iota(jnp.int32, (t, t), 0)
    ci = lax.broadcasted_iota(jnp.int32, (t, t), 1)
    tri = ri >= ci
    acs = jnp.dot(tri.astype(F32), adt, precision=HI, preferred_element_type=F32)

    hpg = SSD_H // SSD_G
    gw = hpg * SSD_P
    lane_g = lax.broadcasted_iota(jnp.int32, (t, gw), 1)
    row_g = lax.broadcasted_iota(jnp.int32, (t, gw), 0)
    seg = lane_g // SSD_P
    eye_g = row_g == lane_g % SSD_P
    tri_g = row_g >= lane_g % SSD_P
    blockdiag = (lax.broadcasted_iota(jnp.int32, (hpg * t, gw), 0) // t
                 == lax.broadcasted_iota(jnp.int32, (hpg * t, gw), 1) // SSD_P)

    def per_head_lanes(mat, g):
        out = jnp.broadcast_to(mat[:, g * hpg + hpg - 1:g * hpg + hpg], (t, gw))
        for r in range(hpg - 2, -1, -1):
            out = jnp.where(seg == r, jnp.broadcast_to(mat[:, g * hpg + r:g * hpg + r + 1], (t, gw)), out)
        return out

    for g in range(SSD_G):
        cols = slice(g * gw, (g + 1) * gw)
        bg = xbc[:, SSD_DI + g * SSD_N:SSD_DI + (g + 1) * SSD_N].astype(BF16)
        cg = xbc[:, SSD_DI + SSD_G * SSD_N + g * SSD_N:SSD_DI + SSD_G * SSD_N + (g + 1) * SSD_N].astype(BF16)
        acol = per_head_lanes(acs, g)
        dtx = per_head_lanes(dt, g)
        arow = jnp.sum(jnp.where(eye_g, acol, 0.0), axis=0, keepdims=True)
        decay = jnp.exp(jnp.where(tri_g, acol - arow, -jnp.inf))
        cb = lax.dot_general(cg, jnp.concatenate([bg] * hpg, axis=0), NT, preferred_element_type=F32)
        xg = xbc[:, cols]
        xdt = xg * dtx
        xdtb = xdt.astype(BF16)
        rhs = jnp.where(blockdiag, jnp.concatenate([xdtb] * hpg, axis=0), jnp.zeros((), BF16))
        ydiag = jnp.dot((cb * decay).astype(BF16), rhs, preferred_element_type=F32)
        hg = hst[:, cols]
        yoff = jnp.dot(cg, hg.astype(BF16), preferred_element_type=F32) * jnp.exp(acol)
        alast = acol[t - 1:t, :]
        snew = lax.dot_general(bg, (xdt * jnp.exp(alast - acol)).astype(BF16), TN, preferred_element_type=F32)
        hst[:, cols] = jnp.exp(alast) * hg + snew
        y = ydiag + yoff + dsk_ref[:, cols] * xg
        y = y * _silu(z_ref[:, cols].astype(F32))
        y_ref[:, cols] = _rms(y, ng_ref[:, cols]).astype(BF16)

    p = scc_ref[...].astype(F32) * sch_ref[...].astype(F32)
    p = jnp.where(valid, p, 0.0)
    sbuf[8:8 + t, :] = p
    conv = scw_ref[0:1, :] * sbuf[6:6 + t, :]
    for i in range(1, SC_K):
        conv = conv + scw_ref[i:i + 1, :] * sbuf[6 + i:6 + i + t, :]
    ysc_ref[...] = (scb_ref[...].astype(F32) * conv).astype(BF16)
    sbuf[0:8, :] = sbuf[t:t + 8, :]

    @pl.when(c == last)
    def _():
        for r in range(ns):
            cnew_ref[0, r:r + 1, :] = buf[3 * r:3 * r + 1, :]
        hnew_ref[0] = hst[...]
        scnew_ref[0] = sbuf[8 - (SC_K - 1):8, :]


def _mixer_alias_kernel(nskip, *refs):
    n_in = 18
    _mixer_kernel(nskip, *refs[:n_in], *refs[n_in + 2:])


def _mixer(u, dtf, rb0, nb, nch, nskip, cprev, h0, scprev, lw, keep=None):
    r = u.shape[0]
    t = CH

    def rowmap(cb):
        return lambda b, c: (rb0 + b * nch + _stored_chunk(c, nch), cb)

    def stmap(b, c):
        return (b, 0, 0)

    def full(shape):
        return pl.BlockSpec(shape, lambda b, c: (0,) * len(shape))

    in_specs = [
        pl.BlockSpec((t, SSD_DI), rowmap(C_Z // SSD_DI)),
        pl.BlockSpec((t, SSD_DI), rowmap(C_X // SSD_DI)),
        pl.BlockSpec((t, SSD_DI), rowmap(C_BC // SSD_DI)),
        pl.BlockSpec((t, LANE), rowmap(0)),
        pl.BlockSpec((t, SC_D), rowmap(C_SCB // SC_D)),
        pl.BlockSpec((t, SC_D), rowmap(C_SCC // SC_D)),
        pl.BlockSpec((t, SC_D), rowmap(C_SCH // SC_D)),
        pl.BlockSpec((1, SSD_K - 1, SSD_CD), stmap),
        pl.BlockSpec((1, SSD_N, SSD_DI), stmap),
        pl.BlockSpec((1, SC_K - 1, SC_D), stmap),
        full((SSD_K, SSD_CD)), full((1, SSD_CD)), full((1, LANE)), full((1, LANE)),
        full((1, SSD_DI)), full((1, SSD_DI)), full((SC_K, SC_D)),
        full((CH * (SSD_K - 1) + TAIL, TAIL + CH)),
    ]
    out_specs = [
        pl.BlockSpec((t, SSD_DI), rowmap(0)),
        pl.BlockSpec((t, SC_D), rowmap(0)),
        pl.BlockSpec((1, SSD_K - 1, SSD_CD), stmap),
        pl.BlockSpec((1, SSD_N, SSD_DI), stmap),
        pl.BlockSpec((1, SC_K - 1, SC_D), stmap),
    ]
    out_shape = [
        jax.ShapeDtypeStruct((r, SSD_DI), BF16),
        jax.ShapeDtypeStruct((r, SC_D), BF16),
        jax.ShapeDtypeStruct((nb, SSD_K - 1, SSD_CD), F32),
        jax.ShapeDtypeStruct((nb, SSD_N, SSD_DI), F32),
        jax.ShapeDtypeStruct((nb, SC_K - 1, SC_D), F32),
    ]
    args = [u, u, u, dtf, u, u, u, cprev, h0, scprev,
            lw["cw"], lw["cb"], lw["dtb"], lw["alog"], lw["dsk"], lw["ng"], lw["scw"],
            jnp.asarray(_shift_matrix(SSD_K), BF16)]
    kern = functools.partial(_mixer_kernel, nskip)
    aliases = {}
    if keep is not None:
        aliases = {len(args): 0, len(args) + 1: 1}
        args += list(keep)
        in_specs += [pl.BlockSpec(memory_space=pl.ANY)] * 2
        kern = functools.partial(_mixer_alias_kernel, nskip)
    return pl.pallas_call(
        kern,
        grid=(nb, nch),
        in_specs=in_specs, out_specs=out_specs, out_shape=out_shape,
        scratch_shapes=[pltpu.VMEM((TAIL, SSD_CD), F32), pltpu.VMEM((8 + t, SC_D), F32),
                        pltpu.VMEM((SSD_N, SSD_DI), F32)],
        input_output_aliases=aliases,
        compiler_params=_cp(2), name="mixer",
    )(*args)


def _mla_proj_kernel(blk_ref, tab_ref, qn_ref, kvn_ref, wa_ref, wb_ref, wk_ref, wv_ref,
                     q_ref, ckv_ref, kpe_ref, kcat_ref, v_ref):
    blk = blk_ref[...]
    cq = _rms(blk[:, :QL].astype(F32), qn_ref[...]).astype(BF16)
    ckv = _rms(blk[:, QL:QL + KVL].astype(F32), kvn_ref[...])
    ckv_ref[...] = ckv
    tab = tab_ref[...]
    tabr = pltpu.roll(tab, ROPE, 1)
    prod = blk[:, QL + KVL:QL + KVL + LANE].astype(F32) * tab
    ksum = prod + pltpu.roll(prod, ROPE, 1)
    lane = lax.broadcasted_iota(jnp.int32, ksum.shape, 1)
    kpe = jnp.where(lane < ROPE, ksum, 0.0)
    kpe_ref[...] = kpe
    qa = jnp.dot(cq, wa_ref[...], preferred_element_type=F32)
    qb = jnp.dot(cq, wb_ref[...], preferred_element_type=F32)
    ckvb = ckv.astype(BF16)
    kn = jnp.dot(ckvb, wk_ref[...], preferred_element_type=F32)
    v_ref[...] = jnp.dot(ckvb, wv_ref[...], preferred_element_type=F32).astype(BF16)
    kpeb = kpe.astype(BF16)
    tab_q = tab * QSCALE
    tabr_q = tabr * QSCALE
    for h in range(MLA_H):
        q_ref[:, h * HK:h * HK + NOPE] = (qa[:, h * HK:h * HK + NOPE] * QSCALE).astype(BF16)
        q_ref[:, h * HK + NOPE:(h + 1) * HK] = (
            qa[:, h * HK + NOPE:(h + 1) * HK] * tab_q + qb[:, h * LANE:(h + 1) * LANE] * tabr_q).astype(BF16)
        kcat_ref[:, h * HK:h * HK + NOPE] = kn[:, h * NOPE:(h + 1) * NOPE].astype(BF16)
        kcat_ref[:, h * HK + NOPE:(h + 1) * HK] = kpeb


def _mla_proj(u, tab, lw):
    r = u.shape[0]
    bm = _pick(r, (544, 512, 272, 256, 136, 128, 64))

    def full(shape):
        return pl.BlockSpec(shape, lambda i: (0,) * len(shape))

    def row(w):
        return pl.BlockSpec((bm, w), lambda i: (i, 0))

    return pl.pallas_call(
        _mla_proj_kernel,
        grid=(r // bm,),
        in_specs=[pl.BlockSpec((bm, MLA_W), lambda i: (i, C_MLA // MLA_W)), row(LANE),
                  full((1, QL)), full((1, KVL)), full((QL, MLA_H * HK)), full((QL, MLA_H * LANE)),
                  full((KVL, MLA_H * NOPE)), full((KVL, MLA_H * MLA_V))],
        out_specs=[row(MLA_H * HK), row(KVL), row(LANE), row(MLA_H * HK), row(MLA_H * MLA_V)],
        out_shape=[jax.ShapeDtypeStruct((r, MLA_H * HK), BF16), jax.ShapeDtypeStruct((r, KVL), F32),
                   jax.ShapeDtypeStruct((r, LANE), F32), jax.ShapeDtypeStruct((r, MLA_H * HK), BF16),
                   jax.ShapeDtypeStruct((r, MLA_H * MLA_V), BF16)],
        compiler_params=_cp(1), name="mla_proj",
    )(u, tab, lw["qn"], lw["kvn"], lw["wa"], lw["wb"], lw["wk"], lw["wv"])


def _flash_kernel(it_ref, jt_ref, q_ref, k_ref, v_ref, k0_ref, v0_ref, mask_ref, pad_ref, o_ref,
                  qt_s, m_s, l_s, acc_s, s_scr, p_scr, a_s):
    tq = q_ref.shape[0]
    tk = k_ref.shape[0]
    step = pl.program_id(0)
    i = it_ref[step]
    j = jt_ref[step]

    @pl.when(j == 0)
    def _init():
        for h in range(MLA_H):
            qt_s[h * HK:(h + 1) * HK, :] = q_ref[:, h * HK:(h + 1) * HK].T
        for h in range(MLA_H):
            s0 = jnp.dot(k0_ref[:, h * HK:(h + 1) * HK], qt_s[h * HK:(h + 1) * HK, :],
                         preferred_element_type=F32) + pad_ref[...]
            m0 = jnp.max(s0, axis=0, keepdims=True)
            p0 = jnp.exp2(s0 - m0)
            rows = slice(h * MLA_V, (h + 1) * MLA_V)
            m_s[h] = m0
            l_s[h] = jnp.sum(p0, axis=0, keepdims=True)
            acc_s[rows, :] = lax.dot_general(v0_ref[:, rows], p0.astype(BF16), TN, preferred_element_type=F32)

    def tile(masked):
        nkb = tk // CH

        def block(sref, kb):
            blk = sref[kb * CH:(kb + 1) * CH, :]
            if masked:
                blk = blk + mask_ref[kb * CH:(kb + 1) * CH, :]
            return blk

        def scores(h):
            s_scr[h % 2] = jnp.dot(k_ref[:, h * HK:(h + 1) * HK], qt_s[h * HK:(h + 1) * HK, :],
                                   preferred_element_type=F32)

        def softmax(h):
            sref = s_scr.at[h % 2]
            pref = p_scr.at[h % 2]
            mx = block(sref, 0).reshape(CH // 8, 8, tq).max(axis=0)
            for kb in range(1, nkb):
                mx = jnp.maximum(mx, block(sref, kb).reshape(CH // 8, 8, tq).max(axis=0))
            m_prev = m_s[h]
            m_new = jnp.maximum(m_prev, jnp.max(mx, axis=0, keepdims=True))
            alpha = jnp.exp2(m_prev - m_new)
            part = jnp.zeros((8, tq), F32)
            for kb in range(nkb):
                e = jnp.exp2(block(sref, kb) - m_new)
                part = part + jnp.sum(e.reshape(CH // 8, 8, tq), axis=0)
                pref[kb * CH:(kb + 1) * CH, :] = e.astype(BF16)
            l_s[h] = alpha * l_s[h] + jnp.sum(part, axis=0, keepdims=True)
            m_s[h] = m_new
            a_s[h % 2] = alpha

        def weighted_values(h):
            rows = slice(h * MLA_V, (h + 1) * MLA_V)
            acc_s[rows, :] = a_s[h % 2] * acc_s[rows, :] + lax.dot_general(
                v_ref[:, rows], p_scr[h % 2], TN, preferred_element_type=F32)

        scores(0)
        for h in range(MLA_H + 1):
            if h + 1 < MLA_H:
                scores(h + 1)
            if h >= 1:
                weighted_values(h - 1)
            if h < MLA_H:
                softmax(h)

    @pl.when(j < i)
    def _():
        tile(False)

    @pl.when(j == i)
    def _():
        tile(True)
        for h in range(MLA_H):
            rows = slice(h * MLA_V, (h + 1) * MLA_V)
            o_ref[:, rows] = (acc_s[rows, :] / l_s[h]).T.astype(BF16)


def _pad_mask(n):
    return jnp.asarray(np.where(np.arange(CH)[:, None] + 0 * np.arange(n)[None, :] < NSKIP, -np.inf, 0.0),
                       F32)


def _flash(q, kcat, v, seq):
    r = q.shape[0]
    tq = _pick(seq, (FLASH_TQ, 256, 128))
    nq = seq // tq
    c0 = seq // CH
    it = np.concatenate([np.full((i + 1,), i, np.int32) for i in range(nq)])
    jt = np.concatenate([np.arange(i + 1, dtype=np.int32) for i in range(nq)])
    kk = np.arange(tq)[:, None]
    qq = np.arange(tq)[None, :]
    diag = jnp.asarray(np.where(kk // CH > qq // CH, -np.inf, 0.0), F32)
    grid_spec = pltpu.PrefetchScalarGridSpec(
        num_scalar_prefetch=2,
        grid=(int(it.shape[0]),),
        in_specs=[pl.BlockSpec((tq, MLA_H * HK), lambda s, it, jt: (it[s], 0)),
                  pl.BlockSpec((tq, MLA_H * HK), lambda s, it, jt: (jt[s], 0)),
                  pl.BlockSpec((tq, MLA_H * MLA_V), lambda s, it, jt: (jt[s], 0)),
                  pl.BlockSpec((CH, MLA_H * HK), lambda s, it, jt: (c0, 0)),
                  pl.BlockSpec((CH, MLA_H * MLA_V), lambda s, it, jt: (c0, 0)),
                  pl.BlockSpec((tq, tq), lambda s, it, jt: (0, 0)),
                  pl.BlockSpec((CH, tq), lambda s, it, jt: (0, 0))],
        out_specs=pl.BlockSpec((tq, MLA_H * MLA_V), lambda s, it, jt: (it[s], 0)),
        scratch_shapes=[pltpu.VMEM((MLA_H * HK, tq), BF16),
                        pltpu.VMEM((MLA_H, 1, tq), F32), pltpu.VMEM((MLA_H, 1, tq), F32),
                        pltpu.VMEM((MLA_H * MLA_V, tq), F32),
                        pltpu.VMEM((2, tq, tq), F32), pltpu.VMEM((2, tq, tq), BF16),
                        pltpu.VMEM((2, 1, tq), F32)],
    )
    return pl.pallas_call(
        _flash_kernel,
        grid_spec=grid_spec,
        out_shape=jax.ShapeDtypeStruct((r, MLA_H * MLA_V), BF16),
        compiler_params=_cp(1), name="flash",
    )(jnp.asarray(it), jnp.asarray(jt), q, kcat, v, kcat, v, diag, _pad_mask(tq))


def _attn0_kernel(q_ref, k_ref, v_ref, keep_ref, o_ref):
    del keep_ref
    col = lax.broadcasted_iota(jnp.int32, (CH, CH), 1)
    for h in range(MLA_H):
        s = lax.dot_general(q_ref[:, h * HK:(h + 1) * HK], k_ref[:, h * HK:(h + 1) * HK], NT,
                            preferred_element_type=F32)
        s = jnp.where(col >= NSKIP, s, -jnp.inf)
        p = jnp.exp2(s - jnp.max(s, axis=-1, keepdims=True))
        o = jnp.dot(p.astype(BF16), v_ref[:, h * MLA_V:(h + 1) * MLA_V], preferred_element_type=F32)
        o_ref[:, h * MLA_V:(h + 1) * MLA_V] = (o / jnp.sum(p, axis=-1, keepdims=True)).astype(BF16)


def _attn0(q, kcat, v, c0, o_frames):
    def blk(w):
        return pl.BlockSpec((CH, w), lambda i: (c0, 0))

    return pl.pallas_call(
        _attn0_kernel,
        grid=(1,),
        in_specs=[blk(MLA_H * HK), blk(MLA_H * HK), blk(MLA_H * MLA_V), pl.BlockSpec(memory_space=pl.ANY)],
        out_specs=blk(MLA_H * MLA_V),
        out_shape=jax.ShapeDtypeStruct(o_frames.shape, BF16),
        input_output_aliases={3: 0},
        compiler_params=_cp(1), name="attn0",
    )(q, kcat, v, o_frames)


def _cached_attn_kernel(q_ref, cnew_ref, pnew_ref, ckv_ref, cpe_ref, wkt_ref, wv_ref, o_ref, call, peall):
    past = ckv_ref.shape[2]
    call[0:past, :] = ckv_ref[0, 0].astype(BF16)
    call[past:past + CH, :] = cnew_ref[...].astype(BF16)
    peall[...] = jnp.zeros(peall.shape, BF16)
    peall[0:past, 0:ROPE] = cpe_ref[0, 0].astype(BF16)
    peall[past:past + CH, :] = pnew_ref[...].astype(BF16)
    qlat = []
    qpe = []
    for h in range(MLA_H):
        qn = q_ref[:, h * HK:h * HK + NOPE]
        qlat.append(jnp.dot(qn, wkt_ref[h], preferred_element_type=F32).astype(BF16))
        qpe.append(q_ref[:, h * HK + NOPE:(h + 1) * HK])
    qlat = jnp.concatenate(qlat, axis=0)
    qpe = jnp.concatenate(qpe, axis=0)
    s = (lax.dot_general(qlat, call[...], NT, preferred_element_type=F32)
         + lax.dot_general(qpe, peall[...], NT, preferred_element_type=F32))
    p = jnp.exp2(s - jnp.max(s, axis=-1, keepdims=True))
    p = p / jnp.sum(p, axis=-1, keepdims=True)
    olat = jnp.dot(p.astype(BF16), call[...], preferred_element_type=F32).astype(BF16)
    for h in range(MLA_H):
        o_ref[:, h * MLA_V:(h + 1) * MLA_V] = jnp.dot(
            olat[h * CH:(h + 1) * CH, :], wv_ref[h], preferred_element_type=F32).astype(BF16)


def _cached_attn_alias_kernel(q_ref, cnew_ref, pnew_ref, ckv_ref, cpe_ref, wkt_ref, wv_ref, keep_ref, o_ref,
                              call, peall):
    del keep_ref
    _cached_attn_kernel(q_ref, cnew_ref, pnew_ref, ckv_ref, cpe_ref, wkt_ref, wv_ref, o_ref, call, peall)


def _cached_attn(q, ckv, kpe, cache_kv, cache_pe, layer, rb0, lw, o_prompt):
    nb, past = cache_kv.shape[1], cache_kv.shape[2]

    def rowmap(b):
        return (rb0 + b, 0)

    return pl.pallas_call(
        _cached_attn_alias_kernel,
        grid=(nb,),
        in_specs=[pl.BlockSpec((CH, MLA_H * HK), rowmap),
                  pl.BlockSpec((CH, KVL), rowmap),
                  pl.BlockSpec((CH, LANE), rowmap),
                  pl.BlockSpec((1, 1, past, KVL), lambda b: (layer, b, 0, 0)),
                  pl.BlockSpec((1, 1, past, ROPE), lambda b: (layer, b, 0, 0)),
                  pl.BlockSpec((MLA_H, NOPE, KVL), lambda b: (0, 0, 0)),
                  pl.BlockSpec((MLA_H, KVL, MLA_V), lambda b: (0, 0, 0)),
                  pl.BlockSpec(memory_space=pl.ANY)],
        out_specs=pl.BlockSpec((CH, MLA_H * MLA_V), rowmap),
        out_shape=jax.ShapeDtypeStruct(o_prompt.shape, BF16),
        scratch_shapes=[pltpu.VMEM((past + CH, KVL), BF16), pltpu.VMEM((past + CH, LANE), BF16)],
        input_output_aliases={7: 0},
        compiler_params=_cp(1), name="cached_attn",
    )(q, ckv, kpe, cache_kv, cache_pe, lw["wkt"], lw["wv3"], o_prompt)


TAIL = 16


def _shift_matrix(k, rows=CH):
    ns = k - 1
    pm = np.zeros((rows * ns + TAIL, TAIL + rows), np.float32)
    for d in range(1, ns + 1):
        for t in range(rows):
            if t - d >= 0:
                pm[(d - 1) * rows + t, TAIL + t - d] = 1.0
            else:
                r = ns + t - d
                pm[(d - 1) * rows + t, 3 * r:3 * r + 3] = 1.0
    for r in range(ns):
        pm[rows * ns + 3 * r, TAIL + rows - ns + r] = 1.0
    return pm


def _load_tail(tail_s, prev):
    hi = prev.astype(BF16).astype(F32)
    rest = prev - hi
    mid = rest.astype(BF16).astype(F32)
    lo = (rest - mid).astype(BF16).astype(F32)
    tail_s[...] = jnp.zeros(tail_s.shape, F32)
    for r in range(prev.shape[0]):
        tail_s[3 * r:3 * r + 1, :] = hi[r:r + 1]
        tail_s[3 * r + 1:3 * r + 2, :] = mid[r:r + 1]
        tail_s[3 * r + 2:3 * r + 3, :] = lo[r:r + 1]


def _delayed(pm_ref, tail_s, x, ns):
    t = x.shape[0]
    ext = jnp.concatenate([tail_s[...].astype(BF16), x], axis=0)
    out = jnp.dot(pm_ref[...], ext, preferred_element_type=F32)
    tail_s[...] = out[t * ns:, :]
    return [out[(d - 1) * t:d * t, :] for d in range(1, ns + 1)]


def _act_kernel(nskip, g_ref, u_ref, prev_ref, w_ref, pm_ref, a_ref, new_ref, tail_s):
    t = g_ref.shape[0]
    ns = FFN_K - 1
    c = pl.program_id(1)

    @pl.when(c == 0)
    def _():
        _load_tail(tail_s, prev_ref[0])

    rows = lax.broadcasted_iota(jnp.int32, (t, 1), 0) + c * t
    g = g_ref[...]
    g = jnp.where(rows >= nskip, g, jnp.zeros((), g.dtype))
    delayed = _delayed(pm_ref, tail_s, g, ns)
    conv = w_ref[ns:ns + 1, :] * g.astype(F32)
    for d in range(1, ns + 1):
        conv = conv + w_ref[ns - d:ns - d + 1, :] * delayed[d - 1]
    a_ref[...] = (_silu(conv) * u_ref[...].astype(F32)).astype(BF16)

    @pl.when(c == pl.num_programs(1) - 1)
    def _():
        for r in range(ns):
            new_ref[0, r:r + 1, :] = tail_s[3 * r:3 * r + 1, :]


def _act(gate, up, rb0, nb, nch, nskip, prev, w, act_in=None, t=CH):
    r, n = gate.shape

    def rowmap(b, c):
        return (rb0 + b * nch + c, 0)

    pm = jnp.asarray(_shift_matrix(FFN_K, t), BF16)
    args = [gate, up, prev, w, pm]
    in_specs = [pl.BlockSpec((t, n), rowmap), pl.BlockSpec((t, n), rowmap),
                pl.BlockSpec((1, FFN_K - 1, n), lambda b, c: (b, 0, 0)),
                pl.BlockSpec((FFN_K, n), lambda b, c: (0, 0)),
                pl.BlockSpec(pm.shape, lambda b, c: (0, 0))]
    kern = functools.partial(_act_kernel, nskip)
    aliases = {}
    if act_in is not None:
        aliases = {len(args): 0}
        args.append(act_in)
        in_specs.append(pl.BlockSpec(memory_space=pl.ANY))
        kern = functools.partial(_act_alias_kernel, nskip)
    return pl.pallas_call(
        kern,
        grid=(nb, nch),
        in_specs=in_specs,
        out_specs=[pl.BlockSpec((t, n), rowmap), pl.BlockSpec((1, FFN_K - 1, n), lambda b, c: (b, 0, 0))],
        out_shape=[jax.ShapeDtypeStruct((r, n), BF16), jax.ShapeDtypeStruct((nb, FFN_K - 1, n), F32)],
        scratch_shapes=[pltpu.VMEM((TAIL, n), F32)],
        input_output_aliases=aliases,
        compiler_params=_cp(2), name="act",
    )(*args)


def _act_alias_kernel(nskip, g_ref, u_ref, prev_ref, w_ref, pm_ref, keep_ref, a_ref, new_ref, tail_s):
    del keep_ref
    _act_kernel(nskip, g_ref, u_ref, prev_ref, w_ref, pm_ref, a_ref, new_ref, tail_s)


def _cast_cols_kernel(x_ref, o_ref):
    n = x_ref.shape[1]
    o_ref[:, :n] = x_ref[...].astype(BF16)
    if o_ref.shape[1] > n:
        o_ref[:, n:] = jnp.zeros((o_ref.shape[0], o_ref.shape[1] - n), BF16)


def _cast_pad_cols(w, layer, n_out):
    _, k, n = w.shape
    tr = _pick(k, (128, 64))
    return pl.pallas_call(
        _cast_cols_kernel,
        grid=(k // tr,),
        in_specs=[pl.BlockSpec((None, tr, n), lambda i: (layer, i, 0))],
        out_specs=pl.BlockSpec((tr, n_out), lambda i: (i, 0)),
        out_shape=jax.ShapeDtypeStruct((k, n_out), BF16),
        compiler_params=_cp(1), name="cast_cols",
    )(w)


def _cast_rows_kernel(n_full, x_ref, o_ref):
    @pl.when(pl.program_id(0) < n_full)
    def _():
        o_ref[...] = x_ref[...].astype(BF16)

    @pl.when(pl.program_id(0) >= n_full)
    def _():
        o_ref[...] = jnp.zeros(o_ref.shape, BF16)


def _cast_pad_rows(w, layer, k_out):
    _, k, n = w.shape
    tr = 256
    assert k % tr == 0 and k_out % tr == 0
    n_full = k // tr
    return pl.pallas_call(
        functools.partial(_cast_rows_kernel, n_full),
        grid=(k_out // tr,),
        in_specs=[pl.BlockSpec((None, tr, n), lambda i: (layer, jnp.minimum(i, n_full - 1), 0))],
        out_specs=pl.BlockSpec((tr, n), lambda i: (i, 0)),
        out_shape=jax.ShapeDtypeStruct((k_out, n), BF16),
        compiler_params=_cp(1), name="cast_rows",
    )(w)


PACK = 256


def _pack_win_kernel(a_ref, b_ref, o_ref):
    j = pl.program_id(0)
    o_dt = SSD_DI + SSD_CD
    n_plain = o_dt // PACK
    last = pl.num_programs(0) - 1
    half = ROPE // 2

    def emit(rows):
        o_ref[...] = rows.T.astype(BF16)

    @pl.when(j < n_plain)
    def _():
        emit(a_ref[...])

    @pl.when((j >= n_plain) & (j < last))
    def _():
        emit(jnp.concatenate([a_ref[SSD_H:, :], b_ref[:SSD_H, :]], axis=0))

    @pl.when(j == last)
    def _():
        kr0 = a_ref[SSD_H:SSD_H + half, :]
        kr1 = a_ref[SSD_H + half:SSD_H + ROPE, :]
        emit(jnp.concatenate([kr0, kr1, kr1, kr0, b_ref[:SSD_H, :],
                              jnp.zeros((PACK - 2 * ROPE - SSD_H, a_ref.shape[1]), F32)], axis=0))


def _pack_win(w_in, layer):
    _, k, n = w_in.shape
    assert (SSD_DI + SSD_CD) % PACK == 0 and NIN % PACK == 0 and n - (NIN - PACK) == SSD_H + ROPE
    wt = jnp.swapaxes(w_in, 1, 2)
    nblk = NIN // PACK
    n_plain = (SSD_DI + SSD_CD) // PACK

    def b_index(j):
        return jnp.where((j < n_plain) | (j == nblk - 1), n_plain, j + 1)

    return pl.pallas_call(
        _pack_win_kernel,
        grid=(nblk,),
        in_specs=[pl.BlockSpec((None, PACK, k), lambda j: (layer, j, 0)),
                  pl.BlockSpec((None, PACK, k), lambda j: (layer, b_index(j), 0))],
        out_specs=pl.BlockSpec((k, PACK), lambda j: (0, j)),
        out_shape=jax.ShapeDtypeStruct((k, NIN), BF16),
        compiler_params=_cp(1), name="pack_win",
    )(wt, wt)


def _swap_half(w):
    half = w.shape[-1] // 2
    return jnp.concatenate([w[..., half:], w[..., :half]], axis=-1)


def _layer_weights(i, w_in, ssd_conv_w, ssd_conv_b, ssd_dt_bias, ssd_a_log, ssd_d, ssd_norm, sc_conv_w,
                   mla_q_norm, mla_w_uq, mla_kv_norm, mla_w_ukv, w_out, ffn_w_gate, ffn_w_up, ffn_conv_w,
                   ffn_w_down):
    win = _pack_win(w_in, i)
    uq = mla_w_uq[i].reshape(QL, MLA_H, NOPE + ROPE)
    pe = uq[..., NOPE:]
    zq = jnp.zeros((QL, MLA_H, HK - NOPE - ROPE), F32)
    wa = jnp.concatenate([uq[..., :NOPE], pe, zq], axis=-1).reshape(QL, MLA_H * HK).astype(BF16)
    wb = jnp.concatenate([_swap_half(pe), zq], axis=-1).reshape(QL, MLA_H * LANE).astype(BF16)
    ukv = mla_w_ukv[i].reshape(KVL, MLA_H, NOPE + MLA_V)
    padf = ((0, 0), (0, DFP - D_FF))
    return dict(
        win=win,
        cw=ssd_conv_w[i], cb=ssd_conv_b[i].reshape(1, SSD_CD),
        dtb=jnp.pad(ssd_dt_bias[i], (0, LANE - SSD_H)).reshape(1, LANE),
        alog=jnp.pad(ssd_a_log[i], (0, LANE - SSD_H)).reshape(1, LANE),
        dsk=jnp.repeat(ssd_d[i], SSD_P).reshape(1, SSD_DI),
        ng=ssd_norm[i].reshape(1, SSD_DI),
        scw=sc_conv_w[i],
        qn=mla_q_norm[i].reshape(1, QL), kvn=mla_kv_norm[i].reshape(1, KVL),
        wa=wa, wb=wb,
        wk=ukv[..., :NOPE].reshape(KVL, MLA_H * NOPE).astype(BF16),
        wv=ukv[..., NOPE:].reshape(KVL, MLA_H * MLA_V).astype(BF16),
        wkt=jnp.transpose(ukv[..., :NOPE], (1, 2, 0)).astype(BF16),
        wv3=jnp.transpose(ukv[..., NOPE:], (1, 0, 2)).astype(BF16),
        wout=_cast_pad_cols(w_out, i, D_MODEL),
        wg=_cast_pad_cols(ffn_w_gate, i, DFP),
        wu=_cast_pad_cols(ffn_w_up, i, DFP),
        fcw=jnp.pad(ffn_conv_w[i], padf),
        wd=_cast_pad_rows(ffn_w_down, i, DFP),
    )


def _rope_table(seq, ns, past):
    half = ROPE // 2
    pos = jnp.concatenate([N_META + jnp.arange(seq, dtype=jnp.int32),
                           jnp.maximum(jnp.arange(CH, dtype=jnp.int32) - NSKIP, 0),
                           N_META + past + jnp.arange(ns, dtype=jnp.int32) % CH])
    inv = THETA ** (-jnp.arange(half, dtype=F32) / half)
    ang = pos.astype(F32)[:, None] * inv[None, :]
    cos, sin = jnp.cos(ang), jnp.sin(ang)
    return jnp.concatenate([cos, cos, -sin, sin], axis=1)


def kernel(x_prompt, x_sample, cache_kv_latent, cache_k_rope, state_ssm, state_ssd_conv, state_sconv, state_ffn_conv, meta_tokens, norm_mix_pre, norm_mix_post, norm_ffn_pre, norm_ffn_post, w_in, ssd_conv_w, ssd_conv_b, ssd_dt_bias, ssd_a_log, ssd_d, ssd_norm, sc_conv_w, mla_q_norm, mla_w_uq, mla_kv_norm, mla_w_ukv, w_out, ffn_w_gate, ffn_w_up, ffn_conv_w, ffn_w_down):
    bp, seq, d = x_prompt.shape
    nb, ls, _ = x_sample.shape
    depth, _, past, _ = cache_kv_latent.shape
    assert bp == 1 and ls == CH and seq % CH == 0 and d == D_MODEL
    lp = CH + seq
    ns = nb * ls
    npc = lp // CH
    c0 = seq // CH

    x, h = _assemble(x_prompt[0], x_sample.reshape(ns, d), meta_tokens.astype(F32), norm_mix_pre[0])
    tab = _rope_table(seq, ns, past)
    zero_c = jnp.zeros((1, SSD_K - 1, SSD_CD), F32)
    zero_h = jnp.zeros((1, SSD_N, SSD_DI), F32)

    def state_in(s):
        return jnp.transpose(s, (0, 3, 1, 2)).reshape(s.shape[0], SSD_N, SSD_DI)

    def state_out(s):
        return jnp.transpose(s.reshape(s.shape[0], SSD_N, SSD_H, SSD_P), (0, 2, 3, 1))
    zero_s = jnp.zeros((1, SC_K - 1, SC_D), F32)
    zero_f = jnp.zeros((1, FFN_K - 1, DFP), F32)
    padf = ((0, 0), (0, 0), (0, DFP - D_FF))

    outs_p, outs_s = [], []
    for i in range(depth):
        lw = _layer_weights(i, w_in, ssd_conv_w, ssd_conv_b, ssd_dt_bias, ssd_a_log, ssd_d, ssd_norm,
                            sc_conv_w, mla_q_norm, mla_w_uq, mla_kv_norm, mla_w_ukv, w_out, ffn_w_gate,
                            ffn_w_up, ffn_conv_w, ffn_w_down)
        u, dtf = _inproj(h, lw["win"])

        yp, yscp, cnew_p, hnew_p, scnew_p = _mixer(u, dtf, 0, 1, npc, NSKIP, zero_c, zero_h, zero_s, lw)
        y_ssd, y_sc, cnew_s, hnew_s, scnew_s = _mixer(u, dtf, npc, nb, 1, 0, state_ssd_conv[i],
                                                      state_in(state_ssm[i]), state_sconv[i], lw,
                                                      keep=(yp, yscp))
        hnew_p, hnew_s = state_out(hnew_p), state_out(hnew_s)

        q, ckv, kpe, kcat, v = _mla_proj(u, tab, lw)
        y_mla = _attn0(q, kcat, v, c0, _flash(q, kcat, v, seq))
        y_mla = _cached_attn(q, ckv, kpe, cache_kv_latent, cache_k_rope, i, npc, lw, y_mla)

        mix = _outproj(y_ssd, y_sc, y_mla, lw["wout"])
        x, h = _resid_norm(x, mix, norm_mix_post[i], norm_ffn_pre[i])

        gate, up = _gateup(h, lw["wg"], lw["wu"])
        act, f0 = _act(gate, up, c0, 1, 1, NSKIP, zero_f, lw["fcw"])
        act, fnew_p = _act(gate, up, 0, 1, seq // ACT_ROWS, 0, f0, lw["fcw"], act_in=act, t=ACT_ROWS)
        act, fnew_s = _act(gate, up, npc, nb, 1, 0, jnp.pad(state_ffn_conv[i], padf), lw["fcw"], act_in=act)
        f = _down(act, lw["wd"])
        if i + 1 < depth:
            x, h = _resid_norm(x, f, norm_ffn_post[i], norm_mix_pre[i + 1])
        else:
            y_prompt = _resid_out(x, f, norm_ffn_post[i], 0, seq)
            y_sample = _resid_out(x, f, norm_ffn_post[i], lp, ns)

        meta_rows = slice(seq + NSKIP, lp)
        outs_p.append((jnp.concatenate([ckv[meta_rows], ckv[:seq]], axis=0)[None],
                       jnp.concatenate([kpe[meta_rows, :ROPE], kpe[:seq, :ROPE]], axis=0)[None],
                       hnew_p, cnew_p, scnew_p, fnew_p[:, :, :D_FF]))
        outs_s.append((ckv[lp:].reshape(nb, ls, KVL), kpe[lp:, :ROPE].reshape(nb, ls, ROPE), hnew_s, cnew_s,
                       scnew_s, fnew_s[:, :, :D_FF]))

    def stack(outs, j):
        return jnp.stack([o[j] for o in outs], axis=0)

    return (y_prompt[None], y_sample.reshape(nb, ls, d),
            stack(outs_p, 0), stack(outs_p, 1), stack(outs_p, 2), stack(outs_p, 3), stack(outs_p, 4), stack(outs_p, 5),
            stack(outs_s, 0), stack(outs_s, 1), stack(outs_s, 2), stack(outs_s, 3), stack(outs_s, 4), stack(outs_s, 5))
```

```python
import functools

import jax
import jax.numpy as jnp
import numpy as np
from jax import lax
from jax.experimental import pallas as pl
from jax.experimental.pallas import tpu as pltpu

F32 = jnp.float32
BF16 = jnp.bfloat16

D_MODEL = 4096
N_META = 16
CH = 64
NSKIP = CH - N_META
EPS = 1e-6
SSD_P = 64
SSD_DI = 2048
SSD_H = 32
SSD_G = 8
SSD_N = 128
SSD_K = 4
SSD_CD = 4096
SC_D = 1024
SC_K = 3
MLA_H = 8
NOPE = 128
ROPE = 64
MLA_V = 128
QL = 768
KVL = 512
HK = 256
SCALE = (NOPE + ROPE) ** -0.5
QSCALE = SCALE * 1.4426950408889634
THETA = 10000.0
D_FF = 11008
DFP = 11264
FFN_K = 3
LANE = 128

C_Z, C_X, C_BC, C_SCB, C_SCC, C_SCH, C_MLA, NIN = 0, 2048, 4096, 6144, 7168, 8192, 9216, 10752
MLA_W = 1536

FLASH_TQ = 512
ACT_ROWS = 128
VMEM_LIMIT = 56 * 1024 * 1024
HI = lax.Precision.HIGHEST
NT = (((1,), (1,)), ((), ()))
TN = (((0,), (0,)), ((), ()))


def _cp(n, flags=None):
    return pltpu.CompilerParams(dimension_semantics=("arbitrary",) * n, vmem_limit_bytes=VMEM_LIMIT, flags=flags)


def _pick(n, cands):
    for c in cands:
        if n % c == 0:
            return c
    raise ValueError(f"no tile for {n}")


MM_ROWS = (1088, 544, 512, 272, 256, 136, 128, 64)
EW_ROWS = (272, 256, 136, 128, 64)


def _stored_chunk(c, nch):
    return (c + nch - 1) % nch


def _rms(x, g):
    return x * lax.rsqrt(jnp.mean(x * x, axis=-1, keepdims=True) + EPS) * g


def _silu(x):
    return x * jax.nn.sigmoid(x)


def _norm_rows_kernel(x_in_ref, g_ref, x_ref, h_ref):
    x = x_in_ref[...]
    x_ref[...] = x
    h_ref[...] = _rms(x, g_ref[...]).astype(BF16)


def _tail_rows_kernel(xs_ref, meta_ref, g_ref, keep_x, keep_h, x_ref, h_ref):
    del keep_x, keep_h
    c = pl.program_id(0)

    def emit(x):
        x_ref[...] = x
        h_ref[...] = _rms(x, g_ref[...]).astype(BF16)

    @pl.when(c == 0)
    def _():
        emit(jnp.concatenate([jnp.zeros((NSKIP, x_ref.shape[1]), F32), meta_ref[...]], axis=0))

    @pl.when(c > 0)
    def _():
        emit(xs_ref[...])


def _assemble(x_prompt, x_sample, meta, g):
    seq, d = x_prompt.shape
    ns = x_sample.shape[0]
    r = seq + CH + ns
    bm = _pick(seq, (512, 256, 128, 64))
    vec = pl.BlockSpec((1, d), lambda c: (0, 0))
    out_shape = [jax.ShapeDtypeStruct((r, d), F32), jax.ShapeDtypeStruct((r, d), BF16)]
    big = pl.BlockSpec((bm, d), lambda c: (c, 0))
    x, h = pl.pallas_call(
        _norm_rows_kernel, grid=(seq // bm,), in_specs=[big, vec], out_specs=[big, big],
        out_shape=out_shape, compiler_params=_cp(1), name="assemble_frames",
    )(x_prompt, g.reshape(1, d))
    c0 = seq // CH
    small = pl.BlockSpec((CH, d), lambda c: (c0 + c, 0))
    anyspec = pl.BlockSpec(memory_space=pl.ANY)
    return pl.pallas_call(
        _tail_rows_kernel, grid=(1 + ns // CH,),
        in_specs=[pl.BlockSpec((CH, d), lambda c: (jnp.maximum(c - 1, 0), 0)),
                  pl.BlockSpec((N_META, d), lambda c: (0, 0)), vec, anyspec, anyspec],
        out_specs=[small, small], out_shape=out_shape, input_output_aliases={3: 0, 4: 1},
        compiler_params=_cp(1), name="assemble_tail",
    )(x_sample, meta, g.reshape(1, d), x, h)


def _resid_out_kernel(x_ref, m_ref, gp_ref, y_ref):
    y_ref[...] = x_ref[...] + _rms(m_ref[...], gp_ref[...])


def _resid_out(x, m, g_post, row0, rows):
    d = x.shape[1]
    bm = next(b for b in (512, 256, 128, 64) if rows % b == 0 and row0 % b == 0)
    src = pl.BlockSpec((bm, d), lambda c: (row0 // bm + c, 0))
    return pl.pallas_call(
        _resid_out_kernel, grid=(rows // bm,),
        in_specs=[src, src, pl.BlockSpec((1, d), lambda c: (0, 0))],
        out_specs=pl.BlockSpec((bm, d), lambda c: (c, 0)),
        out_shape=jax.ShapeDtypeStruct((rows, d), F32),
        compiler_params=_cp(1), name="resid_out",
    )(x, m, g_post.reshape(1, d))


def _resid_norm_kernel(x_ref, m_ref, gp_ref, gn_ref, x2_ref, h_ref):
    x2 = x_ref[...] + _rms(m_ref[...], gp_ref[...])
    x2_ref[...] = x2
    h_ref[...] = _rms(x2, gn_ref[...]).astype(BF16)


def _resid_norm(x, m, g_post, g_next):
    r, d = x.shape
    bm = _pick(r, EW_ROWS)
    row = pl.BlockSpec((bm, d), lambda i: (i, 0))
    vec = pl.BlockSpec((1, d), lambda i: (0, 0))
    return pl.pallas_call(
        _resid_norm_kernel, grid=(r // bm,), in_specs=[row, row, vec, vec], out_specs=[row, row],
        out_shape=[jax.ShapeDtypeStruct((r, d), F32), jax.ShapeDtypeStruct((r, d), BF16)],
        compiler_params=_cp(1), name="resid_norm",
    )(x, m, g_post.reshape(1, d), g_next.reshape(1, d))


def _inproj_kernel(x_ref, w_ref, u_ref, dt_ref):
    acc = jnp.dot(x_ref[...], w_ref[...], preferred_element_type=F32)
    u_ref[...] = acc.astype(BF16)

    @pl.when(pl.program_id(1) == pl.num_programs(1) - 1)
    def _():
        dt_ref[...] = acc[:, acc.shape[1] - LANE:]


def _inproj(h, w):
    r, k = h.shape
    n = w.shape[1]
    bm = _pick(r, MM_ROWS)
    bn = 768
    return pl.pallas_call(
        _inproj_kernel,
        grid=(r // bm, n // bn),
        in_specs=[pl.BlockSpec((bm, k), lambda i, j: (i, 0)), pl.BlockSpec((k, bn), lambda i, j: (0, j))],
        out_specs=[pl.BlockSpec((bm, bn), lambda i, j: (i, j)), pl.BlockSpec((bm, LANE), lambda i, j: (i, 0))],
        out_shape=[jax.ShapeDtypeStruct((r, n), BF16), jax.ShapeDtypeStruct((r, LANE), F32)],
        compiler_params=_cp(2), name="inproj",
    )(h, w)


def _outproj_kernel(a_ref, b_ref, c_ref, wa_ref, wb_ref, wc_ref, o_ref):
    acc = jnp.dot(a_ref[...], wa_ref[...], preferred_element_type=F32)
    acc = acc + jnp.dot(b_ref[...], wb_ref[...], preferred_element_type=F32)
    acc = acc + jnp.dot(c_ref[...], wc_ref[...], preferred_element_type=F32)
    o_ref[...] = acc


def _outproj(y_ssd, y_sc, y_mla, w):
    r = y_ssd.shape[0]
    n = w.shape[1]
    bm = _pick(r, MM_ROWS)
    bn = 1024
    return pl.pallas_call(
        _outproj_kernel,
        grid=(r // bm, n // bn),
        in_specs=[pl.BlockSpec((bm, SSD_DI), lambda i, j: (i, 0)),
                  pl.BlockSpec((bm, SC_D), lambda i, j: (i, 0)),
                  pl.BlockSpec((bm, SC_D), lambda i, j: (i, 0)),
                  pl.BlockSpec((SSD_DI, bn), lambda i, j: (0, j)),
                  pl.BlockSpec((SC_D, bn), lambda i, j: (2, j)),
                  pl.BlockSpec((SC_D, bn), lambda i, j: (3, j))],
        out_specs=pl.BlockSpec((bm, bn), lambda i, j: (i, j)),
        out_shape=jax.ShapeDtypeStruct((r, n), F32),
        compiler_params=_cp(2), name="outproj",
    )(y_ssd, y_sc, y_mla, w, w, w)


def _gateup_kernel(x_ref, wg_ref, wu_ref, g_ref, u_ref):
    x = x_ref[...]
    g_ref[...] = jnp.dot(x, wg_ref[...], preferred_element_type=F32).astype(BF16)
    u_ref[...] = jnp.dot(x, wu_ref[...], preferred_element_type=F32).astype(BF16)


def _gateup(h, wg, wu):
    r, k = h.shape
    n = wg.shape[1]
    bm = _pick(r, MM_ROWS)
    bn = 512
    wspec = pl.BlockSpec((k, bn), lambda i, j: (0, j))
    ospec = pl.BlockSpec((bm, bn), lambda i, j: (i, j))
    return pl.pallas_call(
        _gateup_kernel,
        grid=(r // bm, n // bn),
        in_specs=[pl.BlockSpec((bm, k), lambda i, j: (i, 0)), wspec, wspec],
        out_specs=[ospec, ospec],
        out_shape=[jax.ShapeDtypeStruct((r, n), BF16)] * 2,
        compiler_params=_cp(2), name="gateup",
    )(h, wg, wu)


def _down_kernel(x_ref, w_ref, o_ref):
    o_ref[...] = jnp.dot(x_ref[...], w_ref[...], preferred_element_type=F32)


def _down(a, w):
    r, k = a.shape
    n = w.shape[1]
    bm = _pick(r, MM_ROWS[1:])
    bn = 512
    return pl.pallas_call(
        _down_kernel,
        grid=(r // bm, n // bn),
        in_specs=[pl.BlockSpec((bm, k), lambda i, j: (i, 0)), pl.BlockSpec((k, bn), lambda i, j: (0, j))],
        out_specs=pl.BlockSpec((bm, bn), lambda i, j: (i, j)),
        out_shape=jax.ShapeDtypeStruct((r, n), F32),
        compiler_params=_cp(2), name="down",
    )(a, w)


def _mixer_kernel(nskip, z_ref, x_ref, bc_ref, dt_ref, scb_ref, scc_ref, sch_ref,
                  cprev_ref, h0_ref, scprev_ref,
                  cw_ref, cb_ref, dtb_ref, alog_ref, dsk_ref, ng_ref, scw_ref, pm_ref,
                  y_ref, ysc_ref, cnew_ref, hnew_ref, scnew_ref,
                  buf, sbuf, hst):
    t = CH
    c = pl.program_id(1)
    last = pl.num_programs(1) - 1

    ns = SSD_K - 1

    @pl.when(c == 0)
    def _init():
        _load_tail(buf, cprev_ref[0])
        sbuf[0:8, :] = jnp.zeros((8, SC_D), F32)
        sbuf[8 - (SC_K - 1):8, :] = scprev_ref[0]
        hst[...] = h0_ref[0]

    rows = lax.broadcasted_iota(jnp.int32, (t, 1), 0) + c * t
    valid = rows >= nskip

    xin = jnp.concatenate([x_ref[...], bc_ref[...]], axis=1)
    xin = jnp.where(valid, xin, jnp.zeros((), xin.dtype))
    delayed = _delayed(pm_ref, buf, xin, ns)
    acc = cb_ref[...] + cw_ref[ns:ns + 1, :] * xin.astype(F32)
    for d in range(1, ns + 1):
        acc = acc + cw_ref[ns - d:ns - d + 1, :] * delayed[d - 1]
    xbc = _silu(acc)

    lane = lax.broadcasted_iota(jnp.int32, (t, LANE), 1)
    v = dt_ref[...] + dtb_ref[...]
    dt = jnp.maximum(v, 0.0) + jnp.log1p(jnp.exp(-jnp.abs(v)))
    dt = jnp.where(valid & (lane < SSD_H), dt, 0.0)
    adt = dt * (-jnp.exp(alog_ref[...]))
    ri = lax.broadcasted_iota(jnp.int32, (t, t), 0)
    ci = lax.broadcasted_iota(jnp.int32, (t, t), 1)
    tri = ri >= ci
    acs = jnp.dot(tri.astype(F32), adt, precision=HI, preferred_element_type=F32)

    hpg = SSD_H // SSD_G
    gw = hpg * SSD_P
    lane_g = lax.broadcasted_iota(jnp.int32, (t, gw), 1)
    row_g = lax.broadcasted_iota(jnp.int32, (t, gw), 0)
    seg = lane_g // SSD_P
    eye_g = row_g == lane_g % SSD_P
    tri_g = row_g >= lane_g % SSD_P
    blockdiag = (lax.broadcasted_iota(jnp.int32, (hpg * t, gw), 0) // t
                 == lax.broadcasted_iota(jnp.int32, (hpg * t, gw), 1) // SSD_P)

    def per_head_lanes(mat, g):
        out = jnp.broadcast_to(mat[:, g * hpg + hpg - 1:g * hpg + hpg], (t, gw))
        for r in range(hpg - 2, -1, -1):
            out = jnp.where(seg == r, jnp.broadcast_to(mat[:, g * hpg + r:g * hpg + r + 1], (t, gw)), out)
        return out

    for g in range(SSD_G):
        cols = slice(g * gw, (g + 1) * gw)
        bg = xbc[:, SSD_DI + g * SSD_N:SSD_DI + (g + 1) * SSD_N].astype(BF16)
        cg = xbc[:, SSD_DI + SSD_G * SSD_N + g * SSD_N:SSD_DI + SSD_G * SSD_N + (g + 1) * SSD_N].astype(BF16)
        acol = per_head_lanes(acs, g)
        dtx = per_head_lanes(dt, g)
        arow = jnp.sum(jnp.where(eye_g, acol, 0.0), axis=0, keepdims=True)
        decay = jnp.exp(jnp.where(tri_g, acol - arow, -jnp.inf))
        cb = lax.dot_general(cg, jnp.concatenate([bg] * hpg, axis=0), NT, preferred_element_type=F32)
        xg = xbc[:, cols]
        xdt = xg * dtx
        xdtb = xdt.astype(BF16)
        rhs = jnp.where(blockdiag, jnp.concatenate([xdtb] * hpg, axis=0), jnp.zeros((), BF16))
        ydiag = jnp.dot((cb * decay).astype(BF16), rhs, preferred_element_type=F32)
        hg = hst[:, cols]
        yoff = jnp.dot(cg, hg.astype(BF16), preferred_element_type=F32) * jnp.exp(acol)
        alast = acol[t - 1:t, :]
        snew = lax.dot_general(bg, (xdt * jnp.exp(alast - acol)).astype(BF16), TN, preferred_element_type=F32)
        hst[:, cols] = jnp.exp(alast) * hg + snew
        y = ydiag + yoff + dsk_ref[:, cols] * xg
        y = y * _silu(z_ref[:, cols].astype(F32))
        y_ref[:, cols] = _rms(y, ng_ref[:, cols]).astype(BF16)

    p = scc_ref[...].astype(F32) * sch_ref[...].astype(F32)
    p = jnp.where(valid, p, 0.0)
    sbuf[8:8 + t, :] = p
    conv = scw_ref[0:1, :] * sbuf[6:6 + t, :]
    for i in range(1, SC_K):
        conv = conv + scw_ref[i:i + 1, :] * sbuf[6 + i:6 + i + t, :]
    ysc_ref[...] = (scb_ref[...].astype(F32) * conv).astype(BF16)
    sbuf[0:8, :] = sbuf[t:t + 8, :]

    @pl.when(c == last)
    def _():
        for r in range(ns):
            cnew_ref[0, r:r + 1, :] = buf[3 * r:3 * r + 1, :]
        hnew_ref[0] = hst[...]
        scnew_ref[0] = sbuf[8 - (SC_K - 1):8, :]


def _mixer_alias_kernel(nskip, *refs):
    n_in = 18
    _mixer_kernel(nskip, *refs[:n_in], *refs[n_in + 2:])


def _mixer(u, dtf, rb0, nb, nch, nskip, cprev, h0, scprev, lw, keep=None):
    r = u.shape[0]
    t = CH

    def rowmap(cb):
        return lambda b, c: (rb0 + b * nch + _stored_chunk(c, nch), cb)

    def stmap(b, c):
        return (b, 0, 0)

    def full(shape):
        return pl.BlockSpec(shape, lambda b, c: (0,) * len(shape))

    in_specs = [
        pl.BlockSpec((t, SSD_DI), rowmap(C_Z // SSD_DI)),
        pl.BlockSpec((t, SSD_DI), rowmap(C_X // SSD_DI)),
        pl.BlockSpec((t, SSD_DI), rowmap(C_BC // SSD_DI)),
        pl.BlockSpec((t, LANE), rowmap(0)),
        pl.BlockSpec((t, SC_D), rowmap(C_SCB // SC_D)),
        pl.BlockSpec((t, SC_D), rowmap(C_SCC // SC_D)),
        pl.BlockSpec((t, SC_D), rowmap(C_SCH // SC_D)),
        pl.BlockSpec((1, SSD_K - 1, SSD_CD), stmap),
        pl.BlockSpec((1, SSD_N, SSD_DI), stmap),
        pl.BlockSpec((1, SC_K - 1, SC_D), stmap),
        full((SSD_K, SSD_CD)), full((1, SSD_CD)), full((1, LANE)), full((1, LANE)),
        full((1, SSD_DI)), full((1, SSD_DI)), full((SC_K, SC_D)),
        full((CH * (SSD_K - 1) + TAIL, TAIL + CH)),
    ]
    out_specs = [
        pl.BlockSpec((t, SSD_DI), rowmap(0)),
        pl.BlockSpec((t, SC_D), rowmap(0)),
        pl.BlockSpec((1, SSD_K - 1, SSD_CD), stmap),
        pl.BlockSpec((1, SSD_N, SSD_DI), stmap),
        pl.BlockSpec((1, SC_K - 1, SC_D), stmap),
    ]
    out_shape = [
        jax.ShapeDtypeStruct((r, SSD_DI), BF16),
        jax.ShapeDtypeStruct((r, SC_D), BF16),
        jax.ShapeDtypeStruct((nb, SSD_K - 1, SSD_CD), F32),
        jax.ShapeDtypeStruct((nb, SSD_N, SSD_DI), F32),
        jax.ShapeDtypeStruct((nb, SC_K - 1, SC_D), F32),
    ]
    args = [u, u, u, dtf, u, u, u, cprev, h0, scprev,
            lw["cw"], lw["cb"], lw["dtb"], lw["alog"], lw["dsk"], lw["ng"], lw["scw"],
            jnp.asarray(_shift_matrix(SSD_K), BF16)]
    kern = functools.partial(_mixer_kernel, nskip)
    aliases = {}
    if keep is not None:
        aliases = {len(args): 0, len(args) + 1: 1}
        args += list(keep)
        in_specs += [pl.BlockSpec(memory_space=pl.ANY)] * 2
        kern = functools.partial(_mixer_alias_kernel, nskip)
    return pl.pallas_call(
        kern,
        grid=(nb, nch),
        in_specs=in_specs, out_specs=out_specs, out_shape=out_shape,
        scratch_shapes=[pltpu.VMEM((TAIL, SSD_CD), F32), pltpu.VMEM((8 + t, SC_D), F32),
                        pltpu.VMEM((SSD_N, SSD_DI), F32)],
        input_output_aliases=aliases,
        compiler_params=_cp(2), name="mixer",
    )(*args)


def _mla_proj_kernel(blk_ref, tab_ref, qn_ref, kvn_ref, wa_ref, wb_ref, wk_ref, wv_ref,
                     q_ref, ckv_ref, kpe_ref, kcat_ref, v_ref):
    blk = blk_ref[...]
    cq = _rms(blk[:, :QL].astype(F32), qn_ref[...]).astype(BF16)
    ckv = _rms(blk[:, QL:QL + KVL].astype(F32), kvn_ref[...])
    ckv_ref[...] = ckv
    tab = tab_ref[...]
    tabr = pltpu.roll(tab, ROPE, 1)
    prod = blk[:, QL + KVL:QL + KVL + LANE].astype(F32) * tab
    ksum = prod + pltpu.roll(prod, ROPE, 1)
    lane = lax.broadcasted_iota(jnp.int32, ksum.shape, 1)
    kpe = jnp.where(lane < ROPE, ksum, 0.0)
    kpe_ref[...] = kpe
    qa = jnp.dot(cq, wa_ref[...], preferred_element_type=F32)
    qb = jnp.dot(cq, wb_ref[...], preferred_element_type=F32)
    ckvb = ckv.astype(BF16)
    kn = jnp.dot(ckvb, wk_ref[...], preferred_element_type=F32)
    v_ref[...] = jnp.dot(ckvb, wv_ref[...], preferred_element_type=F32).astype(BF16)
    kpeb = kpe.astype(BF16)
    tab_q = tab * QSCALE
    tabr_q = tabr * QSCALE
    for h in range(MLA_H):
        q_ref[:, h * HK:h * HK + NOPE] = (qa[:, h * HK:h * HK + NOPE] * QSCALE).astype(BF16)
        q_ref[:, h * HK + NOPE:(h + 1) * HK] = (
            qa[:, h * HK + NOPE:(h + 1) * HK] * tab_q + qb[:, h * LANE:(h + 1) * LANE] * tabr_q).astype(BF16)
        kcat_ref[:, h * HK:h * HK + NOPE] = kn[:, h * NOPE:(h + 1) * NOPE].astype(BF16)
        kcat_ref[:, h * HK + NOPE:(h + 1) * HK] = kpeb


def _mla_proj(u, tab, lw):
    r = u.shape[0]
    bm = _pick(r, (544, 512, 272, 256, 136, 128, 64))

    def full(shape):
        return pl.BlockSpec(shape, lambda i: (0,) * len(shape))

    def row(w):
        return pl.BlockSpec((bm, w), lambda i: (i, 0))

    return pl.pallas_call(
        _mla_proj_kernel,
        grid=(r // bm,),
        in_specs=[pl.BlockSpec((bm, MLA_W), lambda i: (i, C_MLA // MLA_W)), row(LANE),
                  full((1, QL)), full((1, KVL)), full((QL, MLA_H * HK)), full((QL, MLA_H * LANE)),
                  full((KVL, MLA_H * NOPE)), full((KVL, MLA_H * MLA_V))],
        out_specs=[row(MLA_H * HK), row(KVL), row(LANE), row(MLA_H * HK), row(MLA_H * MLA_V)],
        out_shape=[jax.ShapeDtypeStruct((r, MLA_H * HK), BF16), jax.ShapeDtypeStruct((r, KVL), F32),
                   jax.ShapeDtypeStruct((r, LANE), F32), jax.ShapeDtypeStruct((r, MLA_H * HK), BF16),
                   jax.ShapeDtypeStruct((r, MLA_H * MLA_V), BF16)],
        compiler_params=_cp(1), name="mla_proj",
    )(u, tab, lw["qn"], lw["kvn"], lw["wa"], lw["wb"], lw["wk"], lw["wv"])


def _flash_kernel(it_ref, jt_ref, q_ref, k_ref, v_ref, k0_ref, v0_ref, mask_ref, pad_ref, o_ref,
                  qt_s, m_s, l_s, acc_s, s_scr, p_scr, a_s):
    tq = q_ref.shape[0]
    tk = k_ref.shape[0]
    step = pl.program_id(0)
    i = it_ref[step]
    j = jt_ref[step]

    @pl.when(j == 0)
    def _init():
        for h in range(MLA_H):
            qt_s[h * HK:(h + 1) * HK, :] = q_ref[:, h * HK:(h + 1) * HK].T
        for h in range(MLA_H):
            s0 = jnp.dot(k0_ref[:, h * HK:(h + 1) * HK], qt_s[h * HK:(h + 1) * HK, :],
                         preferred_element_type=F32) + pad_ref[...]
            m0 = jnp.max(s0, axis=0, keepdims=True)
            p0 = jnp.exp2(s0 - m0)
            rows = slice(h * MLA_V, (h + 1) * MLA_V)
            m_s[h] = m0
            l_s[h] = jnp.sum(p0, axis=0, keepdims=True)
            acc_s[rows, :] = lax.dot_general(v0_ref[:, rows], p0.astype(BF16), TN, preferred_element_type=F32)

    def tile(masked):
        nkb = tk // CH

        def block(sref, kb):
            blk = sref[kb * CH:(kb + 1) * CH, :]
            if masked:
                blk = blk + mask_ref[kb * CH:(kb + 1) * CH, :]
            return blk

        def scores(h):
            s_scr[h % 2] = jnp.dot(k_ref[:, h * HK:(h + 1) * HK], qt_s[h * HK:(h + 1) * HK, :],
                                   preferred_element_type=F32)

        def softmax(h):
            sref = s_scr.at[h % 2]
            pref = p_scr.at[h % 2]
            mx = block(sref, 0).reshape(CH // 8, 8, tq).max(axis=0)
            for kb in range(1, nkb):
                mx = jnp.maximum(mx, block(sref, kb).reshape(CH // 8, 8, tq).max(axis=0))
            m_prev = m_s[h]
            m_new = jnp.maximum(m_prev, jnp.max(mx, axis=0, keepdims=True))
            alpha = jnp.exp2(m_prev - m_new)
            part = jnp.zeros((8, tq), F32)
            for kb in range(nkb):
                e = jnp.exp2(block(sref, kb) - m_new)
                part = part + jnp.sum(e.reshape(CH // 8, 8, tq), axis=0)
                pref[kb * CH:(kb + 1) * CH, :] = e.astype(BF16)
            l_s[h] = alpha * l_s[h] + jnp.sum(part, axis=0, keepdims=True)
            m_s[h] = m_new
            a_s[h % 2] = alpha

        def weighted_values(h):
            rows = slice(h * MLA_V, (h + 1) * MLA_V)
            acc_s[rows, :] = a_s[h % 2] * acc_s[rows, :] + lax.dot_general(
                v_ref[:, rows], p_scr[h % 2], TN, preferred_element_type=F32)

        scores(0)
        for h in range(MLA_H + 1):
            if h + 1 < MLA_H:
                scores(h + 1)
            if h >= 1:
                weighted_values(h - 1)
            if h < MLA_H:
                softmax(h)

    @pl.when(j < i)
    def _():
        tile(False)

    @pl.when(j == i)
    def _():
        tile(True)
        for h in range(MLA_H):
            rows = slice(h * MLA_V, (h + 1) * MLA_V)
            o_ref[:, rows] = (acc_s[rows, :] / l_s[h]).T.astype(BF16)


def _pad_mask(n):
    return jnp.asarray(np.where(np.arange(CH)[:, None] + 0 * np.arange(n)[None, :] < NSKIP, -np.inf, 0.0),
                       F32)


def _flash(q, kcat, v, seq):
    r = q.shape[0]
    tq = _pick(seq, (FLASH_TQ, 256, 128))
    nq = seq // tq
    c0 = seq // CH
    it = np.concatenate([np.full((i + 1,), i, np.int32) for i in range(nq)])
    jt = np.concatenate([np.arange(i + 1, dtype=np.int32) for i in range(nq)])
    kk = np.arange(tq)[:, None]
    qq = np.arange(tq)[None, :]
    diag = jnp.asarray(np.where(kk // CH > qq // CH, -np.inf, 0.0), F32)
    grid_spec = pltpu.PrefetchScalarGridSpec(
        num_scalar_prefetch=2,
        grid=(int(it.shape[0]),),
        in_specs=[pl.BlockSpec((tq, MLA_H * HK), lambda s, it, jt: (it[s], 0)),
                  pl.BlockSpec((tq, MLA_H * HK), lambda s, it, jt: (jt[s], 0)),
                  pl.BlockSpec((tq, MLA_H * MLA_V), lambda s, it, jt: (jt[s], 0)),
                  pl.BlockSpec((CH, MLA_H * HK), lambda s, it, jt: (c0, 0)),
                  pl.BlockSpec((CH, MLA_H * MLA_V), lambda s, it, jt: (c0, 0)),
                  pl.BlockSpec((tq, tq), lambda s, it, jt: (0, 0)),
                  pl.BlockSpec((CH, tq), lambda s, it, jt: (0, 0))],
        out_specs=pl.BlockSpec((tq, MLA_H * MLA_V), lambda s, it, jt: (it[s], 0)),
        scratch_shapes=[pltpu.VMEM((MLA_H * HK, tq), BF16),
                        pltpu.VMEM((MLA_H, 1, tq), F32), pltpu.VMEM((MLA_H, 1, tq), F32),
                        pltpu.VMEM((MLA_H * MLA_V, tq), F32),
                        pltpu.VMEM((2, tq, tq), F32), pltpu.VMEM((2, tq, tq), BF16),
                        pltpu.VMEM((2, 1, tq), F32)],
    )
    return pl.pallas_call(
        _flash_kernel,
        grid_spec=grid_spec,
        out_shape=jax.ShapeDtypeStruct((r, MLA_H * MLA_V), BF16),
        compiler_params=_cp(1), name="flash",
    )(jnp.asarray(it), jnp.asarray(jt), q, kcat, v, kcat, v, diag, _pad_mask(tq))


def _attn0_kernel(q_ref, k_ref, v_ref, keep_ref, o_ref):
    del keep_ref
    col = lax.broadcasted_iota(jnp.int32, (CH, CH), 1)
    for h in range(MLA_H):
        s = lax.dot_general(q_ref[:, h * HK:(h + 1) * HK], k_ref[:, h * HK:(h + 1) * HK], NT,
                            preferred_element_type=F32)
        s = jnp.where(col >= NSKIP, s, -jnp.inf)
        p = jnp.exp2(s - jnp.max(s, axis=-1, keepdims=True))
        o = jnp.dot(p.astype(BF16), v_ref[:, h * MLA_V:(h + 1) * MLA_V], preferred_element_type=F32)
        o_ref[:, h * MLA_V:(h + 1) * MLA_V] = (o / jnp.sum(p, axis=-1, keepdims=True)).astype(BF16)


def _attn0(q, kcat, v, c0, o_frames):
    def blk(w):
        return pl.BlockSpec((CH, w), lambda i: (c0, 0))

    return pl.pallas_call(
        _attn0_kernel,
        grid=(1,),
        in_specs=[blk(MLA_H * HK), blk(MLA_H * HK), blk(MLA_H * MLA_V), pl.BlockSpec(memory_space=pl.ANY)],
        out_specs=blk(MLA_H * MLA_V),
        out_shape=jax.ShapeDtypeStruct(o_frames.shape, BF16),
        input_output_aliases={3: 0},
        compiler_params=_cp(1), name="attn0",
    )(q, kcat, v, o_frames)


def _cached_attn_kernel(q_ref, cnew_ref, pnew_ref, ckv_ref, cpe_ref, wkt_ref, wv_ref, o_ref, call, peall):
    past = ckv_ref.shape[2]
    call[0:past, :] = ckv_ref[0, 0].astype(BF16)
    call[past:past + CH, :] = cnew_ref[...].astype(BF16)
    peall[...] = jnp.zeros(peall.shape, BF16)
    peall[0:past, 0:ROPE] = cpe_ref[0, 0].astype(BF16)
    peall[past:past + CH, :] = pnew_ref[...].astype(BF16)
    qlat = []
    qpe = []
    for h in range(MLA_H):
        qn = q_ref[:, h * HK:h * HK + NOPE]
        qlat.append(jnp.dot(qn, wkt_ref[h], preferred_element_type=F32).astype(BF16))
        qpe.append(q_ref[:, h * HK + NOPE:(h + 1) * HK])
    qlat = jnp.concatenate(qlat, axis=0)
    qpe = jnp.concatenate(qpe, axis=0)
    s = (lax.dot_general(qlat, call[...], NT, preferred_element_type=F32)
         + lax.dot_general(qpe, peall[...], NT, preferred_element_type=F32))
    p = jnp.exp2(s - jnp.max(s, axis=-1, keepdims=True))
    p = p / jnp.sum(p, axis=-1, keepdims=True)
    olat = jnp.dot(p.astype(BF16), call[...], preferred_element_type=F32).astype(BF16)
    for h in range(MLA_H):
        o_ref[:, h * MLA_V:(h + 1) * MLA_V] = jnp.dot(
            olat[h * CH:(h + 1) * CH, :], wv_ref[h], preferred_element_type=F32).astype(BF16)


def _cached_attn_alias_kernel(q_ref, cnew_ref, pnew_ref, ckv_ref, cpe_ref, wkt_ref, wv_ref, keep_ref, o_ref,
                              call, peall):
    del keep_ref
    _cached_attn_kernel(q_ref, cnew_ref, pnew_ref, ckv_ref, cpe_ref, wkt_ref, wv_ref, o_ref, call, peall)


def _cached_attn(q, ckv, kpe, cache_kv, cache_pe, layer, rb0, lw, o_prompt):
    nb, past = cache_kv.shape[1], cache_kv.shape[2]

    def rowmap(b):
        return (rb0 + b, 0)

    return pl.pallas_call(
        _cached_attn_alias_kernel,
        grid=(nb,),
        in_specs=[pl.BlockSpec((CH, MLA_H * HK), rowmap),
                  pl.BlockSpec((CH, KVL), rowmap),
                  pl.BlockSpec((CH, LANE), rowmap),
                  pl.BlockSpec((1, 1, past, KVL), lambda b: (layer, b, 0, 0)),
                  pl.BlockSpec((1, 1, past, ROPE), lambda b: (layer, b, 0, 0)),
                  pl.BlockSpec((MLA_H, NOPE, KVL), lambda b: (0, 0, 0)),
                  pl.BlockSpec((MLA_H, KVL, MLA_V), lambda b: (0, 0, 0)),
                  pl.BlockSpec(memory_space=pl.ANY)],
        out_specs=pl.BlockSpec((CH, MLA_H * MLA_V), rowmap),
        out_shape=jax.ShapeDtypeStruct(o_prompt.shape, BF16),
        scratch_shapes=[pltpu.VMEM((past + CH, KVL), BF16), pltpu.VMEM((past + CH, LANE), BF16)],
        input_output_aliases={7: 0},
        compiler_params=_cp(1), name="cached_attn",
    )(q, ckv, kpe, cache_kv, cache_pe, lw["wkt"], lw["wv3"], o_prompt)


TAIL = 16


def _shift_matrix(k, rows=CH):
    ns = k - 1
    pm = np.zeros((rows * ns + TAIL, TAIL + rows), np.float32)
    for d in range(1, ns + 1):
        for t in range(rows):
            if t - d >= 0:
                pm[(d - 1) * rows + t, TAIL + t - d] = 1.0
            else:
                r = ns + t - d
                pm[(d - 1) * rows + t, 3 * r:3 * r + 3] = 1.0
    for r in range(ns):
        pm[rows * ns + 3 * r, TAIL + rows - ns + r] = 1.0
    return pm


def _load_tail(tail_s, prev):
    hi = prev.astype(BF16).astype(F32)
    rest = prev - hi
    mid = rest.astype(BF16).astype(F32)
    lo = (rest - mid).astype(BF16).astype(F32)
    tail_s[...] = jnp.zeros(tail_s.shape, F32)
    for r in range(prev.shape[0]):
        tail_s[3 * r:3 * r + 1, :] = hi[r:r + 1]
        tail_s[3 * r + 1:3 * r + 2, :] = mid[r:r + 1]
        tail_s[3 * r + 2:3 * r + 3, :] = lo[r:r + 1]


def _delayed(pm_ref, tail_s, x, ns):
    t = x.shape[0]
    ext = jnp.concatenate([tail_s[...].astype(BF16), x], axis=0)
    out = jnp.dot(pm_ref[...], ext, preferred_element_type=F32)
    tail_s[...] = out[t * ns:, :]
    return [out[(d - 1) * t:d * t, :] for d in range(1, ns + 1)]


def _act_kernel(nskip, g_ref, u_ref, prev_ref, w_ref, pm_ref, a_ref, new_ref, tail_s):
    t = g_ref.shape[0]
    ns = FFN_K - 1
    c = pl.program_id(1)

    @pl.when(c == 0)
    def _():
        _load_tail(tail_s, prev_ref[0])

    rows = lax.broadcasted_iota(jnp.int32, (t, 1), 0) + c * t
    g = g_ref[...]
    g = jnp.where(rows >= nskip, g, jnp.zeros((), g.dtype))
    delayed = _delayed(pm_ref, tail_s, g, ns)
    conv = w_ref[ns:ns + 1, :] * g.astype(F32)
    for d in range(1, ns + 1):
        conv = conv + w_ref[ns - d:ns - d + 1, :] * delayed[d - 1]
    a_ref[...] = (_silu(conv) * u_ref[...].astype(F32)).astype(BF16)

    @pl.when(c == pl.num_programs(1) - 1)
    def _():
        for r in range(ns):
            new_ref[0, r:r + 1, :] = tail_s[3 * r:3 * r + 1, :]


def _act(gate, up, rb0, nb, nch, nskip, prev, w, act_in=None, t=CH):
    r, n = gate.shape

    def rowmap(b, c):
        return (rb0 + b * nch + c, 0)

    pm = jnp.asarray(_shift_matrix(FFN_K, t), BF16)
    args = [gate, up, prev, w, pm]
    in_specs = [pl.BlockSpec((t, n), rowmap), pl.BlockSpec((t, n), rowmap),
                pl.BlockSpec((1, FFN_K - 1, n), lambda b, c: (b, 0, 0)),
                pl.BlockSpec((FFN_K, n), lambda b, c: (0, 0)),
                pl.BlockSpec(pm.shape, lambda b, c: (0, 0))]
    kern = functools.partial(_act_kernel, nskip)
    aliases = {}
    if act_in is not None:
        aliases = {len(args): 0}
        args.append(act_in)
        in_specs.append(pl.BlockSpec(memory_space=pl.ANY))
        kern = functools.partial(_act_alias_kernel, nskip)
    return pl.pallas_call(
        kern,
        grid=(nb, nch),
        in_specs=in_specs,
        out_specs=[pl.BlockSpec((t, n), rowmap), pl.BlockSpec((1, FFN_K - 1, n), lambda b, c: (b, 0, 0))],
        out_shape=[jax.ShapeDtypeStruct((r, n), BF16), jax.ShapeDtypeStruct((nb, FFN_K - 1, n), F32)],
        scratch_shapes=[pltpu.VMEM((TAIL, n), F32)],
        input_output_aliases=aliases,
        compiler_params=_cp(2), name="act",
    )(*args)


def _act_alias_kernel(nskip, g_ref, u_ref, prev_ref, w_ref, pm_ref, keep_ref, a_ref, new_ref, tail_s):
    del keep_ref
    _act_kernel(nskip, g_ref, u_ref, prev_ref, w_ref, pm_ref, a_ref, new_ref, tail_s)


def _cast_cols_kernel(x_ref, o_ref):
    n = x_ref.shape[1]
    o_ref[:, :n] = x_ref[...].astype(BF16)
    if o_ref.shape[1] > n:
        o_ref[:, n:] = jnp.zeros((o_ref.shape[0], o_ref.shape[1] - n), BF16)


def _cast_pad_cols(w, layer, n_out):
    _, k, n = w.shape
    tr = _pick(k, (128, 64))
    return pl.pallas_call(
        _cast_cols_kernel,
        grid=(k // tr,),
        in_specs=[pl.BlockSpec((None, tr, n), lambda i: (layer, i, 0))],
        out_specs=pl.BlockSpec((tr, n_out), lambda i: (i, 0)),
        out_shape=jax.ShapeDtypeStruct((k, n_out), BF16),
        compiler_params=_cp(1), name="cast_cols",
    )(w)


def _cast_rows_kernel(n_full, x_ref, o_ref):
    @pl.when(pl.program_id(0) < n_full)
    def _():
        o_ref[...] = x_ref[...].astype(BF16)

    @pl.when(pl.program_id(0) >= n_full)
    def _():
        o_ref[...] = jnp.zeros(o_ref.shape, BF16)


def _cast_pad_rows(w, layer, k_out):
    _, k, n = w.shape
    tr = 256
    assert k % tr == 0 and k_out % tr == 0
    n_full = k // tr
    return pl.pallas_call(
        functools.partial(_cast_rows_kernel, n_full),
        grid=(k_out // tr,),
        in_specs=[pl.BlockSpec((None, tr, n), lambda i: (layer, jnp.minimum(i, n_full - 1), 0))],
        out_specs=pl.BlockSpec((tr, n), lambda i: (i, 0)),
        out_shape=jax.ShapeDtypeStruct((k_out, n), BF16),
        compiler_params=_cp(1), name="cast_rows",
    )(w)


PACK = 256


def _pack_win_kernel(a_ref, b_ref, o_ref):
    j = pl.program_id(0)
    o_dt = SSD_DI + SSD_CD
    n_plain = o_dt // PACK
    last = pl.num_programs(0) - 1
    half = ROPE // 2

    def emit(rows):
        o_ref[...] = rows.T.astype(BF16)

    @pl.when(j < n_plain)
    def _():
        emit(a_ref[...])

    @pl.when((j >= n_plain) & (j < last))
    def _():
        emit(jnp.concatenate([a_ref[SSD_H:, :], b_ref[:SSD_H, :]], axis=0))

    @pl.when(j == last)
    def _():
        kr0 = a_ref[SSD_H:SSD_H + half, :]
        kr1 = a_ref[SSD_H + half:SSD_H + ROPE, :]
        emit(jnp.concatenate([kr0, kr1, kr1, kr0, b_ref[:SSD_H, :],
                              jnp.zeros((PACK - 2 * ROPE - SSD_H, a_ref.shape[1]), F32)], axis=0))


def _pack_win(w_in, layer):
    _, k, n = w_in.shape
    assert (SSD_DI + SSD_CD) % PACK == 0 and NIN % PACK == 0 and n - (NIN - PACK) == SSD_H + ROPE
    wt = jnp.swapaxes(w_in, 1, 2)
    nblk = NIN // PACK
    n_plain = (SSD_DI + SSD_CD) // PACK

    def b_index(j):
        return jnp.where((j < n_plain) | (j == nblk - 1), n_plain, j + 1)

    return pl.pallas_call(
        _pack_win_kernel,
        grid=(nblk,),
        in_specs=[pl.BlockSpec((None, PACK, k), lambda j: (layer, j, 0)),
                  pl.BlockSpec((None, PACK, k), lambda j: (layer, b_index(j), 0))],
        out_specs=pl.BlockSpec((k, PACK), lambda j: (0, j)),
        out_shape=jax.ShapeDtypeStruct((k, NIN), BF16),
        compiler_params=_cp(1), name="pack_win",
    )(wt, wt)


def _swap_half(w):
    half = w.shape[-1] // 2
    return jnp.concatenate([w[..., half:], w[..., :half]], axis=-1)


def _layer_weights(i, w_in, ssd_conv_w, ssd_conv_b, ssd_dt_bias, ssd_a_log, ssd_d, ssd_norm, sc_conv_w,
                   mla_q_norm, mla_w_uq, mla_kv_norm, mla_w_ukv, w_out, ffn_w_gate, ffn_w_up, ffn_conv_w,
                   ffn_w_down):
    win = _pack_win(w_in, i)
    uq = mla_w_uq[i].reshape(QL, MLA_H, NOPE + ROPE)
    pe = uq[..., NOPE:]
    zq = jnp.zeros((QL, MLA_H, HK - NOPE - ROPE), F32)
    wa = jnp.concatenate([uq[..., :NOPE], pe, zq], axis=-1).reshape(QL, MLA_H * HK).astype(BF16)
    wb = jnp.concatenate([_swap_half(pe), zq], axis=-1).reshape(QL, MLA_H * LANE).astype(BF16)
    ukv = mla_w_ukv[i].reshape(KVL, MLA_H, NOPE + MLA_V)
    padf = ((0, 0), (0, DFP - D_FF))
    return dict(
        win=win,
        cw=ssd_conv_w[i], cb=ssd_conv_b[i].reshape(1, SSD_CD),
        dtb=jnp.pad(ssd_dt_bias[i], (0, LANE - SSD_H)).reshape(1, LANE),
        alog=jnp.pad(ssd_a_log[i], (0, LANE - SSD_H)).reshape(1, LANE),
        dsk=jnp.repeat(ssd_d[i], SSD_P).reshape(1, SSD_DI),
        ng=ssd_norm[i].reshape(1, SSD_DI),
        scw=sc_conv_w[i],
        qn=mla_q_norm[i].reshape(1, QL), kvn=mla_kv_norm[i].reshape(1, KVL),
        wa=wa, wb=wb,
        wk=ukv[..., :NOPE].reshape(KVL, MLA_H * NOPE).astype(BF16),
        wv=ukv[..., NOPE:].reshape(KVL, MLA_H * MLA_V).astype(BF16),
        wkt=jnp.transpose(ukv[..., :NOPE], (1, 2, 0)).astype(BF16),
        wv3=jnp.transpose(ukv[..., NOPE:], (1, 0, 2)).astype(BF16),
        wout=_cast_pad_cols(w_out, i, D_MODEL),
        wg=_cast_pad_cols(ffn_w_gate, i, DFP),
        wu=_cast_pad_cols(ffn_w_up, i, DFP),
        fcw=jnp.pad(ffn_conv_w[i], padf),
        wd=_cast_pad_rows(ffn_w_down, i, DFP),
    )


def _rope_table(seq, ns, past):
    half = ROPE // 2
    pos = jnp.concatenate([N_META + jnp.arange(seq, dtype=jnp.int32),
                           jnp.maximum(jnp.arange(CH, dtype=jnp.int32) - NSKIP, 0),
                           N_META + past + jnp.arange(ns, dtype=jnp.int32) % CH])
    inv = THETA ** (-jnp.arange(half, dtype=F32) / half)
    ang = pos.astype(F32)[:, None] * inv[None, :]
    cos, sin = jnp.cos(ang), jnp.sin(ang)
    return jnp.concatenate([cos, cos, -sin, sin], axis=1)


def kernel(x_prompt, x_sample, cache_kv_latent, cache_k_rope, state_ssm, state_ssd_conv, state_sconv, state_ffn_conv, meta_tokens, norm_mix_pre, norm_mix_post, norm_ffn_pre, norm_ffn_post, w_in, ssd_conv_w, ssd_conv_b, ssd_dt_bias, ssd_a_log, ssd_d, ssd_norm, sc_conv_w, mla_q_norm, mla_w_uq, mla_kv_norm, mla_w_ukv, w_out, ffn_w_gate, ffn_w_up, ffn_conv_w, ffn_w_down):
    bp, seq, d = x_prompt.shape
    nb, ls, _ = x_sample.shape
    depth, _, past, _ = cache_kv_latent.shape
    assert bp == 1 and ls == CH and seq % CH == 0 and d == D_MODEL
    lp = CH + seq
    ns = nb * ls
    npc = lp // CH
    c0 = seq // CH

    x, h = _assemble(x_prompt[0], x_sample.reshape(ns, d), meta_tokens.astype(F32), norm_mix_pre[0])
    tab = _rope_table(seq, ns, past)
    zero_c = jnp.zeros((1, SSD_K - 1, SSD_CD), F32)
    zero_h = jnp.zeros((1, SSD_N, SSD_DI), F32)

    def state_in(s):
        return jnp.transpose(s, (0, 3, 1, 2)).reshape(s.shape[0], SSD_N, SSD_DI)

    def state_out(s):
        return jnp.transpose(s.reshape(s.shape[0], SSD_N, SSD_H, SSD_P), (0, 2, 3, 1))
    zero_s = jnp.zeros((1, SC_K - 1, SC_D), F32)
    zero_f = jnp.zeros((1, FFN_K - 1, DFP), F32)
    padf = ((0, 0), (0, 0), (0, DFP - D_FF))

    outs_p, outs_s = [], []
    for i in range(depth):
        lw = _layer_weights(i, w_in, ssd_conv_w, ssd_conv_b, ssd_dt_bias, ssd_a_log, ssd_d, ssd_norm,
                            sc_conv_w, mla_q_norm, mla_w_uq, mla_kv_norm, mla_w_ukv, w_out, ffn_w_gate,
                            ffn_w_up, ffn_conv_w, ffn_w_down)
        u, dtf = _inproj(h, lw["win"])

        yp, yscp, cnew_p, hnew_p, scnew_p = _mixer(u, dtf, 0, 1, npc, NSKIP, zero_c, zero_h, zero_s, lw)
        y_ssd, y_sc, cnew_s, hnew_s, scnew_s = _mixer(u, dtf, npc, nb, 1, 0, state_ssd_conv[i],
                                                      state_in(state_ssm[i]), state_sconv[i], lw,
                                                      keep=(yp, yscp))
        hnew_p, hnew_s = state_out(hnew_p), state_out(hnew_s)

        q, ckv, kpe, kcat, v = _mla_proj(u, tab, lw)
        y_mla = _attn0(q, kcat, v, c0, _flash(q, kcat, v, seq))
        y_mla = _cached_attn(q, ckv, kpe, cache_kv_latent, cache_k_rope, i, npc, lw, y_mla)

        mix = _outproj(y_ssd, y_sc, y_mla, lw["wout"])
        x, h = _resid_norm(x, mix, norm_mix_post[i], norm_ffn_pre[i])

        gate, up = _gateup(h, lw["wg"], lw["wu"])
        act, f0 = _act(gate, up, c0, 1, 1, NSKIP, zero_f, lw["fcw"])
        act, fnew_p = _act(gate, up, 0, 1, seq // ACT_ROWS, 0, f0, lw["fcw"], act_in=act, t=ACT_ROWS)
        act, fnew_s = _act(gate, up, npc, nb, 1, 0, jnp.pad(state_ffn_conv[i], padf), lw["fcw"], act_in=act)
        f = _down(act, lw["wd"])
        if i + 1 < depth:
            x, h = _resid_norm(x, f, norm_ffn_post[i], norm_mix_pre[i + 1])
        else:
            y_prompt = _resid_out(x, f, norm_ffn_post[i], 0, seq)
            y_sample = _resid_out(x, f, norm_ffn_post[i], lp, ns)

        meta_rows = slice(seq + NSKIP, lp)
        outs_p.append((jnp.concatenate([ckv[meta_rows], ckv[:seq]], axis=0)[None],
                       jnp.concatenate([kpe[meta_rows, :ROPE], kpe[:seq, :ROPE]], axis=0)[None],
                       hnew_p, cnew_p, scnew_p, fnew_p[:, :, :D_FF]))
        outs_s.append((ckv[lp:].reshape(nb, ls, KVL), kpe[lp:, :ROPE].reshape(nb, ls, ROPE), hnew_s, cnew_s,
                       scnew_s, fnew_s[:, :, :D_FF]))

    def stack(outs, j):
        return jnp.stack([o[j] for o in outs], axis=0)

    return (y_prompt[None], y_sample.reshape(nb, ls, d),
            stack(outs_p, 0), stack(outs_p, 1), stack(outs_p, 2), stack(outs_p, 3), stack(outs_p, 4), stack(outs_p, 5),
            stack(outs_s, 0), stack(outs_s, 1), stack(outs_s, 2), stack(outs_s, 3), stack(outs_s, 4), stack(outs_s, 5))
```

```python
import functools

import jax
import jax.numpy as jnp
import numpy as np
from jax import lax
from jax.experimental import pallas as pl
from jax.experimental.pallas import tpu as pltpu

F32 = jnp.float32
BF16 = jnp.bfloat16

D_MODEL = 4096
N_META = 16
CH = 64
NSKIP = CH - N_META
EPS = 1e-6
SSD_P = 64
SSD_DI = 2048
SSD_H = 32
SSD_G = 8
SSD_N = 128
SSD_K = 4
SSD_CD = 4096
SC_D = 1024
SC_K = 3
MLA_H = 8
NOPE = 128
ROPE = 64
MLA_V = 128
QL = 768
KVL = 512
HK = 256
SCALE = (NOPE + ROPE) ** -0.5
QSCALE = SCALE * 1.4426950408889634
THETA = 10000.0
D_FF = 11008
DFP = 11264
FFN_K = 3
LANE = 128

C_Z, C_X, C_BC, C_SCB, C_SCC, C_SCH, C_MLA, NIN = 0, 2048, 4096, 6144, 7168, 8192, 9216, 10752
MLA_W = 1536

FLASH_TQ = 512
ACT_ROWS = 128
VMEM_LIMIT = 56 * 1024 * 1024
HI = lax.Precision.HIGHEST
NT = (((1,), (1,)), ((), ()))
TN = (((0,), (0,)), ((), ()))


def _cp(n, flags=None):
    return pltpu.CompilerParams(dimension_semantics=("arbitrary",) * n, vmem_limit_bytes=VMEM_LIMIT, flags=flags)


def _pick(n, cands):
    for c in cands:
        if n % c == 0:
            return c
    raise ValueError(f"no tile for {n}")


MM_ROWS = (1088, 544, 512, 272, 256, 136, 128, 64)
EW_ROWS = (272, 256, 136, 128, 64)


def _stored_chunk(c, nch):
    return (c + nch - 1) % nch


def _rms(x, g):
    return x * lax.rsqrt(jnp.mean(x * x, axis=-1, keepdims=True) + EPS) * g


def _silu(x):
    return x * jax.nn.sigmoid(x)


def _norm_rows_kernel(x_in_ref, g_ref, x_ref, h_ref):
    x = x_in_ref[...]
    x_ref[...] = x
    h_ref[...] = _rms(x, g_ref[...]).astype(BF16)


def _tail_rows_kernel(xs_ref, meta_ref, g_ref, keep_x, keep_h, x_ref, h_ref):
    del keep_x, keep_h
    c = pl.program_id(0)

    def emit(x):
        x_ref[...] = x
        h_ref[...] = _rms(x, g_ref[...]).astype(BF16)

    @pl.when(c == 0)
    def _():
        emit(jnp.concatenate([jnp.zeros((NSKIP, x_ref.shape[1]), F32), meta_ref[...]], axis=0))

    @pl.when(c > 0)
    def _():
        emit(xs_ref[...])


def _assemble(x_prompt, x_sample, meta, g):
    seq, d = x_prompt.shape
    ns = x_sample.shape[0]
    r = seq + CH + ns
    bm = _pick(seq, (512, 256, 128, 64))
    vec = pl.BlockSpec((1, d), lambda c: (0, 0))
    out_shape = [jax.ShapeDtypeStruct((r, d), F32), jax.ShapeDtypeStruct((r, d), BF16)]
    big = pl.BlockSpec((bm, d), lambda c: (c, 0))
    x, h = pl.pallas_call(
        _norm_rows_kernel, grid=(seq // bm,), in_specs=[big, vec], out_specs=[big, big],
        out_shape=out_shape, compiler_params=_cp(1), name="assemble_frames",
    )(x_prompt, g.reshape(1, d))
    c0 = seq // CH
    small = pl.BlockSpec((CH, d), lambda c: (c0 + c, 0))
    anyspec = pl.BlockSpec(memory_space=pl.ANY)
    return pl.pallas_call(
        _tail_rows_kernel, grid=(1 + ns // CH,),
        in_specs=[pl.BlockSpec((CH, d), lambda c: (jnp.maximum(c - 1, 0), 0)),
                  pl.BlockSpec((N_META, d), lambda c: (0, 0)), vec, anyspec, anyspec],
        out_specs=[small, small], out_shape=out_shape, input_output_aliases={3: 0, 4: 1},
        compiler_params=_cp(1), name="assemble_tail",
    )(x_sample, meta, g.reshape(1, d), x, h)


def _resid_out_kernel(x_ref, m_ref, gp_ref, y_ref):
    y_ref[...] = x_ref[...] + _rms(m_ref[...], gp_ref[...])


def _resid_out(x, m, g_post, row0, rows):
    d = x.shape[1]
    bm = next(b for b in (512, 256, 128, 64) if rows % b == 0 and row0 % b == 0)
    src = pl.BlockSpec((bm, d), lambda c: (row0 // bm + c, 0))
    return pl.pallas_call(
        _resid_out_kernel, grid=(rows // bm,),
        in_specs=[src, src, pl.BlockSpec((1, d), lambda c: (0, 0))],
        out_specs=pl.BlockSpec((bm, d), lambda c: (c, 0)),
        out_shape=jax.ShapeDtypeStruct((rows, d), F32),
        compiler_params=_cp(1), name="resid_out",
    )(x, m, g_post.reshape(1, d))


def _resid_norm_kernel(x_ref, m_ref, gp_ref, gn_ref, x2_ref, h_ref):
    x2 = x_ref[...] + _rms(m_ref[...], gp_ref[...])
    x2_ref[...] = x2
    h_ref[...] = _rms(x2, gn_ref[...]).astype(BF16)


def _resid_norm(x, m, g_post, g_next):
    r, d = x.shape
    bm = _pick(r, EW_ROWS)
    row = pl.BlockSpec((bm, d), lambda i: (i, 0))
    vec = pl.BlockSpec((1, d), lambda i: (0, 0))
    return pl.pallas_call(
        _resid_norm_kernel, grid=(r // bm,), in_specs=[row, row, vec, vec], out_specs=[row, row],
        out_shape=[jax.ShapeDtypeStruct((r, d), F32), jax.ShapeDtypeStruct((r, d), BF16)],
        compiler_params=_cp(1), name="resid_norm",
    )(x, m, g_post.reshape(1, d), g_next.reshape(1, d))


def _inproj_kernel(x_ref, w_ref, u_ref, dt_ref):
    acc = jnp.dot(x_ref[...], w_ref[...], preferred_element_type=F32)
    u_ref[...] = acc.astype(BF16)

    @pl.when(pl.program_id(1) == pl.num_programs(1) - 1)
    def _():
        dt_ref[...] = acc[:, acc.shape[1] - LANE:]


def _inproj(h, w):
    r, k = h.shape
    n = w.shape[1]
    bm = _pick(r, MM_ROWS)
    bn = 768
    return pl.pallas_call(
        _inproj_kernel,
        grid=(r // bm, n // bn),
        in_specs=[pl.BlockSpec((bm, k), lambda i, j: (i, 0)), pl.BlockSpec((k, bn), lambda i, j: (0, j))],
        out_specs=[pl.BlockSpec((bm, bn), lambda i, j: (i, j)), pl.BlockSpec((bm, LANE), lambda i, j: (i, 0))],
        out_shape=[jax.ShapeDtypeStruct((r, n), BF16), jax.ShapeDtypeStruct((r, LANE), F32)],
        compiler_params=_cp(2), name="inproj",
    )(h, w)


def _outproj_kernel(a_ref, b_ref, c_ref, wa_ref, wb_ref, wc_ref, o_ref):
    acc = jnp.dot(a_ref[...], wa_ref[...], preferred_element_type=F32)
    acc = acc + jnp.dot(b_ref[...], wb_ref[...], preferred_element_type=F32)
    acc = acc + jnp.dot(c_ref[...], wc_ref[...], preferred_element_type=F32)
    o_ref[...] = acc


def _outproj(y_ssd, y_sc, y_mla, w):
    r = y_ssd.shape[0]
    n = w.shape[1]
    bm = _pick(r, MM_ROWS)
    bn = 1024
    return pl.pallas_call(
        _outproj_kernel,
        grid=(r // bm, n // bn),
        in_specs=[pl.BlockSpec((bm, SSD_DI), lambda i, j: (i, 0)),
                  pl.BlockSpec((bm, SC_D), lambda i, j: (i, 0)),
                  pl.BlockSpec((bm, SC_D), lambda i, j: (i, 0)),
                  pl.BlockSpec((SSD_DI, bn), lambda i, j: (0, j)),
                  pl.BlockSpec((SC_D, bn), lambda i, j: (2, j)),
                  pl.BlockSpec((SC_D, bn), lambda i, j: (3, j))],
        out_specs=pl.BlockSpec((bm, bn), lambda i, j: (i, j)),
        out_shape=jax.ShapeDtypeStruct((r, n), F32),
        compiler_params=_cp(2), name="outproj",
    )(y_ssd, y_sc, y_mla, w, w, w)


def _gateup_kernel(x_ref, wg_ref, wu_ref, g_ref, u_ref):
    x = x_ref[...]
    g_ref[...] = jnp.dot(x, wg_ref[...], preferred_element_type=F32).astype(BF16)
    u_ref[...] = jnp.dot(x, wu_ref[...], preferred_element_type=F32).astype(BF16)


def _gateup(h, wg, wu):
    r, k = h.shape
    n = wg.shape[1]
    bm = _pick(r, MM_ROWS)
    bn = 512
    wspec = pl.BlockSpec((k, bn), lambda i, j: (0, j))
    ospec = pl.BlockSpec((bm, bn), lambda i, j: (i, j))
    return pl.pallas_call(
        _gateup_kernel,
        grid=(r // bm, n // bn),
        in_specs=[pl.BlockSpec((bm, k), lambda i, j: (i, 0)), wspec, wspec],
        out_specs=[ospec, ospec],
        out_shape=[jax.ShapeDtypeStruct((r, n), BF16)] * 2,
        compiler_params=_cp(2), name="gateup",
    )(h, wg, wu)


def _down_kernel(x_ref, w_ref, o_ref):
    o_ref[...] = jnp.dot(x_ref[...], w_ref[...], preferred_element_type=F32)


def _down(a, w):
    r, k = a.shape
    n = w.shape[1]
    bm = _pick(r, MM_ROWS[1:])
    bn = 512
    return pl.pallas_call(
        _down_kernel,
        grid=(r // bm, n // bn),
        in_specs=[pl.BlockSpec((bm, k), lambda i, j: (i, 0)), pl.BlockSpec((k, bn), lambda i, j: (0, j))],
        out_specs=pl.BlockSpec((bm, bn), lambda i, j: (i, j)),
        out_shape=jax.ShapeDtypeStruct((r, n), F32),
        compiler_params=_cp(2), name="down",
    )(a, w)


def _mixer_kernel(nskip, z_ref, x_ref, bc_ref, dt_ref, scb_ref, scc_ref, sch_ref,
                  cprev_ref, h0_ref, scprev_ref,
                  cw_ref, cb_ref, dtb_ref, alog_ref, dsk_ref, ng_ref, scw_ref, pm_ref,
                  y_ref, ysc_ref, cnew_ref, hnew_ref, scnew_ref,
                  buf, sbuf, hst):
    t = CH
    c = pl.program_id(1)
    last = pl.num_programs(1) - 1

    ns = SSD_K - 1

    @pl.when(c == 0)
    def _init():
        _load_tail(buf, cprev_ref[0])
        sbuf[0:8, :] = jnp.zeros((8, SC_D), F32)
        sbuf[8 - (SC_K - 1):8, :] = scprev_ref[0]
        hst[...] = h0_ref[0]

    rows = lax.broadcasted_iota(jnp.int32, (t, 1), 0) + c * t
    valid = rows >= nskip

    xin = jnp.concatenate([x_ref[...], bc_ref[...]], axis=1)
    xin = jnp.where(valid, xin, jnp.zeros((), xin.dtype))
    delayed = _delayed(pm_ref, buf, xin, ns)
    acc = cb_ref[...] + cw_ref[ns:ns + 1, :] * xin.astype(F32)
    for d in range(1, ns + 1):
        acc = acc + cw_ref[ns - d:ns - d + 1, :] * delayed[d - 1]
    xbc = _silu(acc)

    lane = lax.broadcasted_iota(jnp.int32, (t, LANE), 1)
    v = dt_ref[...] + dtb_ref[...]
    dt = jnp.maximum(v, 0.0) + jnp.log1p(jnp.exp(-jnp.abs(v)))
    dt = jnp.where(valid & (lane < SSD_H), dt, 0.0)
    adt = dt * (-jnp.exp(alog_ref[...]))
    ri = lax.broadcasted_iota(jnp.int32, (t, t), 0)
    ci = lax.broadcasted_iota(jnp.int32, (t, t), 1)
    tri = ri >= ci
    acs = jnp.dot(tri.astype(F32), adt, precision=HI, preferred_element_type=F32)

    hpg = SSD_H // SSD_G
    gw = hpg * SSD_P
    lane_g = lax.broadcasted_iota(jnp.int32, (t, gw), 1)
    row_g = lax.broadcasted_iota(jnp.int32, (t, gw), 0)
    seg = lane_g // SSD_P
    eye_g = row_g == lane_g % SSD_P
    tri_g = row_g >= lane_g % SSD_P
    blockdiag = (lax.broadcasted_iota(jnp.int32, (hpg * t, gw), 0) // t
                 == lax.broadcasted_iota(jnp.int32, (hpg * t, gw), 1) // SSD_P)

    def per_head_lanes(mat, g):
        out = jnp.broadcast_to(mat[:, g * hpg + hpg - 1:g * hpg + hpg], (t, gw))
        for r in range(hpg - 2, -1, -1):
            out = jnp.where(seg == r, jnp.broadcast_to(mat[:, g * hpg + r:g * hpg + r + 1], (t, gw)), out)
        return out

    for g in range(SSD_G):
        cols = slice(g * gw, (g + 1) * gw)
        bg = xbc[:, SSD_DI + g * SSD_N:SSD_DI + (g + 1) * SSD_N].astype(BF16)
        cg = xbc[:, SSD_DI + SSD_G * SSD_N + g * SSD_N:SSD_DI + SSD_G * SSD_N + (g + 1) * SSD_N].astype(BF16)
        acol = per_head_lanes(acs, g)
        dtx = per_head_lanes(dt, g)
        arow = jnp.sum(jnp.where(eye_g, acol, 0.0), axis=0, keepdims=True)
        decay = jnp.exp(jnp.where(tri_g, acol - arow, -jnp.inf))
        cb = lax.dot_general(cg, jnp.concatenate([bg] * hpg, axis=0), NT, preferred_element_type=F32)
        xg = xbc[:, cols]
        xdt = xg * dtx
        xdtb = xdt.astype(BF16)
        rhs = jnp.where(blockdiag, jnp.concatenate([xdtb] * hpg, axis=0), jnp.zeros((), BF16))
        ydiag = jnp.dot((cb * decay).astype(BF16), rhs, preferred_element_type=F32)
        hg = hst[:, cols]
        yoff = jnp.dot(cg, hg.astype(BF16), preferred_element_type=F32) * jnp.exp(acol)
        alast = acol[t - 1:t, :]
        snew = lax.dot_general(bg, (xdt * jnp.exp(alast - acol)).astype(BF16), TN, preferred_element_type=F32)
        hst[:, cols] = jnp.exp(alast) * hg + snew
        y = ydiag + yoff + dsk_ref[:, cols] * xg
        y = y * _silu(z_ref[:, cols].astype(F32))
        y_ref[:, cols] = _rms(y, ng_ref[:, cols]).astype(BF16)

    p = scc_ref[...].astype(F32) * sch_ref[...].astype(F32)
    p = jnp.where(valid, p, 0.0)
    sbuf[8:8 + t, :] = p
    conv = scw_ref[0:1, :] * sbuf[6:6 + t, :]
    for i in range(1, SC_K):
        conv = conv + scw_ref[i:i + 1, :] * sbuf[6 + i:6 + i + t, :]
    ysc_ref[...] = (scb_ref[...].astype(F32) * conv).astype(BF16)
    sbuf[0:8, :] = sbuf[t:t + 8, :]

    @pl.when(c == last)
    def _():
        for r in range(ns):
            cnew_ref[0, r:r + 1, :] = buf[3 * r:3 * r + 1, :]
        hnew_ref[0] = hst[...]
        scnew_ref[0] = sbuf[8 - (SC_K - 1):8, :]


def _mixer_alias_kernel(nskip, *refs):
    n_in = 18
    _mixer_kernel(nskip, *refs[:n_in], *refs[n_in + 2:])


def _mixer(u, dtf, rb0, nb, nch, nskip, cprev, h0, scprev, lw, keep=None):
    r = u.shape[0]
    t = CH

    def rowmap(cb):
        return lambda b, c: (rb0 + b * nch + _stored_chunk(c, nch), cb)

    def stmap(b, c):
        return (b, 0, 0)

    def full(shape):
        return pl.BlockSpec(shape, lambda b, c: (0,) * len(shape))

    in_specs = [
        pl.BlockSpec((t, SSD_DI), rowmap(C_Z // SSD_DI)),
        pl.BlockSpec((t, SSD_DI), rowmap(C_X // SSD_DI)),
        pl.BlockSpec((t, SSD_DI), rowmap(C_BC // SSD_DI)),
        pl.BlockSpec((t, LANE), rowmap(0)),
        pl.BlockSpec((t, SC_D), rowmap(C_SCB // SC_D)),
        pl.BlockSpec((t, SC_D), rowmap(C_SCC // SC_D)),
        pl.BlockSpec((t, SC_D), rowmap(C_SCH // SC_D)),
        pl.BlockSpec((1, SSD_K - 1, SSD_CD), stmap),
        pl.BlockSpec((1, SSD_N, SSD_DI), stmap),
        pl.BlockSpec((1, SC_K - 1, SC_D), stmap),
        full((SSD_K, SSD_CD)), full((1, SSD_CD)), full((1, LANE)), full((1, LANE)),
        full((1, SSD_DI)), full((1, SSD_DI)), full((SC_K, SC_D)),
        full((CH * (SSD_K - 1) + TAIL, TAIL + CH)),
    ]
    out_specs = [
        pl.BlockSpec((t, SSD_DI), rowmap(0)),
        pl.BlockSpec((t, SC_D), rowmap(0)),
        pl.BlockSpec((1, SSD_K - 1, SSD_CD), stmap),
        pl.BlockSpec((1, SSD_N, SSD_DI), stmap),
        pl.BlockSpec((1, SC_K - 1, SC_D), stmap),
    ]
    out_shape = [
        jax.ShapeDtypeStruct((r, SSD_DI), BF16),
        jax.ShapeDtypeStruct((r, SC_D), BF16),
        jax.ShapeDtypeStruct((nb, SSD_K - 1, SSD_CD), F32),
        jax.ShapeDtypeStruct((nb, SSD_N, SSD_DI), F32),
        jax.ShapeDtypeStruct((nb, SC_K - 1, SC_D), F32),
    ]
    args = [u, u, u, dtf, u, u, u, cprev, h0, scprev,
            lw["cw"], lw["cb"], lw["dtb"], lw["alog"], lw["dsk"], lw["ng"], lw["scw"],
            jnp.asarray(_shift_matrix(SSD_K), BF16)]
    kern = functools.partial(_mixer_kernel, nskip)
    aliases = {}
    if keep is not None:
        aliases = {len(args): 0, len(args) + 1: 1}
        args += list(keep)
        in_specs += [pl.BlockSpec(memory_space=pl.ANY)] * 2
        kern = functools.partial(_mixer_alias_kernel, nskip)
    return pl.pallas_call(
        kern,
        grid=(nb, nch),
        in_specs=in_specs, out_specs=out_specs, out_shape=out_shape,
        scratch_shapes=[pltpu.VMEM((TAIL, SSD_CD), F32), pltpu.VMEM((8 + t, SC_D), F32),
                        pltpu.VMEM((SSD_N, SSD_DI), F32)],
        input_output_aliases=aliases,
        compiler_params=_cp(2), name="mixer",
    )(*args)


def _mla_proj_kernel(blk_ref, tab_ref, qn_ref, kvn_ref, wa_ref, wb_ref, wk_ref, wv_ref,
                     q_ref, ckv_ref, kpe_ref, kcat_ref, v_ref):
    blk = blk_ref[...]
    cq = _rms(blk[:, :QL].astype(F32), qn_ref[...]).astype(BF16)
    ckv = _rms(blk[:, QL:QL + KVL].astype(F32), kvn_ref[...])
    ckv_ref[...] = ckv
    tab = tab_ref[...]
    tabr = pltpu.roll(tab, ROPE, 1)
    prod = blk[:, QL + KVL:QL + KVL + LANE].astype(F32) * tab
    ksum = prod + pltpu.roll(prod, ROPE, 1)
    lane = lax.broadcasted_iota(jnp.int32, ksum.shape, 1)
    kpe = jnp.where(lane < ROPE, ksum, 0.0)
    kpe_ref[...] = kpe
    qa = jnp.dot(cq, wa_ref[...], preferred_element_type=F32)
    qb = jnp.dot(cq, wb_ref[...], preferred_element_type=F32)
    ckvb = ckv.astype(BF16)
    kn = jnp.dot(ckvb, wk_ref[...], preferred_element_type=F32)
    v_ref[...] = jnp.dot(ckvb, wv_ref[...], preferred_element_type=F32).astype(BF16)
    kpeb = kpe.astype(BF16)
    tab_q = tab * QSCALE
    tabr_q = tabr * QSCALE
    for h in range(MLA_H):
        q_ref[:, h * HK:h * HK + NOPE] = (qa[:, h * HK:h * HK + NOPE] * QSCALE).astype(BF16)
        q_ref[:, h * HK + NOPE:(h + 1) * HK] = (
            qa[:, h * HK + NOPE:(h + 1) * HK] * tab_q + qb[:, h * LANE:(h + 1) * LANE] * tabr_q).astype(BF16)
        kcat_ref[:, h * HK:h * HK + NOPE] = kn[:, h * NOPE:(h + 1) * NOPE].astype(BF16)
        kcat_ref[:, h * HK + NOPE:(h + 1) * HK] = kpeb


def _mla_proj(u, tab, lw):
    r = u.shape[0]
    bm = _pick(r, (544, 512, 272, 256, 136, 128, 64))

    def full(shape):
        return pl.BlockSpec(shape, lambda i: (0,) * len(shape))

    def row(w):
        return pl.BlockSpec((bm, w), lambda i: (i, 0))

    return pl.pallas_call(
        _mla_proj_kernel,
        grid=(r // bm,),
        in_specs=[pl.BlockSpec((bm, MLA_W), lambda i: (i, C_MLA // MLA_W)), row(LANE),
                  full((1, QL)), full((1, KVL)), full((QL, MLA_H * HK)), full((QL, MLA_H * LANE)),
                  full((KVL, MLA_H * NOPE)), full((KVL, MLA_H * MLA_V))],
        out_specs=[row(MLA_H * HK), row(KVL), row(LANE), row(MLA_H * HK), row(MLA_H * MLA_V)],
        out_shape=[jax.ShapeDtypeStruct((r, MLA_H * HK), BF16), jax.ShapeDtypeStruct((r, KVL), F32),
                   jax.ShapeDtypeStruct((r, LANE), F32), jax.ShapeDtypeStruct((r, MLA_H * HK), BF16),
                   jax.ShapeDtypeStruct((r, MLA_H * MLA_V), BF16)],
        compiler_params=_cp(1), name="mla_proj",
    )(u, tab, lw["qn"], lw["kvn"], lw["wa"], lw["wb"], lw["wk"], lw["wv"])


def _flash_kernel(it_ref, jt_ref, q_ref, k_ref, v_ref, k0_ref, v0_ref, mask_ref, pad_ref, o_ref,
                  qt_s, m_s, l_s, acc_s, s_scr, p_scr, a_s):
    tq = q_ref.shape[0]
    tk = k_ref.shape[0]
    step = pl.program_id(0)
    i = it_ref[step]
    j = jt_ref[step]

    @pl.when(j == 0)
    def _init():
        for h in range(MLA_H):
            qt_s[h * HK:(h + 1) * HK, :] = q_ref[:, h * HK:(h + 1) * HK].T
        for h in range(MLA_H):
            s0 = jnp.dot(k0_ref[:, h * HK:(h + 1) * HK], qt_s[h * HK:(h + 1) * HK, :],
                         preferred_element_type=F32) + pad_ref[...]
            m0 = jnp.max(s0, axis=0, keepdims=True)
            p0 = jnp.exp2(s0 - m0)
            rows = slice(h * MLA_V, (h + 1) * MLA_V)
            m_s[h] = m0
            l_s[h] = jnp.sum(p0, axis=0, keepdims=True)
            acc_s[rows, :] = lax.dot_general(v0_ref[:, rows], p0.astype(BF16), TN, preferred_element_type=F32)

    def tile(masked):
        nkb = tk // CH

        def block(sref, kb):
            blk = sref[kb * CH:(kb + 1) * CH, :]
            if masked:
                blk = blk + mask_ref[kb * CH:(kb + 1) * CH, :]
            return blk

        def scores(h):
            s_scr[h % 2] = jnp.dot(k_ref[:, h * HK:(h + 1) * HK], qt_s[h * HK:(h + 1) * HK, :],
                                   preferred_element_type=F32)

        def softmax(h):
            sref = s_scr.at[h % 2]
            pref = p_scr.at[h % 2]
            mx = block(sref, 0).reshape(CH // 8, 8, tq).max(axis=0)
            for kb in range(1, nkb):
                mx = jnp.maximum(mx, block(sref, kb).reshape(CH // 8, 8, tq).max(axis=0))
            m_prev = m_s[h]
            m_new = jnp.maximum(m_prev, jnp.max(mx, axis=0, keepdims=True))
            alpha = jnp.exp2(m_prev - m_new)
            part = jnp.zeros((8, tq), F32)
            for kb in range(nkb):
                e = jnp.exp2(block(sref, kb) - m_new)
                part = part + jnp.sum(e.reshape(CH // 8, 8, tq), axis=0)
                pref[kb * CH:(kb + 1) * CH, :] = e.astype(BF16)
            l_s[h] = alpha * l_s[h] + jnp.sum(part, axis=0, keepdims=True)
            m_s[h] = m_new
            a_s[h % 2] = alpha

        def weighted_values(h):
            rows = slice(h * MLA_V, (h + 1) * MLA_V)
            acc_s[rows, :] = a_s[h % 2] * acc_s[rows, :] + lax.dot_general(
                v_ref[:, rows], p_scr[h % 2], TN, preferred_element_type=F32)

        scores(0)
        for h in range(MLA_H + 1):
            if h + 1 < MLA_H:
                scores(h + 1)
            if h >= 1:
                weighted_values(h - 1)
            if h < MLA_H:
                softmax(h)

    @pl.when(j < i)
    def _():
        tile(False)

    @pl.when(j == i)
    def _():
        tile(True)
        for h in range(MLA_H):
            rows = slice(h * MLA_V, (h + 1) * MLA_V)
            o_ref[:, rows] = (acc_s[rows, :] / l_s[h]).T.astype(BF16)


def _pad_mask(n):
    return jnp.asarray(np.where(np.arange(CH)[:, None] + 0 * np.arange(n)[None, :] < NSKIP, -np.inf, 0.0),
                       F32)


def _flash(q, kcat, v, seq):
    r = q.shape[0]
    tq = _pick(seq, (FLASH_TQ, 256, 128))
    nq = seq // tq
    c0 = seq // CH
    it = np.concatenate([np.full((i + 1,), i, np.int32) for i in range(nq)])
    jt = np.concatenate([np.arange(i + 1, dtype=np.int32) for i in range(nq)])
    kk = np.arange(tq)[:, None]
    qq = np.arange(tq)[None, :]
    diag = jnp.asarray(np.where(kk // CH > qq // CH, -np.inf, 0.0), F32)
    grid_spec = pltpu.PrefetchScalarGridSpec(
        num_scalar_prefetch=2,
        grid=(int(it.shape[0]),),
        in_specs=[pl.BlockSpec((tq, MLA_H * HK), lambda s, it, jt: (it[s], 0)),
                  pl.BlockSpec((tq, MLA_H * HK), lambda s, it, jt: (jt[s], 0)),
                  pl.BlockSpec((tq, MLA_H * MLA_V), lambda s, it, jt: (jt[s], 0)),
                  pl.BlockSpec((CH, MLA_H * HK), lambda s, it, jt: (c0, 0)),
                  pl.BlockSpec((CH, MLA_H * MLA_V), lambda s, it, jt: (c0, 0)),
                  pl.BlockSpec((tq, tq), lambda s, it, jt: (0, 0)),
                  pl.BlockSpec((CH, tq), lambda s, it, jt: (0, 0))],
        out_specs=pl.BlockSpec((tq, MLA_H * MLA_V), lambda s, it, jt: (it[s], 0)),
        scratch_shapes=[pltpu.VMEM((MLA_H * HK, tq), BF16),
                        pltpu.VMEM((MLA_H, 1, tq), F32), pltpu.VMEM((MLA_H, 1, tq), F32),
                        pltpu.VMEM((MLA_H * MLA_V, tq), F32),
                        pltpu.VMEM((2, tq, tq), F32), pltpu.VMEM((2, tq, tq), BF16),
                        pltpu.VMEM((2, 1, tq), F32)],
    )
    return pl.pallas_call(
        _flash_kernel,
        grid_spec=grid_spec,
        out_shape=jax.ShapeDtypeStruct((r, MLA_H * MLA_V), BF16),
        compiler_params=_cp(1), name="flash",
    )(jnp.asarray(it), jnp.asarray(jt), q, kcat, v, kcat, v, diag, _pad_mask(tq))


def _attn0_kernel(q_ref, k_ref, v_ref, keep_ref, o_ref):
    del keep_ref
    col = lax.broadcasted_iota(jnp.int32, (CH, CH), 1)
    for h in range(MLA_H):
        s = lax.dot_general(q_ref[:, h * HK:(h + 1) * HK], k_ref[:, h * HK:(h + 1) * HK], NT,
                            preferred_element_type=F32)
        s = jnp.where(col >= NSKIP, s, -jnp.inf)
        p = jnp.exp2(s - jnp.max(s, axis=-1, keepdims=True))
        o = jnp.dot(p.astype(BF16), v_ref[:, h * MLA_V:(h + 1) * MLA_V], preferred_element_type=F32)
        o_ref[:, h * MLA_V:(h + 1) * MLA_V] = (o / jnp.sum(p, axis=-1, keepdims=True)).astype(BF16)


def _attn0(q, kcat, v, c0, o_frames):
    def blk(w):
        return pl.BlockSpec((CH, w), lambda i: (c0, 0))

    return pl.pallas_call(
        _attn0_kernel,
        grid=(1,),
        in_specs=[blk(MLA_H * HK), blk(MLA_H * HK), blk(MLA_H * MLA_V), pl.BlockSpec(memory_space=pl.ANY)],
        out_specs=blk(MLA_H * MLA_V),
        out_shape=jax.ShapeDtypeStruct(o_frames.shape, BF16),
        input_output_aliases={3: 0},
        compiler_params=_cp(1), name="attn0",
    )(q, kcat, v, o_frames)


def _cached_attn_kernel(q_ref, cnew_ref, pnew_ref, ckv_ref, cpe_ref, wkt_ref, wv_ref, o_ref, call, pet):
    past = ckv_ref.shape[2]
    call[0:past, :] = ckv_ref[0, 0].astype(BF16)
    call[past:past + CH, :] = cnew_ref[...].astype(BF16)
    pet[0:ROPE, :] = cpe_ref[0, 0].astype(BF16)
    pet[ROPE:, :] = jnp.zeros((LANE - ROPE, past), BF16)
    qlat = []
    qpe = []
    for h in range(MLA_H):
        qn = q_ref[:, h * HK:h * HK + NOPE]
        qlat.append(jnp.dot(qn, wkt_ref[h], preferred_element_type=F32).astype(BF16))
        qpe.append(q_ref[:, h * HK + NOPE:(h + 1) * HK])
    qlat = jnp.concatenate(qlat, axis=0)
    qpe = jnp.concatenate(qpe, axis=0)
    s_pe = jnp.concatenate(
        [jnp.dot(qpe, pet[...], preferred_element_type=F32),
         lax.dot_general(qpe, pnew_ref[...].astype(BF16), NT, preferred_element_type=F32)], axis=1)
    s = lax.dot_general(qlat, call[...], NT, preferred_element_type=F32) + s_pe
    p = jnp.exp2(s - jnp.max(s, axis=-1, keepdims=True))
    p = p / jnp.sum(p, axis=-1, keepdims=True)
    olat = jnp.dot(p.astype(BF16), call[...], preferred_element_type=F32).astype(BF16)
    for h in range(MLA_H):
        o_ref[:, h * MLA_V:(h + 1) * MLA_V] = jnp.dot(
            olat[h * CH:(h + 1) * CH, :], wv_ref[h], preferred_element_type=F32).astype(BF16)


def _cached_attn_alias_kernel(q_ref, cnew_ref, pnew_ref, ckv_ref, cpe_ref, wkt_ref, wv_ref, keep_ref, o_ref,
                              call, pet):
    del keep_ref
    _cached_attn_kernel(q_ref, cnew_ref, pnew_ref, ckv_ref, cpe_ref, wkt_ref, wv_ref, o_ref, call, pet)


def _cached_attn(q, ckv, kpe, cache_kv, cache_pe_t, layer, rb0, lw, o_prompt):
    nb, past = cache_kv.shape[1], cache_kv.shape[2]

    def rowmap(b):
        return (rb0 + b, 0)

    return pl.pallas_call(
        _cached_attn_alias_kernel,
        grid=(nb,),
        in_specs=[pl.BlockSpec((CH, MLA_H * HK), rowmap),
                  pl.BlockSpec((CH, KVL), rowmap),
                  pl.BlockSpec((CH, LANE), rowmap),
                  pl.BlockSpec((1, 1, past, KVL), lambda b: (layer, b, 0, 0)),
                  pl.BlockSpec((1, 1, ROPE, past), lambda b: (layer, b, 0, 0)),
                  pl.BlockSpec((MLA_H, NOPE, KVL), lambda b: (0, 0, 0)),
                  pl.BlockSpec((MLA_H, KVL, MLA_V), lambda b: (0, 0, 0)),
                  pl.BlockSpec(memory_space=pl.ANY)],
        out_specs=pl.BlockSpec((CH, MLA_H * MLA_V), rowmap),
        out_shape=jax.ShapeDtypeStruct(o_prompt.shape, BF16),
        scratch_shapes=[pltpu.VMEM((past + CH, KVL), BF16), pltpu.VMEM((LANE, past), BF16)],
        input_output_aliases={7: 0},
        compiler_params=_cp(1), name="cached_attn",
    )(q, ckv, kpe, cache_kv, cache_pe_t, lw["wkt"], lw["wv3"], o_prompt)


TAIL = 16


def _shift_matrix(k, rows=CH):
    ns = k - 1
    pm = np.zeros((rows * ns + TAIL, TAIL + rows), np.float32)
    for d in range(1, ns + 1):
        for t in range(rows):
            if t - d >= 0:
                pm[(d - 1) * rows + t, TAIL + t - d] = 1.0
            else:
                r = ns + t - d
                pm[(d - 1) * rows + t, 3 * r:3 * r + 3] = 1.0
    for r in range(ns):
        pm[rows * ns + 3 * r, TAIL + rows - ns + r] = 1.0
    return pm


def _load_tail(tail_s, prev):
    hi = prev.astype(BF16).astype(F32)
    rest = prev - hi
    mid = rest.astype(BF16).astype(F32)
    lo = (rest - mid).astype(BF16).astype(F32)
    tail_s[...] = jnp.zeros(tail_s.shape, F32)
    for r in range(prev.shape[0]):
        tail_s[3 * r:3 * r + 1, :] = hi[r:r + 1]
        tail_s[3 * r + 1:3 * r + 2, :] = mid[r:r + 1]
        tail_s[3 * r + 2:3 * r + 3, :] = lo[r:r + 1]


def _delayed(pm_ref, tail_s, x, ns):
    t = x.shape[0]
    ext = jnp.concatenate([tail_s[...].astype(BF16), x], axis=0)
    out = jnp.dot(pm_ref[...], ext, preferred_element_type=F32)
    tail_s[...] = out[t * ns:, :]
    return [out[(d - 1) * t:d * t, :] for d in range(1, ns + 1)]


def _act_kernel(nskip, g_ref, u_ref, prev_ref, w_ref, pm_ref, a_ref, new_ref, tail_s):
    t = g_ref.shape[0]
    ns = FFN_K - 1
    c = pl.program_id(1)

    @pl.when(c == 0)
    def _():
        _load_tail(tail_s, prev_ref[0])

    rows = lax.broadcasted_iota(jnp.int32, (t, 1), 0) + c * t
    g = g_ref[...]
    g = jnp.where(rows >= nskip, g, jnp.zeros((), g.dtype))
    delayed = _delayed(pm_ref, tail_s, g, ns)
    conv = w_ref[ns:ns + 1, :] * g.astype(F32)
    for d in range(1, ns + 1):
        conv = conv + w_ref[ns - d:ns - d + 1, :] * delayed[d - 1]
    a_ref[...] = (_silu(conv) * u_ref[...].astype(F32)).astype(BF16)

    @pl.when(c == pl.num_programs(1) - 1)
    def _():
        for r in range(ns):
            new_ref[0, r:r + 1, :] = tail_s[3 * r:3 * r + 1, :]


def _act(gate, up, rb0, nb, nch, nskip, prev, w, act_in=None, t=CH):
    r, n = gate.shape

    def rowmap(b, c):
        return (rb0 + b * nch + c, 0)

    pm = jnp.asarray(_shift_matrix(FFN_K, t), BF16)
    args = [gate, up, prev, w, pm]
    in_specs = [pl.BlockSpec((t, n), rowmap), pl.BlockSpec((t, n), rowmap),
                pl.BlockSpec((1, FFN_K - 1, n), lambda b, c: (b, 0, 0)),
                pl.BlockSpec((FFN_K, n), lambda b, c: (0, 0)),
                pl.BlockSpec(pm.shape, lambda b, c: (0, 0))]
    kern = functools.partial(_act_kernel, nskip)
    aliases = {}
    if act_in is not None:
        aliases = {len(args): 0}
        args.append(act_in)
        in_specs.append(pl.BlockSpec(memory_space=pl.ANY))
        kern = functools.partial(_act_alias_kernel, nskip)
    return pl.pallas_call(
        kern,
        grid=(nb, nch),
        in_specs=in_specs,
        out_specs=[pl.BlockSpec((t, n), rowmap), pl.BlockSpec((1, FFN_K - 1, n), lambda b, c: (b, 0, 0))],
        out_shape=[jax.ShapeDtypeStruct((r, n), BF16), jax.ShapeDtypeStruct((nb, FFN_K - 1, n), F32)],
        scratch_shapes=[pltpu.VMEM((TAIL, n), F32)],
        input_output_aliases=aliases,
        compiler_params=_cp(2), name="act",
    )(*args)


def _act_alias_kernel(nskip, g_ref, u_ref, prev_ref, w_ref, pm_ref, keep_ref, a_ref, new_ref, tail_s):
    del keep_ref
    _act_kernel(nskip, g_ref, u_ref, prev_ref, w_ref, pm_ref, a_ref, new_ref, tail_s)


def _cast_cols_kernel(x_ref, o_ref):
    n = x_ref.shape[1]
    o_ref[:, :n] = x_ref[...].astype(BF16)
    if o_ref.shape[1] > n:
        o_ref[:, n:] = jnp.zeros((o_ref.shape[0], o_ref.shape[1] - n), BF16)


def _cast_pad_cols(w, layer, n_out):
    _, k, n = w.shape
    tr = _pick(k, (128, 64))
    return pl.pallas_call(
        _cast_cols_kernel,
        grid=(k // tr,),
        in_specs=[pl.BlockSpec((None, tr, n), lambda i: (layer, i, 0))],
        out_specs=pl.BlockSpec((tr, n_out), lambda i: (i, 0)),
        out_shape=jax.ShapeDtypeStruct((k, n_out), BF16),
        compiler_params=_cp(1), name="cast_cols",
    )(w)


def _cast_rows_kernel(n_full, x_ref, o_ref):
    @pl.when(pl.program_id(0) < n_full)
    def _():
        o_ref[...] = x_ref[...].astype(BF16)

    @pl.when(pl.program_id(0) >= n_full)
    def _():
        o_ref[...] = jnp.zeros(o_ref.shape, BF16)


def _cast_pad_rows(w, layer, k_out):
    _, k, n = w.shape
    tr = 256
    assert k % tr == 0 and k_out % tr == 0
    n_full = k // tr
    return pl.pallas_call(
        functools.partial(_cast_rows_kernel, n_full),
        grid=(k_out // tr,),
        in_specs=[pl.BlockSpec((None, tr, n), lambda i: (layer, jnp.minimum(i, n_full - 1), 0))],
        out_specs=pl.BlockSpec((tr, n), lambda i: (i, 0)),
        out_shape=jax.ShapeDtypeStruct((k_out, n), BF16),
        compiler_params=_cp(1), name="cast_rows",
    )(w)


PACK = 256


def _pack_win_kernel(a_ref, b_ref, o_ref):
    j = pl.program_id(0)
    o_dt = SSD_DI + SSD_CD
    n_plain = o_dt // PACK
    last = pl.num_programs(0) - 1
    half = ROPE // 2

    def emit(rows):
        o_ref[...] = rows.T.astype(BF16)

    @pl.when(j < n_plain)
    def _():
        emit(a_ref[...])

    @pl.when((j >= n_plain) & (j < last))
    def _():
        emit(jnp.concatenate([a_ref[SSD_H:, :], b_ref[:SSD_H, :]], axis=0))

    @pl.when(j == last)
    def _():
        kr0 = a_ref[SSD_H:SSD_H + half, :]
        kr1 = a_ref[SSD_H + half:SSD_H + ROPE, :]
        emit(jnp.concatenate([kr0, kr1, kr1, kr0, b_ref[:SSD_H, :],
                              jnp.zeros((PACK - 2 * ROPE - SSD_H, a_ref.shape[1]), F32)], axis=0))


def _pack_win(w_in, layer):
    _, k, n = w_in.shape
    assert (SSD_DI + SSD_CD) % PACK == 0 and NIN % PACK == 0 and n - (NIN - PACK) == SSD_H + ROPE
    wt = jnp.swapaxes(w_in, 1, 2)
    nblk = NIN // PACK
    n_plain = (SSD_DI + SSD_CD) // PACK

    def b_index(j):
        return jnp.where((j < n_plain) | (j == nblk - 1), n_plain, j + 1)

    return pl.pallas_call(
        _pack_win_kernel,
        grid=(nblk,),
        in_specs=[pl.BlockSpec((None, PACK, k), lambda j: (layer, j, 0)),
                  pl.BlockSpec((None, PACK, k), lambda j: (layer, b_index(j), 0))],
        out_specs=pl.BlockSpec((k, PACK), lambda j: (0, j)),
        out_shape=jax.ShapeDtypeStruct((k, NIN), BF16),
        compiler_params=_cp(1), name="pack_win",
    )(wt, wt)


def _swap_half(w):
    half = w.shape[-1] // 2
    return jnp.concatenate([w[..., half:], w[..., :half]], axis=-1)


def _layer_weights(i, w_in, ssd_conv_w, ssd_conv_b, ssd_dt_bias, ssd_a_log, ssd_d, ssd_norm, sc_conv_w,
                   mla_q_norm, mla_w_uq, mla_kv_norm, mla_w_ukv, w_out, ffn_w_gate, ffn_w_up, ffn_conv_w,
                   ffn_w_down):
    win = _pack_win(w_in, i)
    uq = mla_w_uq[i].reshape(QL, MLA_H, NOPE + ROPE)
    pe = uq[..., NOPE:]
    zq = jnp.zeros((QL, MLA_H, HK - NOPE - ROPE), F32)
    wa = jnp.concatenate([uq[..., :NOPE], pe, zq], axis=-1).reshape(QL, MLA_H * HK).astype(BF16)
    wb = jnp.concatenate([_swap_half(pe), zq], axis=-1).reshape(QL, MLA_H * LANE).astype(BF16)
    ukv = mla_w_ukv[i].reshape(KVL, MLA_H, NOPE + MLA_V)
    padf = ((0, 0), (0, DFP - D_FF))
    return dict(
        win=win,
        cw=ssd_conv_w[i], cb=ssd_conv_b[i].reshape(1, SSD_CD),
        dtb=jnp.pad(ssd_dt_bias[i], (0, LANE - SSD_H)).reshape(1, LANE),
        alog=jnp.pad(ssd_a_log[i], (0, LANE - SSD_H)).reshape(1, LANE),
        dsk=jnp.repeat(ssd_d[i], SSD_P).reshape(1, SSD_DI),
        ng=ssd_norm[i].reshape(1, SSD_DI),
        scw=sc_conv_w[i],
        qn=mla_q_norm[i].reshape(1, QL), kvn=mla_kv_norm[i].reshape(1, KVL),
        wa=wa, wb=wb,
        wk=ukv[..., :NOPE].reshape(KVL, MLA_H * NOPE).astype(BF16),
        wv=ukv[..., NOPE:].reshape(KVL, MLA_H * MLA_V).astype(BF16),
        wkt=jnp.transpose(ukv[..., :NOPE], (1, 2, 0)).astype(BF16),
        wv3=jnp.transpose(ukv[..., NOPE:], (1, 0, 2)).astype(BF16),
        wout=_cast_pad_cols(w_out, i, D_MODEL),
        wg=_cast_pad_cols(ffn_w_gate, i, DFP),
        wu=_cast_pad_cols(ffn_w_up, i, DFP),
        fcw=jnp.pad(ffn_conv_w[i], padf),
        wd=_cast_pad_rows(ffn_w_down, i, DFP),
    )


def _rope_table(seq, ns, past):
    half = ROPE // 2
    pos = jnp.concatenate([N_META + jnp.arange(seq, dtype=jnp.int32),
                           jnp.maximum(jnp.arange(CH, dtype=jnp.int32) - NSKIP, 0),
                           N_META + past + jnp.arange(ns, dtype=jnp.int32) % CH])
    inv = THETA ** (-jnp.arange(half, dtype=F32) / half)
    ang = pos.astype(F32)[:, None] * inv[None, :]
    cos, sin = jnp.cos(ang), jnp.sin(ang)
    return jnp.concatenate([cos, cos, -sin, sin], axis=1)


def kernel(x_prompt, x_sample, cache_kv_latent, cache_k_rope, state_ssm, state_ssd_conv, state_sconv, state_ffn_conv, meta_tokens, norm_mix_pre, norm_mix_post, norm_ffn_pre, norm_ffn_post, w_in, ssd_conv_w, ssd_conv_b, ssd_dt_bias, ssd_a_log, ssd_d, ssd_norm, sc_conv_w, mla_q_norm, mla_w_uq, mla_kv_norm, mla_w_ukv, w_out, ffn_w_gate, ffn_w_up, ffn_conv_w, ffn_w_down):
    bp, seq, d = x_prompt.shape
    nb, ls, _ = x_sample.shape
    depth, _, past, _ = cache_kv_latent.shape
    assert bp == 1 and ls == CH and seq % CH == 0 and d == D_MODEL
    lp = CH + seq
    ns = nb * ls
    npc = lp // CH
    c0 = seq // CH

    x, h = _assemble(x_prompt[0], x_sample.reshape(ns, d), meta_tokens.astype(F32), norm_mix_pre[0])
    tab = _rope_table(seq, ns, past)
    cache_pe_t = jnp.swapaxes(cache_k_rope, 2, 3)
    zero_c = jnp.zeros((1, SSD_K - 1, SSD_CD), F32)
    zero_h = jnp.zeros((1, SSD_N, SSD_DI), F32)

    def state_in(s):
        return jnp.transpose(s, (0, 3, 1, 2)).reshape(s.shape[0], SSD_N, SSD_DI)

    def state_out(s):
        return jnp.transpose(s.reshape(s.shape[0], SSD_N, SSD_H, SSD_P), (0, 2, 3, 1))
    zero_s = jnp.zeros((1, SC_K - 1, SC_D), F32)
    zero_f = jnp.zeros((1, FFN_K - 1, DFP), F32)
    padf = ((0, 0), (0, 0), (0, DFP - D_FF))

    outs_p, outs_s = [], []
    for i in range(depth):
        lw = _layer_weights(i, w_in, ssd_conv_w, ssd_conv_b, ssd_dt_bias, ssd_a_log, ssd_d, ssd_norm,
                            sc_conv_w, mla_q_norm, mla_w_uq, mla_kv_norm, mla_w_ukv, w_out, ffn_w_gate,
                            ffn_w_up, ffn_conv_w, ffn_w_down)
        u, dtf = _inproj(h, lw["win"])

        yp, yscp, cnew_p, hnew_p, scnew_p = _mixer(u, dtf, 0, 1, npc, NSKIP, zero_c, zero_h, zero_s, lw)
        y_ssd, y_sc, cnew_s, hnew_s, scnew_s = _mixer(u, dtf, npc, nb, 1, 0, state_ssd_conv[i],
                                                      state_in(state_ssm[i]), state_sconv[i], lw,
                                                      keep=(yp, yscp))
        hnew_p, hnew_s = state_out(hnew_p), state_out(hnew_s)

        q, ckv, kpe, kcat, v = _mla_proj(u, tab, lw)
        y_mla = _attn0(q, kcat, v, c0, _flash(q, kcat, v, seq))
        y_mla = _cached_attn(q, ckv, kpe, cache_kv_latent, cache_pe_t, i, npc, lw, y_mla)

        mix = _outproj(y_ssd, y_sc, y_mla, lw["wout"])
        x, h = _resid_norm(x, mix, norm_mix_post[i], norm_ffn_pre[i])

        gate, up = _gateup(h, lw["wg"], lw["wu"])
        act, f0 = _act(gate, up, c0, 1, 1, NSKIP, zero_f, lw["fcw"])
        act, fnew_p = _act(gate, up, 0, 1, seq // ACT_ROWS, 0, f0, lw["fcw"], act_in=act, t=ACT_ROWS)
        act, fnew_s = _act(gate, up, npc, nb, 1, 0, jnp.pad(state_ffn_conv[i], padf), lw["fcw"], act_in=act)
        f = _down(act, lw["wd"])
        if i + 1 < depth:
            x, h = _resid_norm(x, f, norm_ffn_post[i], norm_mix_pre[i + 1])
        else:
            y_prompt = _resid_out(x, f, norm_ffn_post[i], 0, seq)
            y_sample = _resid_out(x, f, norm_ffn_post[i], lp, ns)

        meta_rows = slice(seq + NSKIP, lp)
        outs_p.append((jnp.concatenate([ckv[meta_rows], ckv[:seq]], axis=0)[None],
                       jnp.concatenate([kpe[meta_rows, :ROPE], kpe[:seq, :ROPE]], axis=0)[None],
                       hnew_p, cnew_p, scnew_p, fnew_p[:, :, :D_FF]))
        outs_s.append((ckv[lp:].reshape(nb, ls, KVL), kpe[lp:, :ROPE].reshape(nb, ls, ROPE), hnew_s, cnew_s,
                       scnew_s, fnew_s[:, :, :D_FF]))

    def stack(outs, j):
        return jnp.stack([o[j] for o in outs], axis=0)

    return (y_prompt[None], y_sample.reshape(nb, ls, d),
            stack(outs_p, 0), stack(outs_p, 1), stack(outs_p, 2), stack(outs_p, 3), stack(outs_p, 4), stack(outs_p, 5),
            stack(outs_s, 0), stack(outs_s, 1), stack(outs_s, 2), stack(outs_s, 3), stack(outs_s, 4), stack(outs_s, 5))
```

```python
import functools

import jax
import jax.numpy as jnp
import numpy as np
from jax import lax
from jax.experimental import pallas as pl
from jax.experimental.pallas import tpu as pltpu

F32 = jnp.float32
BF16 = jnp.bfloat16

D_MODEL = 4096
N_META = 16
CH = 64
NSKIP = CH - N_META
EPS = 1e-6
SSD_P = 64
SSD_DI = 2048
SSD_H = 32
SSD_G = 8
SSD_N = 128
SSD_K = 4
SSD_CD = 4096
SC_D = 1024
SC_K = 3
MLA_H = 8
NOPE = 128
ROPE = 64
MLA_V = 128
QL = 768
KVL = 512
HK = 256
SCALE = (NOPE + ROPE) ** -0.5
QSCALE = SCALE * 1.4426950408889634
THETA = 10000.0
D_FF = 11008
DFP = 11264
FFN_K = 3
LANE = 128

C_Z, C_X, C_BC, C_SCB, C_SCC, C_SCH, C_MLA, NIN = 0, 2048, 4096, 6144, 7168, 8192, 9216, 10752
MLA_W = 1536

FLASH_TQ = 512
ACT_ROWS = 128
VMEM_LIMIT = 56 * 1024 * 1024
HI = lax.Precision.HIGHEST
NT = (((1,), (1,)), ((), ()))
TN = (((0,), (0,)), ((), ()))


def _cp(n, flags=None):
    return pltpu.CompilerParams(dimension_semantics=("arbitrary",) * n, vmem_limit_bytes=VMEM_LIMIT, flags=flags)


def _pick(n, cands):
    for c in cands:
        if n % c == 0:
            return c
    raise ValueError(f"no tile for {n}")


MM_ROWS = (1088, 544, 512, 272, 256, 136, 128, 64)
EW_ROWS = (272, 256, 136, 128, 64)


def _stored_chunk(c, nch):
    return (c + nch - 1) % nch


def _rms(x, g):
    return x * lax.rsqrt(jnp.mean(x * x, axis=-1, keepdims=True) + EPS) * g


def _silu(x):
    return x * jax.nn.sigmoid(x)


def _norm_rows_kernel(x_in_ref, g_ref, x_ref, h_ref):
    x = x_in_ref[...]
    x_ref[...] = x
    h_ref[...] = _rms(x, g_ref[...]).astype(BF16)


def _tail_rows_kernel(xs_ref, meta_ref, g_ref, keep_x, keep_h, x_ref, h_ref):
    del keep_x, keep_h
    c = pl.program_id(0)

    def emit(x):
        x_ref[...] = x
        h_ref[...] = _rms(x, g_ref[...]).astype(BF16)

    @pl.when(c == 0)
    def _():
        emit(jnp.concatenate([jnp.zeros((NSKIP, x_ref.shape[1]), F32), meta_ref[...]], axis=0))

    @pl.when(c > 0)
    def _():
        emit(xs_ref[...])


def _assemble(x_prompt, x_sample, meta, g):
    seq, d = x_prompt.shape
    ns = x_sample.shape[0]
    r = seq + CH + ns
    bm = _pick(seq, (512, 256, 128, 64))
    vec = pl.BlockSpec((1, d), lambda c: (0, 0))
    out_shape = [jax.ShapeDtypeStruct((r, d), F32), jax.ShapeDtypeStruct((r, d), BF16)]
    big = pl.BlockSpec((bm, d), lambda c: (c, 0))
    x, h = pl.pallas_call(
        _norm_rows_kernel, grid=(seq // bm,), in_specs=[big, vec], out_specs=[big, big],
        out_shape=out_shape, compiler_params=_cp(1), name="assemble_frames",
    )(x_prompt, g.reshape(1, d))
    c0 = seq // CH
    small = pl.BlockSpec((CH, d), lambda c: (c0 + c, 0))
    anyspec = pl.BlockSpec(memory_space=pl.ANY)
    return pl.pallas_call(
        _tail_rows_kernel, grid=(1 + ns // CH,),
        in_specs=[pl.BlockSpec((CH, d), lambda c: (jnp.maximum(c - 1, 0), 0)),
                  pl.BlockSpec((N_META, d), lambda c: (0, 0)), vec, anyspec, anyspec],
        out_specs=[small, small], out_shape=out_shape, input_output_aliases={3: 0, 4: 1},
        compiler_params=_cp(1), name="assemble_tail",
    )(x_sample, meta, g.reshape(1, d), x, h)


def _resid_out_kernel(x_ref, m_ref, gp_ref, y_ref):
    y_ref[...] = x_ref[...] + _rms(m_ref[...].astype(F32), gp_ref[...])


def _resid_out(x, m, g_post, row0, rows):
    d = x.shape[1]
    bm = next(b for b in (512, 256, 128, 64) if rows % b == 0 and row0 % b == 0)
    src = pl.BlockSpec((bm, d), lambda c: (row0 // bm + c, 0))
    return pl.pallas_call(
        _resid_out_kernel, grid=(rows // bm,),
        in_specs=[src, src, pl.BlockSpec((1, d), lambda c: (0, 0))],
        out_specs=pl.BlockSpec((bm, d), lambda c: (c, 0)),
        out_shape=jax.ShapeDtypeStruct((rows, d), F32),
        compiler_params=_cp(1), name="resid_out",
    )(x, m, g_post.reshape(1, d))


def _resid_norm_kernel(x_ref, m_ref, gp_ref, gn_ref, x2_ref, h_ref):
    x2 = x_ref[...] + _rms(m_ref[...].astype(F32), gp_ref[...])
    x2_ref[...] = x2
    h_ref[...] = _rms(x2, gn_ref[...]).astype(BF16)


def _resid_norm(x, m, g_post, g_next):
    r, d = x.shape
    bm = _pick(r, EW_ROWS)
    row = pl.BlockSpec((bm, d), lambda i: (i, 0))
    vec = pl.BlockSpec((1, d), lambda i: (0, 0))
    return pl.pallas_call(
        _resid_norm_kernel, grid=(r // bm,), in_specs=[row, row, vec, vec], out_specs=[row, row],
        out_shape=[jax.ShapeDtypeStruct((r, d), F32), jax.ShapeDtypeStruct((r, d), BF16)],
        compiler_params=_cp(1), name="resid_norm",
    )(x, m, g_post.reshape(1, d), g_next.reshape(1, d))


def _inproj_kernel(x_ref, w_ref, u_ref, dt_ref):
    acc = jnp.dot(x_ref[...], w_ref[...], preferred_element_type=F32)
    u_ref[...] = acc.astype(BF16)

    @pl.when(pl.program_id(1) == pl.num_programs(1) - 1)
    def _():
        dt_ref[...] = acc[:, acc.shape[1] - LANE:]


def _inproj(h, w):
    r, k = h.shape
    n = w.shape[1]
    bm = _pick(r, MM_ROWS)
    bn = 768
    return pl.pallas_call(
        _inproj_kernel,
        grid=(r // bm, n // bn),
        in_specs=[pl.BlockSpec((bm, k), lambda i, j: (i, 0)), pl.BlockSpec((k, bn), lambda i, j: (0, j))],
        out_specs=[pl.BlockSpec((bm, bn), lambda i, j: (i, j)), pl.BlockSpec((bm, LANE), lambda i, j: (i, 0))],
        out_shape=[jax.ShapeDtypeStruct((r, n), BF16), jax.ShapeDtypeStruct((r, LANE), F32)],
        compiler_params=_cp(2), name="inproj",
    )(h, w)


def _outproj_kernel(a_ref, b_ref, c_ref, wa_ref, wb_ref, wc_ref, o_ref):
    acc = jnp.dot(a_ref[...], wa_ref[...], preferred_element_type=F32)
    acc = acc + jnp.dot(b_ref[...], wb_ref[...], preferred_element_type=F32)
    acc = acc + jnp.dot(c_ref[...], wc_ref[...], preferred_element_type=F32)
    o_ref[...] = acc.astype(o_ref.dtype)


def _outproj(y_ssd, y_sc, y_mla, w):
    r = y_ssd.shape[0]
    n = w.shape[1]
    bm = _pick(r, MM_ROWS)
    bn = 1024
    return pl.pallas_call(
        _outproj_kernel,
        grid=(r // bm, n // bn),
        in_specs=[pl.BlockSpec((bm, SSD_DI), lambda i, j: (i, 0)),
                  pl.BlockSpec((bm, SC_D), lambda i, j: (i, 0)),
                  pl.BlockSpec((bm, SC_D), lambda i, j: (i, 0)),
                  pl.BlockSpec((SSD_DI, bn), lambda i, j: (0, j)),
                  pl.BlockSpec((SC_D, bn), lambda i, j: (2, j)),
                  pl.BlockSpec((SC_D, bn), lambda i, j: (3, j))],
        out_specs=pl.BlockSpec((bm, bn), lambda i, j: (i, j)),
        out_shape=jax.ShapeDtypeStruct((r, n), BF16),
        compiler_params=_cp(2), name="outproj",
    )(y_ssd, y_sc, y_mla, w, w, w)


def _gateup_kernel(x_ref, wg_ref, wu_ref, g_ref, u_ref):
    x = x_ref[...]
    g_ref[...] = jnp.dot(x, wg_ref[...], preferred_element_type=F32).astype(BF16)
    u_ref[...] = jnp.dot(x, wu_ref[...], preferred_element_type=F32).astype(BF16)


def _gateup(h, wg, wu):
    r, k = h.shape
    n = wg.shape[1]
    bm = _pick(r, MM_ROWS)
    bn = 512
    wspec = pl.BlockSpec((k, bn), lambda i, j: (0, j))
    ospec = pl.BlockSpec((bm, bn), lambda i, j: (i, j))
    return pl.pallas_call(
        _gateup_kernel,
        grid=(r // bm, n // bn),
        in_specs=[pl.BlockSpec((bm, k), lambda i, j: (i, 0)), wspec, wspec],
        out_specs=[ospec, ospec],
        out_shape=[jax.ShapeDtypeStruct((r, n), BF16)] * 2,
        compiler_params=_cp(2), name="gateup",
    )(h, wg, wu)


def _down_kernel(x_ref, w_ref, o_ref):
    o_ref[...] = jnp.dot(x_ref[...], w_ref[...], preferred_element_type=F32).astype(o_ref.dtype)


def _down(a, w):
    r, k = a.shape
    n = w.shape[1]
    bm = _pick(r, MM_ROWS[1:])
    bn = 512
    return pl.pallas_call(
        _down_kernel,
        grid=(r // bm, n // bn),
        in_specs=[pl.BlockSpec((bm, k), lambda i, j: (i, 0)), pl.BlockSpec((k, bn), lambda i, j: (0, j))],
        out_specs=pl.BlockSpec((bm, bn), lambda i, j: (i, j)),
        out_shape=jax.ShapeDtypeStruct((r, n), BF16),
        compiler_params=_cp(2), name="down",
    )(a, w)


def _mixer_kernel(nskip, z_ref, x_ref, bc_ref, dt_ref, scb_ref, scc_ref, sch_ref,
                  cprev_ref, h0_ref, scprev_ref,
                  cw_ref, cb_ref, dtb_ref, alog_ref, dsk_ref, ng_ref, scw_ref, pm_ref,
                  y_ref, ysc_ref, cnew_ref, hnew_ref, scnew_ref,
                  buf, sbuf, hst):
    t = CH
    c = pl.program_id(1)
    last = pl.num_programs(1) - 1

    ns = SSD_K - 1

    @pl.when(c == 0)
    def _init():
        _load_tail(buf, cprev_ref[0])
        sbuf[0:8, :] = jnp.zeros((8, SC_D), F32)
        sbuf[8 - (SC_K - 1):8, :] = scprev_ref[0]
        hst[...] = h0_ref[0]

    rows = lax.broadcasted_iota(jnp.int32, (t, 1), 0) + c * t
    valid = rows >= nskip

    xin = jnp.concatenate([x_ref[...], bc_ref[...]], axis=1)
    xin = jnp.where(valid, xin, jnp.zeros((), xin.dtype))
    delayed = _delayed(pm_ref, buf, xin, ns)
    acc = cb_ref[...] + cw_ref[ns:ns + 1, :] * xin.astype(F32)
    for d in range(1, ns + 1):
        acc = acc + cw_ref[ns - d:ns - d + 1, :] * delayed[d - 1]
    xbc = _silu(acc)

    lane = lax.broadcasted_iota(jnp.int32, (t, LANE), 1)
    v = dt_ref[...] + dtb_ref[...]
    dt = jnp.maximum(v, 0.0) + jnp.log1p(jnp.exp(-jnp.abs(v)))
    dt = jnp.where(valid & (lane < SSD_H), dt, 0.0)
    adt = dt * (-jnp.exp(alog_ref[...]))
    ri = lax.broadcasted_iota(jnp.int32, (t, t), 0)
    ci = lax.broadcasted_iota(jnp.int32, (t, t), 1)
    tri = ri >= ci
    acs = jnp.dot(tri.astype(F32), adt, precision=HI, preferred_element_type=F32)

    hpg = SSD_H // SSD_G
    gw = hpg * SSD_P
    lane_g = lax.broadcasted_iota(jnp.int32, (t, gw), 1)
    row_g = lax.broadcasted_iota(jnp.int32, (t, gw), 0)
    seg = lane_g // SSD_P
    eye_g = row_g == lane_g % SSD_P
    tri_g = row_g >= lane_g % SSD_P
    blockdiag = (lax.broadcasted_iota(jnp.int32, (hpg * t, gw), 0) // t
                 == lax.broadcasted_iota(jnp.int32, (hpg * t, gw), 1) // SSD_P)

    def per_head_lanes(mat, g):
        out = jnp.broadcast_to(mat[:, g * hpg + hpg - 1:g * hpg + hpg], (t, gw))
        for r in range(hpg - 2, -1, -1):
            out = jnp.where(seg == r, jnp.broadcast_to(mat[:, g * hpg + r:g * hpg + r + 1], (t, gw)), out)
        return out

    for g in range(SSD_G):
        cols = slice(g * gw, (g + 1) * gw)
        bg = xbc[:, SSD_DI + g * SSD_N:SSD_DI + (g + 1) * SSD_N].astype(BF16)
        cg = xbc[:, SSD_DI + SSD_G * SSD_N + g * SSD_N:SSD_DI + SSD_G * SSD_N + (g + 1) * SSD_N].astype(BF16)
        acol = per_head_lanes(acs, g)
        dtx = per_head_lanes(dt, g)
        arow = jnp.sum(jnp.where(eye_g, acol, 0.0), axis=0, keepdims=True)
        decay = jnp.exp(jnp.where(tri_g, acol - arow, -jnp.inf))
        cb = lax.dot_general(cg, jnp.concatenate([bg] * hpg, axis=0), NT, preferred_element_type=F32)
        xg = xbc[:, cols]
        xdt = xg * dtx
        xdtb = xdt.astype(BF16)
        rhs = jnp.where(blockdiag, jnp.concatenate([xdtb] * hpg, axis=0), jnp.zeros((), BF16))
        ydiag = jnp.dot((cb * decay).astype(BF16), rhs, preferred_element_type=F32)
        hg = hst[:, cols]
        yoff = jnp.dot(cg, hg.astype(BF16), preferred_element_type=F32) * jnp.exp(acol)
        alast = acol[t - 1:t, :]
        snew = lax.dot_general(bg, (xdt * jnp.exp(alast - acol)).astype(BF16), TN, preferred_element_type=F32)
        hst[:, cols] = jnp.exp(alast) * hg + snew
        y = ydiag + yoff + dsk_ref[:, cols] * xg
        y = y * _silu(z_ref[:, cols].astype(F32))
        y_ref[:, cols] = _rms(y, ng_ref[:, cols]).astype(BF16)

    p = scc_ref[...].astype(F32) * sch_ref[...].astype(F32)
    p = jnp.where(valid, p, 0.0)
    sbuf[8:8 + t, :] = p
    conv = scw_ref[0:1, :] * sbuf[6:6 + t, :]
    for i in range(1, SC_K):
        conv = conv + scw_ref[i:i + 1, :] * sbuf[6 + i:6 + i + t, :]
    ysc_ref[...] = (scb_ref[...].astype(F32) * conv).astype(BF16)
    sbuf[0:8, :] = sbuf[t:t + 8, :]

    @pl.when(c == last)
    def _():
        for r in range(ns):
            cnew_ref[0, r:r + 1, :] = buf[3 * r:3 * r + 1, :]
        hnew_ref[0] = hst[...]
        scnew_ref[0] = sbuf[8 - (SC_K - 1):8, :]


def _mixer_alias_kernel(nskip, *refs):
    n_in = 18
    _mixer_kernel(nskip, *refs[:n_in], *refs[n_in + 2:])


def _mixer(u, dtf, rb0, nb, nch, nskip, cprev, h0, scprev, lw, keep=None):
    r = u.shape[0]
    t = CH

    def rowmap(cb):
        return lambda b, c: (rb0 + b * nch + _stored_chunk(c, nch), cb)

    def stmap(b, c):
        return (b, 0, 0)

    def full(shape):
        return pl.BlockSpec(shape, lambda b, c: (0,) * len(shape))

    in_specs = [
        pl.BlockSpec((t, SSD_DI), rowmap(C_Z // SSD_DI)),
        pl.BlockSpec((t, SSD_DI), rowmap(C_X // SSD_DI)),
        pl.BlockSpec((t, SSD_DI), rowmap(C_BC // SSD_DI)),
        pl.BlockSpec((t, LANE), rowmap(0)),
        pl.BlockSpec((t, SC_D), rowmap(C_SCB // SC_D)),
        pl.BlockSpec((t, SC_D), rowmap(C_SCC // SC_D)),
        pl.BlockSpec((t, SC_D), rowmap(C_SCH // SC_D)),
        pl.BlockSpec((1, SSD_K - 1, SSD_CD), stmap),
        pl.BlockSpec((1, SSD_N, SSD_DI), stmap),
        pl.BlockSpec((1, SC_K - 1, SC_D), stmap),
        full((SSD_K, SSD_CD)), full((1, SSD_CD)), full((1, LANE)), full((1, LANE)),
        full((1, SSD_DI)), full((1, SSD_DI)), full((SC_K, SC_D)),
        full((CH * (SSD_K - 1) + TAIL, TAIL + CH)),
    ]
    out_specs = [
        pl.BlockSpec((t, SSD_DI), rowmap(0)),
        pl.BlockSpec((t, SC_D), rowmap(0)),
        pl.BlockSpec((1, SSD_K - 1, SSD_CD), stmap),
        pl.BlockSpec((1, SSD_N, SSD_DI), stmap),
        pl.BlockSpec((1, SC_K - 1, SC_D), stmap),
    ]
    out_shape = [
        jax.ShapeDtypeStruct((r, SSD_DI), BF16),
        jax.ShapeDtypeStruct((r, SC_D), BF16),
        jax.ShapeDtypeStruct((nb, SSD_K - 1, SSD_CD), F32),
        jax.ShapeDtypeStruct((nb, SSD_N, SSD_DI), F32),
        jax.ShapeDtypeStruct((nb, SC_K - 1, SC_D), F32),
    ]
    args = [u, u, u, dtf, u, u, u, cprev, h0, scprev,
            lw["cw"], lw["cb"], lw["dtb"], lw["alog"], lw["dsk"], lw["ng"], lw["scw"],
            jnp.asarray(_shift_matrix(SSD_K), BF16)]
    kern = functools.partial(_mixer_kernel, nskip)
    aliases = {}
    if keep is not None:
        aliases = {len(args): 0, len(args) + 1: 1}
        args += list(keep)
        in_specs += [pl.BlockSpec(memory_space=pl.ANY)] * 2
        kern = functools.partial(_mixer_alias_kernel, nskip)
    return pl.pallas_call(
        kern,
        grid=(nb, nch),
        in_specs=in_specs, out_specs=out_specs, out_shape=out_shape,
        scratch_shapes=[pltpu.VMEM((TAIL, SSD_CD), F32), pltpu.VMEM((8 + t, SC_D), F32),
                        pltpu.VMEM((SSD_N, SSD_DI), F32)],
        input_output_aliases=aliases,
        compiler_params=_cp(2), name="mixer",
    )(*args)


def _mla_proj_kernel(blk_ref, tab_ref, qn_ref, kvn_ref, wa_ref, wb_ref, wk_ref, wv_ref,
                     q_ref, ckv_ref, kpe_ref, kcat_ref, v_ref):
    blk = blk_ref[...]
    cq = _rms(blk[:, :QL].astype(F32), qn_ref[...]).astype(BF16)
    ckv = _rms(blk[:, QL:QL + KVL].astype(F32), kvn_ref[...])
    ckv_ref[...] = ckv
    tab = tab_ref[...]
    tabr = pltpu.roll(tab, ROPE, 1)
    prod = blk[:, QL + KVL:QL + KVL + LANE].astype(F32) * tab
    ksum = prod + pltpu.roll(prod, ROPE, 1)
    lane = lax.broadcasted_iota(jnp.int32, ksum.shape, 1)
    kpe = jnp.where(lane < ROPE, ksum, 0.0)
    kpe_ref[...] = kpe
    qa = jnp.dot(cq, wa_ref[...], preferred_element_type=F32)
    qb = jnp.dot(cq, wb_ref[...], preferred_element_type=F32)
    ckvb = ckv.astype(BF16)
    kn = jnp.dot(ckvb, wk_ref[...], preferred_element_type=F32)
    v_ref[...] = jnp.dot(ckvb, wv_ref[...], preferred_element_type=F32).astype(BF16)
    kpeb = kpe.astype(BF16)
    tab_q = tab * QSCALE
    tabr_q = tabr * QSCALE
    for h in range(MLA_H):
        q_ref[:, h * HK:h * HK + NOPE] = (qa[:, h * HK:h * HK + NOPE] * QSCALE).astype(BF16)
        q_ref[:, h * HK + NOPE:(h + 1) * HK] = (
            qa[:, h * HK + NOPE:(h + 1) * HK] * tab_q + qb[:, h * LANE:(h + 1) * LANE] * tabr_q).astype(BF16)
        kcat_ref[:, h * HK:h * HK + NOPE] = kn[:, h * NOPE:(h + 1) * NOPE].astype(BF16)
        kcat_ref[:, h * HK + NOPE:(h + 1) * HK] = kpeb


def _mla_proj(u, tab, lw):
    r = u.shape[0]
    bm = _pick(r, (544, 512, 272, 256, 136, 128, 64))

    def full(shape):
        return pl.BlockSpec(shape, lambda i: (0,) * len(shape))

    def row(w):
        return pl.BlockSpec((bm, w), lambda i: (i, 0))

    return pl.pallas_call(
        _mla_proj_kernel,
        grid=(r // bm,),
        in_specs=[pl.BlockSpec((bm, MLA_W), lambda i: (i, C_MLA // MLA_W)), row(LANE),
                  full((1, QL)), full((1, KVL)), full((QL, MLA_H * HK)), full((QL, MLA_H * LANE)),
                  full((KVL, MLA_H * NOPE)), full((KVL, MLA_H * MLA_V))],
        out_specs=[row(MLA_H * HK), row(KVL), row(LANE), row(MLA_H * HK), row(MLA_H * MLA_V)],
        out_shape=[jax.ShapeDtypeStruct((r, MLA_H * HK), BF16), jax.ShapeDtypeStruct((r, KVL), F32),
                   jax.ShapeDtypeStruct((r, LANE), F32), jax.ShapeDtypeStruct((r, MLA_H * HK), BF16),
                   jax.ShapeDtypeStruct((r, MLA_H * MLA_V), BF16)],
        compiler_params=_cp(1), name="mla_proj",
    )(u, tab, lw["qn"], lw["kvn"], lw["wa"], lw["wb"], lw["wk"], lw["wv"])


def _flash_kernel(it_ref, jt_ref, q_ref, k_ref, v_ref, k0_ref, v0_ref, mask_ref, pad_ref, o_ref,
                  qt_s, m_s, l_s, acc_s, s_scr, p_scr, a_s):
    tq = q_ref.shape[0]
    tk = k_ref.shape[0]
    step = pl.program_id(0)
    i = it_ref[step]
    j = jt_ref[step]

    @pl.when(j == 0)
    def _init():
        for h in range(MLA_H):
            qt_s[h * HK:(h + 1) * HK, :] = q_ref[:, h * HK:(h + 1) * HK].T
        for h in range(MLA_H):
            s0 = jnp.dot(k0_ref[:, h * HK:(h + 1) * HK], qt_s[h * HK:(h + 1) * HK, :],
                         preferred_element_type=F32) + pad_ref[...]
            m0 = jnp.max(s0, axis=0, keepdims=True)
            p0 = jnp.exp2(s0 - m0)
            rows = slice(h * MLA_V, (h + 1) * MLA_V)
            m_s[h] = m0
            l_s[h] = jnp.sum(p0, axis=0, keepdims=True)
            acc_s[rows, :] = lax.dot_general(v0_ref[:, rows], p0.astype(BF16), TN, preferred_element_type=F32)

    def tile(masked):
        nkb = tk // CH

        def block(sref, kb):
            blk = sref[kb * CH:(kb + 1) * CH, :]
            if masked:
                blk = blk + mask_ref[kb * CH:(kb + 1) * CH, :]
            return blk

        def scores(h):
            s_scr[h % 2] = jnp.dot(k_ref[:, h * HK:(h + 1) * HK], qt_s[h * HK:(h + 1) * HK, :],
                                   preferred_element_type=F32)

        def softmax(h):
            sref = s_scr.at[h % 2]
            pref = p_scr.at[h % 2]
            mx = block(sref, 0).reshape(CH // 8, 8, tq).max(axis=0)
            for kb in range(1, nkb):
                mx = jnp.maximum(mx, block(sref, kb).reshape(CH // 8, 8, tq).max(axis=0))
            m_prev = m_s[h]
            m_new = jnp.maximum(m_prev, jnp.max(mx, axis=0, keepdims=True))
            alpha = jnp.exp2(m_prev - m_new)
            part = jnp.zeros((8, tq), F32)
            for kb in range(nkb):
                e = jnp.exp2(block(sref, kb) - m_new)
                part = part + jnp.sum(e.reshape(CH // 8, 8, tq), axis=0)
                pref[kb * CH:(kb + 1) * CH, :] = e.astype(BF16)
            l_s[h] = alpha * l_s[h] + jnp.sum(part, axis=0, keepdims=True)
            m_s[h] = m_new
            a_s[h % 2] = alpha

        def weighted_values(h):
            rows = slice(h * MLA_V, (h + 1) * MLA_V)
            acc_s[rows, :] = a_s[h % 2] * acc_s[rows, :] + lax.dot_general(
                v_ref[:, rows], p_scr[h % 2], TN, preferred_element_type=F32)

        scores(0)
        for h in range(MLA_H + 1):
            if h + 1 < MLA_H:
                scores(h + 1)
            if h >= 1:
                weighted_values(h - 1)
            if h < MLA_H:
                softmax(h)

    @pl.when(j < i)
    def _():
        tile(False)

    @pl.when(j == i)
    def _():
        tile(True)
        for h in range(MLA_H):
            rows = slice(h * MLA_V, (h + 1) * MLA_V)
            o_ref[:, rows] = (acc_s[rows, :] / l_s[h]).T.astype(BF16)


def _pad_mask(n):
    return jnp.asarray(np.where(np.arange(CH)[:, None] + 0 * np.arange(n)[None, :] < NSKIP, -np.inf, 0.0),
                       F32)


def _flash(q, kcat, v, seq):
    r = q.shape[0]
    tq = _pick(seq, (FLASH_TQ, 256, 128))
    nq = seq // tq
    c0 = seq // CH
    it = np.concatenate([np.full((i + 1,), i, np.int32) for i in range(nq)])
    jt = np.concatenate([np.arange(i + 1, dtype=np.int32) for i in range(nq)])
    kk = np.arange(tq)[:, None]
    qq = np.arange(tq)[None, :]
    diag = jnp.asarray(np.where(kk // CH > qq // CH, -np.inf, 0.0), F32)
    grid_spec = pltpu.PrefetchScalarGridSpec(
        num_scalar_prefetch=2,
        grid=(int(it.shape[0]),),
        in_specs=[pl.BlockSpec((tq, MLA_H * HK), lambda s, it, jt: (it[s], 0)),
                  pl.BlockSpec((tq, MLA_H * HK), lambda s, it, jt: (jt[s], 0)),
                  pl.BlockSpec((tq, MLA_H * MLA_V), lambda s, it, jt: (jt[s], 0)),
                  pl.BlockSpec((CH, MLA_H * HK), lambda s, it, jt: (c0, 0)),
                  pl.BlockSpec((CH, MLA_H * MLA_V), lambda s, it, jt: (c0, 0)),
                  pl.BlockSpec((tq, tq), lambda s, it, jt: (0, 0)),
                  pl.BlockSpec((CH, tq), lambda s, it, jt: (0, 0))],
        out_specs=pl.BlockSpec((tq, MLA_H * MLA_V), lambda s, it, jt: (it[s], 0)),
        scratch_shapes=[pltpu.VMEM((MLA_H * HK, tq), BF16),
                        pltpu.VMEM((MLA_H, 1, tq), F32), pltpu.VMEM((MLA_H, 1, tq), F32),
                        pltpu.VMEM((MLA_H * MLA_V, tq), F32),
                        pltpu.VMEM((2, tq, tq), F32), pltpu.VMEM((2, tq, tq), BF16),
                        pltpu.VMEM((2, 1, tq), F32)],
    )
    return pl.pallas_call(
        _flash_kernel,
        grid_spec=grid_spec,
        out_shape=jax.ShapeDtypeStruct((r, MLA_H * MLA_V), BF16),
        compiler_params=_cp(1), name="flash",
    )(jnp.asarray(it), jnp.asarray(jt), q, kcat, v, kcat, v, diag, _pad_mask(tq))


def _attn0_kernel(q_ref, k_ref, v_ref, keep_ref, o_ref):
    del keep_ref
    col = lax.broadcasted_iota(jnp.int32, (CH, CH), 1)
    for h in range(MLA_H):
        s = lax.dot_general(q_ref[:, h * HK:(h + 1) * HK], k_ref[:, h * HK:(h + 1) * HK], NT,
                            preferred_element_type=F32)
        s = jnp.where(col >= NSKIP, s, -jnp.inf)
        p = jnp.exp2(s - jnp.max(s, axis=-1, keepdims=True))
        o = jnp.dot(p.astype(BF16), v_ref[:, h * MLA_V:(h + 1) * MLA_V], preferred_element_type=F32)
        o_ref[:, h * MLA_V:(h + 1) * MLA_V] = (o / jnp.sum(p, axis=-1, keepdims=True)).astype(BF16)


def _attn0(q, kcat, v, c0, o_frames):
    def blk(w):
        return pl.BlockSpec((CH, w), lambda i: (c0, 0))

    return pl.pallas_call(
        _attn0_kernel,
        grid=(1,),
        in_specs=[blk(MLA_H * HK), blk(MLA_H * HK), blk(MLA_H * MLA_V), pl.BlockSpec(memory_space=pl.ANY)],
        out_specs=blk(MLA_H * MLA_V),
        out_shape=jax.ShapeDtypeStruct(o_frames.shape, BF16),
        input_output_aliases={3: 0},
        compiler_params=_cp(1), name="attn0",
    )(q, kcat, v, o_frames)


def _cached_attn_kernel(q_ref, cnew_ref, pnew_ref, ckv_ref, cpe_ref, wkt_ref, wv_ref, o_ref, call, pet):
    past = ckv_ref.shape[2]
    call[0:past, :] = ckv_ref[0, 0].astype(BF16)
    call[past:past + CH, :] = cnew_ref[...].astype(BF16)
    pet[0:ROPE, :] = cpe_ref[0, 0].astype(BF16)
    pet[ROPE:, :] = jnp.zeros((LANE - ROPE, past), BF16)
    qlat = []
    qpe = []
    for h in range(MLA_H):
        qn = q_ref[:, h * HK:h * HK + NOPE]
        qlat.append(jnp.dot(qn, wkt_ref[h], preferred_element_type=F32).astype(BF16))
        qpe.append(q_ref[:, h * HK + NOPE:(h + 1) * HK])
    qlat = jnp.concatenate(qlat, axis=0)
    qpe = jnp.concatenate(qpe, axis=0)
    s_pe = jnp.concatenate(
        [jnp.dot(qpe, pet[...], preferred_element_type=F32),
         lax.dot_general(qpe, pnew_ref[...].astype(BF16), NT, preferred_element_type=F32)], axis=1)
    s = lax.dot_general(qlat, call[...], NT, preferred_element_type=F32) + s_pe
    p = jnp.exp2(s - jnp.max(s, axis=-1, keepdims=True))
    p = p / jnp.sum(p, axis=-1, keepdims=True)
    olat = jnp.dot(p.astype(BF16), call[...], preferred_element_type=F32).astype(BF16)
    for h in range(MLA_H):
        o_ref[:, h * MLA_V:(h + 1) * MLA_V] = jnp.dot(
            olat[h * CH:(h + 1) * CH, :], wv_ref[h], preferred_element_type=F32).astype(BF16)


def _cached_attn_alias_kernel(q_ref, cnew_ref, pnew_ref, ckv_ref, cpe_ref, wkt_ref, wv_ref, keep_ref, o_ref,
                              call, pet):
    del keep_ref
    _cached_attn_kernel(q_ref, cnew_ref, pnew_ref, ckv_ref, cpe_ref, wkt_ref, wv_ref, o_ref, call, pet)


def _cached_attn(q, ckv, kpe, cache_kv, cache_pe_t, layer, rb0, lw, o_prompt):
    nb, past = cache_kv.shape[1], cache_kv.shape[2]

    def rowmap(b):
        return (rb0 + b, 0)

    return pl.pallas_call(
        _cached_attn_alias_kernel,
        grid=(nb,),
        in_specs=[pl.BlockSpec((CH, MLA_H * HK), rowmap),
                  pl.BlockSpec((CH, KVL), rowmap),
                  pl.BlockSpec((CH, LANE), rowmap),
                  pl.BlockSpec((1, 1, past, KVL), lambda b: (layer, b, 0, 0)),
                  pl.BlockSpec((1, 1, ROPE, past), lambda b: (layer, b, 0, 0)),
                  pl.BlockSpec((MLA_H, NOPE, KVL), lambda b: (0, 0, 0)),
                  pl.BlockSpec((MLA_H, KVL, MLA_V), lambda b: (0, 0, 0)),
                  pl.BlockSpec(memory_space=pl.ANY)],
        out_specs=pl.BlockSpec((CH, MLA_H * MLA_V), rowmap),
        out_shape=jax.ShapeDtypeStruct(o_prompt.shape, BF16),
        scratch_shapes=[pltpu.VMEM((past + CH, KVL), BF16), pltpu.VMEM((LANE, past), BF16)],
        input_output_aliases={7: 0},
        compiler_params=_cp(1), name="cached_attn",
    )(q, ckv, kpe, cache_kv, cache_pe_t, lw["wkt"], lw["wv3"], o_prompt)


TAIL = 16


def _shift_matrix(k, rows=CH):
    ns = k - 1
    pm = np.zeros((rows * ns + TAIL, TAIL + rows), np.float32)
    for d in range(1, ns + 1):
        for t in range(rows):
            if t - d >= 0:
                pm[(d - 1) * rows + t, TAIL + t - d] = 1.0
            else:
                r = ns + t - d
                pm[(d - 1) * rows + t, 3 * r:3 * r + 3] = 1.0
    for r in range(ns):
        pm[rows * ns + 3 * r, TAIL + rows - ns + r] = 1.0
    return pm


def _load_tail(tail_s, prev):
    hi = prev.astype(BF16).astype(F32)
    rest = prev - hi
    mid = rest.astype(BF16).astype(F32)
    lo = (rest - mid).astype(BF16).astype(F32)
    tail_s[...] = jnp.zeros(tail_s.shape, F32)
    for r in range(prev.shape[0]):
        tail_s[3 * r:3 * r + 1, :] = hi[r:r + 1]
        tail_s[3 * r + 1:3 * r + 2, :] = mid[r:r + 1]
        tail_s[3 * r + 2:3 * r + 3, :] = lo[r:r + 1]


def _delayed(pm_ref, tail_s, x, ns):
    t = x.shape[0]
    ext = jnp.concatenate([tail_s[...].astype(BF16), x], axis=0)
    out = jnp.dot(pm_ref[...], ext, preferred_element_type=F32)
    tail_s[...] = out[t * ns:, :]
    return [out[(d - 1) * t:d * t, :] for d in range(1, ns + 1)]


def _act_kernel(nskip, g_ref, u_ref, prev_ref, w_ref, pm_ref, a_ref, new_ref, tail_s):
    t = g_ref.shape[0]
    ns = FFN_K - 1
    c = pl.program_id(1)

    @pl.when(c == 0)
    def _():
        _load_tail(tail_s, prev_ref[0])

    rows = lax.broadcasted_iota(jnp.int32, (t, 1), 0) + c * t
    g = g_ref[...]
    g = jnp.where(rows >= nskip, g, jnp.zeros((), g.dtype))
    delayed = _delayed(pm_ref, tail_s, g, ns)
    conv = w_ref[ns:ns + 1, :] * g.astype(F32)
    for d in range(1, ns + 1):
        conv = conv + w_ref[ns - d:ns - d + 1, :] * delayed[d - 1]
    a_ref[...] = (_silu(conv) * u_ref[...].astype(F32)).astype(BF16)

    @pl.when(c == pl.num_programs(1) - 1)
    def _():
        for r in range(ns):
            new_ref[0, r:r + 1, :] = tail_s[3 * r:3 * r + 1, :]


def _act(gate, up, rb0, nb, nch, nskip, prev, w, act_in=None, t=CH):
    r, n = gate.shape

    def rowmap(b, c):
        return (rb0 + b * nch + c, 0)

    pm = jnp.asarray(_shift_matrix(FFN_K, t), BF16)
    args = [gate, up, prev, w, pm]
    in_specs = [pl.BlockSpec((t, n), rowmap), pl.BlockSpec((t, n), rowmap),
                pl.BlockSpec((1, FFN_K - 1, n), lambda b, c: (b, 0, 0)),
                pl.BlockSpec((FFN_K, n), lambda b, c: (0, 0)),
                pl.BlockSpec(pm.shape, lambda b, c: (0, 0))]
    kern = functools.partial(_act_kernel, nskip)
    aliases = {}
    if act_in is not None:
        aliases = {len(args): 0}
        args.append(act_in)
        in_specs.append(pl.BlockSpec(memory_space=pl.ANY))
        kern = functools.partial(_act_alias_kernel, nskip)
    return pl.pallas_call(
        kern,
        grid=(nb, nch),
        in_specs=in_specs,
        out_specs=[pl.BlockSpec((t, n), rowmap), pl.BlockSpec((1, FFN_K - 1, n), lambda b, c: (b, 0, 0))],
        out_shape=[jax.ShapeDtypeStruct((r, n), BF16), jax.ShapeDtypeStruct((nb, FFN_K - 1, n), F32)],
        scratch_shapes=[pltpu.VMEM((TAIL, n), F32)],
        input_output_aliases=aliases,
        compiler_params=_cp(2), name="act",
    )(*args)


def _act_alias_kernel(nskip, g_ref, u_ref, prev_ref, w_ref, pm_ref, keep_ref, a_ref, new_ref, tail_s):
    del keep_ref
    _act_kernel(nskip, g_ref, u_ref, prev_ref, w_ref, pm_ref, a_ref, new_ref, tail_s)


def _cast_cols_kernel(x_ref, o_ref):
    n = x_ref.shape[1]
    o_ref[:, :n] = x_ref[...].astype(BF16)
    if o_ref.shape[1] > n:
        o_ref[:, n:] = jnp.zeros((o_ref.shape[0], o_ref.shape[1] - n), BF16)


def _cast_pad_cols(w, layer, n_out):
    _, k, n = w.shape
    tr = _pick(k, (128, 64))
    return pl.pallas_call(
        _cast_cols_kernel,
        grid=(k // tr,),
        in_specs=[pl.BlockSpec((None, tr, n), lambda i: (layer, i, 0))],
        out_specs=pl.BlockSpec((tr, n_out), lambda i: (i, 0)),
        out_shape=jax.ShapeDtypeStruct((k, n_out), BF16),
        compiler_params=_cp(1), name="cast_cols",
    )(w)


def _cast_rows_kernel(n_full, x_ref, o_ref):
    @pl.when(pl.program_id(0) < n_full)
    def _():
        o_ref[...] = x_ref[...].astype(BF16)

    @pl.when(pl.program_id(0) >= n_full)
    def _():
        o_ref[...] = jnp.zeros(o_ref.shape, BF16)


def _cast_pad_rows(w, layer, k_out):
    _, k, n = w.shape
    tr = 256
    assert k % tr == 0 and k_out % tr == 0
    n_full = k // tr
    return pl.pallas_call(
        functools.partial(_cast_rows_kernel, n_full),
        grid=(k_out // tr,),
        in_specs=[pl.BlockSpec((None, tr, n), lambda i: (layer, jnp.minimum(i, n_full - 1), 0))],
        out_specs=pl.BlockSpec((tr, n), lambda i: (i, 0)),
        out_shape=jax.ShapeDtypeStruct((k_out, n), BF16),
        compiler_params=_cp(1), name="cast_rows",
    )(w)


PACK = 256


def _pack_win_kernel(a_ref, b_ref, o_ref):
    j = pl.program_id(0)
    o_dt = SSD_DI + SSD_CD
    n_plain = o_dt // PACK
    last = pl.num_programs(0) - 1
    half = ROPE // 2

    def emit(rows):
        o_ref[...] = rows.T.astype(BF16)

    @pl.when(j < n_plain)
    def _():
        emit(a_ref[...])

    @pl.when((j >= n_plain) & (j < last))
    def _():
        emit(jnp.concatenate([a_ref[SSD_H:, :], b_ref[:SSD_H, :]], axis=0))

    @pl.when(j == last)
    def _():
        kr0 = a_ref[SSD_H:SSD_H + half, :]
        kr1 = a_ref[SSD_H + half:SSD_H + ROPE, :]
        emit(jnp.concatenate([kr0, kr1, kr1, kr0, b_ref[:SSD_H, :],
                              jnp.zeros((PACK - 2 * ROPE - SSD_H, a_ref.shape[1]), F32)], axis=0))


def _pack_win(w_in, layer):
    _, k, n = w_in.shape
    assert (SSD_DI + SSD_CD) % PACK == 0 and NIN % PACK == 0 and n - (NIN - PACK) == SSD_H + ROPE
    wt = jnp.swapaxes(w_in, 1, 2)
    nblk = NIN // PACK
    n_plain = (SSD_DI + SSD_CD) // PACK

    def b_index(j):
        return jnp.where((j < n_plain) | (j == nblk - 1), n_plain, j + 1)

    return pl.pallas_call(
        _pack_win_kernel,
        grid=(nblk,),
        in_specs=[pl.BlockSpec((None, PACK, k), lambda j: (layer, j, 0)),
                  pl.BlockSpec((None, PACK, k), lambda j: (layer, b_index(j), 0))],
        out_specs=pl.BlockSpec((k, PACK), lambda j: (0, j)),
        out_shape=jax.ShapeDtypeStruct((k, NIN), BF16),
        compiler_params=_cp(1), name="pack_win",
    )(wt, wt)


def _swap_half(w):
    half = w.shape[-1] // 2
    return jnp.concatenate([w[..., half:], w[..., :half]], axis=-1)


def _layer_weights(i, w_in, ssd_conv_w, ssd_conv_b, ssd_dt_bias, ssd_a_log, ssd_d, ssd_norm, sc_conv_w,
                   mla_q_norm, mla_w_uq, mla_kv_norm, mla_w_ukv, w_out, ffn_w_gate, ffn_w_up, ffn_conv_w,
                   ffn_w_down):
    win = _pack_win(w_in, i)
    uq = mla_w_uq[i].reshape(QL, MLA_H, NOPE + ROPE)
    pe = uq[..., NOPE:]
    zq = jnp.zeros((QL, MLA_H, HK - NOPE - ROPE), F32)
    wa = jnp.concatenate([uq[..., :NOPE], pe, zq], axis=-1).reshape(QL, MLA_H * HK).astype(BF16)
    wb = jnp.concatenate([_swap_half(pe), zq], axis=-1).reshape(QL, MLA_H * LANE).astype(BF16)
    ukv = mla_w_ukv[i].reshape(KVL, MLA_H, NOPE + MLA_V)
    padf = ((0, 0), (0, DFP - D_FF))
    return dict(
        win=win,
        cw=ssd_conv_w[i], cb=ssd_conv_b[i].reshape(1, SSD_CD),
        dtb=jnp.pad(ssd_dt_bias[i], (0, LANE - SSD_H)).reshape(1, LANE),
        alog=jnp.pad(ssd_a_log[i], (0, LANE - SSD_H)).reshape(1, LANE),
        dsk=jnp.repeat(ssd_d[i], SSD_P).reshape(1, SSD_DI),
        ng=ssd_norm[i].reshape(1, SSD_DI),
        scw=sc_conv_w[i],
        qn=mla_q_norm[i].reshape(1, QL), kvn=mla_kv_norm[i].reshape(1, KVL),
        wa=wa, wb=wb,
        wk=ukv[..., :NOPE].reshape(KVL, MLA_H * NOPE).astype(BF16),
        wv=ukv[..., NOPE:].reshape(KVL, MLA_H * MLA_V).astype(BF16),
        wkt=jnp.transpose(ukv[..., :NOPE], (1, 2, 0)).astype(BF16),
        wv3=jnp.transpose(ukv[..., NOPE:], (1, 0, 2)).astype(BF16),
        wout=_cast_pad_cols(w_out, i, D_MODEL),
        wg=_cast_pad_cols(ffn_w_gate, i, DFP),
        wu=_cast_pad_cols(ffn_w_up, i, DFP),
        fcw=jnp.pad(ffn_conv_w[i], padf),
        wd=_cast_pad_rows(ffn_w_down, i, DFP),
    )


def _rope_table(seq, ns, past):
    half = ROPE // 2
    pos = jnp.concatenate([N_META + jnp.arange(seq, dtype=jnp.int32),
                           jnp.maximum(jnp.arange(CH, dtype=jnp.int32) - NSKIP, 0),
                           N_META + past + jnp.arange(ns, dtype=jnp.int32) % CH])
    inv = THETA ** (-jnp.arange(half, dtype=F32) / half)
    ang = pos.astype(F32)[:, None] * inv[None, :]
    cos, sin = jnp.cos(ang), jnp.sin(ang)
    return jnp.concatenate([cos, cos, -sin, sin], axis=1)


def kernel(x_prompt, x_sample, cache_kv_latent, cache_k_rope, state_ssm, state_ssd_conv, state_sconv, state_ffn_conv, meta_tokens, norm_mix_pre, norm_mix_post, norm_ffn_pre, norm_ffn_post, w_in, ssd_conv_w, ssd_conv_b, ssd_dt_bias, ssd_a_log, ssd_d, ssd_norm, sc_conv_w, mla_q_norm, mla_w_uq, mla_kv_norm, mla_w_ukv, w_out, ffn_w_gate, ffn_w_up, ffn_conv_w, ffn_w_down):
    bp, seq, d = x_prompt.shape
    nb, ls, _ = x_sample.shape
    depth, _, past, _ = cache_kv_latent.shape
    assert bp == 1 and ls == CH and seq % CH == 0 and d == D_MODEL
    lp = CH + seq
    ns = nb * ls
    npc = lp // CH
    c0 = seq // CH

    x, h = _assemble(x_prompt[0], x_sample.reshape(ns, d), meta_tokens.astype(F32), norm_mix_pre[0])
    tab = _rope_table(seq, ns, past)
    cache_pe_t = jnp.swapaxes(cache_k_rope, 2, 3)
    zero_c = jnp.zeros((1, SSD_K - 1, SSD_CD), F32)
    zero_h = jnp.zeros((1, SSD_N, SSD_DI), F32)

    def state_in(s):
        return jnp.transpose(s, (0, 3, 1, 2)).reshape(s.shape[0], SSD_N, SSD_DI)

    def state_out(s):
        return jnp.transpose(s.reshape(s.shape[0], SSD_N, SSD_H, SSD_P), (0, 2, 3, 1))
    zero_s = jnp.zeros((1, SC_K - 1, SC_D), F32)
    zero_f = jnp.zeros((1, FFN_K - 1, DFP), F32)
    padf = ((0, 0), (0, 0), (0, DFP - D_FF))

    outs_p, outs_s = [], []
    for i in range(depth):
        lw = _layer_weights(i, w_in, ssd_conv_w, ssd_conv_b, ssd_dt_bias, ssd_a_log, ssd_d, ssd_norm,
                            sc_conv_w, mla_q_norm, mla_w_uq, mla_kv_norm, mla_w_ukv, w_out, ffn_w_gate,
                            ffn_w_up, ffn_conv_w, ffn_w_down)
        u, dtf = _inproj(h, lw["win"])

        yp, yscp, cnew_p, hnew_p, scnew_p = _mixer(u, dtf, 0, 1, npc, NSKIP, zero_c, zero_h, zero_s, lw)
        y_ssd, y_sc, cnew_s, hnew_s, scnew_s = _mixer(u, dtf, npc, nb, 1, 0, state_ssd_conv[i],
                                                      state_in(state_ssm[i]), state_sconv[i], lw,
                                                      keep=(yp, yscp))
        hnew_p, hnew_s = state_out(hnew_p), state_out(hnew_s)

        q, ckv, kpe, kcat, v = _mla_proj(u, tab, lw)
        y_mla = _attn0(q, kcat, v, c0, _flash(q, kcat, v, seq))
        y_mla = _cached_attn(q, ckv, kpe, cache_kv_latent, cache_pe_t, i, npc, lw, y_mla)

        mix = _outproj(y_ssd, y_sc, y_mla, lw["wout"])
        x, h = _resid_norm(x, mix, norm_mix_post[i], norm_ffn_pre[i])

        gate, up = _gateup(h, lw["wg"], lw["wu"])
        act, f0 = _act(gate, up, c0, 1, 1, NSKIP, zero_f, lw["fcw"])
        act, fnew_p = _act(gate, up, 0, 1, seq // ACT_ROWS, 0, f0, lw["fcw"], act_in=act, t=ACT_ROWS)
        act, fnew_s = _act(gate, up, npc, nb, 1, 0, jnp.pad(state_ffn_conv[i], padf), lw["fcw"], act_in=act)
        f = _down(act, lw["wd"])
        if i + 1 < depth:
            x, h = _resid_norm(x, f, norm_ffn_post[i], norm_mix_pre[i + 1])
        else:
            y_prompt = _resid_out(x, f, norm_ffn_post[i], 0, seq)
            y_sample = _resid_out(x, f, norm_ffn_post[i], lp, ns)

        meta_rows = slice(seq + NSKIP, lp)
        outs_p.append((jnp.concatenate([ckv[meta_rows], ckv[:seq]], axis=0)[None],
                       jnp.concatenate([kpe[meta_rows, :ROPE], kpe[:seq, :ROPE]], axis=0)[None],
                       hnew_p, cnew_p, scnew_p, fnew_p[:, :, :D_FF]))
        outs_s.append((ckv[lp:].reshape(nb, ls, KVL), kpe[lp:, :ROPE].reshape(nb, ls, ROPE), hnew_s, cnew_s,
                       scnew_s, fnew_s[:, :, :D_FF]))

    def stack(outs, j):
        return jnp.stack([o[j] for o in outs], axis=0)

    return (y_prompt[None], y_sample.reshape(nb, ls, d),
            stack(outs_p, 0), stack(outs_p, 1), stack(outs_p, 2), stack(outs_p, 3), stack(outs_p, 4), stack(outs_p, 5),
            stack(outs_s, 0), stack(outs_s, 1), stack(outs_s, 2), stack(outs_s, 3), stack(outs_s, 4), stack(outs_s, 5))
```

```python
import functools

import jax
import jax.numpy as jnp
import numpy as np
from jax import lax
from jax.experimental import pallas as pl
from jax.experimental.pallas import tpu as pltpu

F32 = jnp.float32
BF16 = jnp.bfloat16

D_MODEL = 4096
N_META = 16
CH = 64
NSKIP = CH - N_META
EPS = 1e-6
SSD_P = 64
SSD_DI = 2048
SSD_H = 32
SSD_G = 8
SSD_N = 128
SSD_K = 4
SSD_CD = 4096
SC_D = 1024
SC_K = 3
MLA_H = 8
NOPE = 128
ROPE = 64
MLA_V = 128
QL = 768
KVL = 512
HK = 256
SCALE = (NOPE + ROPE) ** -0.5
QSCALE = SCALE * 1.4426950408889634
THETA = 10000.0
D_FF = 11008
DFP = 11264
FFN_K = 3
LANE = 128

C_Z, C_X, C_BC, C_SCB, C_SCC, C_SCH, C_MLA, NIN = 0, 2048, 4096, 6144, 7168, 8192, 9216, 10752
MLA_W = 1536

FLASH_TQ = 512
ACT_ROWS = 128
VMEM_LIMIT = 56 * 1024 * 1024
HI = lax.Precision.HIGHEST
NT = (((1,), (1,)), ((), ()))
TN = (((0,), (0,)), ((), ()))


def _cp(n, flags=None):
    return pltpu.CompilerParams(dimension_semantics=("arbitrary",) * n, vmem_limit_bytes=VMEM_LIMIT, flags=flags)


def _pick(n, cands):
    for c in cands:
        if n % c == 0:
            return c
    raise ValueError(f"no tile for {n}")


MM_ROWS = (1088, 544, 512, 272, 256, 136, 128, 64)
EW_ROWS = (272, 256, 136, 128, 64)


def _stored_chunk(c, nch):
    return (c + nch - 1) % nch


def _rms(x, g):
    return x * lax.rsqrt(jnp.mean(x * x, axis=-1, keepdims=True) + EPS) * g


def _silu(x):
    return x * jax.nn.sigmoid(x)


def _norm_rows_kernel(x_in_ref, g_ref, x_ref, h_ref):
    x = x_in_ref[...]
    x_ref[...] = x
    h_ref[...] = _rms(x, g_ref[...]).astype(BF16)


def _tail_rows_kernel(xs_ref, meta_ref, g_ref, keep_x, keep_h, x_ref, h_ref):
    del keep_x, keep_h
    c = pl.program_id(0)

    def emit(x):
        x_ref[...] = x
        h_ref[...] = _rms(x, g_ref[...]).astype(BF16)

    @pl.when(c == 0)
    def _():
        emit(jnp.concatenate([jnp.zeros((NSKIP, x_ref.shape[1]), F32), meta_ref[...]], axis=0))

    @pl.when(c > 0)
    def _():
        emit(xs_ref[...])


def _assemble(x_prompt, x_sample, meta, g):
    seq, d = x_prompt.shape
    ns = x_sample.shape[0]
    r = seq + CH + ns
    bm = _pick(seq, (512, 256, 128, 64))
    vec = pl.BlockSpec((1, d), lambda c: (0, 0))
    out_shape = [jax.ShapeDtypeStruct((r, d), F32), jax.ShapeDtypeStruct((r, d), BF16)]
    big = pl.BlockSpec((bm, d), lambda c: (c, 0))
    x, h = pl.pallas_call(
        _norm_rows_kernel, grid=(seq // bm,), in_specs=[big, vec], out_specs=[big, big],
        out_shape=out_shape, compiler_params=_cp(1), name="assemble_frames",
    )(x_prompt, g.reshape(1, d))
    c0 = seq // CH
    small = pl.BlockSpec((CH, d), lambda c: (c0 + c, 0))
    anyspec = pl.BlockSpec(memory_space=pl.ANY)
    return pl.pallas_call(
        _tail_rows_kernel, grid=(1 + ns // CH,),
        in_specs=[pl.BlockSpec((CH, d), lambda c: (jnp.maximum(c - 1, 0), 0)),
                  pl.BlockSpec((N_META, d), lambda c: (0, 0)), vec, anyspec, anyspec],
        out_specs=[small, small], out_shape=out_shape, input_output_aliases={3: 0, 4: 1},
        compiler_params=_cp(1), name="assemble_tail",
    )(x_sample, meta, g.reshape(1, d), x, h)


def _resid_out_kernel(x_ref, m_ref, gp_ref, y_ref):
    y_ref[...] = x_ref[...] + _rms(m_ref[...], gp_ref[...])


def _resid_out(x, m, g_post, row0, rows):
    d = x.shape[1]
    bm = next(b for b in (512, 256, 128, 64) if rows % b == 0 and row0 % b == 0)
    src = pl.BlockSpec((bm, d), lambda c: (row0 // bm + c, 0))
    return pl.pallas_call(
        _resid_out_kernel, grid=(rows // bm,),
        in_specs=[src, src, pl.BlockSpec((1, d), lambda c: (0, 0))],
        out_specs=pl.BlockSpec((bm, d), lambda c: (c, 0)),
        out_shape=jax.ShapeDtypeStruct((rows, d), F32),
        compiler_params=_cp(1), name="resid_out",
    )(x, m, g_post.reshape(1, d))


def _resid_norm_kernel(x_ref, m_ref, gp_ref, gn_ref, x2_ref, h_ref):
    x2 = x_ref[...] + _rms(m_ref[...], gp_ref[...])
    x2_ref[...] = x2
    h_ref[...] = _rms(x2, gn_ref[...]).astype(BF16)


def _resid_norm(x, m, g_post, g_next):
    r, d = x.shape
    bm = _pick(r, EW_ROWS)
    row = pl.BlockSpec((bm, d), lambda i: (i, 0))
    vec = pl.BlockSpec((1, d), lambda i: (0, 0))
    return pl.pallas_call(
        _resid_norm_kernel, grid=(r // bm,), in_specs=[row, row, vec, vec], out_specs=[row, row],
        out_shape=[jax.ShapeDtypeStruct((r, d), F32), jax.ShapeDtypeStruct((r, d), BF16)],
        compiler_params=_cp(1), name="resid_norm",
    )(x, m, g_post.reshape(1, d), g_next.reshape(1, d))


def _inproj_kernel(x_ref, w_ref, u_ref, dt_ref):
    acc = jnp.dot(x_ref[...], w_ref[...], preferred_element_type=F32)
    u_ref[...] = acc.astype(BF16)

    @pl.when(pl.program_id(1) == pl.num_programs(1) - 1)
    def _():
        dt_ref[...] = acc[:, acc.shape[1] - LANE:]


def _inproj(h, w):
    r, k = h.shape
    n = w.shape[1]
    bm = _pick(r, MM_ROWS)
    bn = 768
    return pl.pallas_call(
        _inproj_kernel,
        grid=(r // bm, n // bn),
        in_specs=[pl.BlockSpec((bm, k), lambda i, j: (i, 0)), pl.BlockSpec((k, bn), lambda i, j: (0, j))],
        out_specs=[pl.BlockSpec((bm, bn), lambda i, j: (i, j)), pl.BlockSpec((bm, LANE), lambda i, j: (i, 0))],
        out_shape=[jax.ShapeDtypeStruct((r, n), BF16), jax.ShapeDtypeStruct((r, LANE), F32)],
        compiler_params=_cp(2), name="inproj",
    )(h, w)


def _outproj_kernel(a_ref, b_ref, c_ref, w_ref, o_ref, x_s):
    @pl.when(pl.program_id(1) == 0)
    def _():
        x_s[:, 0:SSD_DI] = a_ref[...]
        x_s[:, SSD_DI:SSD_DI + SC_D] = b_ref[...]
        x_s[:, SSD_DI + SC_D:] = c_ref[...]

    o_ref[...] = jnp.dot(x_s[...], w_ref[...], preferred_element_type=F32)


def _outproj(y_ssd, y_sc, y_mla, w):
    r = y_ssd.shape[0]
    k, n = w.shape
    bm = _pick(r, MM_ROWS)
    bn = 512
    return pl.pallas_call(
        _outproj_kernel,
        grid=(r // bm, n // bn),
        in_specs=[pl.BlockSpec((bm, SSD_DI), lambda i, j: (i, 0)),
                  pl.BlockSpec((bm, SC_D), lambda i, j: (i, 0)),
                  pl.BlockSpec((bm, SC_D), lambda i, j: (i, 0)),
                  pl.BlockSpec((k, bn), lambda i, j: (0, j))],
        out_specs=pl.BlockSpec((bm, bn), lambda i, j: (i, j)),
        out_shape=jax.ShapeDtypeStruct((r, n), F32),
        scratch_shapes=[pltpu.VMEM((bm, k), BF16)],
        compiler_params=_cp(2), name="outproj",
    )(y_ssd, y_sc, y_mla, w)


def _gateup_kernel(x_ref, wg_ref, wu_ref, g_ref, u_ref):
    x = x_ref[...]
    g_ref[...] = jnp.dot(x, wg_ref[...], preferred_element_type=F32).astype(BF16)
    u_ref[...] = jnp.dot(x, wu_ref[...], preferred_element_type=F32).astype(BF16)


def _gateup(h, wg, wu):
    r, k = h.shape
    n = wg.shape[1]
    bm = _pick(r, MM_ROWS)
    bn = 512
    wspec = pl.BlockSpec((k, bn), lambda i, j: (0, j))
    ospec = pl.BlockSpec((bm, bn), lambda i, j: (i, j))
    return pl.pallas_call(
        _gateup_kernel,
        grid=(r // bm, n // bn),
        in_specs=[pl.BlockSpec((bm, k), lambda i, j: (i, 0)), wspec, wspec],
        out_specs=[ospec, ospec],
        out_shape=[jax.ShapeDtypeStruct((r, n), BF16)] * 2,
        compiler_params=_cp(2), name="gateup",
    )(h, wg, wu)


def _down_kernel(x_ref, w_ref, o_ref):
    o_ref[...] = jnp.dot(x_ref[...], w_ref[...], preferred_element_type=F32)


def _down(a, w):
    r, k = a.shape
    n = w.shape[1]
    bm = _pick(r, MM_ROWS[1:])
    bn = 512
    return pl.pallas_call(
        _down_kernel,
        grid=(r // bm, n // bn),
        in_specs=[pl.BlockSpec((bm, k), lambda i, j: (i, 0)), pl.BlockSpec((k, bn), lambda i, j: (0, j))],
        out_specs=pl.BlockSpec((bm, bn), lambda i, j: (i, j)),
        out_shape=jax.ShapeDtypeStruct((r, n), F32),
        compiler_params=_cp(2), name="down",
    )(a, w)


def _mixer_kernel(nskip, z_ref, x_ref, bc_ref, dt_ref, scb_ref, scc_ref, sch_ref,
                  cprev_ref, h0_ref, scprev_ref,
                  cw_ref, cb_ref, dtb_ref, alog_ref, dsk_ref, ng_ref, scw_ref, pm_ref,
                  y_ref, ysc_ref, cnew_ref, hnew_ref, scnew_ref,
                  buf, sbuf, hst):
    t = CH
    c = pl.program_id(1)
    last = pl.num_programs(1) - 1

    ns = SSD_K - 1

    @pl.when(c == 0)
    def _init():
        _load_tail(buf, cprev_ref[0])
        sbuf[0:8, :] = jnp.zeros((8, SC_D), F32)
        sbuf[8 - (SC_K - 1):8, :] = scprev_ref[0]
        hst[...] = h0_ref[0]

    rows = lax.broadcasted_iota(jnp.int32, (t, 1), 0) + c * t
    valid = rows >= nskip

    xin = jnp.concatenate([x_ref[...], bc_ref[...]], axis=1)
    xin = jnp.where(valid, xin, jnp.zeros((), xin.dtype))
    delayed = _delayed(pm_ref, buf, xin, ns)
    acc = cb_ref[...] + cw_ref[ns:ns + 1, :] * xin.astype(F32)
    for d in range(1, ns + 1):
        acc = acc + cw_ref[ns - d:ns - d + 1, :] * delayed[d - 1]
    xbc = _silu(acc)

    lane = lax.broadcasted_iota(jnp.int32, (t, LANE), 1)
    v = dt_ref[...] + dtb_ref[...]
    dt = jnp.maximum(v, 0.0) + jnp.log1p(jnp.exp(-jnp.abs(v)))
    dt = jnp.where(valid & (lane < SSD_H), dt, 0.0)
    adt = dt * (-jnp.exp(alog_ref[...]))
    ri = lax.broadcasted_iota(jnp.int32, (t, t), 0)
    ci = lax.broadcasted_iota(jnp.int32, (t, t), 1)
    tri = ri >= ci
    acs = jnp.dot(tri.astype(F32), adt, precision=HI, preferred_element_type=F32)

    hpg = SSD_H // SSD_G
    gw = hpg * SSD_P
    lane_g = lax.broadcasted_iota(jnp.int32, (t, gw), 1)
    row_g = lax.broadcasted_iota(jnp.int32, (t, gw), 0)
    seg = lane_g // SSD_P
    eye_g = row_g == lane_g % SSD_P
    tri_g = row_g >= lane_g % SSD_P
    blockdiag = (lax.broadcasted_iota(jnp.int32, (hpg * t, gw), 0) // t
                 == lax.broadcasted_iota(jnp.int32, (hpg * t, gw), 1) // SSD_P)

    def per_head_lanes(mat, g):
        out = jnp.broadcast_to(mat[:, g * hpg + hpg - 1:g * hpg + hpg], (t, gw))
        for r in range(hpg - 2, -1, -1):
            out = jnp.where(seg == r, jnp.broadcast_to(mat[:, g * hpg + r:g * hpg + r + 1], (t, gw)), out)
        return out

    for g in range(SSD_G):
        cols = slice(g * gw, (g + 1) * gw)
        bg = xbc[:, SSD_DI + g * SSD_N:SSD_DI + (g + 1) * SSD_N].astype(BF16)
        cg = xbc[:, SSD_DI + SSD_G * SSD_N + g * SSD_N:SSD_DI + SSD_G * SSD_N + (g + 1) * SSD_N].astype(BF16)
        acol = per_head_lanes(acs, g)
        dtx = per_head_lanes(dt, g)
        arow = jnp.sum(jnp.where(eye_g, acol, 0.0), axis=0, keepdims=True)
        decay = jnp.exp(jnp.where(tri_g, acol - arow, -jnp.inf))
        cb = lax.dot_general(cg, jnp.concatenate([bg] * hpg, axis=0), NT, preferred_element_type=F32)
        xg = xbc[:, cols]
        xdt = xg * dtx
        xdtb = xdt.astype(BF16)
        rhs = jnp.where(blockdiag, jnp.concatenate([xdtb] * hpg, axis=0), jnp.zeros((), BF16))
        ydiag = jnp.dot((cb * decay).astype(BF16), rhs, preferred_element_type=F32)
        hg = hst[:, cols]
        yoff = jnp.dot(cg, hg.astype(BF16), preferred_element_type=F32) * jnp.exp(acol)
        alast = acol[t - 1:t, :]
        snew = lax.dot_general(bg, (xdt * jnp.exp(alast - acol)).astype(BF16), TN, preferred_element_type=F32)
        hst[:, cols] = jnp.exp(alast) * hg + snew
        y = ydiag + yoff + dsk_ref[:, cols] * xg
        y = y * _silu(z_ref[:, cols].astype(F32))
        y_ref[:, cols] = _rms(y, ng_ref[:, cols]).astype(BF16)

    p = scc_ref[...].astype(F32) * sch_ref[...].astype(F32)
    p = jnp.where(valid, p, 0.0)
    sbuf[8:8 + t, :] = p
    conv = scw_ref[0:1, :] * sbuf[6:6 + t, :]
    for i in range(1, SC_K):
        conv = conv + scw_ref[i:i + 1, :] * sbuf[6 + i:6 + i + t, :]
    ysc_ref[...] = (scb_ref[...].astype(F32) * conv).astype(BF16)
    sbuf[0:8, :] = sbuf[t:t + 8, :]

    @pl.when(c == last)
    def _():
        for r in range(ns):
            cnew_ref[0, r:r + 1, :] = buf[3 * r:3 * r + 1, :]
        hnew_ref[0] = hst[...]
        scnew_ref[0] = sbuf[8 - (SC_K - 1):8, :]


def _mixer_alias_kernel(nskip, *refs):
    n_in = 18
    _mixer_kernel(nskip, *refs[:n_in], *refs[n_in + 2:])


def _mixer(u, dtf, rb0, nb, nch, nskip, cprev, h0, scprev, lw, keep=None):
    r = u.shape[0]
    t = CH

    def rowmap(cb):
        return lambda b, c: (rb0 + b * nch + _stored_chunk(c, nch), cb)

    def stmap(b, c):
        return (b, 0, 0)

    def full(shape):
        return pl.BlockSpec(shape, lambda b, c: (0,) * len(shape))

    in_specs = [
        pl.BlockSpec((t, SSD_DI), rowmap(C_Z // SSD_DI)),
        pl.BlockSpec((t, SSD_DI), rowmap(C_X // SSD_DI)),
        pl.BlockSpec((t, SSD_DI), rowmap(C_BC // SSD_DI)),
        pl.BlockSpec((t, LANE), rowmap(0)),
        pl.BlockSpec((t, SC_D), rowmap(C_SCB // SC_D)),
        pl.BlockSpec((t, SC_D), rowmap(C_SCC // SC_D)),
        pl.BlockSpec((t, SC_D), rowmap(C_SCH // SC_D)),
        pl.BlockSpec((1, SSD_K - 1, SSD_CD), stmap),
        pl.BlockSpec((1, SSD_N, SSD_DI), stmap),
        pl.BlockSpec((1, SC_K - 1, SC_D), stmap),
        full((SSD_K, SSD_CD)), full((1, SSD_CD)), full((1, LANE)), full((1, LANE)),
        full((1, SSD_DI)), full((1, SSD_DI)), full((SC_K, SC_D)),
        full((CH * (SSD_K - 1) + TAIL, TAIL + CH)),
    ]
    out_specs = [
        pl.BlockSpec((t, SSD_DI), rowmap(0)),
        pl.BlockSpec((t, SC_D), rowmap(0)),
        pl.BlockSpec((1, SSD_K - 1, SSD_CD), stmap),
        pl.BlockSpec((1, SSD_N, SSD_DI), stmap),
        pl.BlockSpec((1, SC_K - 1, SC_D), stmap),
    ]
    out_shape = [
        jax.ShapeDtypeStruct((r, SSD_DI), BF16),
        jax.ShapeDtypeStruct((r, SC_D), BF16),
        jax.ShapeDtypeStruct((nb, SSD_K - 1, SSD_CD), F32),
        jax.ShapeDtypeStruct((nb, SSD_N, SSD_DI), F32),
        jax.ShapeDtypeStruct((nb, SC_K - 1, SC_D), F32),
    ]
    args = [u, u, u, dtf, u, u, u, cprev, h0, scprev,
            lw["cw"], lw["cb"], lw["dtb"], lw["alog"], lw["dsk"], lw["ng"], lw["scw"],
            jnp.asarray(_shift_matrix(SSD_K), BF16)]
    kern = functools.partial(_mixer_kernel, nskip)
    aliases = {}
    if keep is not None:
        aliases = {len(args): 0, len(args) + 1: 1}
        args += list(keep)
        in_specs += [pl.BlockSpec(memory_space=pl.ANY)] * 2
        kern = functools.partial(_mixer_alias_kernel, nskip)
    return pl.pallas_call(
        kern,
        grid=(nb, nch),
        in_specs=in_specs, out_specs=out_specs, out_shape=out_shape,
        scratch_shapes=[pltpu.VMEM((TAIL, SSD_CD), F32), pltpu.VMEM((8 + t, SC_D), F32),
                        pltpu.VMEM((SSD_N, SSD_DI), F32)],
        input_output_aliases=aliases,
        compiler_params=_cp(2), name="mixer",
    )(*args)


def _mla_proj_kernel(blk_ref, tab_ref, qn_ref, kvn_ref, wa_ref, wb_ref, wk_ref, wv_ref,
                     q_ref, ckv_ref, kpe_ref, kcat_ref, v_ref):
    blk = blk_ref[...]
    cq = _rms(blk[:, :QL].astype(F32), qn_ref[...]).astype(BF16)
    ckv = _rms(blk[:, QL:QL + KVL].astype(F32), kvn_ref[...])
    ckv_ref[...] = ckv
    tab = tab_ref[...]
    tabr = pltpu.roll(tab, ROPE, 1)
    prod = blk[:, QL + KVL:QL + KVL + LANE].astype(F32) * tab
    ksum = prod + pltpu.roll(prod, ROPE, 1)
    lane = lax.broadcasted_iota(jnp.int32, ksum.shape, 1)
    kpe = jnp.where(lane < ROPE, ksum, 0.0)
    kpe_ref[...] = kpe
    qa = jnp.dot(cq, wa_ref[...], preferred_element_type=F32)
    qb = jnp.dot(cq, wb_ref[...], preferred_element_type=F32)
    ckvb = ckv.astype(BF16)
    kn = jnp.dot(ckvb, wk_ref[...], preferred_element_type=F32)
    v_ref[...] = jnp.dot(ckvb, wv_ref[...], preferred_element_type=F32).astype(BF16)
    kpeb = kpe.astype(BF16)
    tab_q = tab * QSCALE
    tabr_q = tabr * QSCALE
    for h in range(MLA_H):
        q_ref[:, h * HK:h * HK + NOPE] = (qa[:, h * HK:h * HK + NOPE] * QSCALE).astype(BF16)
        q_ref[:, h * HK + NOPE:(h + 1) * HK] = (
            qa[:, h * HK + NOPE:(h + 1) * HK] * tab_q + qb[:, h * LANE:(h + 1) * LANE] * tabr_q).astype(BF16)
        kcat_ref[:, h * HK:h * HK + NOPE] = kn[:, h * NOPE:(h + 1) * NOPE].astype(BF16)
        kcat_ref[:, h * HK + NOPE:(h + 1) * HK] = kpeb


def _mla_proj(u, tab, lw):
    r = u.shape[0]
    bm = _pick(r, (544, 512, 272, 256, 136, 128, 64))

    def full(shape):
        return pl.BlockSpec(shape, lambda i: (0,) * len(shape))

    def row(w):
        return pl.BlockSpec((bm, w), lambda i: (i, 0))

    return pl.pallas_call(
        _mla_proj_kernel,
        grid=(r // bm,),
        in_specs=[pl.BlockSpec((bm, MLA_W), lambda i: (i, C_MLA // MLA_W)), row(LANE),
                  full((1, QL)), full((1, KVL)), full((QL, MLA_H * HK)), full((QL, MLA_H * LANE)),
                  full((KVL, MLA_H * NOPE)), full((KVL, MLA_H * MLA_V))],
        out_specs=[row(MLA_H * HK), row(KVL), row(LANE), row(MLA_H * HK), row(MLA_H * MLA_V)],
        out_shape=[jax.ShapeDtypeStruct((r, MLA_H * HK), BF16), jax.ShapeDtypeStruct((r, KVL), F32),
                   jax.ShapeDtypeStruct((r, LANE), F32), jax.ShapeDtypeStruct((r, MLA_H * HK), BF16),
                   jax.ShapeDtypeStruct((r, MLA_H * MLA_V), BF16)],
        compiler_params=_cp(1), name="mla_proj",
    )(u, tab, lw["qn"], lw["kvn"], lw["wa"], lw["wb"], lw["wk"], lw["wv"])


def _flash_kernel(it_ref, jt_ref, q_ref, k_ref, v_ref, k0_ref, v0_ref, mask_ref, pad_ref, o_ref,
                  qt_s, m_s, l_s, acc_s, s_scr, p_scr, a_s):
    tq = q_ref.shape[0]
    tk = k_ref.shape[0]
    step = pl.program_id(0)
    i = it_ref[step]
    j = jt_ref[step]

    @pl.when(j == 0)
    def _init():
        for h in range(MLA_H):
            qt_s[h * HK:(h + 1) * HK, :] = q_ref[:, h * HK:(h + 1) * HK].T
        for h in range(MLA_H):
            s0 = jnp.dot(k0_ref[:, h * HK:(h + 1) * HK], qt_s[h * HK:(h + 1) * HK, :],
                         preferred_element_type=F32) + pad_ref[...]
            m0 = jnp.max(s0, axis=0, keepdims=True)
            p0 = jnp.exp2(s0 - m0)
            rows = slice(h * MLA_V, (h + 1) * MLA_V)
            m_s[h] = m0
            l_s[h] = jnp.sum(p0, axis=0, keepdims=True)
            acc_s[rows, :] = lax.dot_general(v0_ref[:, rows], p0.astype(BF16), TN, preferred_element_type=F32)

    def tile(masked):
        nkb = tk // CH

        def block(sref, kb):
            blk = sref[kb * CH:(kb + 1) * CH, :]
            if masked:
                blk = blk + mask_ref[kb * CH:(kb + 1) * CH, :]
            return blk

        def scores(h):
            s_scr[h % 2] = jnp.dot(k_ref[:, h * HK:(h + 1) * HK], qt_s[h * HK:(h + 1) * HK, :],
                                   preferred_element_type=F32)

        def softmax(h):
            sref = s_scr.at[h % 2]
            pref = p_scr.at[h % 2]
            mx = block(sref, 0).reshape(CH // 8, 8, tq).max(axis=0)
            for kb in range(1, nkb):
                mx = jnp.maximum(mx, block(sref, kb).reshape(CH // 8, 8, tq).max(axis=0))
            m_prev = m_s[h]
            m_new = jnp.maximum(m_prev, jnp.max(mx, axis=0, keepdims=True))
            alpha = jnp.exp2(m_prev - m_new)
            part = jnp.zeros((8, tq), F32)
            for kb in range(nkb):
                e = jnp.exp2(block(sref, kb) - m_new)
                part = part + jnp.sum(e.reshape(CH // 8, 8, tq), axis=0)
                pref[kb * CH:(kb + 1) * CH, :] = e.astype(BF16)
            l_s[h] = alpha * l_s[h] + jnp.sum(part, axis=0, keepdims=True)
            m_s[h] = m_new
            a_s[h % 2] = alpha

        def weighted_values(h):
            rows = slice(h * MLA_V, (h + 1) * MLA_V)
            acc_s[rows, :] = a_s[h % 2] * acc_s[rows, :] + lax.dot_general(
                v_ref[:, rows], p_scr[h % 2], TN, preferred_element_type=F32)

        scores(0)
        for h in range(MLA_H + 1):
            if h + 1 < MLA_H:
                scores(h + 1)
            if h >= 1:
                weighted_values(h - 1)
            if h < MLA_H:
                softmax(h)

    @pl.when(j < i)
    def _():
        tile(False)

    @pl.when(j == i)
    def _():
        tile(True)
        for h in range(MLA_H):
            rows = slice(h * MLA_V, (h + 1) * MLA_V)
            o_ref[:, rows] = (acc_s[rows, :] / l_s[h]).T.astype(BF16)


def _pad_mask(n):
    return jnp.asarray(np.where(np.arange(CH)[:, None] + 0 * np.arange(n)[None, :] < NSKIP, -np.inf, 0.0),
                       F32)


def _flash(q, kcat, v, seq):
    r = q.shape[0]
    tq = _pick(seq, (FLASH_TQ, 256, 128))
    nq = seq // tq
    c0 = seq // CH
    it = np.concatenate([np.full((i + 1,), i, np.int32) for i in range(nq)])
    jt = np.concatenate([np.arange(i + 1, dtype=np.int32) for i in range(nq)])
    kk = np.arange(tq)[:, None]
    qq = np.arange(tq)[None, :]
    diag = jnp.asarray(np.where(kk // CH > qq // CH, -np.inf, 0.0), F32)
    grid_spec = pltpu.PrefetchScalarGridSpec(
        num_scalar_prefetch=2,
        grid=(int(it.shape[0]),),
        in_specs=[pl.BlockSpec((tq, MLA_H * HK), lambda s, it, jt: (it[s], 0)),
                  pl.BlockSpec((tq, MLA_H * HK), lambda s, it, jt: (jt[s], 0)),
                  pl.BlockSpec((tq, MLA_H * MLA_V), lambda s, it, jt: (jt[s], 0)),
                  pl.BlockSpec((CH, MLA_H * HK), lambda s, it, jt: (c0, 0)),
                  pl.BlockSpec((CH, MLA_H * MLA_V), lambda s, it, jt: (c0, 0)),
                  pl.BlockSpec((tq, tq), lambda s, it, jt: (0, 0)),
                  pl.BlockSpec((CH, tq), lambda s, it, jt: (0, 0))],
        out_specs=pl.BlockSpec((tq, MLA_H * MLA_V), lambda s, it, jt: (it[s], 0)),
        scratch_shapes=[pltpu.VMEM((MLA_H * HK, tq), BF16),
                        pltpu.VMEM((MLA_H, 1, tq), F32), pltpu.VMEM((MLA_H, 1, tq), F32),
                        pltpu.VMEM((MLA_H * MLA_V, tq), F32),
                        pltpu.VMEM((2, tq, tq), F32), pltpu.VMEM((2, tq, tq), BF16),
                        pltpu.VMEM((2, 1, tq), F32)],
    )
    return pl.pallas_call(
        _flash_kernel,
        grid_spec=grid_spec,
        out_shape=jax.ShapeDtypeStruct((r, MLA_H * MLA_V), BF16),
        compiler_params=_cp(1), name="flash",
    )(jnp.asarray(it), jnp.asarray(jt), q, kcat, v, kcat, v, diag, _pad_mask(tq))


def _attn0_kernel(q_ref, k_ref, v_ref, keep_ref, o_ref):
    del keep_ref
    col = lax.broadcasted_iota(jnp.int32, (CH, CH), 1)
    for h in range(MLA_H):
        s = lax.dot_general(q_ref[:, h * HK:(h + 1) * HK], k_ref[:, h * HK:(h + 1) * HK], NT,
                            preferred_element_type=F32)
        s = jnp.where(col >= NSKIP, s, -jnp.inf)
        p = jnp.exp2(s - jnp.max(s, axis=-1, keepdims=True))
        o = jnp.dot(p.astype(BF16), v_ref[:, h * MLA_V:(h + 1) * MLA_V], preferred_element_type=F32)
        o_ref[:, h * MLA_V:(h + 1) * MLA_V] = (o / jnp.sum(p, axis=-1, keepdims=True)).astype(BF16)


def _attn0(q, kcat, v, c0, o_frames):
    def blk(w):
        return pl.BlockSpec((CH, w), lambda i: (c0, 0))

    return pl.pallas_call(
        _attn0_kernel,
        grid=(1,),
        in_specs=[blk(MLA_H * HK), blk(MLA_H * HK), blk(MLA_H * MLA_V), pl.BlockSpec(memory_space=pl.ANY)],
        out_specs=blk(MLA_H * MLA_V),
        out_shape=jax.ShapeDtypeStruct(o_frames.shape, BF16),
        input_output_aliases={3: 0},
        compiler_params=_cp(1), name="attn0",
    )(q, kcat, v, o_frames)


def _cached_attn_kernel(q_ref, cnew_ref, pnew_ref, ckv_ref, cpe_ref, wkt_ref, wv_ref, o_ref, call, pet):
    past = ckv_ref.shape[2]
    call[0:past, :] = ckv_ref[0, 0].astype(BF16)
    call[past:past + CH, :] = cnew_ref[...].astype(BF16)
    pet[0:ROPE, :] = cpe_ref[0, 0].astype(BF16)
    pet[ROPE:, :] = jnp.zeros((LANE - ROPE, past), BF16)
    qlat = []
    qpe = []
    for h in range(MLA_H):
        qn = q_ref[:, h * HK:h * HK + NOPE]
        qlat.append(jnp.dot(qn, wkt_ref[h], preferred_element_type=F32).astype(BF16))
        qpe.append(q_ref[:, h * HK + NOPE:(h + 1) * HK])
    qlat = jnp.concatenate(qlat, axis=0)
    qpe = jnp.concatenate(qpe, axis=0)
    s_pe = jnp.concatenate(
        [jnp.dot(qpe, pet[...], preferred_element_type=F32),
         lax.dot_general(qpe, pnew_ref[...].astype(BF16), NT, preferred_element_type=F32)], axis=1)
    s = lax.dot_general(qlat, call[...], NT, preferred_element_type=F32) + s_pe
    p = jnp.exp2(s - jnp.max(s, axis=-1, keepdims=True))
    p = p / jnp.sum(p, axis=-1, keepdims=True)
    olat = jnp.dot(p.astype(BF16), call[...], preferred_element_type=F32).astype(BF16)
    for h in range(MLA_H):
        o_ref[:, h * MLA_V:(h + 1) * MLA_V] = jnp.dot(
            olat[h * CH:(h + 1) * CH, :], wv_ref[h], preferred_element_type=F32).astype(BF16)


def _cached_attn_alias_kernel(q_ref, cnew_ref, pnew_ref, ckv_ref, cpe_ref, wkt_ref, wv_ref, keep_ref, o_ref,
                              call, pet):
    del keep_ref
    _cached_attn_kernel(q_ref, cnew_ref, pnew_ref, ckv_ref, cpe_ref, wkt_ref, wv_ref, o_ref, call, pet)


def _cached_attn(q, ckv, kpe, cache_kv, cache_pe_t, layer, rb0, lw, o_prompt):
    nb, past = cache_kv.shape[1], cache_kv.shape[2]

    def rowmap(b):
        return (rb0 + b, 0)

    return pl.pallas_call(
        _cached_attn_alias_kernel,
        grid=(nb,),
        in_specs=[pl.BlockSpec((CH, MLA_H * HK), rowmap),
                  pl.BlockSpec((CH, KVL), rowmap),
                  pl.BlockSpec((CH, LANE), rowmap),
                  pl.BlockSpec((1, 1, past, KVL), lambda b: (layer, b, 0, 0)),
                  pl.BlockSpec((1, 1, ROPE, past), lambda b: (layer, b, 0, 0)),
                  pl.BlockSpec((MLA_H, NOPE, KVL), lambda b: (0, 0, 0)),
                  pl.BlockSpec((MLA_H, KVL, MLA_V), lambda b: (0, 0, 0)),
                  pl.BlockSpec(memory_space=pl.ANY)],
        out_specs=pl.BlockSpec((CH, MLA_H * MLA_V), rowmap),
        out_shape=jax.ShapeDtypeStruct(o_prompt.shape, BF16),
        scratch_shapes=[pltpu.VMEM((past + CH, KVL), BF16), pltpu.VMEM((LANE, past), BF16)],
        input_output_aliases={7: 0},
        compiler_params=_cp(1), name="cached_attn",
    )(q, ckv, kpe, cache_kv, cache_pe_t, lw["wkt"], lw["wv3"], o_prompt)


TAIL = 16


def _shift_matrix(k, rows=CH):
    ns = k - 1
    pm = np.zeros((rows * ns + TAIL, TAIL + rows), np.float32)
    for d in range(1, ns + 1):
        for t in range(rows):
            if t - d >= 0:
                pm[(d - 1) * rows + t, TAIL + t - d] = 1.0
            else:
                r = ns + t - d
                pm[(d - 1) * rows + t, 3 * r:3 * r + 3] = 1.0
    for r in range(ns):
        pm[rows * ns + 3 * r, TAIL + rows - ns + r] = 1.0
    return pm


def _load_tail(tail_s, prev):
    hi = prev.astype(BF16).astype(F32)
    rest = prev - hi
    mid = rest.astype(BF16).astype(F32)
    lo = (rest - mid).astype(BF16).astype(F32)
    tail_s[...] = jnp.zeros(tail_s.shape, F32)
    for r in range(prev.shape[0]):
        tail_s[3 * r:3 * r + 1, :] = hi[r:r + 1]
        tail_s[3 * r + 1:3 * r + 2, :] = mid[r:r + 1]
        tail_s[3 * r + 2:3 * r + 3, :] = lo[r:r + 1]


def _delayed(pm_ref, tail_s, x, ns):
    t = x.shape[0]
    ext = jnp.concatenate([tail_s[...].astype(BF16), x], axis=0)
    out = jnp.dot(pm_ref[...], ext, preferred_element_type=F32)
    tail_s[...] = out[t * ns:, :]
    return [out[(d - 1) * t:d * t, :] for d in range(1, ns + 1)]


def _act_kernel(nskip, g_ref, u_ref, prev_ref, w_ref, pm_ref, a_ref, new_ref, tail_s):
    t = g_ref.shape[0]
    ns = FFN_K - 1
    c = pl.program_id(1)

    @pl.when(c == 0)
    def _():
        _load_tail(tail_s, prev_ref[0])

    rows = lax.broadcasted_iota(jnp.int32, (t, 1), 0) + c * t
    g = g_ref[...]
    g = jnp.where(rows >= nskip, g, jnp.zeros((), g.dtype))
    delayed = _delayed(pm_ref, tail_s, g, ns)
    conv = w_ref[ns:ns + 1, :] * g.astype(F32)
    for d in range(1, ns + 1):
        conv = conv + w_ref[ns - d:ns - d + 1, :] * delayed[d - 1]
    a_ref[...] = (_silu(conv) * u_ref[...].astype(F32)).astype(BF16)

    @pl.when(c == pl.num_programs(1) - 1)
    def _():
        for r in range(ns):
            new_ref[0, r:r + 1, :] = tail_s[3 * r:3 * r + 1, :]


def _act(gate, up, rb0, nb, nch, nskip, prev, w, act_in=None, t=CH):
    r, n = gate.shape

    def rowmap(b, c):
        return (rb0 + b * nch + c, 0)

    pm = jnp.asarray(_shift_matrix(FFN_K, t), BF16)
    args = [gate, up, prev, w, pm]
    in_specs = [pl.BlockSpec((t, n), rowmap), pl.BlockSpec((t, n), rowmap),
                pl.BlockSpec((1, FFN_K - 1, n), lambda b, c: (b, 0, 0)),
                pl.BlockSpec((FFN_K, n), lambda b, c: (0, 0)),
                pl.BlockSpec(pm.shape, lambda b, c: (0, 0))]
    kern = functools.partial(_act_kernel, nskip)
    aliases = {}
    if act_in is not None:
        aliases = {len(args): 0}
        args.append(act_in)
        in_specs.append(pl.BlockSpec(memory_space=pl.ANY))
        kern = functools.partial(_act_alias_kernel, nskip)
    return pl.pallas_call(
        kern,
        grid=(nb, nch),
        in_specs=in_specs,
        out_specs=[pl.BlockSpec((t, n), rowmap), pl.BlockSpec((1, FFN_K - 1, n), lambda b, c: (b, 0, 0))],
        out_shape=[jax.ShapeDtypeStruct((r, n), BF16), jax.ShapeDtypeStruct((nb, FFN_K - 1, n), F32)],
        scratch_shapes=[pltpu.VMEM((TAIL, n), F32)],
        input_output_aliases=aliases,
        compiler_params=_cp(2), name="act",
    )(*args)


def _act_alias_kernel(nskip, g_ref, u_ref, prev_ref, w_ref, pm_ref, keep_ref, a_ref, new_ref, tail_s):
    del keep_ref
    _act_kernel(nskip, g_ref, u_ref, prev_ref, w_ref, pm_ref, a_ref, new_ref, tail_s)


def _cast_cols_kernel(x_ref, o_ref):
    n = x_ref.shape[1]
    o_ref[:, :n] = x_ref[...].astype(BF16)
    if o_ref.shape[1] > n:
        o_ref[:, n:] = jnp.zeros((o_ref.shape[0], o_ref.shape[1] - n), BF16)


def _cast_pad_cols(w, layer, n_out):
    _, k, n = w.shape
    tr = _pick(k, (128, 64))
    return pl.pallas_call(
        _cast_cols_kernel,
        grid=(k // tr,),
        in_specs=[pl.BlockSpec((None, tr, n), lambda i: (layer, i, 0))],
        out_specs=pl.BlockSpec((tr, n_out), lambda i: (i, 0)),
        out_shape=jax.ShapeDtypeStruct((k, n_out), BF16),
        compiler_params=_cp(1), name="cast_cols",
    )(w)


def _cast_rows_kernel(n_full, x_ref, o_ref):
    @pl.when(pl.program_id(0) < n_full)
    def _():
        o_ref[...] = x_ref[...].astype(BF16)

    @pl.when(pl.program_id(0) >= n_full)
    def _():
        o_ref[...] = jnp.zeros(o_ref.shape, BF16)


def _cast_pad_rows(w, layer, k_out):
    _, k, n = w.shape
    tr = 256
    assert k % tr == 0 and k_out % tr == 0
    n_full = k // tr
    return pl.pallas_call(
        functools.partial(_cast_rows_kernel, n_full),
        grid=(k_out // tr,),
        in_specs=[pl.BlockSpec((None, tr, n), lambda i: (layer, jnp.minimum(i, n_full - 1), 0))],
        out_specs=pl.BlockSpec((tr, n), lambda i: (i, 0)),
        out_shape=jax.ShapeDtypeStruct((k_out, n), BF16),
        compiler_params=_cp(1), name="cast_rows",
    )(w)


PACK = 256


def _pack_win_kernel(a_ref, b_ref, o_ref):
    j = pl.program_id(0)
    o_dt = SSD_DI + SSD_CD
    n_plain = o_dt // PACK
    last = pl.num_programs(0) - 1
    half = ROPE // 2

    def emit(rows):
        o_ref[...] = rows.T.astype(BF16)

    @pl.when(j < n_plain)
    def _():
        emit(a_ref[...])

    @pl.when((j >= n_plain) & (j < last))
    def _():
        emit(jnp.concatenate([a_ref[SSD_H:, :], b_ref[:SSD_H, :]], axis=0))

    @pl.when(j == last)
    def _():
        kr0 = a_ref[SSD_H:SSD_H + half, :]
        kr1 = a_ref[SSD_H + half:SSD_H + ROPE, :]
        emit(jnp.concatenate([kr0, kr1, kr1, kr0, b_ref[:SSD_H, :],
                              jnp.zeros((PACK - 2 * ROPE - SSD_H, a_ref.shape[1]), F32)], axis=0))


def _pack_win(w_in, layer):
    _, k, n = w_in.shape
    assert (SSD_DI + SSD_CD) % PACK == 0 and NIN % PACK == 0 and n - (NIN - PACK) == SSD_H + ROPE
    wt = jnp.swapaxes(w_in, 1, 2)
    nblk = NIN // PACK
    n_plain = (SSD_DI + SSD_CD) // PACK

    def b_index(j):
        return jnp.where((j < n_plain) | (j == nblk - 1), n_plain, j + 1)

    return pl.pallas_call(
        _pack_win_kernel,
        grid=(nblk,),
        in_specs=[pl.BlockSpec((None, PACK, k), lambda j: (layer, j, 0)),
                  pl.BlockSpec((None, PACK, k), lambda j: (layer, b_index(j), 0))],
        out_specs=pl.BlockSpec((k, PACK), lambda j: (0, j)),
        out_shape=jax.ShapeDtypeStruct((k, NIN), BF16),
        compiler_params=_cp(1), name="pack_win",
    )(wt, wt)


def _swap_half(w):
    half = w.shape[-1] // 2
    return jnp.concatenate([w[..., half:], w[..., :half]], axis=-1)


def _layer_weights(i, w_in, ssd_conv_w, ssd_conv_b, ssd_dt_bias, ssd_a_log, ssd_d, ssd_norm, sc_conv_w,
                   mla_q_norm, mla_w_uq, mla_kv_norm, mla_w_ukv, w_out, ffn_w_gate, ffn_w_up, ffn_conv_w,
                   ffn_w_down):
    win = _pack_win(w_in, i)
    uq = mla_w_uq[i].reshape(QL, MLA_H, NOPE + ROPE)
    pe = uq[..., NOPE:]
    zq = jnp.zeros((QL, MLA_H, HK - NOPE - ROPE), F32)
    wa = jnp.concatenate([uq[..., :NOPE], pe, zq], axis=-1).reshape(QL, MLA_H * HK).astype(BF16)
    wb = jnp.concatenate([_swap_half(pe), zq], axis=-1).reshape(QL, MLA_H * LANE).astype(BF16)
    ukv = mla_w_ukv[i].reshape(KVL, MLA_H, NOPE + MLA_V)
    padf = ((0, 0), (0, DFP - D_FF))
    return dict(
        win=win,
        cw=ssd_conv_w[i], cb=ssd_conv_b[i].reshape(1, SSD_CD),
        dtb=jnp.pad(ssd_dt_bias[i], (0, LANE - SSD_H)).reshape(1, LANE),
        alog=jnp.pad(ssd_a_log[i], (0, LANE - SSD_H)).reshape(1, LANE),
        dsk=jnp.repeat(ssd_d[i], SSD_P).reshape(1, SSD_DI),
        ng=ssd_norm[i].reshape(1, SSD_DI),
        scw=sc_conv_w[i],
        qn=mla_q_norm[i].reshape(1, QL), kvn=mla_kv_norm[i].reshape(1, KVL),
        wa=wa, wb=wb,
        wk=ukv[..., :NOPE].reshape(KVL, MLA_H * NOPE).astype(BF16),
        wv=ukv[..., NOPE:].reshape(KVL, MLA_H * MLA_V).astype(BF16),
        wkt=jnp.transpose(ukv[..., :NOPE], (1, 2, 0)).astype(BF16),
        wv3=jnp.transpose(ukv[..., NOPE:], (1, 0, 2)).astype(BF16),
        wout=_cast_pad_cols(w_out, i, D_MODEL),
        wg=_cast_pad_cols(ffn_w_gate, i, DFP),
        wu=_cast_pad_cols(ffn_w_up, i, DFP),
        fcw=jnp.pad(ffn_conv_w[i], padf),
        wd=_cast_pad_rows(ffn_w_down, i, DFP),
    )


def _rope_table(seq, ns, past):
    half = ROPE // 2
    pos = jnp.concatenate([N_META + jnp.arange(seq, dtype=jnp.int32),
                           jnp.maximum(jnp.arange(CH, dtype=jnp.int32) - NSKIP, 0),
                           N_META + past + jnp.arange(ns, dtype=jnp.int32) % CH])
    inv = THETA ** (-jnp.arange(half, dtype=F32) / half)
    ang = pos.astype(F32)[:, None] * inv[None, :]
    cos, sin = jnp.cos(ang), jnp.sin(ang)
    return jnp.concatenate([cos, cos, -sin, sin], axis=1)


def kernel(x_prompt, x_sample, cache_kv_latent, cache_k_rope, state_ssm, state_ssd_conv, state_sconv, state_ffn_conv, meta_tokens, norm_mix_pre, norm_mix_post, norm_ffn_pre, norm_ffn_post, w_in, ssd_conv_w, ssd_conv_b, ssd_dt_bias, ssd_a_log, ssd_d, ssd_norm, sc_conv_w, mla_q_norm, mla_w_uq, mla_kv_norm, mla_w_ukv, w_out, ffn_w_gate, ffn_w_up, ffn_conv_w, ffn_w_down):
    bp, seq, d = x_prompt.shape
    nb, ls, _ = x_sample.shape
    depth, _, past, _ = cache_kv_latent.shape
    assert bp == 1 and ls == CH and seq % CH == 0 and d == D_MODEL
    lp = CH + seq
    ns = nb * ls
    npc = lp // CH
    c0 = seq // CH

    x, h = _assemble(x_prompt[0], x_sample.reshape(ns, d), meta_tokens.astype(F32), norm_mix_pre[0])
    tab = _rope_table(seq, ns, past)
    cache_pe_t = jnp.swapaxes(cache_k_rope, 2, 3)
    zero_c = jnp.zeros((1, SSD_K - 1, SSD_CD), F32)
    zero_h = jnp.zeros((1, SSD_N, SSD_DI), F32)

    def state_in(s):
        return jnp.transpose(s, (0, 3, 1, 2)).reshape(s.shape[0], SSD_N, SSD_DI)

    def state_out(s):
        return jnp.transpose(s.reshape(s.shape[0], SSD_N, SSD_H, SSD_P), (0, 2, 3, 1))
    zero_s = jnp.zeros((1, SC_K - 1, SC_D), F32)
    zero_f = jnp.zeros((1, FFN_K - 1, DFP), F32)
    padf = ((0, 0), (0, 0), (0, DFP - D_FF))

    outs_p, outs_s = [], []
    for i in range(depth):
        lw = _layer_weights(i, w_in, ssd_conv_w, ssd_conv_b, ssd_dt_bias, ssd_a_log, ssd_d, ssd_norm,
                            sc_conv_w, mla_q_norm, mla_w_uq, mla_kv_norm, mla_w_ukv, w_out, ffn_w_gate,
                            ffn_w_up, ffn_conv_w, ffn_w_down)
        u, dtf = _inproj(h, lw["win"])

        yp, yscp, cnew_p, hnew_p, scnew_p = _mixer(u, dtf, 0, 1, npc, NSKIP, zero_c, zero_h, zero_s, lw)
        y_ssd, y_sc, cnew_s, hnew_s, scnew_s = _mixer(u, dtf, npc, nb, 1, 0, state_ssd_conv[i],
                                                      state_in(state_ssm[i]), state_sconv[i], lw,
                                                      keep=(yp, yscp))
        hnew_p, hnew_s = state_out(hnew_p), state_out(hnew_s)

        q, ckv, kpe, kcat, v = _mla_proj(u, tab, lw)
        y_mla = _attn0(q, kcat, v, c0, _flash(q, kcat, v, seq))
        y_mla = _cached_attn(q, ckv, kpe, cache_kv_latent, cache_pe_t, i, npc, lw, y_mla)

        mix = _outproj(y_ssd, y_sc, y_mla, lw["wout"])
        x, h = _resid_norm(x, mix, norm_mix_post[i], norm_ffn_pre[i])

        gate, up = _gateup(h, lw["wg"], lw["wu"])
        act, f0 = _act(gate, up, c0, 1, 1, NSKIP, zero_f, lw["fcw"])
        act, fnew_p = _act(gate, up, 0, 1, seq // ACT_ROWS, 0, f0, lw["fcw"], act_in=act, t=ACT_ROWS)
        act, fnew_s = _act(gate, up, npc, nb, 1, 0, jnp.pad(state_ffn_conv[i], padf), lw["fcw"], act_in=act)
        f = _down(act, lw["wd"])
        if i + 1 < depth:
            x, h = _resid_norm(x, f, norm_ffn_post[i], norm_mix_pre[i + 1])
        else:
            y_prompt = _resid_out(x, f, norm_ffn_post[i], 0, seq)
            y_sample = _resid_out(x, f, norm_ffn_post[i], lp, ns)

        meta_rows = slice(seq + NSKIP, lp)
        outs_p.append((jnp.concatenate([ckv[meta_rows], ckv[:seq]], axis=0)[None],
                       jnp.concatenate([kpe[meta_rows, :ROPE], kpe[:seq, :ROPE]], axis=0)[None],
                       hnew_p, cnew_p, scnew_p, fnew_p[:, :, :D_FF]))
        outs_s.append((ckv[lp:].reshape(nb, ls, KVL), kpe[lp:, :ROPE].reshape(nb, ls, ROPE), hnew_s, cnew_s,
                       scnew_s, fnew_s[:, :, :D_FF]))

    def stack(outs, j):
        return jnp.stack([o[j] for o in outs], axis=0)

    return (y_prompt[None], y_sample.reshape(nb, ls, d),
            stack(outs_p, 0), stack(outs_p, 1), stack(outs_p, 2), stack(outs_p, 3), stack(outs_p, 4), stack(outs_p, 5),
            stack(outs_s, 0), stack(outs_s, 1), stack(outs_s, 2), stack(outs_s, 3), stack(outs_s, 4), stack(outs_s, 5))
```
